```python
import jax, jax.numpy as jnp
from jax import lax
import numpy as np

D_MODEL = 2048
BATCH = 8
SEQ = 4096
DEPTH = 2

N_A_LAYERS = max(1, DEPTH // 2)
N_B_LAYERS = DEPTH - N_A_LAYERS

HGRN_EXPAND = 128
HGRN_HEADS = D_MODEL // HGRN_EXPAND
HGRN_HEAD_V = D_MODEL // HGRN_HEADS
HGRN_CHUNK = 32

HEAD_DIM = 64
N_Q_HEADS = D_MODEL // HEAD_DIM
N_KV_HEADS = N_Q_HEADS // 8
GROUP = N_Q_HEADS // N_KV_HEADS
WINDOW = 128
ROT_DIM = HEAD_DIM // 4
ROPE_THETA = 500000.0

D_FF = 5504
N_SUBLAYERS = 3
NORM_EPS = 1e-6
NEG_INF = -1e30

kernel_name = "yoco_hgrn2_swa_sink_macaron_adaln"


def rmsnorm(x, gain):
    xf = x.astype(jnp.float32)
    y = xf * lax.rsqrt(jnp.mean(xf * xf, axis=-1, keepdims=True) + NORM_EPS)
    return (y * gain.astype(jnp.float32)).astype(x.dtype)


def ada_norm(h, gain, shift, scale):
    return rmsnorm(h, gain) * (1.0 + scale[:, None, :]) + shift[:, None, :]


def swiglu(u, w_in, w_out):
    a, b = jnp.split(u @ w_in, 2, axis=-1)
    return (jax.nn.silu(a) * b) @ w_out


def rope_tables(seq):
    inv_freq = jnp.power(jnp.float32(ROPE_THETA), -jnp.arange(0, ROT_DIM, 2, dtype=jnp.float32) / ROT_DIM)
    ang = jnp.arange(seq, dtype=jnp.float32)[:, None] * inv_freq[None, :]
    return jnp.sin(ang), jnp.cos(ang)


def partial_rotary(t, sin, cos):
    half = ROT_DIM // 2
    s = sin[None, :, None, :].astype(t.dtype)
    c = cos[None, :, None, :].astype(t.dtype)
    t1, t2, rest = t[..., :half], t[..., half:ROT_DIM], t[..., ROT_DIM:]
    return jnp.concatenate([t1 * c - t2 * s, t2 * c + t1 * s, rest], axis=-1)


def hgrn2_chunk_scan(q, k, v, log_f):
    B, S, H, K = q.shape
    V = v.shape[-1]
    nc = S // HGRN_CHUNK

    def to_chunks(t):
        return t.reshape(B, nc, HGRN_CHUNK, H, t.shape[-1]).transpose(1, 0, 3, 2, 4)

    causal = jnp.tril(jnp.ones((HGRN_CHUNK, HGRN_CHUNK), dtype=bool))

    def step(state, inp):
        qc, kc, vc, gc = inp
        b = jnp.cumsum(gc, axis=2)
        inter = jnp.einsum('bhck,bhkv->bhcv', qc * jnp.exp(b), state)
        rel = b[:, :, :, None, :] - b[:, :, None, :, :]
        decay = jnp.exp(jnp.where(causal[None, None, :, :, None], rel, -jnp.inf))
        scores = jnp.einsum('bhtk,bhsk,bhtsk->bhts', qc, kc, decay)
        out = inter + jnp.einsum('bhts,bhsv->bhtv', scores, vc)
        b_last = b[:, :, -1, :]
        k_dec = kc * jnp.exp(b_last[:, :, None, :] - b)
        new_state = jnp.exp(b_last)[..., None] * state + jnp.einsum('bhck,bhcv->bhkv', k_dec, vc)
        return new_state, out

    state0 = jnp.zeros((B, H, K, V), jnp.float32)
    _, outs = lax.scan(step, state0, (to_chunks(q), to_chunks(k), to_chunks(v), to_chunks(log_f)))
    return outs.transpose(1, 0, 3, 2, 4).reshape(B, S, H, V)


def hgrn2_mixer(u, w_in, lower_bound, head_gain, w_out):
    B, S, _ = u.shape
    q, f, i, g = jnp.split(u @ w_in, 4, axis=-1)
    forget = lower_bound + (1.0 - lower_bound) * jax.nn.sigmoid(f.astype(jnp.float32))

    def heads(t):
        return t.reshape(B, S, HGRN_HEADS, -1)

    o = hgrn2_chunk_scan(heads(jax.nn.silu(q).astype(jnp.float32)),
                         heads(1.0 - forget),
                         heads(i.astype(jnp.float32)),
                         heads(jnp.log(forget)))
    o = rmsnorm(o, head_gain.reshape(HGRN_HEADS, HGRN_HEAD_V)).astype(u.dtype)
    o = o.reshape(B, S, D_MODEL) * jax.nn.sigmoid(g)
    return o @ w_out


def shared_kv(h, cs, kv_gain, w_ada_kv, b_ada_kv, w_kv, b_kv, sin, cos):
    B, S, _ = h.shape
    shift, scale = jnp.split(cs @ w_ada_kv + b_ada_kv, 2, axis=-1)
    u = ada_norm(h, kv_gain, shift, scale)
    k, v = jnp.split(u @ w_kv + b_kv, 2, axis=-1)
    k = partial_rotary(k.reshape(B, S, N_KV_HEADS, HEAD_DIM), sin, cos)
    v = v.reshape(B, S, N_KV_HEADS, HEAD_DIM)
    return k, v


def sliding_window_attention(q, k, v, sinks):
    B, S, _, Dh = q.shape
    nb = S // WINDOW
    qb = q.reshape(B, nb, WINDOW, N_KV_HEADS, GROUP, Dh)

    def band(t):
        tb = t.reshape(B, nb, WINDOW, N_KV_HEADS, Dh)
        prev = jnp.pad(tb, ((0, 0), (1, 0), (0, 0), (0, 0), (0, 0)))[:, :nb]
        return jnp.concatenate([prev, tb], axis=2)

    kb, vb = band(k), band(v)
    scores = jnp.einsum('bnqhgd,bnkhd->bnhgqk', qb, kb).astype(jnp.float32) * (Dh ** -0.5)
    qi = jnp.arange(WINDOW)[:, None]
    kj = jnp.arange(2 * WINDOW)[None, :]
    in_window = (kj > qi) & (kj <= qi + WINDOW)
    not_before_start = (jnp.arange(nb)[:, None, None] > 0) | (kj[None] >= WINDOW)
    mask = in_window[None] & not_before_start
    scores = jnp.where(mask[None, :, None, None], scores, NEG_INF)
    sink = sinks.astype(jnp.float32).reshape(N_KV_HEADS, GROUP)[None, None, :, :, None, None]
    m = jnp.maximum(jnp.max(scores, axis=-1, keepdims=True), sink)
    p = jnp.exp(scores - m)
    probs = p / (jnp.sum(p, axis=-1, keepdims=True) + jnp.exp(sink - m))
    out = jnp.einsum('bnhgqk,bnkhd->bnqhgd', probs.astype(v.dtype), vb)
    return out.reshape(B, S, N_Q_HEADS * Dh)


def _fwd_setup_inputs(seed: int = 0) -> dict:
    key = jax.random.key(seed)
    ks = jax.random.split(key, 22)
    f32 = jnp.float32

    def w(k, shape, fan_in, scale=1.0):
        return jax.random.normal(k, shape, f32) * (scale * fan_in ** -0.5)

    def gain(k, shape):
        return 1.0 + 0.02 * jax.random.normal(k, shape, f32)

    qdim = N_Q_HEADS * HEAD_DIM
    kvdim = 2 * N_KV_HEADS * HEAD_DIM
    return {
        "x": jax.random.normal(ks[0], (BATCH, SEQ, D_MODEL), f32),
        "c": jax.random.normal(ks[1], (BATCH, D_MODEL), f32),
        "norm_gain": gain(ks[2], (DEPTH, N_SUBLAYERS, D_MODEL)),
        "w_ada": w(ks[3], (DEPTH, D_MODEL, N_SUBLAYERS * 3 * D_MODEL), D_MODEL, 0.5),
        "b_ada": 0.02 * jax.random.normal(ks[4], (DEPTH, N_SUBLAYERS * 3 * D_MODEL), f32),
        "w_ffn_in": w(ks[5], (DEPTH, 2, D_MODEL, 2 * D_FF), D_MODEL),
        "w_ffn_out": w(ks[6], (DEPTH, 2, D_FF, D_MODEL), D_FF),
        "w_hgrn_in": w(ks[7], (N_A_LAYERS, D_MODEL, 4 * D_MODEL), D_MODEL),
        "hgrn_lb_logits": 0.5 * jax.random.normal(ks[8], (N_A_LAYERS + 1, D_MODEL), f32),
        "hgrn_head_gain": gain(ks[9], (N_A_LAYERS, D_MODEL)),
        "w_hgrn_out": w(ks[10], (N_A_LAYERS, D_MODEL, D_MODEL), D_MODEL),
        "kv_gain": gain(ks[11], (D_MODEL,)),
        "w_ada_kv": w(ks[12], (D_MODEL, 2 * D_MODEL), D_MODEL, 0.5),
        "b_ada_kv": 0.02 * jax.random.normal(ks[13], (2 * D_MODEL,), f32),
        "w_kv": w(ks[14], (D_MODEL, kvdim), D_MODEL),
        "b_kv": 0.02 * jax.random.normal(ks[15], (kvdim,), f32),
        "w_q": w(ks[16], (N_B_LAYERS, D_MODEL, qdim), D_MODEL),
        "b_q": 0.02 * jax.random.normal(ks[17], (N_B_LAYERS, qdim), f32),
        "attn_sinks": jax.random.normal(ks[18], (N_B_LAYERS, N_Q_HEADS), f32),
        "w_attn_out": w(ks[19], (N_B_LAYERS, qdim, D_MODEL), qdim),
        "final_gain": gain(ks[20], (D_MODEL,)),
    }


def _fwd_reference(x, c, norm_gain, w_ada, b_ada, w_ffn_in, w_ffn_out, w_hgrn_in, hgrn_lb_logits,
              hgrn_head_gain, w_hgrn_out, kv_gain, w_ada_kv, b_ada_kv, w_kv, b_kv, w_q, b_q,
              attn_sinks, w_attn_out, final_gain):
    B, S, D = x.shape
    sin, cos = rope_tables(S)
    lb_all = jnp.cumsum(jax.nn.softmax(hgrn_lb_logits.astype(jnp.float32), axis=0), axis=0)
    cs = jax.nn.silu(c)
    h = x
    k_sh, v_sh = None, None
    for layer in range(DEPTH):
        mod = (cs @ w_ada[layer] + b_ada[layer]).reshape(B, N_SUBLAYERS, 3, D)

        u = ada_norm(h, norm_gain[layer, 0], mod[:, 0, 0], mod[:, 0, 1])
        h = h + 0.5 * mod[:, 0, 2][:, None, :] * swiglu(u, w_ffn_in[layer, 0], w_ffn_out[layer, 0])

        u = ada_norm(h, norm_gain[layer, 1], mod[:, 1, 0], mod[:, 1, 1])
        if layer < N_A_LAYERS:
            y = hgrn2_mixer(u, w_hgrn_in[layer], lb_all[layer], hgrn_head_gain[layer], w_hgrn_out[layer])
        else:
            bl = layer - N_A_LAYERS
            q = (u @ w_q[bl] + b_q[bl]).reshape(B, S, N_Q_HEADS, HEAD_DIM)
            q = partial_rotary(q, sin, cos)
            y = sliding_window_attention(q, k_sh, v_sh, attn_sinks[bl]) @ w_attn_out[bl]
        h = h + mod[:, 1, 2][:, None, :] * y

        u = ada_norm(h, norm_gain[layer, 2], mod[:, 2, 0], mod[:, 2, 1])
        h = h + 0.5 * mod[:, 2, 2][:, None, :] * swiglu(u, w_ffn_in[layer, 1], w_ffn_out[layer, 1])

        if layer == N_A_LAYERS - 1:
            k_sh, v_sh = shared_kv(h, cs, kv_gain, w_ada_kv, b_ada_kv, w_kv, b_kv, sin, cos)

    return rmsnorm(h, final_gain)


import jax as _jax
import jax.numpy as _jnp

TWIN_FORMAT = 'train_step'
FWD_PARAMS = ['x', 'c', 'norm_gain', 'w_ada', 'b_ada', 'w_ffn_in', 'w_ffn_out', 'w_hgrn_in', 'hgrn_lb_logits', 'hgrn_head_gain', 'w_hgrn_out', 'kv_gain', 'w_ada_kv', 'b_ada_kv', 'w_kv', 'b_kv', 'w_q', 'b_q', 'attn_sinks', 'w_attn_out', 'final_gain']
TWIN_WEIGHTS = ['norm_gain', 'w_ada', 'b_ada', 'w_ffn_in', 'w_ffn_out', 'w_hgrn_in', 'hgrn_lb_logits', 'hgrn_head_gain', 'w_hgrn_out', 'kv_gain', 'w_ada_kv', 'b_ada_kv', 'w_kv', 'b_kv', 'w_q', 'b_q', 'attn_sinks', 'w_attn_out', 'final_gain']
TWIN_DIFF_INPUT = 'x'
TWIN_INPUTS = ['x', 'c', 'norm_gain', 'w_ada', 'b_ada', 'w_ffn_in', 'w_ffn_out', 'w_hgrn_in', 'hgrn_lb_logits', 'hgrn_head_gain', 'w_hgrn_out', 'kv_gain', 'w_ada_kv', 'b_ada_kv', 'w_kv', 'b_kv', 'w_q', 'b_q', 'attn_sinks', 'w_attn_out', 'final_gain', 'loss_target', 'm_norm_gain', 'm_w_ada', 'm_b_ada', 'm_w_ffn_in', 'm_w_ffn_out', 'm_w_hgrn_in', 'm_hgrn_lb_logits', 'm_hgrn_head_gain', 'm_w_hgrn_out', 'm_kv_gain', 'm_w_ada_kv', 'm_b_ada_kv', 'm_w_kv', 'm_b_kv', 'm_w_q', 'm_b_q', 'm_attn_sinks', 'm_w_attn_out', 'm_final_gain', 'v_norm_gain', 'v_w_ada', 'v_b_ada', 'v_w_ffn_in', 'v_w_ffn_out', 'v_w_hgrn_in', 'v_hgrn_lb_logits', 'v_hgrn_head_gain', 'v_w_hgrn_out', 'v_kv_gain', 'v_w_ada_kv', 'v_b_ada_kv', 'v_w_kv', 'v_b_kv', 'v_w_q', 'v_b_q', 'v_attn_sinks', 'v_w_attn_out', 'v_final_gain']
TWIN_OUTPUTS = ['loss', 'grad_x', 'grad_norm_gain', 'grad_w_ada', 'grad_b_ada', 'grad_w_ffn_in', 'grad_w_ffn_out', 'grad_w_hgrn_in', 'grad_hgrn_lb_logits', 'grad_hgrn_head_gain', 'grad_w_hgrn_out', 'grad_kv_gain', 'grad_w_ada_kv', 'grad_b_ada_kv', 'grad_w_kv', 'grad_b_kv', 'grad_w_q', 'grad_b_q', 'grad_attn_sinks', 'grad_w_attn_out', 'grad_final_gain', 'delta_norm_gain', 'delta_w_ada', 'delta_b_ada', 'delta_w_ffn_in', 'delta_w_ffn_out', 'delta_w_hgrn_in', 'delta_hgrn_lb_logits', 'delta_hgrn_head_gain', 'delta_w_hgrn_out', 'delta_kv_gain', 'delta_w_ada_kv', 'delta_b_ada_kv', 'delta_w_kv', 'delta_b_kv', 'delta_w_q', 'delta_b_q', 'delta_attn_sinks', 'delta_w_attn_out', 'delta_final_gain', 'new_m_norm_gain', 'new_m_w_ada', 'new_m_b_ada', 'new_m_w_ffn_in', 'new_m_w_ffn_out', 'new_m_w_hgrn_in', 'new_m_hgrn_lb_logits', 'new_m_hgrn_head_gain', 'new_m_w_hgrn_out', 'new_m_kv_gain', 'new_m_w_ada_kv', 'new_m_b_ada_kv', 'new_m_w_kv', 'new_m_b_kv', 'new_m_w_q', 'new_m_b_q', 'new_m_attn_sinks', 'new_m_w_attn_out', 'new_m_final_gain', 'new_v_norm_gain', 'new_v_w_ada', 'new_v_b_ada', 'new_v_w_ffn_in', 'new_v_w_ffn_out', 'new_v_w_hgrn_in', 'new_v_hgrn_lb_logits', 'new_v_hgrn_head_gain', 'new_v_w_hgrn_out', 'new_v_kv_gain', 'new_v_w_ada_kv', 'new_v_b_ada_kv', 'new_v_w_kv', 'new_v_b_kv', 'new_v_w_q', 'new_v_b_q', 'new_v_attn_sinks', 'new_v_w_attn_out', 'new_v_final_gain']
TWIN_LEAF_KINDS = {'loss': 'loss', 'grad_x': 'grad_x', 'grad_norm_gain': 'grad_w', 'grad_w_ada': 'grad_w', 'grad_b_ada': 'grad_w', 'grad_w_ffn_in': 'grad_w', 'grad_w_ffn_out': 'grad_w', 'grad_w_hgrn_in': 'grad_w', 'grad_hgrn_lb_logits': 'grad_w', 'grad_hgrn_head_gain': 'grad_w', 'grad_w_hgrn_out': 'grad_w', 'grad_kv_gain': 'grad_w', 'grad_w_ada_kv': 'grad_w', 'grad_b_ada_kv': 'grad_w', 'grad_w_kv': 'grad_w', 'grad_b_kv': 'grad_w', 'grad_w_q': 'grad_w', 'grad_b_q': 'grad_w', 'grad_attn_sinks': 'grad_w', 'grad_w_attn_out': 'grad_w', 'grad_final_gain': 'grad_w', 'delta_norm_gain': 'delta_w', 'delta_w_ada': 'delta_w', 'delta_b_ada': 'delta_w', 'delta_w_ffn_in': 'delta_w', 'delta_w_ffn_out': 'delta_w', 'delta_w_hgrn_in': 'delta_w', 'delta_hgrn_lb_logits': 'delta_w', 'delta_hgrn_head_gain': 'delta_w', 'delta_w_hgrn_out': 'delta_w', 'delta_kv_gain': 'delta_w', 'delta_w_ada_kv': 'delta_w', 'delta_b_ada_kv': 'delta_w', 'delta_w_kv': 'delta_w', 'delta_b_kv': 'delta_w', 'delta_w_q': 'delta_w', 'delta_b_q': 'delta_w', 'delta_attn_sinks': 'delta_w', 'delta_w_attn_out': 'delta_w', 'delta_final_gain': 'delta_w', 'new_m_norm_gain': 'new_m', 'new_m_w_ada': 'new_m', 'new_m_b_ada': 'new_m', 'new_m_w_ffn_in': 'new_m', 'new_m_w_ffn_out': 'new_m', 'new_m_w_hgrn_in': 'new_m', 'new_m_hgrn_lb_logits': 'new_m', 'new_m_hgrn_head_gain': 'new_m', 'new_m_w_hgrn_out': 'new_m', 'new_m_kv_gain': 'new_m', 'new_m_w_ada_kv': 'new_m', 'new_m_b_ada_kv': 'new_m', 'new_m_w_kv': 'new_m', 'new_m_b_kv': 'new_m', 'new_m_w_q': 'new_m', 'new_m_b_q': 'new_m', 'new_m_attn_sinks': 'new_m', 'new_m_w_attn_out': 'new_m', 'new_m_final_gain': 'new_m', 'new_v_norm_gain': 'new_v', 'new_v_w_ada': 'new_v', 'new_v_b_ada': 'new_v', 'new_v_w_ffn_in': 'new_v', 'new_v_w_ffn_out': 'new_v', 'new_v_w_hgrn_in': 'new_v', 'new_v_hgrn_lb_logits': 'new_v', 'new_v_hgrn_head_gain': 'new_v', 'new_v_w_hgrn_out': 'new_v', 'new_v_kv_gain': 'new_v', 'new_v_w_ada_kv': 'new_v', 'new_v_b_ada_kv': 'new_v', 'new_v_w_kv': 'new_v', 'new_v_b_kv': 'new_v', 'new_v_w_q': 'new_v', 'new_v_b_q': 'new_v', 'new_v_attn_sinks': 'new_v', 'new_v_w_attn_out': 'new_v', 'new_v_final_gain': 'new_v'}


def _forward(args):
    return _fwd_reference(*[args[k] for k in FWD_PARAMS])


def _output_shape():
    def fwd():
        inp = _fwd_setup_inputs(0)
        return _fwd_reference(*[inp[k] for k in FWD_PARAMS])
    out = _jax.eval_shape(fwd)
    return out.shape, out.dtype

N_MICROBATCH = 1
ADAM_LR = 0.001
ADAM_B1 = 0.9
ADAM_B2 = 0.999
ADAM_EPS = 1e-08
ADAM_WD = 0.01
ADAM_STEP = 10
PER_EXAMPLE_BATCH_AXIS = {'x': 0, 'c': 0, 'loss_target': 0}
SHARED_INPUTS = []
_WEIGHT_DTYPES = {'norm_gain': _jnp.float32, 'w_ada': _jnp.float32, 'b_ada': _jnp.float32, 'w_ffn_in': _jnp.float32, 'w_ffn_out': _jnp.float32, 'w_hgrn_in': _jnp.float32, 'hgrn_lb_logits': _jnp.float32, 'hgrn_head_gain': _jnp.float32, 'w_hgrn_out': _jnp.float32, 'kv_gain': _jnp.float32, 'w_ada_kv': _jnp.float32, 'b_ada_kv': _jnp.float32, 'w_kv': _jnp.float32, 'b_kv': _jnp.float32, 'w_q': _jnp.float32, 'b_q': _jnp.float32, 'attn_sinks': _jnp.float32, 'w_attn_out': _jnp.float32, 'final_gain': _jnp.float32}
MOMENT_SCALE = {'norm_gain': 1.203011e-02, 'w_ada': 1.393479e-02, 'b_ada': 2.357396e-02, 'w_ffn_in': 5.760247e-03, 'w_ffn_out': 9.290127e-03, 'w_hgrn_in': 7.997972e-03, 'hgrn_lb_logits': 1.067057e-03, 'hgrn_head_gain': 1.501415e-02, 'w_hgrn_out': 1.461515e-02, 'kv_gain': 8.871202e-03, 'w_ada_kv': 1.141103e-02, 'b_ada_kv': 1.955145e-02, 'w_kv': 2.299669e-02, 'b_kv': 5.407887e-02, 'w_q': 5.788118e-03, 'b_q': 5.385866e-03, 'attn_sinks': 5.480713e-03, 'w_attn_out': 9.935439e-03, 'final_gain': 1.600744e+01}


def _to_microbatches(a, axis):
    t = _jnp.moveaxis(a, axis, 0)
    t = t.reshape((N_MICROBATCH, t.shape[0] // N_MICROBATCH) + t.shape[1:])
    return _jnp.moveaxis(t, 1, axis + 1)


def setup_inputs(seed: int = 0) -> dict:
    inp = _fwd_setup_inputs(seed)
    key = _jax.random.fold_in(_jax.random.key(seed), 7919)
    shape, _ = _output_shape()
    out = dict(inp)
    out["loss_target"] = _jax.random.normal(_jax.random.fold_in(key, 0), shape, _jnp.float32)
    for i, name in enumerate(TWIN_WEIGHTS):
        w = inp[name].astype(_jnp.float32)
        if MOMENT_SCALE is None:
            s = _jnp.sqrt(_jnp.mean(_jnp.square(w)) + 1e-30)
        else:
            s = MOMENT_SCALE[name]
        km, kv = _jax.random.split(_jax.random.fold_in(key, i + 1))
        out[name] = w
        out["m_" + name] = s * _jax.random.normal(km, w.shape, _jnp.float32)
        out["v_" + name] = (s * s) * _jax.random.uniform(kv, w.shape, _jnp.float32, 0.5, 1.5)
    if N_MICROBATCH > 1:
        for name, axis in PER_EXAMPLE_BATCH_AXIS.items():
            out[name] = _to_microbatches(out[name], axis)
    return {'x': out['x'], 'c': out['c'], 'norm_gain': out['norm_gain'], 'w_ada': out['w_ada'], 'b_ada': out['b_ada'], 'w_ffn_in': out['w_ffn_in'], 'w_ffn_out': out['w_ffn_out'], 'w_hgrn_in': out['w_hgrn_in'], 'hgrn_lb_logits': out['hgrn_lb_logits'], 'hgrn_head_gain': out['hgrn_head_gain'], 'w_hgrn_out': out['w_hgrn_out'], 'kv_gain': out['kv_gain'], 'w_ada_kv': out['w_ada_kv'], 'b_ada_kv': out['b_ada_kv'], 'w_kv': out['w_kv'], 'b_kv': out['b_kv'], 'w_q': out['w_q'], 'b_q': out['b_q'], 'attn_sinks': out['attn_sinks'], 'w_attn_out': out['w_attn_out'], 'final_gain': out['final_gain'], 'loss_target': out['loss_target'], 'm_norm_gain': out['m_norm_gain'], 'm_w_ada': out['m_w_ada'], 'm_b_ada': out['m_b_ada'], 'm_w_ffn_in': out['m_w_ffn_in'], 'm_w_ffn_out': out['m_w_ffn_out'], 'm_w_hgrn_in': out['m_w_hgrn_in'], 'm_hgrn_lb_logits': out['m_hgrn_lb_logits'], 'm_hgrn_head_gain': out['m_hgrn_head_gain'], 'm_w_hgrn_out': out['m_w_hgrn_out'], 'm_kv_gain': out['m_kv_gain'], 'm_w_ada_kv': out['m_w_ada_kv'], 'm_b_ada_kv': out['m_b_ada_kv'], 'm_w_kv': out['m_w_kv'], 'm_b_kv': out['m_b_kv'], 'm_w_q': out['m_w_q'], 'm_b_q': out['m_b_q'], 'm_attn_sinks': out['m_attn_sinks'], 'm_w_attn_out': out['m_w_attn_out'], 'm_final_gain': out['m_final_gain'], 'v_norm_gain': out['v_norm_gain'], 'v_w_ada': out['v_w_ada'], 'v_b_ada': out['v_b_ada'], 'v_w_ffn_in': out['v_w_ffn_in'], 'v_w_ffn_out': out['v_w_ffn_out'], 'v_w_hgrn_in': out['v_w_hgrn_in'], 'v_hgrn_lb_logits': out['v_hgrn_lb_logits'], 'v_hgrn_head_gain': out['v_hgrn_head_gain'], 'v_w_hgrn_out': out['v_w_hgrn_out'], 'v_kv_gain': out['v_kv_gain'], 'v_w_ada_kv': out['v_w_ada_kv'], 'v_b_ada_kv': out['v_b_ada_kv'], 'v_w_kv': out['v_w_kv'], 'v_b_kv': out['v_b_kv'], 'v_w_q': out['v_w_q'], 'v_b_q': out['v_b_q'], 'v_attn_sinks': out['v_attn_sinks'], 'v_w_attn_out': out['v_w_attn_out'], 'v_final_gain': out['v_final_gain']}


def _loss(weights, diff, rest, loss_target):
    with _jax.named_scope("forward"):
        args = {**rest, TWIN_DIFF_INPUT: diff, **{k: w.astype(_WEIGHT_DTYPES[k]) for k, w in weights.items()}}
        y = _forward(args)
    with _jax.named_scope("loss_head"):
        err = _jnp.square(y.astype(_jnp.float32) - loss_target)
        return 0.5 * _jnp.sum(_jnp.mean(err, axis=-1)) if err.ndim else 0.5 * err


def _adamw(w, g, m, v):
    m = ADAM_B1 * m + (1.0 - ADAM_B1) * g
    v = ADAM_B2 * v + (1.0 - ADAM_B2) * _jnp.square(g)
    m_hat = m / (1.0 - ADAM_B1 ** ADAM_STEP)
    v_hat = v / (1.0 - ADAM_B2 ** ADAM_STEP)
    delta = -ADAM_LR * (m_hat / (_jnp.sqrt(v_hat) + ADAM_EPS) + ADAM_WD * w)
    return delta, m, v


def reference(x, c, norm_gain, w_ada, b_ada, w_ffn_in, w_ffn_out, w_hgrn_in, hgrn_lb_logits, hgrn_head_gain, w_hgrn_out, kv_gain, w_ada_kv, b_ada_kv, w_kv, b_kv, w_q, b_q, attn_sinks, w_attn_out, final_gain, loss_target, m_norm_gain, m_w_ada, m_b_ada, m_w_ffn_in, m_w_ffn_out, m_w_hgrn_in, m_hgrn_lb_logits, m_hgrn_head_gain, m_w_hgrn_out, m_kv_gain, m_w_ada_kv, m_b_ada_kv, m_w_kv, m_b_kv, m_w_q, m_b_q, m_attn_sinks, m_w_attn_out, m_final_gain, v_norm_gain, v_w_ada, v_b_ada, v_w_ffn_in, v_w_ffn_out, v_w_hgrn_in, v_hgrn_lb_logits, v_hgrn_head_gain, v_w_hgrn_out, v_kv_gain, v_w_ada_kv, v_b_ada_kv, v_w_kv, v_b_kv, v_w_q, v_b_q, v_attn_sinks, v_w_attn_out, v_final_gain):
    given = dict(x=x, c=c, norm_gain=norm_gain, w_ada=w_ada, b_ada=b_ada, w_ffn_in=w_ffn_in, w_ffn_out=w_ffn_out, w_hgrn_in=w_hgrn_in, hgrn_lb_logits=hgrn_lb_logits, hgrn_head_gain=hgrn_head_gain, w_hgrn_out=w_hgrn_out, kv_gain=kv_gain, w_ada_kv=w_ada_kv, b_ada_kv=b_ada_kv, w_kv=w_kv, b_kv=b_kv, w_q=w_q, b_q=b_q, attn_sinks=attn_sinks, w_attn_out=w_attn_out, final_gain=final_gain, loss_target=loss_target, m_norm_gain=m_norm_gain, m_w_ada=m_w_ada, m_b_ada=m_b_ada, m_w_ffn_in=m_w_ffn_in, m_w_ffn_out=m_w_ffn_out, m_w_hgrn_in=m_w_hgrn_in, m_hgrn_lb_logits=m_hgrn_lb_logits, m_hgrn_head_gain=m_hgrn_head_gain, m_w_hgrn_out=m_w_hgrn_out, m_kv_gain=m_kv_gain, m_w_ada_kv=m_w_ada_kv, m_b_ada_kv=m_b_ada_kv, m_w_kv=m_w_kv, m_b_kv=m_b_kv, m_w_q=m_w_q, m_b_q=m_b_q, m_attn_sinks=m_attn_sinks, m_w_attn_out=m_w_attn_out, m_final_gain=m_final_gain, v_norm_gain=v_norm_gain, v_w_ada=v_w_ada, v_b_ada=v_b_ada, v_w_ffn_in=v_w_ffn_in, v_w_ffn_out=v_w_ffn_out, v_w_hgrn_in=v_w_hgrn_in, v_hgrn_lb_logits=v_hgrn_lb_logits, v_hgrn_head_gain=v_hgrn_head_gain, v_w_hgrn_out=v_w_hgrn_out, v_kv_gain=v_kv_gain, v_w_ada_kv=v_w_ada_kv, v_b_ada_kv=v_b_ada_kv, v_w_kv=v_w_kv, v_b_kv=v_b_kv, v_w_q=v_w_q, v_b_q=v_b_q, v_attn_sinks=v_attn_sinks, v_w_attn_out=v_w_attn_out, v_final_gain=v_final_gain)
    weights = {n: given[n] for n in TWIN_WEIGHTS}
    shared = {n: given[n] for n in SHARED_INPUTS}
    per_example = {n: given[n] for n in ['x', 'c']}
    grad_fn = _jax.value_and_grad(_loss, argnums=(0, 1))

    def one_microbatch(ex, loss_target):
        ex = dict(ex)
        diff = ex.pop(TWIN_DIFF_INPUT)
        return grad_fn(weights, diff, {**shared, **ex}, loss_target)

    if N_MICROBATCH == 1:
        loss, (grad_w, grad_x) = one_microbatch(per_example, given["loss_target"])
    else:
        def body(carry, xs):
            loss_sum, grad_sum = carry
            l_k, (gw_k, gx_k) = one_microbatch(xs[0], xs[1])
            with _jax.named_scope("update"):
                return (loss_sum + l_k, _jax.tree.map(_jnp.add, grad_sum, gw_k)), gx_k

        init = (_jnp.zeros((), _jnp.float32), _jax.tree.map(_jnp.zeros_like, weights))
        (loss, grad_w), grad_x = _jax.lax.scan(body, init, (per_example, given["loss_target"]))
    with _jax.named_scope("update"):
        delta_w, new_m, new_v = {}, {}, {}
        for n in TWIN_WEIGHTS:
            delta_w[n], new_m[n], new_v[n] = _adamw(weights[n], grad_w[n], given["m_" + n], given["v_" + n])
    return (loss, grad_x, *[grad_w[n] for n in TWIN_WEIGHTS], *[delta_w[n] for n in TWIN_WEIGHTS],
            *[new_m[n] for n in TWIN_WEIGHTS], *[new_v[n] for n in TWIN_WEIGHTS])
```

```python
import functools

import jax
import jax.numpy as jnp
from jax import lax
from jax.experimental import pallas as pl
from jax.experimental.pallas import tpu as pltpu

F32 = jnp.float32
BF16 = jnp.bfloat16
MESH = pl.DeviceIdType.MESH

N_DEV = 8
V7X_VMEM_LIMIT_BYTES = 56 * 1024 * 1024
LANES = 128

NORM_EPS = 1e-6
NEG_INF = -1e30
HGRN_CHUNK = 32
HGRN_HEAD = 128
ATT_HEAD = 64
ATT_WINDOW = 128
ATT_GROUP = 8
ROT_DIM = 16
ROPE_THETA = 500000.0

ADAM_LR = 0.001
ADAM_B1 = 0.9
ADAM_B2 = 0.999
ADAM_EPS = 1e-08
ADAM_WD = 0.01
ADAM_STEP = 10


def _params(*sem):
    return pltpu.CompilerParams(dimension_semantics=sem, vmem_limit_bytes=V7X_VMEM_LIMIT_BYTES)


def _tile(n, pref, unit=LANES):
    t = (min(n, pref) // unit) * unit
    while t >= unit:
        if n % t == 0:
            return t
        t -= unit
    return n


def _accumulate(prod, o_ref, acc_ref, k, nk):
    if nk == 1:
        o_ref[...] = prod.astype(o_ref.dtype)
        return

    @pl.when(k == 0)
    def _():
        acc_ref[...] = prod

    @pl.when(k > 0)
    def _():
        acc_ref[...] += prod

    @pl.when(k == nk - 1)
    def _():
        o_ref[...] = acc_ref[...].astype(o_ref.dtype)


def mm_nn(a3, b4, out_dtype, name, natural=False, a_natural=False, bias=None):
    JK, JN, kb, nb = b4.shape
    M = a3.shape[0] if a_natural else a3.shape[1]
    tm = min(M, 512)
    tn = _tile(nb, 1024)
    ntn = nb // tn
    nk = JK

    def body(*refs):
        if bias is None:
            a_ref, b_ref, o_ref, acc_ref = refs
        else:
            a_ref, b_ref, bias_ref, o_ref, acc_ref = refs
        prod = jnp.dot(a_ref[...].astype(BF16), b_ref[...].astype(BF16), preferred_element_type=F32)
        if bias is not None:
            prod = prod + bias_ref[...]
        _accumulate(prod, o_ref, acc_ref, pl.program_id(2), nk)

    if a_natural:
        a_spec = pl.BlockSpec((tm, kb), lambda j, i, k: (i, k))
    else:
        a_spec = pl.BlockSpec((None, tm, kb), lambda j, i, k: (k, i, 0))
    b_spec = pl.BlockSpec((None, None, kb, tn), lambda j, i, k: (k, j // ntn, 0, j % ntn))
    in_specs = [a_spec, b_spec]
    args = [a3, b4]
    if bias is not None:
        assert natural and nk == 1
        in_specs.append(pl.BlockSpec((1, tn), lambda j, i, k: (0, j)))
        args.append(bias)
    if natural:
        out_shape = jax.ShapeDtypeStruct((M, JN * nb), out_dtype)
        o_spec = pl.BlockSpec((tm, tn), lambda j, i, k: (i, j))
    else:
        out_shape = jax.ShapeDtypeStruct((JN, M, nb), out_dtype)
        o_spec = pl.BlockSpec((None, tm, tn), lambda j, i, k: (j // ntn, i, j % ntn))
    return pl.pallas_call(
        body, name=name, grid=(JN * ntn, M // tm, nk), in_specs=in_specs, out_specs=o_spec, out_shape=out_shape,
        scratch_shapes=[pltpu.VMEM((tm, tn), F32)],
        compiler_params=_params("parallel", "parallel", "arbitrary"),
    )(*args)


def mm_nt(a3, b4, out_dtype, name, natural=False, a_natural=False):
    JK, JN, nb, kb = b4.shape
    M = a3.shape[0] if a_natural else a3.shape[1]
    tm = min(M, 512)
    tn = _tile(nb, 1024)
    ntn = nb // tn
    nk = JK

    def body(a_ref, b_ref, o_ref, acc_ref):
        prod = lax.dot_general(a_ref[...].astype(BF16), b_ref[...].astype(BF16), (((1,), (1,)), ((), ())),
                               preferred_element_type=F32)
        _accumulate(prod, o_ref, acc_ref, pl.program_id(2), nk)

    if a_natural:
        a_spec = pl.BlockSpec((tm, kb), lambda j, i, k: (i, k))
    else:
        a_spec = pl.BlockSpec((None, tm, kb), lambda j, i, k: (k, i, 0))
    b_spec = pl.BlockSpec((None, None, tn, kb), lambda j, i, k: (k, j // ntn, j % ntn, 0))
    if natural:
        out_shape = jax.ShapeDtypeStruct((M, JN * nb), out_dtype)
        o_spec = pl.BlockSpec((tm, tn), lambda j, i, k: (i, j))
    else:
        out_shape = jax.ShapeDtypeStruct((JN, M, nb), out_dtype)
        o_spec = pl.BlockSpec((None, tm, tn), lambda j, i, k: (j // ntn, i, j % ntn))
    return pl.pallas_call(
        body, name=name, grid=(JN * ntn, M // tm, nk), in_specs=[a_spec, b_spec], out_specs=o_spec, out_shape=out_shape,
        scratch_shapes=[pltpu.VMEM((tm, tn), F32)],
        compiler_params=_params("parallel", "parallel", "arbitrary"),
    )(a3, b4)


def mm_tn(a3, b3, name, a_natural=False, b_natural=False, ja=None, jb=None):
    if a_natural:
        M, JA = a3.shape[0], ja
        ka = a3.shape[1] // JA
    else:
        JA, M, ka = a3.shape
    if b_natural:
        JB = jb
        nb = b3.shape[1] // JB
    else:
        JB, _, nb = b3.shape
    tr = min(M, 512)
    tka = _tile(ka, 1024)
    tnb = _tile(nb, 1024)
    nta, ntb = ka // tka, nb // tnb
    nk = M // tr

    def body(a_ref, b_ref, o_ref, acc_ref):
        prod = lax.dot_general(a_ref[...].astype(BF16), b_ref[...].astype(BF16), (((0,), (0,)), ((), ())),
                               preferred_element_type=F32)
        _accumulate(prod, o_ref, acc_ref, pl.program_id(2), nk)

    if a_natural:
        a_spec = pl.BlockSpec((tr, tka), lambda i, j, k: (k, i))
    else:
        a_spec = pl.BlockSpec((None, tr, tka), lambda i, j, k: (i // nta, k, i % nta))
    if b_natural:
        b_spec = pl.BlockSpec((tr, tnb), lambda i, j, k: (k, j))
    else:
        b_spec = pl.BlockSpec((None, tr, tnb), lambda i, j, k: (j // ntb, k, j % ntb))
    o_spec = pl.BlockSpec((None, None, tka, tnb), lambda i, j, k: (i // nta, j // ntb, i % nta, j % ntb))
    return pl.pallas_call(
        body, name=name, grid=(JA * nta, JB * ntb, nk), in_specs=[a_spec, b_spec], out_specs=o_spec,
        out_shape=jax.ShapeDtypeStruct((JA, JB, ka, nb), F32),
        scratch_shapes=[pltpu.VMEM((tka, tnb), F32)],
        compiler_params=_params("parallel", "parallel", "arbitrary"),
    )(a3, b3)


def _row_tile(S):
    return min(S, 256)


def _adaln(h, gain, shift, scale):
    y = h * lax.rsqrt(jnp.mean(h * h, axis=-1, keepdims=True) + NORM_EPS) * gain
    return y * (1.0 + scale) + shift


def adaln_fwd(h, gain, shift, scale, name):
    S, D = h.shape
    tm = _row_tile(S)

    def body(h_ref, g_ref, sh_ref, sc_ref, u_ref):
        u_ref[...] = _adaln(h_ref[...], g_ref[...], sh_ref[...], sc_ref[...]).astype(BF16)

    row = pl.BlockSpec((tm, D), lambda i: (i, 0))
    vec = pl.BlockSpec((1, D), lambda i: (0, 0))
    return pl.pallas_call(
        body, name=name, grid=(S // tm,), in_specs=[row, vec, vec, vec], out_specs=row,
        out_shape=jax.ShapeDtypeStruct((S, D), BF16), compiler_params=_params("parallel"),
    )(h, gain, shift, scale)


def adaln_bwd(h, gain, shift, scale, du, dres, name):
    S, D = h.shape
    tm = _row_tile(S)

    def body(h_ref, g_ref, sh_ref, sc_ref, du_ref, dres_ref, dh_ref, dg_ref, dsh_ref, dsc_ref):
        _, vjp = jax.vjp(_adaln, h_ref[...], g_ref[...], sh_ref[...], sc_ref[...])
        dh, dg, dsh, dsc = vjp(du_ref[...].astype(F32))
        dh_ref[...] = dres_ref[...] + dh

        @pl.when(pl.program_id(0) == 0)
        def _():
            dg_ref[...] = jnp.zeros_like(dg_ref)
            dsh_ref[...] = jnp.zeros_like(dsh_ref)
            dsc_ref[...] = jnp.zeros_like(dsc_ref)

        dg_ref[...] += dg
        dsh_ref[...] += dsh
        dsc_ref[...] += dsc

    row = pl.BlockSpec((tm, D), lambda i: (i, 0))
    vec = pl.BlockSpec((1, D), lambda i: (0, 0))
    vs = jax.ShapeDtypeStruct((1, D), F32)
    return pl.pallas_call(
        body, name=name, grid=(S // tm,), in_specs=[row, vec, vec, vec, row, row], out_specs=[row, vec, vec, vec],
        out_shape=[jax.ShapeDtypeStruct((S, D), F32), vs, vs, vs], compiler_params=_params("arbitrary"),
    )(h, gain, shift, scale, du, dres)


def resid_fwd(h, y, gate, coef, name):
    S, D = h.shape
    tm = _row_tile(S)

    def body(h_ref, y_ref, g_ref, o_ref):
        o_ref[...] = h_ref[...] + (coef * g_ref[...]) * y_ref[...]

    row = pl.BlockSpec((tm, D), lambda i: (i, 0))
    vec = pl.BlockSpec((1, D), lambda i: (0, 0))
    return pl.pallas_call(
        body, name=name, grid=(S // tm,), in_specs=[row, row, vec], out_specs=row,
        out_shape=jax.ShapeDtypeStruct((S, D), F32), compiler_params=_params("parallel"),
    )(h, y, gate)


def gate_bwd(dh, y, gate, coef, name):
    S, D = dh.shape
    tm = _row_tile(S)

    def body(dh_ref, y_ref, g_ref, dy_ref, dg_ref):
        dh = dh_ref[...]
        dy_ref[...] = ((coef * g_ref[...]) * dh).astype(BF16)

        @pl.when(pl.program_id(0) == 0)
        def _():
            dg_ref[...] = jnp.zeros_like(dg_ref)

        dg_ref[...] += coef * jnp.sum(dh * y_ref[...], axis=0, keepdims=True)

    row = pl.BlockSpec((tm, D), lambda i: (i, 0))
    vec = pl.BlockSpec((1, D), lambda i: (0, 0))
    return pl.pallas_call(
        body, name=name, grid=(S // tm,), in_specs=[row, row, vec], out_specs=[row, vec],
        out_shape=[jax.ShapeDtypeStruct((S, D), BF16), jax.ShapeDtypeStruct((1, D), F32)],
        compiler_params=_params("arbitrary"),
    )(dh, y, gate)


def _swiglu(a, b):
    return a * jax.nn.sigmoid(a) * b


def swiglu_fwd(ab3, name):
    J2, S, nb = ab3.shape
    J = J2 // 2
    tm = _row_tile(S)

    def body(a_ref, b_ref, o_ref):
        o_ref[...] = _swiglu(a_ref[...], b_ref[...]).astype(BF16)

    return pl.pallas_call(
        body, name=name, grid=(J, S // tm),
        in_specs=[pl.BlockSpec((None, tm, nb), lambda j, i: (j, i, 0)),
                  pl.BlockSpec((None, tm, nb), lambda j, i: (j + J, i, 0))],
        out_specs=pl.BlockSpec((None, tm, nb), lambda j, i: (j, i, 0)),
        out_shape=jax.ShapeDtypeStruct((J, S, nb), BF16), compiler_params=_params("parallel", "parallel"),
    )(ab3, ab3)


def swiglu_bwd(ab3, dh3, name):
    J2, S, nb = ab3.shape
    J = J2 // 2
    tm = _row_tile(S)

    def body(a_ref, b_ref, dh_ref, o_ref):
        _, vjp = jax.vjp(_swiglu, a_ref[...], b_ref[...])
        da, db = vjp(dh_ref[...].astype(F32))
        half = pl.program_id(0) // J

        @pl.when(half == 0)
        def _():
            o_ref[...] = da.astype(BF16)

        @pl.when(half == 1)
        def _():
            o_ref[...] = db.astype(BF16)

    return pl.pallas_call(
        body, name=name, grid=(J2, S // tm),
        in_specs=[pl.BlockSpec((None, tm, nb), lambda j, i: (j % J, i, 0)),
                  pl.BlockSpec((None, tm, nb), lambda j, i: (j % J + J, i, 0)),
                  pl.BlockSpec((None, tm, nb), lambda j, i: (j % J, i, 0))],
        out_specs=pl.BlockSpec((None, tm, nb), lambda j, i: (j, i, 0)),
        out_shape=jax.ShapeDtypeStruct((J2, S, nb), BF16), compiler_params=_params("parallel", "parallel"),
    )(ab3, ab3, dh3)


def colsum(x, name):
    S, N = x.shape
    tm = _row_tile(S)

    def body(x_ref, o_ref):
        @pl.when(pl.program_id(0) == 0)
        def _():
            o_ref[...] = jnp.zeros_like(o_ref)

        o_ref[...] += jnp.sum(x_ref[...].astype(F32), axis=0, keepdims=True)

    return pl.pallas_call(
        body, name=name, grid=(S // tm,), in_specs=[pl.BlockSpec((tm, N), lambda i: (i, 0))],
        out_specs=pl.BlockSpec((1, N), lambda i: (0, 0)), out_shape=jax.ShapeDtypeStruct((1, N), F32),
        compiler_params=_params("arbitrary"),
    )(x)


def _final_loss(h, gain, target):
    y = h * lax.rsqrt(jnp.mean(h * h, axis=-1, keepdims=True) + NORM_EPS) * gain
    err = y - target
    return 0.5 * jnp.sum(jnp.mean(err * err, axis=-1))


def final_loss_grad(h, gain, target, name):
    S, D = h.shape
    tm = _row_tile(S)

    def body(h_ref, g_ref, t_ref, loss_ref, dh_ref, dg_ref):
        loss, (dh, dg) = jax.value_and_grad(_final_loss, argnums=(0, 1))(h_ref[...], g_ref[...], t_ref[...])
        dh_ref[...] = dh

        @pl.when(pl.program_id(0) == 0)
        def _():
            loss_ref[...] = jnp.zeros_like(loss_ref)
            dg_ref[...] = jnp.zeros_like(dg_ref)

        loss_ref[...] += jnp.full(loss_ref.shape, loss, F32)
        dg_ref[...] += dg

    row = pl.BlockSpec((tm, D), lambda i: (i, 0))
    vec = pl.BlockSpec((1, D), lambda i: (0, 0))
    return pl.pallas_call(
        body, name=name, grid=(S // tm,), in_specs=[row, vec, row],
        out_specs=[pl.BlockSpec((1, LANES), lambda i: (0, 0)), row, vec],
        out_shape=[jax.ShapeDtypeStruct((1, LANES), F32), jax.ShapeDtypeStruct((S, D), F32),
                   jax.ShapeDtypeStruct((1, D), F32)],
        compiler_params=_params("arbitrary"),
    )(h, gain, target)


def _chunk_consts():
    C = HGRN_CHUNK
    t = lax.broadcasted_iota(jnp.int32, (C, C), 0)
    s = lax.broadcasted_iota(jnp.int32, (C, C), 1)
    lower = (s <= t).astype(F32)
    to_mid = (s < C // 2).astype(F32)
    to_end = jnp.ones((C, C), F32)
    return lower, to_mid, to_end, s <= t


def _dot32(a, b):
    return jnp.dot(a, b, precision=lax.Precision.HIGHEST, preferred_element_type=F32)


def _hgrn_chunk(q_raw, f_raw, i_raw, lb, st):
    lower, to_mid, to_end, causal = _chunk_consts()
    forget = lb + (1.0 - lb) * jax.nn.sigmoid(f_raw)
    g = jnp.log(forget)
    kk = 1.0 - forget
    qs = q_raw * jax.nn.sigmoid(q_raw)
    b = _dot32(lower, g)
    bm = _dot32(to_mid, g)
    bl = _dot32(to_end, g)
    nt = (((1,), (1,)), ((), ()))
    tn = (((0,), (0,)), ((), ()))
    inter = lax.dot_general((qs * jnp.exp(b)).astype(BF16), st.astype(BF16), nt, preferred_element_type=F32)
    qt = (qs * jnp.exp(b - bm)).astype(BF16)
    kt = (kk * jnp.exp(bm - b)).astype(BF16)
    scores = lax.dot_general(qt, kt, nt, preferred_element_type=F32)
    scores = jnp.where(causal, scores, 0.0)
    vb = i_raw.astype(BF16)
    out = inter + jnp.dot(scores.astype(BF16), vb, preferred_element_type=F32)
    kdec = (kk * jnp.exp(bl - b)).astype(BF16)
    new_st = st * jnp.exp(bl[0:1, :]) + lax.dot_general(vb, kdec, tn, preferred_element_type=F32)
    return out, new_st


def hgrn_scan_fwd(proj, lb, name):
    S, D4 = proj.shape
    D = D4 // 4
    H = D // HGRN_HEAD
    C = HGRN_CHUNK
    R = min(S, 128)
    ncr = R // C

    def body(q_ref, f_ref, i_ref, lb_ref, o_ref, st_ref, state):
        @pl.when(pl.program_id(0) == 0)
        def _():
            state[...] = jnp.zeros_like(state)

        def chunk(cc, carry):
            rows = pl.ds(pl.multiple_of(cc * C, C), C)
            for h in range(H):
                ls = pl.ds(h * HGRN_HEAD, HGRN_HEAD)
                st = state[h]
                st_ref[cc, h] = st
                out, new_st = _hgrn_chunk(q_ref[rows, ls], f_ref[rows, ls], i_ref[rows, ls], lb_ref[:, ls], st)
                o_ref[rows, ls] = out
                state[h] = new_st
            return carry

        lax.fori_loop(0, ncr, chunk, 0)

    col = lambda j: pl.BlockSpec((R, D), lambda i: (i, j))
    return pl.pallas_call(
        body, name=name, grid=(S // R,),
        in_specs=[col(0), col(1), col(2), pl.BlockSpec((1, D), lambda i: (0, 0))],
        out_specs=[pl.BlockSpec((R, D), lambda i: (i, 0)),
                   pl.BlockSpec((ncr, H, HGRN_HEAD, HGRN_HEAD), lambda i: (i, 0, 0, 0))],
        out_shape=[jax.ShapeDtypeStruct((S, D), F32), jax.ShapeDtypeStruct((S // C, H, HGRN_HEAD, HGRN_HEAD), F32)],
        scratch_shapes=[pltpu.VMEM((H, HGRN_HEAD, HGRN_HEAD), F32)],
        compiler_params=_params("arbitrary"),
    )(proj, proj, proj, lb)


def hgrn_scan_bwd(proj, lb, states, do, dg, name):
    S, D4 = proj.shape
    D = D4 // 4
    H = D // HGRN_HEAD
    C = HGRN_CHUNK
    R = min(S, 128)
    ncr = R // C
    ng = S // R

    def body(q_ref, f_ref, i_ref, lb_ref, st_ref, do_ref, dg_ref, dp_ref, dlb_ref, dstate):
        @pl.when(pl.program_id(0) == 0)
        def _():
            dstate[...] = jnp.zeros_like(dstate)
            dlb_ref[...] = jnp.zeros_like(dlb_ref)

        dp_ref[:, pl.ds(3 * D, D)] = dg_ref[...].astype(BF16)

        def chunk(t, carry):
            cc = ncr - 1 - t
            rows = pl.ds(pl.multiple_of(cc * C, C), C)
            for h in range(H):
                ls = pl.ds(h * HGRN_HEAD, HGRN_HEAD)
                _, vjp = jax.vjp(_hgrn_chunk, q_ref[rows, ls], f_ref[rows, ls], i_ref[rows, ls], lb_ref[:, ls],
                                 st_ref[cc, h])
                dq, df, di, dlb, dst = vjp((do_ref[rows, ls], dstate[h]))
                dp_ref[rows, pl.ds(h * HGRN_HEAD, HGRN_HEAD)] = dq.astype(BF16)
                dp_ref[rows, pl.ds(D + h * HGRN_HEAD, HGRN_HEAD)] = df.astype(BF16)
                dp_ref[rows, pl.ds(2 * D + h * HGRN_HEAD, HGRN_HEAD)] = di.astype(BF16)
                dlb_ref[:, ls] += dlb
                dstate[h] = dst
            return carry

        lax.fori_loop(0, ncr, chunk, 0)

    col = lambda j: pl.BlockSpec((R, D), lambda i: (ng - 1 - i, j))
    return pl.pallas_call(
        body, name=name, grid=(ng,),
        in_specs=[col(0), col(1), col(2), pl.BlockSpec((1, D), lambda i: (0, 0)),
                  pl.BlockSpec((ncr, H, HGRN_HEAD, HGRN_HEAD), lambda i: (ng - 1 - i, 0, 0, 0)),
                  pl.BlockSpec((R, D), lambda i: (ng - 1 - i, 0)), pl.BlockSpec((R, D), lambda i: (ng - 1 - i, 0))],
        out_specs=[pl.BlockSpec((R, D4), lambda i: (ng - 1 - i, 0)), pl.BlockSpec((1, D), lambda i: (0, 0))],
        out_shape=[jax.ShapeDtypeStruct((S, D4), BF16), jax.ShapeDtypeStruct((1, D), F32)],
        scratch_shapes=[pltpu.VMEM((H, HGRN_HEAD, HGRN_HEAD), F32)],
        compiler_params=_params("arbitrary"),
    )(proj, proj, proj, lb, states, do, dg)


def _head_out(o, g, gain):
    y = o * lax.rsqrt(jnp.mean(o * o, axis=-1, keepdims=True) + NORM_EPS) * gain
    return y * jax.nn.sigmoid(g)


def hgrn_post_fwd(o, proj, gain, name):
    S, D = o.shape
    H = D // HGRN_HEAD
    tm = _row_tile(S)

    def body(o_ref, g_ref, gain_ref, z_ref):
        for h in range(H):
            ls = pl.ds(h * HGRN_HEAD, HGRN_HEAD)
            z_ref[:, ls] = _head_out(o_ref[:, ls], g_ref[:, ls], gain_ref[:, ls]).astype(BF16)

    row = pl.BlockSpec((tm, D), lambda i: (i, 0))
    return pl.pallas_call(
        body, name=name, grid=(S // tm,),
        in_specs=[row, pl.BlockSpec((tm, D), lambda i: (i, 3)), pl.BlockSpec((1, D), lambda i: (0, 0))],
        out_specs=row, out_shape=jax.ShapeDtypeStruct((S, D), BF16), compiler_params=_params("parallel"),
    )(o, proj, gain)


def hgrn_post_bwd(o, proj, gain, dz, name):
    S, D = o.shape
    H = D // HGRN_HEAD
    tm = _row_tile(S)

    def body(o_ref, g_ref, gain_ref, dz_ref, do_ref, dg_ref, dgain_ref):
        @pl.when(pl.program_id(0) == 0)
        def _():
            dgain_ref[...] = jnp.zeros_like(dgain_ref)

        for h in range(H):
            ls = pl.ds(h * HGRN_HEAD, HGRN_HEAD)
            _, vjp = jax.vjp(_head_out, o_ref[:, ls], g_ref[:, ls], gain_ref[:, ls])
            do, dg, dgain = vjp(dz_ref[:, ls].astype(F32))
            do_ref[:, ls] = do
            dg_ref[:, ls] = dg
            dgain_ref[:, ls] += dgain

    row = pl.BlockSpec((tm, D), lambda i: (i, 0))
    vec = pl.BlockSpec((1, D), lambda i: (0, 0))
    return pl.pallas_call(
        body, name=name, grid=(S // tm,),
        in_specs=[row, pl.BlockSpec((tm, D), lambda i: (i, 3)), vec, row], out_specs=[row, row, vec],
        out_shape=[jax.ShapeDtypeStruct((S, D), F32), jax.ShapeDtypeStruct((S, D), F32),
                   jax.ShapeDtypeStruct((1, D), F32)],
        compiler_params=_params("arbitrary"),
    )(o, proj, gain, dz)


def lower_bound_fwd(l0, l1, name):
    def body(a_ref, b_ref, o_ref):
        o_ref[...] = jax.nn.sigmoid(a_ref[...] - b_ref[...])

    return pl.pallas_call(body, name=name, out_shape=jax.ShapeDtypeStruct(l0.shape, F32))(l0, l1)


def lower_bound_bwd(l0, l1, dlb, name):
    def body(a_ref, b_ref, d_ref, o0_ref, o1_ref):
        s = jax.nn.sigmoid(a_ref[...] - b_ref[...])
        d0 = d_ref[...] * s * (1.0 - s)
        o0_ref[...] = d0
        o1_ref[...] = -d0

    sd = jax.ShapeDtypeStruct(l0.shape, F32)
    return pl.pallas_call(body, name=name, out_shape=[sd, sd])(l0, l1, dlb)


def _attn_tile(q, kp, kc, vp, vc, sink, first):
    W = ATT_WINDOW
    nt = (((1,), (1,)), ((), ()))
    qb = q.astype(BF16)
    scale = ATT_HEAD ** -0.5
    sp = lax.dot_general(qb, kp.astype(BF16), nt, preferred_element_type=F32) * scale
    sc = lax.dot_general(qb, kc.astype(BF16), nt, preferred_element_type=F32) * scale
    qi = lax.broadcasted_iota(jnp.int32, sp.shape, 0) & (W - 1)
    kj = lax.broadcasted_iota(jnp.int32, sp.shape, 1)
    sp = jnp.where((kj > qi) & jnp.logical_not(first), sp, NEG_INF)
    sc = jnp.where(kj <= qi, sc, NEG_INF)
    m = jnp.maximum(jnp.maximum(jnp.max(sp, axis=-1, keepdims=True), jnp.max(sc, axis=-1, keepdims=True)), sink)
    pp = jnp.exp(sp - m)
    pc = jnp.exp(sc - m)
    denom = jnp.sum(pp, axis=-1, keepdims=True) + jnp.sum(pc, axis=-1, keepdims=True) + jnp.exp(sink - m)
    out = jnp.dot((pp / denom).astype(BF16), vp.astype(BF16), preferred_element_type=F32)
    return out + jnp.dot((pc / denom).astype(BF16), vc.astype(BF16), preferred_element_type=F32)


def _attn_specs(G, W, Dh):
    q_spec = pl.BlockSpec((None, G, W, Dh), lambda j, n: (j, 0, n, 0))
    prev = pl.BlockSpec((None, W, Dh), lambda j, n: (j, jnp.maximum(n - 1, 0), 0))
    cur = pl.BlockSpec((None, W, Dh), lambda j, n: (j, n, 0))
    sink = pl.BlockSpec((None, G * W, 1), lambda j, n: (j, 0, 0))
    return q_spec, prev, cur, sink


def attn_fwd(q4, k3, v3, sink, name):
    NKV, G, S, Dh = q4.shape
    W = ATT_WINDOW

    def body(q_ref, kp_ref, kc_ref, vp_ref, vc_ref, s_ref, o_ref):
        first = pl.program_id(1) == 0
        out = _attn_tile(q_ref[...].reshape(G * W, Dh), kp_ref[...], kc_ref[...], vp_ref[...], vc_ref[...],
                         s_ref[...], first)
        o_ref[...] = out.reshape(G, W, Dh)

    q_spec, prev, cur, sk = _attn_specs(G, W, Dh)
    return pl.pallas_call(
        body, name=name, grid=(NKV, S // W), in_specs=[q_spec, prev, cur, prev, cur, sk], out_specs=q_spec,
        out_shape=jax.ShapeDtypeStruct(q4.shape, F32), compiler_params=_params("parallel", "parallel"),
    )(q4, k3, k3, v3, v3, sink)


def attn_bwd(q4, k3, v3, sink, do4, name):
    NKV, G, S, Dh = q4.shape
    W = ATT_WINDOW
    nb = S // W

    def body(q_ref, kp_ref, kc_ref, vp_ref, vc_ref, s_ref, do_ref, dq_ref, dkp_ref, dkc_ref, dvp_ref, dvc_ref,
             ds_ref):
        first = pl.program_id(1) == 0
        _, vjp = jax.vjp(functools.partial(_attn_tile, first=first), q_ref[...].reshape(G * W, Dh), kp_ref[...],
                         kc_ref[...], vp_ref[...], vc_ref[...], s_ref[...])
        dq, dkp, dkc, dvp, dvc, ds = vjp(do_ref[...].reshape(G * W, Dh))
        dq_ref[...] = dq.reshape(G, W, Dh)
        dkp_ref[...] = dkp
        dkc_ref[...] = dkc
        dvp_ref[...] = dvp
        dvc_ref[...] = dvc

        @pl.when(first)
        def _():
            ds_ref[...] = jnp.zeros_like(ds_ref)

        ds_ref[...] += jnp.sum(ds.reshape(G, W, 1), axis=1)

    q_spec, prev, cur, sk = _attn_specs(G, W, Dh)
    part = pl.BlockSpec((None, None, W, Dh), lambda j, n: (j, n, 0, 0))
    ps = jax.ShapeDtypeStruct((NKV, nb, W, Dh), F32)
    outs = pl.pallas_call(
        body, name=name, grid=(NKV, nb), in_specs=[q_spec, prev, cur, prev, cur, sk, q_spec],
        out_specs=[q_spec, part, part, part, part, pl.BlockSpec((None, G, 1), lambda j, n: (j, 0, 0))],
        out_shape=[jax.ShapeDtypeStruct(q4.shape, F32), ps, ps, ps, ps, jax.ShapeDtypeStruct((NKV, G, 1), F32)],
        compiler_params=_params("parallel", "arbitrary"),
    )(q4, k3, k3, v3, v3, sink, do4)
    return outs[0], outs[1:5], outs[5]


def band_combine(cur, prev, name):
    NKV, nb, W, Dh = cur.shape

    def body(c_ref, p_ref, o_ref):
        keep = (pl.program_id(1) < nb - 1).astype(F32)
        o_ref[...] = c_ref[...] + keep * p_ref[...]

    return pl.pallas_call(
        body, name=name, grid=(NKV, nb),
        in_specs=[pl.BlockSpec((None, None, W, Dh), lambda j, n: (j, n, 0, 0)),
                  pl.BlockSpec((None, None, W, Dh), lambda j, n: (j, jnp.minimum(n + 1, nb - 1), 0, 0))],
        out_specs=pl.BlockSpec((None, W, Dh), lambda j, n: (j, n, 0)),
        out_shape=jax.ShapeDtypeStruct((NKV, nb * W, Dh), F32), compiler_params=_params("parallel", "parallel"),
    )(cur, prev)


def rope_tables(S):
    half = ROT_DIM // 2
    inv_freq = jnp.power(jnp.float32(ROPE_THETA), -jnp.arange(0, ROT_DIM, 2, dtype=F32) / ROT_DIM)
    ang = jnp.arange(S, dtype=F32)[:, None] * inv_freq[None, :]
    sin, cos = jnp.sin(ang), jnp.cos(ang)
    zeros = jnp.zeros((S, ATT_HEAD - ROT_DIM), F32)
    z8 = jnp.zeros((S, half), F32)
    cfull = jnp.concatenate([cos, cos, jnp.ones((S, ATT_HEAD - ROT_DIM), F32)], axis=1)
    s_next = jnp.concatenate([-sin, z8, zeros], axis=1)
    s_prev = jnp.concatenate([z8, sin, zeros], axis=1)
    two = lambda t: jnp.concatenate([t, t], axis=1)
    return two(cfull), two(s_next), two(s_prev)


def rope(x, tables, sign, name):
    S, Wd = x.shape
    tm = _row_tile(S)
    rep = Wd // LANES
    half = ROT_DIM // 2

    def body(x_ref, c_ref, sn_ref, sp_ref, o_ref):
        xv = x_ref[...]
        c = jnp.tile(c_ref[...], (1, rep))
        sn = jnp.tile(sn_ref[...], (1, rep))
        sp = jnp.tile(sp_ref[...], (1, rep))
        if sign > 0:
            nxt = pltpu.roll(xv, Wd - half, 1)
            prv = pltpu.roll(xv, half, 1)
            o_ref[...] = xv * c + nxt * sn + prv * sp
        else:
            o_ref[...] = xv * c + pltpu.roll(xv * sn, half, 1) + pltpu.roll(xv * sp, Wd - half, 1)

    row = pl.BlockSpec((tm, Wd), lambda i: (i, 0))
    tab = pl.BlockSpec((tm, LANES), lambda i: (i, 0))
    return pl.pallas_call(
        body, name=name, grid=(S // tm,), in_specs=[row, tab, tab, tab], out_specs=row,
        out_shape=jax.ShapeDtypeStruct((S, Wd), F32), compiler_params=_params("parallel"),
    )(x, *tables)


def cond_proj(c_all, w, bias, name):
    B, D = c_all.shape
    N = w.shape[1]
    tn = _tile(N, 512)

    def body(c_ref, w_ref, b_ref, o_ref):
        cv = c_ref[...]
        cs = (cv * jax.nn.sigmoid(cv)).astype(BF16)
        o_ref[...] = jnp.dot(cs, w_ref[...].astype(BF16), preferred_element_type=F32) + b_ref[...]

    return pl.pallas_call(
        body, name=name, grid=(N // tn,),
        in_specs=[pl.BlockSpec((B, D), lambda j: (0, 0)), pl.BlockSpec((D, tn), lambda j: (0, j)),
                  pl.BlockSpec((1, tn), lambda j: (0, j))],
        out_specs=pl.BlockSpec((B, tn), lambda j: (0, j)), out_shape=jax.ShapeDtypeStruct((B, N), F32),
        compiler_params=_params("parallel"),
    )(c_all, w, bias)


def cond_grad(c_all, dmod, name):
    B, D = c_all.shape
    N = dmod.shape[1]
    tn = _tile(N, 512)

    def body(c_ref, d_ref, o_ref):
        cv = c_ref[...]
        cs = (cv * jax.nn.sigmoid(cv)).astype(BF16)
        o_ref[...] = lax.dot_general(cs, d_ref[...].astype(BF16), (((0,), (0,)), ((), ())),
                                     preferred_element_type=F32)

    return pl.pallas_call(
        body, name=name, grid=(N // tn,),
        in_specs=[pl.BlockSpec((B, D), lambda j: (0, 0)), pl.BlockSpec((B, tn), lambda j: (0, j))],
        out_specs=pl.BlockSpec((D, tn), lambda j: (0, j)), out_shape=jax.ShapeDtypeStruct((D, N), F32),
        compiler_params=_params("parallel"),
    )(c_all, dmod)


def rowsum(g, name):
    B, N = g.shape
    tn = _tile(N, 8192)

    def body(g_ref, o_ref):
        acc = g_ref[0:1, :]
        for r in range(1, B):
            acc = acc + g_ref[r:r + 1, :]
        o_ref[...] = acc

    return pl.pallas_call(
        body, name=name, grid=(N // tn,), in_specs=[pl.BlockSpec((B, tn), lambda j: (0, j))],
        out_specs=pl.BlockSpec((1, tn), lambda j: (0, j)), out_shape=jax.ShapeDtypeStruct((1, N), F32),
        compiler_params=_params("parallel"),
    )(g)


def _adam_rows(R, C):
    if R * C * 4 <= (1 << 20) or R % 8:
        return R
    best = 8
    for t in range(8, R + 1, 8):
        if R % t == 0 and t * C * 4 <= (1 << 20):
            best = t
    return best


def adamw(w, m, v, gparts, name):
    R, C = w.shape
    tr = _adam_rows(R, C)
    n = len(gparts)

    def body(*refs):
        w_ref, m_ref, v_ref = refs[:3]
        g_refs = refs[3:3 + n]
        go_ref, d_ref, mo_ref, vo_ref = refs[3 + n:]
        g = g_refs[0][...].astype(F32)
        for r in g_refs[1:]:
            g = g + r[...].astype(F32)
        mn = ADAM_B1 * m_ref[...] + (1.0 - ADAM_B1) * g
        vn = ADAM_B2 * v_ref[...] + (1.0 - ADAM_B2) * jnp.square(g)
        m_hat = mn / (1.0 - ADAM_B1 ** ADAM_STEP)
        v_hat = vn / (1.0 - ADAM_B2 ** ADAM_STEP)
        go_ref[...] = g
        d_ref[...] = -ADAM_LR * (m_hat / (jnp.sqrt(v_hat) + ADAM_EPS) + ADAM_WD * w_ref[...])
        mo_ref[...] = mn
        vo_ref[...] = vn

    spec = pl.BlockSpec((tr, C), lambda i: (i, 0))
    sd = jax.ShapeDtypeStruct((R, C), F32)
    return pl.pallas_call(
        body, name=name, grid=(R // tr,), in_specs=[spec] * (3 + n), out_specs=[spec] * 4,
        out_shape=[sd] * 4, compiler_params=_params("parallel"),
    )(w, m, v, *gparts)


_ANY = pl.BlockSpec(memory_space=pl.ANY)


def _place():
    return lax.axis_index("x"), lax.axis_index("y"), lax.axis_index("c")


def all_gather_multi(xs, name):
    n = len(xs)

    def body(*refs):
        ins, outs = refs[:n], refs[n:2 * n]
        send_sems, recv_sems, local_sems = refs[2 * n:]
        x, y, c = _place()
        me, sibling = (x, y, c), (x, y, 1 - c)
        chips = [(1 - x, y), (x, 1 - y), (1 - x, 1 - y)]

        def slot(p):
            return 4 * p[0] + 2 * p[1] + p[2]

        def copy(t, k, block, to, from_input=False):
            dst = outs[t].at[slot(block)]
            return pltpu.make_async_remote_copy(
                src_ref=ins[t] if from_input else dst, dst_ref=dst, send_sem=send_sems.at[t, k],
                recv_sem=recv_sems.at[t, k], device_id=to, device_id_type=MESH)

        mines, sends = [], []
        for t in range(n):
            mine = pltpu.make_async_copy(ins[t], outs[t].at[slot(me)], local_sems.at[t])
            mine.start()
            mines.append(mine)
            first = [copy(t, 0, me, sibling, True)]
            first += [copy(t, 1 + j, me, (*chip, c), True) for j, chip in enumerate(chips)]
            for cp in first:
                cp.start()
            sends += first
        for t in range(n):
            for j, chip in enumerate(chips):
                copy(t, 1 + j, (*chip, c), me).wait_recv()
                passed = copy(t, 4 + j, (*chip, c), sibling)
                passed.start()
                sends.append(passed)
        for t in range(n):
            copy(t, 0, sibling, me).wait_recv()
            for j, chip in enumerate(chips):
                copy(t, 4 + j, (*chip, 1 - c), me).wait_recv()
        for cp in sends:
            cp.wait_send()
        for mine in mines:
            mine.wait()

    return pl.pallas_call(
        body, name=name, in_specs=[_ANY] * n, out_specs=[_ANY] * n,
        out_shape=[jax.ShapeDtypeStruct((N_DEV,) + a.shape, a.dtype) for a in xs],
        scratch_shapes=[pltpu.SemaphoreType.DMA((n, 7)), pltpu.SemaphoreType.DMA((n, 7)),
                        pltpu.SemaphoreType.DMA((n,))],
    )(*xs)


def pair_exchange(gs, name):
    n = len(gs)

    def body(*refs):
        ins, outs = refs[:n], refs[n:2 * n]
        send_sems, recv_sems = refs[2 * n:]
        x, y, c = _place()
        copies = []
        for t in range(n):
            for q in range(4):
                cp = pltpu.make_async_remote_copy(
                    src_ref=ins[t].at[q, 1 - c], dst_ref=outs[t].at[q], send_sem=send_sems.at[t, q],
                    recv_sem=recv_sems.at[t, q], device_id=(x, y, 1 - c), device_id_type=MESH)
                cp.start()
                copies.append(cp)
        for cp in copies:
            cp.wait()

    return pl.pallas_call(
        body, name=name, in_specs=[_ANY] * n, out_specs=[_ANY] * n,
        out_shape=[jax.ShapeDtypeStruct((4,) + g.shape[2:], g.dtype) for g in gs],
        scratch_shapes=[pltpu.SemaphoreType.DMA((n, 4)), pltpu.SemaphoreType.DMA((n, 4))],
    )(*gs)


def pair_add(g, r, core, name):
    _, _, R, C = g.shape
    tr = _adam_rows(R, C)

    def body(core_ref, g_ref, r_ref, o_ref):
        o_ref[...] = g_ref[...] + r_ref[...]

    return pl.pallas_call(
        body, name=name,
        grid_spec=pltpu.PrefetchScalarGridSpec(
            num_scalar_prefetch=1, grid=(4, R // tr),
            in_specs=[pl.BlockSpec((None, None, tr, C), lambda q, i, core_ref: (q, core_ref[0], i, 0)),
                      pl.BlockSpec((None, tr, C), lambda q, i, core_ref: (q, i, 0))],
            out_specs=pl.BlockSpec((None, tr, C), lambda q, i, core_ref: (q, i, 0))),
        out_shape=jax.ShapeDtypeStruct((4, R, C), F32), compiler_params=_params("parallel", "parallel"),
    )(core, g, r)


def chip_exchange(ps, name):
    n = len(ps)

    def body(*refs):
        ins, outs = refs[:n], refs[n:2 * n]
        send_sems, recv_sems = refs[2 * n:]
        x, y, c = _place()
        chips = [(1 - x, y), (x, 1 - y), (1 - x, 1 - y)]
        copies = []
        for t in range(n):
            for k, chip in enumerate(chips):
                cp = pltpu.make_async_remote_copy(
                    src_ref=ins[t].at[2 * chip[0] + chip[1]], dst_ref=outs[t].at[k], send_sem=send_sems.at[t, k],
                    recv_sem=recv_sems.at[t, k], device_id=(*chip, c), device_id_type=MESH)
                cp.start()
                copies.append(cp)
        for cp in copies:
            cp.wait()

    return pl.pallas_call(
        body, name=name, in_specs=[_ANY] * n, out_specs=[_ANY] * n,
        out_shape=[jax.ShapeDtypeStruct((3,) + p.shape[1:], p.dtype) for p in ps],
        scratch_shapes=[pltpu.SemaphoreType.DMA((n, 3)), pltpu.SemaphoreType.DMA((n, 3))],
    )(*ps)


def _ffn_fwd(u, w_in, w_out, tag):
    ab3 = mm_nn(u[None], w_in[None], F32, tag + "_in")
    h3 = swiglu_fwd(ab3, tag + "_act")
    J = h3.shape[0]
    y = mm_nn(h3, w_out.reshape(J, 1, -1, w_out.shape[-1]), F32, tag + "_out", natural=True)
    return y, (ab3, h3)


def _ffn_bwd(dy, u, ab3, h3, w_in, w_out, tag):
    J = h3.shape[0]
    D = w_out.shape[-1]
    dh3 = mm_nt(dy[None], w_out.reshape(1, J, -1, D), F32, tag + "_dact")
    dab3 = swiglu_bwd(ab3, dh3, tag + "_dab")
    dw_out = mm_tn(h3, dy[None], tag + "_dwout").reshape(w_out.shape)
    dw_in = mm_tn(u[None], dab3, tag + "_dwin").reshape(w_in.shape)
    du = mm_nt(dab3, w_in[:, None], F32, tag + "_du", natural=True)
    return du, dw_in, dw_out


def kernel(x, c, norm_gain, w_ada, b_ada, w_ffn_in, w_ffn_out, w_hgrn_in, hgrn_lb_logits, hgrn_head_gain, w_hgrn_out, kv_gain, w_ada_kv, b_ada_kv, w_kv, b_kv, w_q, b_q, attn_sinks, w_attn_out, final_gain, loss_target, m_norm_gain, m_w_ada, m_b_ada, m_w_ffn_in, m_w_ffn_out, m_w_hgrn_in, m_hgrn_lb_logits, m_hgrn_head_gain, m_w_hgrn_out, m_kv_gain, m_w_ada_kv, m_b_ada_kv, m_w_kv, m_b_kv, m_w_q, m_b_q, m_attn_sinks, m_w_attn_out, m_final_gain, v_norm_gain, v_w_ada, v_b_ada, v_w_ffn_in, v_w_ffn_out, v_w_hgrn_in, v_hgrn_lb_logits, v_hgrn_head_gain, v_w_hgrn_out, v_kv_gain, v_w_ada_kv, v_b_ada_kv, v_w_kv, v_b_kv, v_w_q, v_b_q, v_attn_sinks, v_w_attn_out, v_final_gain):
    xi, yi, ci = _place()
    me = 4 * xi + 2 * yi + ci
    _, S, D = x.shape
    L = norm_gain.shape[0]
    dsh = D // N_DEV
    ada_n = w_ada.shape[2]
    kv_n = w_ada_kv.shape[1]
    NQ = D // ATT_HEAD
    NKV = NQ // ATT_GROUP
    kvd = NKV * ATT_HEAD
    h0 = x[0]
    target = loss_target[0]

    def my_cols(a, n):
        return lax.dynamic_slice_in_dim(a, me * n, n, axis=a.ndim - 1)

    lb_sh = lower_bound_fwd(hgrn_lb_logits[0:1], hgrn_lb_logits[1:2], "lb_fwd")
    small = jnp.concatenate([c, norm_gain.reshape(1, L * 3 * dsh), hgrn_head_gain, lb_sh], axis=1)
    (g1,) = all_gather_multi([small], "gather_cond")
    g1 = g1.reshape(N_DEV, -1)
    c_all = g1[:, :D]
    gains = g1[:, D:D + L * 3 * dsh].reshape(N_DEV, L * 3, dsh).transpose(1, 0, 2).reshape(L, 3, 1, D)
    head_gain = g1[:, D + L * 3 * dsh:D + (L * 3 + 1) * dsh].reshape(1, D)
    lb0 = g1[:, D + (L * 3 + 1) * dsh:].reshape(1, D)

    parts = [cond_proj(c_all, w_ada[l], my_cols(b_ada[l:l + 1], ada_n), f"mod{l}") for l in range(L)]
    parts.append(cond_proj(c_all, w_ada_kv, my_cols(b_ada_kv[None], kv_n), "mod_kv"))
    (g2,) = all_gather_multi([jnp.concatenate(parts, axis=1)], "gather_mod")
    mine2 = lax.dynamic_index_in_dim(g2, me, axis=1, keepdims=False)
    mod = [mine2[:, l * ada_n:(l + 1) * ada_n].reshape(3, 3, 1, D) for l in range(L)]
    mod_kv = mine2[:, L * ada_n:].reshape(2, 1, D)

    bf = lambda a: a.astype(BF16)
    shards = [bf(w_ffn_in[l, i]) for l in range(L) for i in range(2)]
    shards += [bf(w_ffn_out[l, i]) for l in range(L) for i in range(2)]
    shards += [bf(w_hgrn_in[0]), bf(w_hgrn_out[0]), bf(w_kv), bf(w_q[0]), bf(w_attn_out[0])]
    gathered = all_gather_multi(shards, "gather_weights")
    W_in = [[gathered[2 * l + i] for i in range(2)] for l in range(L)]
    W_out = [[gathered[2 * L + 2 * l + i] for i in range(2)] for l in range(L)]
    W_hin, W_hout, W_kv, W_q, W_o = gathered[4 * L:]
    full = lambda w: w.reshape(1, 1, -1, w.shape[-1])

    tables = rope_tables(S)
    sink_col = jnp.broadcast_to(attn_sinks.reshape(NKV, ATT_GROUP, 1, 1), (NKV, ATT_GROUP, ATT_WINDOW, 1))
    sink_col = sink_col.reshape(NKV, ATT_GROUP * ATT_WINDOW, 1)

    def to_heads(t, n):
        return t.reshape(S, n, ATT_HEAD).transpose(1, 0, 2)

    def from_heads(t):
        return t.transpose(1, 0, 2).reshape(S, -1)

    h = h0
    saved = {}
    for l in range(L):
        for s in (0, 1, 2):
            tag = f"l{l}s{s}"
            shift, scale, gate = mod[l][s, 0], mod[l][s, 1], mod[l][s, 2]
            u = adaln_fwd(h, gains[l, s], shift, scale, tag + "_norm")
            if s != 1:
                i = s // 2
                y, res = _ffn_fwd(u, W_in[l][i], W_out[l][i], tag)
                coef = 0.5
            elif l == 0:
                proj = mm_nn(u[None], W_hin[None], F32, tag + "_proj", natural=True)
                o, states = hgrn_scan_fwd(proj, lb0, tag + "_scan")
                z = hgrn_post_fwd(o, proj, head_gain, tag + "_post")
                y = mm_nn(z[None], full(W_hout), F32, tag + "_out", natural=True)
                res = (proj, o, states, z)
                coef = 1.0
            else:
                q = mm_nn(u[None], full(W_q), F32, tag + "_q", natural=True, bias=b_q)
                q4 = to_heads(rope(q, tables, 1, tag + "_rope"), NQ).reshape(NKV, ATT_GROUP, S, ATT_HEAD)
                att4 = attn_fwd(q4, k3, v3, sink_col, tag + "_attn")
                att = from_heads(att4.reshape(NQ, S, ATT_HEAD))
                y = mm_nn(att[None], full(W_o), F32, tag + "_out", natural=True)
                res = (q4, att)
                coef = 1.0
            saved[(l, s)] = (h, u, y, res)
            h = resid_fwd(h, y, gate, coef, tag + "_res")
        if l == 0:
            h_kv = h
            u_kv = adaln_fwd(h, kv_gain[None], mod_kv[0], mod_kv[1], "kv_norm")
            kvp = mm_nn(u_kv[None], full(W_kv), F32, "kv_proj", natural=True, bias=b_kv[None])
            k3 = to_heads(rope(kvp[:, :kvd], tables, 1, "kv_rope"), NKV)
            v3 = to_heads(kvp[:, kvd:], NKV)

    loss_row, dh, d_final_gain = final_loss_grad(h, final_gain[None], target, "final")
    loss = lax.psum(loss_row[0, 0], ("x", "y", "c"))

    d_mod = [[None] * 3 for _ in range(L)]
    d_gain = [[None] * 3 for _ in range(L)]
    dW_in = [[None] * 2 for _ in range(L)]
    dW_out = [[None] * 2 for _ in range(L)]
    for l in reversed(range(L)):
        if l == 0:
            dkv = jnp.concatenate([rope(from_heads(dk3), tables, -1, "kv_drope"), from_heads(dv3)], axis=1)
            dW_kv = mm_tn(u_kv[None], dkv[None], "kv_dw").reshape(W_kv.shape)
            db_kv = colsum(dkv, "kv_db")
            du_kv = mm_nt(dkv[None], full(W_kv), F32, "kv_du", natural=True)
            dh, d_kv_gain, d_kv_shift, d_kv_scale = adaln_bwd(h_kv, kv_gain[None], mod_kv[0], mod_kv[1], du_kv, dh,
                                                              "kv_dnorm")
        for s in (2, 1, 0):
            tag = f"l{l}s{s}"
            shift, scale, gate = mod[l][s, 0], mod[l][s, 1], mod[l][s, 2]
            h_in, u, y, res = saved[(l, s)]
            dy, d_gate = gate_bwd(dh, y, gate, 0.5 if s != 1 else 1.0, tag + "_dres")
            if s != 1:
                i = s // 2
                du, dW_in[l][i], dW_out[l][i] = _ffn_bwd(dy, u, res[0], res[1], W_in[l][i], W_out[l][i], tag)
            elif l == 0:
                proj, o, states, z = res
                dW_hout = mm_tn(z[None], dy[None], tag + "_dwout").reshape(W_hout.shape)
                dz = mm_nt(dy[None], full(W_hout), F32, tag + "_dz", natural=True)
                do, dg, d_head_gain = hgrn_post_bwd(o, proj, head_gain, dz, tag + "_dpost")
                dproj, d_lb0 = hgrn_scan_bwd(proj, lb0, states, do, dg, tag + "_dscan")
                dW_hin = mm_tn(u[None], dproj, tag + "_dwin", b_natural=True, jb=N_DEV).reshape(W_hin.shape)
                du = mm_nt(dproj, W_hin[:, None], F32, tag + "_du", natural=True, a_natural=True)
            else:
                q4, att = res
                dW_o = mm_tn(att[None], dy[None], tag + "_dwout").reshape(W_o.shape)
                datt = mm_nt(dy[None], full(W_o), F32, tag + "_datt", natural=True)
                datt4 = to_heads(datt, NQ).reshape(NKV, ATT_GROUP, S, ATT_HEAD)
                dq4, (dkp, dkc, dvp, dvc), d_sink = attn_bwd(q4, k3, v3, sink_col, datt4, tag + "_dattn")
                dk3 = band_combine(dkc, dkp, tag + "_dk")
                dv3 = band_combine(dvc, dvp, tag + "_dv")
                dq = rope(from_heads(dq4.reshape(NQ, S, ATT_HEAD)), tables, -1, tag + "_drope")
                dW_q = mm_tn(u[None], dq[None], tag + "_dwq").reshape(W_q.shape)
                db_q = colsum(dq, tag + "_dbq")
                du = mm_nt(dq[None], full(W_q), F32, tag + "_du", natural=True)
            dh, dg_, dsh_, dsc_ = adaln_bwd(h_in, gains[l, s], shift, scale, du, dh, tag + "_dnorm")
            d_gain[l][s] = dg_
            d_mod[l][s] = jnp.concatenate([dsh_, dsc_, d_gate], axis=1)
    grad_x = dh[None]

    pad = lambda a, n: jnp.pad(a, ((0, 0), (0, n - a.shape[1])))
    pieces = [jnp.concatenate(d_mod[l], axis=1) for l in range(L)]
    pieces += [d_kv_shift, d_kv_scale]
    pieces += [d_gain[l][s] for l in range(L) for s in range(3)]
    pieces += [d_head_gain, d_lb0, d_kv_gain, db_kv, db_q, pad(d_sink.reshape(1, NQ), LANES), d_final_gain]
    (g3,) = all_gather_multi([jnp.concatenate(pieces, axis=1)], "gather_small_grads")
    g3 = g3.reshape(N_DEV, -1)
    tot = rowsum(g3, "sum_small_grads")
    offs = [0]
    for p in pieces:
        offs.append(offs[-1] + p.shape[1])
    seg = lambda k: tot[:, offs[k]:offs[k + 1]]
    k0 = 0
    g_b_ada = jnp.concatenate([seg(l) for l in range(L)], axis=0)
    k0 += L
    g_b_ada_kv = jnp.concatenate([seg(k0), seg(k0 + 1)], axis=1)[0]
    k0 += 2
    g_norm_gain = jnp.stack([my_cols(seg(k0 + j), dsh)[0] for j in range(3 * L)]).reshape(L, 3, dsh)
    k0 += 3 * L
    g_head_gain = my_cols(seg(k0), dsh)
    d_lb_sh = my_cols(seg(k0 + 1), dsh)
    g_kv_gain = seg(k0 + 2)[0]
    g_b_kv = seg(k0 + 3)[0]
    g_b_q = seg(k0 + 4)
    g_sinks = seg(k0 + 5)[:, :NQ]
    g_final_gain = seg(k0 + 6)[0]
    dl0, dl1 = lower_bound_bwd(hgrn_lb_logits[0:1], hgrn_lb_logits[1:2], d_lb_sh, "lb_bwd")
    g_lb_logits = jnp.concatenate([dl0, dl1], axis=0)

    g_w_ada = jnp.stack([cond_grad(c_all, lax.dynamic_slice_in_dim(g3, offs[l] + me * ada_n, ada_n, axis=1),
                                   f"dw_ada{l}") for l in range(L)])
    g_w_ada_kv = cond_grad(c_all, lax.dynamic_slice_in_dim(g3, offs[L] + me * kv_n, kv_n, axis=1), "dw_ada_kv")

    big = [dW_in[l][i] for l in range(L) for i in range(2)] + [dW_out[l][i] for l in range(L) for i in range(2)]
    big += [dW_hin, dW_hout, dW_kv, dW_q, dW_o]
    big4 = [g.reshape((4, 2) + g.shape[1:]) for g in big]
    from_sibling = pair_exchange(big4, "grads_pair")
    core = ci.astype(jnp.int32).reshape(1)
    sums = [pair_add(g, r, core, f"grads_pair_add{t}") for t, (g, r) in enumerate(zip(big4, from_sibling))]
    from_chips = chip_exchange(sums, "grads_chips")
    chip = 2 * xi + yi
    g_parts = [[lax.dynamic_index_in_dim(p, chip, axis=0, keepdims=False), r[0], r[1], r[2]]
               for p, r in zip(sums, from_chips)]

    def update(w, m, v, parts, name):
        shape = w.shape
        two = lambda a: a.reshape(-1, shape[-1])
        outs = adamw(two(w), two(m), two(v), [two(p) for p in parts], name)
        return [o.reshape(shape) for o in outs]

    def update_stack(w, m, v, parts_list, name):
        lead = w.shape[:-2]
        w2, m2, v2 = (a.reshape((-1,) + a.shape[-2:]) for a in (w, m, v))
        outs = [update(w2[j], m2[j], v2[j], parts_list[j], f"{name}{j}") for j in range(w2.shape[0])]
        return [jnp.stack([o[k] for o in outs]).reshape(w.shape) for k in range(4)]

    res = {}
    res["norm_gain"] = update(norm_gain, m_norm_gain, v_norm_gain, [g_norm_gain], "adam_norm_gain")
    res["w_ada"] = update(w_ada, m_w_ada, v_w_ada, [g_w_ada], "adam_w_ada")
    res["b_ada"] = update(b_ada, m_b_ada, v_b_ada, [g_b_ada], "adam_b_ada")
    res["w_ffn_in"] = update_stack(w_ffn_in, m_w_ffn_in, v_w_ffn_in, g_parts[0:2 * L], "adam_w_ffn_in")
    res["w_ffn_out"] = update_stack(w_ffn_out, m_w_ffn_out, v_w_ffn_out, g_parts[2 * L:4 * L], "adam_w_ffn_out")
    res["w_hgrn_in"] = update_stack(w_hgrn_in, m_w_hgrn_in, v_w_hgrn_in, g_parts[4 * L:4 * L + 1], "adam_w_hgrn_in")
    res["hgrn_lb_logits"] = update(hgrn_lb_logits, m_hgrn_lb_logits, v_hgrn_lb_logits, [g_lb_logits], "adam_lb")
    res["hgrn_head_gain"] = update(hgrn_head_gain, m_hgrn_head_gain, v_hgrn_head_gain, [g_head_gain], "adam_head_gain")
    res["w_hgrn_out"] = update_stack(w_hgrn_out, m_w_hgrn_out, v_w_hgrn_out, g_parts[4 * L + 1:4 * L + 2], "adam_w_hgrn_out")
    res["kv_gain"] = update(kv_gain, m_kv_gain, v_kv_gain, [g_kv_gain], "adam_kv_gain")
    res["w_ada_kv"] = update(w_ada_kv, m_w_ada_kv, v_w_ada_kv, [g_w_ada_kv], "adam_w_ada_kv")
    res["b_ada_kv"] = update(b_ada_kv, m_b_ada_kv, v_b_ada_kv, [g_b_ada_kv], "adam_b_ada_kv")
    res["w_kv"] = update(w_kv, m_w_kv, v_w_kv, g_parts[4 * L + 2], "adam_w_kv")
    res["b_kv"] = update(b_kv, m_b_kv, v_b_kv, [g_b_kv], "adam_b_kv")
    res["w_q"] = update_stack(w_q, m_w_q, v_w_q, g_parts[4 * L + 3:4 * L + 4], "adam_w_q")
    res["b_q"] = update(b_q, m_b_q, v_b_q, [g_b_q], "adam_b_q")
    res["attn_sinks"] = update(attn_sinks, m_attn_sinks, v_attn_sinks, [g_sinks], "adam_sinks")
    res["w_attn_out"] = update_stack(w_attn_out, m_w_attn_out, v_w_attn_out, g_parts[4 * L + 4:4 * L + 5], "adam_w_attn_out")
    res["final_gain"] = update(final_gain, m_final_gain, v_final_gain, [g_final_gain], "adam_final_gain")

    names = ["norm_gain", "w_ada", "b_ada", "w_ffn_in", "w_ffn_out", "w_hgrn_in", "hgrn_lb_logits", "hgrn_head_gain",
             "w_hgrn_out", "kv_gain", "w_ada_kv", "b_ada_kv", "w_kv", "b_kv", "w_q", "b_q", "attn_sinks", "w_attn_out",
             "final_gain"]
    return (loss, grad_x, *[res[n][0] for n in names], *[res[n][1] for n in names], *[res[n][2] for n in names],
            *[res[n][3] for n in names])
```

```python
import functools

import jax
import jax.numpy as jnp
from jax import lax
from jax.experimental import pallas as pl
from jax.experimental.pallas import tpu as pltpu

F32 = jnp.float32
BF16 = jnp.bfloat16
MESH = pl.DeviceIdType.MESH

N_DEV = 8
V7X_VMEM_LIMIT_BYTES = 56 * 1024 * 1024
LANES = 128

NORM_EPS = 1e-6
NEG_INF = -1e30
HGRN_CHUNK = 32
HGRN_HEAD = 128
ATT_HEAD = 64
ATT_WINDOW = 128
ATT_GROUP = 8
ROT_DIM = 16
ROPE_THETA = 500000.0

ADAM_LR = 0.001
ADAM_B1 = 0.9
ADAM_B2 = 0.999
ADAM_EPS = 1e-08
ADAM_WD = 0.01
ADAM_STEP = 10


def _params(*sem):
    return pltpu.CompilerParams(dimension_semantics=sem, vmem_limit_bytes=V7X_VMEM_LIMIT_BYTES)


def _tile(n, pref, unit=LANES):
    t = (min(n, pref) // unit) * unit
    while t >= unit:
        if n % t == 0:
            return t
        t -= unit
    return n


_ANY = pl.BlockSpec(memory_space=pl.ANY)


class _Side:
    def __init__(self, inputs, out_shapes, sem_shapes, start, finish):
        self.inputs, self.out_shapes, self.sem_shapes = list(inputs), list(out_shapes), list(sem_shapes)
        self.start, self.finish = start, finish


def _call(body, *, name, grid, in_specs, out_specs, out_shape, args, semantics, scratch_shapes=(), side=None):
    in_specs, out_specs, out_shape = list(in_specs), list(out_specs), list(out_shape)
    scratch_shapes = list(scratch_shapes)
    if side is None:
        outs = pl.pallas_call(
            body, name=name, grid=grid, in_specs=in_specs, out_specs=out_specs, out_shape=out_shape,
            scratch_shapes=scratch_shapes, compiler_params=_params(*semantics))(*args)
        return list(outs)
    n_in, n_out, n_scr = len(in_specs), len(out_specs), len(scratch_shapes)
    s_in, s_out = len(side.inputs), len(side.out_shapes)

    def carried(*refs):
        ins, refs = refs[:n_in], refs[n_in:]
        side_ins, refs = refs[:s_in], refs[s_in:]
        outs, refs = refs[:n_out], refs[n_out:]
        side_outs, refs = refs[:s_out], refs[s_out:]
        scratch, sems = refs[:n_scr], refs[n_scr:]
        first = pl.program_id(0) == 0
        last = pl.program_id(0) == grid[0] - 1
        for d in range(1, len(grid)):
            first = jnp.logical_and(first, pl.program_id(d) == 0)
            last = jnp.logical_and(last, pl.program_id(d) == grid[d] - 1)

        @pl.when(first)
        def _():
            side.start(side_ins, side_outs, sems)

        body(*ins, *outs, *scratch)

        @pl.when(last)
        def _():
            side.finish(side_ins, side_outs, sems)

    outs = pl.pallas_call(
        carried, name=name, grid=grid, in_specs=in_specs + [_ANY] * s_in, out_specs=out_specs + [_ANY] * s_out,
        out_shape=out_shape + side.out_shapes, scratch_shapes=scratch_shapes + side.sem_shapes,
        compiler_params=_params(*(["arbitrary"] * len(grid))))(*args, *side.inputs)
    return list(outs[:n_out]), list(outs[n_out:])


def _accumulate(prod, o_ref, acc_ref, k, nk):
    if nk == 1:
        o_ref[...] = prod.astype(o_ref.dtype)
        return

    @pl.when(k == 0)
    def _():
        acc_ref[...] = prod

    @pl.when(k > 0)
    def _():
        acc_ref[...] += prod

    @pl.when(k == nk - 1)
    def _():
        o_ref[...] = acc_ref[...].astype(o_ref.dtype)


def mm_nn(a3, b4, out_dtype, name, natural=False, a_natural=False, bias=None, side=None):
    JK, JN, kb, nb = b4.shape
    M = a3.shape[0] if a_natural else a3.shape[1]
    tm = min(M, 512)
    tn = _tile(nb, 1024)
    ntn = nb // tn
    nk = JK

    def body(*refs):
        if bias is None:
            a_ref, b_ref, o_ref, acc_ref = refs
        else:
            a_ref, b_ref, bias_ref, o_ref, acc_ref = refs
        prod = jnp.dot(a_ref[...].astype(BF16), b_ref[...].astype(BF16), preferred_element_type=F32)
        if bias is not None:
            prod = prod + bias_ref[...]
        _accumulate(prod, o_ref, acc_ref, pl.program_id(2), nk)

    if a_natural:
        a_spec = pl.BlockSpec((tm, kb), lambda j, i, k: (i, k))
    else:
        a_spec = pl.BlockSpec((None, tm, kb), lambda j, i, k: (k, i, 0))
    b_spec = pl.BlockSpec((None, None, kb, tn), lambda j, i, k: (k, j // ntn, 0, j % ntn))
    in_specs = [a_spec, b_spec]
    args = [a3, b4]
    if bias is not None:
        assert natural and nk == 1
        in_specs.append(pl.BlockSpec((1, tn), lambda j, i, k: (0, j)))
        args.append(bias)
    if natural:
        out_shape = jax.ShapeDtypeStruct((M, JN * nb), out_dtype)
        o_spec = pl.BlockSpec((tm, tn), lambda j, i, k: (i, j))
    else:
        out_shape = jax.ShapeDtypeStruct((JN, M, nb), out_dtype)
        o_spec = pl.BlockSpec((None, tm, tn), lambda j, i, k: (j // ntn, i, j % ntn))
    res = _call(body, name=name, grid=(JN * ntn, M // tm, nk), in_specs=in_specs, out_specs=[o_spec],
                out_shape=[out_shape], scratch_shapes=[pltpu.VMEM((tm, tn), F32)], args=args,
                semantics=("parallel", "parallel", "arbitrary"), side=side)
    return res[0] if side is None else (res[0][0], res[1])


def mm_nt(a3, b4, out_dtype, name, natural=False, a_natural=False, side=None):
    JK, JN, nb, kb = b4.shape
    M = a3.shape[0] if a_natural else a3.shape[1]
    tm = min(M, 512)
    tn = _tile(nb, 1024)
    ntn = nb // tn
    nk = JK

    def body(a_ref, b_ref, o_ref, acc_ref):
        prod = lax.dot_general(a_ref[...].astype(BF16), b_ref[...].astype(BF16), (((1,), (1,)), ((), ())),
                               preferred_element_type=F32)
        _accumulate(prod, o_ref, acc_ref, pl.program_id(2), nk)

    if a_natural:
        a_spec = pl.BlockSpec((tm, kb), lambda j, i, k: (i, k))
    else:
        a_spec = pl.BlockSpec((None, tm, kb), lambda j, i, k: (k, i, 0))
    b_spec = pl.BlockSpec((None, None, tn, kb), lambda j, i, k: (k, j // ntn, j % ntn, 0))
    if natural:
        out_shape = jax.ShapeDtypeStruct((M, JN * nb), out_dtype)
        o_spec = pl.BlockSpec((tm, tn), lambda j, i, k: (i, j))
    else:
        out_shape = jax.ShapeDtypeStruct((JN, M, nb), out_dtype)
        o_spec = pl.BlockSpec((None, tm, tn), lambda j, i, k: (j // ntn, i, j % ntn))
    res = _call(body, name=name, grid=(JN * ntn, M // tm, nk), in_specs=[a_spec, b_spec], out_specs=[o_spec],
                out_shape=[out_shape], scratch_shapes=[pltpu.VMEM((tm, tn), F32)], args=[a3, b4],
                semantics=("parallel", "parallel", "arbitrary"), side=side)
    return res[0] if side is None else (res[0][0], res[1])


def mm_tn(a3, b3, name, a_natural=False, b_natural=False, ja=None, jb=None, out_dtype=F32, side=None):
    if a_natural:
        M, JA = a3.shape[0], ja
        ka = a3.shape[1] // JA
    else:
        JA, M, ka = a3.shape
    if b_natural:
        JB = jb
        nb = b3.shape[1] // JB
    else:
        JB, _, nb = b3.shape
    tr = min(M, 512)
    tka = _tile(ka, 1024)
    tnb = _tile(nb, 1024)
    nta, ntb = ka // tka, nb // tnb
    nk = M // tr

    def body(a_ref, b_ref, o_ref, acc_ref):
        prod = lax.dot_general(a_ref[...].astype(BF16), b_ref[...].astype(BF16), (((0,), (0,)), ((), ())),
                               preferred_element_type=F32)
        _accumulate(prod, o_ref, acc_ref, pl.program_id(2), nk)

    if a_natural:
        a_spec = pl.BlockSpec((tr, tka), lambda i, j, k: (k, i))
    else:
        a_spec = pl.BlockSpec((None, tr, tka), lambda i, j, k: (i // nta, k, i % nta))
    if b_natural:
        b_spec = pl.BlockSpec((tr, tnb), lambda i, j, k: (k, j))
    else:
        b_spec = pl.BlockSpec((None, tr, tnb), lambda i, j, k: (j // ntb, k, j % ntb))
    o_spec = pl.BlockSpec((None, None, tka, tnb), lambda i, j, k: (i // nta, j // ntb, i % nta, j % ntb))
    res = _call(body, name=name, grid=(JA * nta, JB * ntb, nk), in_specs=[a_spec, b_spec], out_specs=[o_spec],
                out_shape=[jax.ShapeDtypeStruct((JA, JB, ka, nb), out_dtype)],
                scratch_shapes=[pltpu.VMEM((tka, tnb), F32)], args=[a3, b3],
                semantics=("parallel", "parallel", "arbitrary"), side=side)
    return res[0] if side is None else (res[0][0], res[1])


def _row_tile(S):
    return min(S, 256)


def _adaln(h, gain, shift, scale):
    y = h * lax.rsqrt(jnp.mean(h * h, axis=-1, keepdims=True) + NORM_EPS) * gain
    return y * (1.0 + scale) + shift


def adaln_fwd(h, gain, shift, scale, name):
    S, D = h.shape
    tm = _row_tile(S)

    def body(h_ref, g_ref, sh_ref, sc_ref, u_ref):
        u_ref[...] = _adaln(h_ref[...], g_ref[...], sh_ref[...], sc_ref[...]).astype(BF16)

    row = pl.BlockSpec((tm, D), lambda i: (i, 0))
    vec = pl.BlockSpec((1, D), lambda i: (0, 0))
    return pl.pallas_call(
        body, name=name, grid=(S // tm,), in_specs=[row, vec, vec, vec], out_specs=row,
        out_shape=jax.ShapeDtypeStruct((S, D), BF16), compiler_params=_params("parallel"),
    )(h, gain, shift, scale)


def adaln_bwd(h, gain, shift, scale, du, dres, name):
    S, D = h.shape
    tm = _row_tile(S)

    def body(h_ref, g_ref, sh_ref, sc_ref, du_ref, dres_ref, dh_ref, dg_ref, dsh_ref, dsc_ref):
        _, vjp = jax.vjp(_adaln, h_ref[...], g_ref[...], sh_ref[...], sc_ref[...])
        dh, dg, dsh, dsc = vjp(du_ref[...].astype(F32))
        dh_ref[...] = dres_ref[...] + dh

        @pl.when(pl.program_id(0) == 0)
        def _():
            dg_ref[...] = jnp.zeros_like(dg_ref)
            dsh_ref[...] = jnp.zeros_like(dsh_ref)
            dsc_ref[...] = jnp.zeros_like(dsc_ref)

        dg_ref[...] += dg
        dsh_ref[...] += dsh
        dsc_ref[...] += dsc

    row = pl.BlockSpec((tm, D), lambda i: (i, 0))
    vec = pl.BlockSpec((1, D), lambda i: (0, 0))
    vs = jax.ShapeDtypeStruct((1, D), F32)
    return pl.pallas_call(
        body, name=name, grid=(S // tm,), in_specs=[row, vec, vec, vec, row, row], out_specs=[row, vec, vec, vec],
        out_shape=[jax.ShapeDtypeStruct((S, D), F32), vs, vs, vs], compiler_params=_params("arbitrary"),
    )(h, gain, shift, scale, du, dres)


def resid_fwd(h, y, gate, coef, name):
    S, D = h.shape
    tm = _row_tile(S)

    def body(h_ref, y_ref, g_ref, o_ref):
        o_ref[...] = h_ref[...] + (coef * g_ref[...]) * y_ref[...]

    row = pl.BlockSpec((tm, D), lambda i: (i, 0))
    vec = pl.BlockSpec((1, D), lambda i: (0, 0))
    return pl.pallas_call(
        body, name=name, grid=(S // tm,), in_specs=[row, row, vec], out_specs=row,
        out_shape=jax.ShapeDtypeStruct((S, D), F32), compiler_params=_params("parallel"),
    )(h, y, gate)


def gate_bwd(dh, y, gate, coef, name):
    S, D = dh.shape
    tm = _row_tile(S)

    def body(dh_ref, y_ref, g_ref, dy_ref, dg_ref):
        dh = dh_ref[...]
        dy_ref[...] = ((coef * g_ref[...]) * dh).astype(BF16)

        @pl.when(pl.program_id(0) == 0)
        def _():
            dg_ref[...] = jnp.zeros_like(dg_ref)

        dg_ref[...] += coef * jnp.sum(dh * y_ref[...], axis=0, keepdims=True)

    row = pl.BlockSpec((tm, D), lambda i: (i, 0))
    vec = pl.BlockSpec((1, D), lambda i: (0, 0))
    return pl.pallas_call(
        body, name=name, grid=(S // tm,), in_specs=[row, row, vec], out_specs=[row, vec],
        out_shape=[jax.ShapeDtypeStruct((S, D), BF16), jax.ShapeDtypeStruct((1, D), F32)],
        compiler_params=_params("arbitrary"),
    )(dh, y, gate)


def _swiglu(a, b):
    return a * jax.nn.sigmoid(a) * b


def swiglu_fwd(ab3, name, side=None):
    J2, S, nb = ab3.shape
    J = J2 // 2
    tm = _row_tile(S)

    def body(a_ref, b_ref, o_ref):
        o_ref[...] = _swiglu(a_ref[...], b_ref[...]).astype(BF16)

    res = _call(body, name=name, grid=(J, S // tm),
                in_specs=[pl.BlockSpec((None, tm, nb), lambda j, i: (j, i, 0)),
                          pl.BlockSpec((None, tm, nb), lambda j, i: (j + J, i, 0))],
                out_specs=[pl.BlockSpec((None, tm, nb), lambda j, i: (j, i, 0))],
                out_shape=[jax.ShapeDtypeStruct((J, S, nb), BF16)], args=[ab3, ab3],
                semantics=("parallel", "parallel"), side=side)
    return res[0] if side is None else (res[0][0], res[1])


def swiglu_bwd(ab3, dh3, name, side=None):
    J2, S, nb = ab3.shape
    J = J2 // 2
    tm = _row_tile(S)

    def body(ab_ref, dh_ref, o_ref):
        _, vjp = jax.vjp(_swiglu, ab_ref[0], ab_ref[1])
        da, db = vjp(dh_ref[...].astype(F32))
        o_ref[0] = da.astype(BF16)
        o_ref[1] = db.astype(BF16)

    both = pl.BlockSpec((2, None, tm, nb), lambda j, i: (0, j, i, 0))
    res = _call(body, name=name, grid=(J, S // tm),
                in_specs=[both, pl.BlockSpec((None, tm, nb), lambda j, i: (j, i, 0))], out_specs=[both],
                out_shape=[jax.ShapeDtypeStruct((2, J, S, nb), BF16)], args=[ab3.reshape(2, J, S, nb), dh3],
                semantics=("parallel", "parallel"), side=side)
    if side is None:
        return res[0].reshape(J2, S, nb)
    return res[0][0].reshape(J2, S, nb), res[1]


def colsum(x, name):
    S, N = x.shape
    tm = _row_tile(S)

    def body(x_ref, o_ref):
        @pl.when(pl.program_id(0) == 0)
        def _():
            o_ref[...] = jnp.zeros_like(o_ref)

        o_ref[...] += jnp.sum(x_ref[...].astype(F32), axis=0, keepdims=True)

    return pl.pallas_call(
        body, name=name, grid=(S // tm,), in_specs=[pl.BlockSpec((tm, N), lambda i: (i, 0))],
        out_specs=pl.BlockSpec((1, N), lambda i: (0, 0)), out_shape=jax.ShapeDtypeStruct((1, N), F32),
        compiler_params=_params("arbitrary"),
    )(x)


def _final_loss(h, gain, target):
    y = h * lax.rsqrt(jnp.mean(h * h, axis=-1, keepdims=True) + NORM_EPS) * gain
    err = y - target
    return 0.5 * jnp.sum(jnp.mean(err * err, axis=-1))


def final_loss_grad(h, gain, target, name):
    S, D = h.shape
    tm = _row_tile(S)

    def body(h_ref, g_ref, t_ref, loss_ref, dh_ref, dg_ref):
        loss, (dh, dg) = jax.value_and_grad(_final_loss, argnums=(0, 1))(h_ref[...], g_ref[...], t_ref[...])
        dh_ref[...] = dh

        @pl.when(pl.program_id(0) == 0)
        def _():
            loss_ref[...] = jnp.zeros_like(loss_ref)
            dg_ref[...] = jnp.zeros_like(dg_ref)

        loss_ref[...] += jnp.full(loss_ref.shape, loss, F32)
        dg_ref[...] += dg

    row = pl.BlockSpec((tm, D), lambda i: (i, 0))
    vec = pl.BlockSpec((1, D), lambda i: (0, 0))
    return pl.pallas_call(
        body, name=name, grid=(S // tm,), in_specs=[row, vec, row],
        out_specs=[pl.BlockSpec((1, LANES), lambda i: (0, 0)), row, vec],
        out_shape=[jax.ShapeDtypeStruct((1, LANES), F32), jax.ShapeDtypeStruct((S, D), F32),
                   jax.ShapeDtypeStruct((1, D), F32)],
        compiler_params=_params("arbitrary"),
    )(h, gain, target)


def _chunk_consts():
    C = HGRN_CHUNK
    t = lax.broadcasted_iota(jnp.int32, (C, C), 0)
    s = lax.broadcasted_iota(jnp.int32, (C, C), 1)
    lower = (s <= t).astype(F32)
    to_mid = (s < C // 2).astype(F32)
    to_end = jnp.ones((C, C), F32)
    return lower, to_mid, to_end, s <= t


def _dot32(a, b):
    return jnp.dot(a, b, precision=lax.Precision.HIGHEST, preferred_element_type=F32)


def _hgrn_chunk(q_raw, f_raw, i_raw, lb, st):
    lower, to_mid, to_end, causal = _chunk_consts()
    forget = lb + (1.0 - lb) * jax.nn.sigmoid(f_raw)
    g = jnp.log(forget)
    kk = 1.0 - forget
    qs = q_raw * jax.nn.sigmoid(q_raw)
    b = _dot32(lower, g)
    bm = _dot32(to_mid, g)
    bl = _dot32(to_end, g)
    nt = (((1,), (1,)), ((), ()))
    tn = (((0,), (0,)), ((), ()))
    inter = lax.dot_general((qs * jnp.exp(b)).astype(BF16), st.astype(BF16), nt, preferred_element_type=F32)
    qt = (qs * jnp.exp(b - bm)).astype(BF16)
    kt = (kk * jnp.exp(bm - b)).astype(BF16)
    scores = lax.dot_general(qt, kt, nt, preferred_element_type=F32)
    scores = jnp.where(causal, scores, 0.0)
    vb = i_raw.astype(BF16)
    out = inter + jnp.dot(scores.astype(BF16), vb, preferred_element_type=F32)
    kdec = (kk * jnp.exp(bl - b)).astype(BF16)
    new_st = st * jnp.exp(bl[0:1, :]) + lax.dot_general(vb, kdec, tn, preferred_element_type=F32)
    return out, new_st


def hgrn_scan_fwd(proj, lb, name, side=None):
    S, D4 = proj.shape
    D = D4 // 4
    H = D // HGRN_HEAD
    C = HGRN_CHUNK
    R = min(S, 128)
    ncr = R // C

    def body(q_ref, f_ref, i_ref, lb_ref, o_ref, st_ref, state):
        @pl.when(pl.program_id(0) == 0)
        def _():
            state[...] = jnp.zeros_like(state)

        def chunk(cc, carry):
            rows = pl.ds(pl.multiple_of(cc * C, C), C)
            for h in range(H):
                ls = pl.ds(h * HGRN_HEAD, HGRN_HEAD)
                st = state[h]
                st_ref[cc, h] = st
                out, new_st = _hgrn_chunk(q_ref[rows, ls], f_ref[rows, ls], i_ref[rows, ls], lb_ref[:, ls], st)
                o_ref[rows, ls] = out
                state[h] = new_st
            return carry

        lax.fori_loop(0, ncr, chunk, 0)

    col = lambda j: pl.BlockSpec((R, D), lambda i: (i, j))
    res = _call(body, name=name, grid=(S // R,),
                in_specs=[col(0), col(1), col(2), pl.BlockSpec((1, D), lambda i: (0, 0))],
                out_specs=[pl.BlockSpec((R, D), lambda i: (i, 0)),
                           pl.BlockSpec((ncr, H, HGRN_HEAD, HGRN_HEAD), lambda i: (i, 0, 0, 0))],
                out_shape=[jax.ShapeDtypeStruct((S, D), F32),
                           jax.ShapeDtypeStruct((S // C, H, HGRN_HEAD, HGRN_HEAD), F32)],
                scratch_shapes=[pltpu.VMEM((H, HGRN_HEAD, HGRN_HEAD), F32)], args=[proj, proj, proj, lb],
                semantics=("arbitrary",), side=side)
    return res if side is None else (res[0], res[1])


def hgrn_scan_bwd(proj, lb, states, do, dg, name, side=None):
    S, D4 = proj.shape
    D = D4 // 4
    H = D // HGRN_HEAD
    C = HGRN_CHUNK
    R = min(S, 128)
    ncr = R // C
    ng = S // R

    def body(q_ref, f_ref, i_ref, lb_ref, st_ref, do_ref, dg_ref, dp_ref, dlb_ref, dstate):
        @pl.when(pl.program_id(0) == 0)
        def _():
            dstate[...] = jnp.zeros_like(dstate)
            dlb_ref[...] = jnp.zeros_like(dlb_ref)

        dp_ref[:, pl.ds(3 * D, D)] = dg_ref[...].astype(BF16)

        def chunk(t, carry):
            cc = ncr - 1 - t
            rows = pl.ds(pl.multiple_of(cc * C, C), C)
            for h in range(H):
                ls = pl.ds(h * HGRN_HEAD, HGRN_HEAD)
                _, vjp = jax.vjp(_hgrn_chunk, q_ref[rows, ls], f_ref[rows, ls], i_ref[rows, ls], lb_ref[:, ls],
                                 st_ref[cc, h])
                dq, df, di, dlb, dst = vjp((do_ref[rows, ls], dstate[h]))
                dp_ref[rows, pl.ds(h * HGRN_HEAD, HGRN_HEAD)] = dq.astype(BF16)
                dp_ref[rows, pl.ds(D + h * HGRN_HEAD, HGRN_HEAD)] = df.astype(BF16)
                dp_ref[rows, pl.ds(2 * D + h * HGRN_HEAD, HGRN_HEAD)] = di.astype(BF16)
                dlb_ref[:, ls] += dlb
                dstate[h] = dst
            return carry

        lax.fori_loop(0, ncr, chunk, 0)

    col = lambda j: pl.BlockSpec((R, D), lambda i: (ng - 1 - i, j))
    res = _call(body, name=name, grid=(ng,),
                in_specs=[col(0), col(1), col(2), pl.BlockSpec((1, D), lambda i: (0, 0)),
                          pl.BlockSpec((ncr, H, HGRN_HEAD, HGRN_HEAD), lambda i: (ng - 1 - i, 0, 0, 0)),
                          pl.BlockSpec((R, D), lambda i: (ng - 1 - i, 0)),
                          pl.BlockSpec((R, D), lambda i: (ng - 1 - i, 0))],
                out_specs=[pl.BlockSpec((R, D4), lambda i: (ng - 1 - i, 0)), pl.BlockSpec((1, D), lambda i: (0, 0))],
                out_shape=[jax.ShapeDtypeStruct((S, D4), BF16), jax.ShapeDtypeStruct((1, D), F32)],
                scratch_shapes=[pltpu.VMEM((H, HGRN_HEAD, HGRN_HEAD), F32)],
                args=[proj, proj, proj, lb, states, do, dg], semantics=("arbitrary",), side=side)
    return res if side is None else (res[0], res[1])


def _head_out(o, g, gain):
    y = o * lax.rsqrt(jnp.mean(o * o, axis=-1, keepdims=True) + NORM_EPS) * gain
    return y * jax.nn.sigmoid(g)


def hgrn_post_fwd(o, proj, gain, name):
    S, D = o.shape
    H = D // HGRN_HEAD
    tm = _row_tile(S)

    def body(o_ref, g_ref, gain_ref, z_ref):
        for h in range(H):
            ls = pl.ds(h * HGRN_HEAD, HGRN_HEAD)
            z_ref[:, ls] = _head_out(o_ref[:, ls], g_ref[:, ls], gain_ref[:, ls]).astype(BF16)

    row = pl.BlockSpec((tm, D), lambda i: (i, 0))
    return pl.pallas_call(
        body, name=name, grid=(S // tm,),
        in_specs=[row, pl.BlockSpec((tm, D), lambda i: (i, 3)), pl.BlockSpec((1, D), lambda i: (0, 0))],
        out_specs=row, out_shape=jax.ShapeDtypeStruct((S, D), BF16), compiler_params=_params("parallel"),
    )(o, proj, gain)


def hgrn_post_bwd(o, proj, gain, dz, name):
    S, D = o.shape
    H = D // HGRN_HEAD
    tm = _row_tile(S)

    def body(o_ref, g_ref, gain_ref, dz_ref, do_ref, dg_ref, dgain_ref):
        @pl.when(pl.program_id(0) == 0)
        def _():
            dgain_ref[...] = jnp.zeros_like(dgain_ref)

        for h in range(H):
            ls = pl.ds(h * HGRN_HEAD, HGRN_HEAD)
            _, vjp = jax.vjp(_head_out, o_ref[:, ls], g_ref[:, ls], gain_ref[:, ls])
            do, dg, dgain = vjp(dz_ref[:, ls].astype(F32))
            do_ref[:, ls] = do
            dg_ref[:, ls] = dg
            dgain_ref[:, ls] += dgain

    row = pl.BlockSpec((tm, D), lambda i: (i, 0))
    vec = pl.BlockSpec((1, D), lambda i: (0, 0))
    return pl.pallas_call(
        body, name=name, grid=(S // tm,),
        in_specs=[row, pl.BlockSpec((tm, D), lambda i: (i, 3)), vec, row], out_specs=[row, row, vec],
        out_shape=[jax.ShapeDtypeStruct((S, D), F32), jax.ShapeDtypeStruct((S, D), F32),
                   jax.ShapeDtypeStruct((1, D), F32)],
        compiler_params=_params("arbitrary"),
    )(o, proj, gain, dz)


def lower_bound_fwd(l0, l1, name):
    def body(a_ref, b_ref, o_ref):
        o_ref[...] = jax.nn.sigmoid(a_ref[...] - b_ref[...])

    return pl.pallas_call(body, name=name, out_shape=jax.ShapeDtypeStruct(l0.shape, F32))(l0, l1)


def lower_bound_bwd(l0, l1, dlb, name):
    def body(a_ref, b_ref, d_ref, o0_ref, o1_ref):
        s = jax.nn.sigmoid(a_ref[...] - b_ref[...])
        d0 = d_ref[...] * s * (1.0 - s)
        o0_ref[...] = d0
        o1_ref[...] = -d0

    sd = jax.ShapeDtypeStruct(l0.shape, F32)
    return pl.pallas_call(body, name=name, out_shape=[sd, sd])(l0, l1, dlb)


def _attn_tile(q, kp, kc, vp, vc, sink, first):
    W = ATT_WINDOW
    nt = (((1,), (1,)), ((), ()))
    qb = q.astype(BF16)
    scale = ATT_HEAD ** -0.5
    sp = lax.dot_general(qb, kp.astype(BF16), nt, preferred_element_type=F32) * scale
    sc = lax.dot_general(qb, kc.astype(BF16), nt, preferred_element_type=F32) * scale
    qi = lax.broadcasted_iota(jnp.int32, sp.shape, 0) & (W - 1)
    kj = lax.broadcasted_iota(jnp.int32, sp.shape, 1)
    sp = jnp.where((kj > qi) & jnp.logical_not(first), sp, NEG_INF)
    sc = jnp.where(kj <= qi, sc, NEG_INF)
    m = jnp.maximum(jnp.maximum(jnp.max(sp, axis=-1, keepdims=True), jnp.max(sc, axis=-1, keepdims=True)), sink)
    pp = jnp.exp(sp - m)
    pc = jnp.exp(sc - m)
    denom = jnp.sum(pp, axis=-1, keepdims=True) + jnp.sum(pc, axis=-1, keepdims=True) + jnp.exp(sink - m)
    out = jnp.dot((pp / denom).astype(BF16), vp.astype(BF16), preferred_element_type=F32)
    return out + jnp.dot((pc / denom).astype(BF16), vc.astype(BF16), preferred_element_type=F32)


def _attn_specs(G, W, Dh):
    q_spec = pl.BlockSpec((None, G, W, Dh), lambda j, n: (j, 0, n, 0))
    prev = pl.BlockSpec((None, W, Dh), lambda j, n: (j, jnp.maximum(n - 1, 0), 0))
    cur = pl.BlockSpec((None, W, Dh), lambda j, n: (j, n, 0))
    sink = pl.BlockSpec((None, G * W, 1), lambda j, n: (j, 0, 0))
    return q_spec, prev, cur, sink


def attn_fwd(q4, k3, v3, sink, name):
    NKV, G, S, Dh = q4.shape
    W = ATT_WINDOW

    def body(q_ref, kp_ref, kc_ref, vp_ref, vc_ref, s_ref, o_ref):
        first = pl.program_id(1) == 0
        out = _attn_tile(q_ref[...].reshape(G * W, Dh), kp_ref[...], kc_ref[...], vp_ref[...], vc_ref[...],
                         s_ref[...], first)
        o_ref[...] = out.reshape(G, W, Dh)

    q_spec, prev, cur, sk = _attn_specs(G, W, Dh)
    return pl.pallas_call(
        body, name=name, grid=(NKV, S // W), in_specs=[q_spec, prev, cur, prev, cur, sk], out_specs=q_spec,
        out_shape=jax.ShapeDtypeStruct(q4.shape, F32), compiler_params=_params("parallel", "parallel"),
    )(q4, k3, k3, v3, v3, sink)


def attn_bwd(q4, k3, v3, sink, do4, name, side=None):
    NKV, G, S, Dh = q4.shape
    W = ATT_WINDOW
    nb = S // W

    def body(q_ref, kp_ref, kc_ref, vp_ref, vc_ref, s_ref, do_ref, dq_ref, dkp_ref, dkc_ref, dvp_ref, dvc_ref,
             ds_ref):
        first = pl.program_id(1) == 0
        _, vjp = jax.vjp(functools.partial(_attn_tile, first=first), q_ref[...].reshape(G * W, Dh), kp_ref[...],
                         kc_ref[...], vp_ref[...], vc_ref[...], s_ref[...])
        dq, dkp, dkc, dvp, dvc, ds = vjp(do_ref[...].reshape(G * W, Dh))
        dq_ref[...] = dq.reshape(G, W, Dh)
        dkp_ref[...] = dkp
        dkc_ref[...] = dkc
        dvp_ref[...] = dvp
        dvc_ref[...] = dvc

        @pl.when(first)
        def _():
            ds_ref[...] = jnp.zeros_like(ds_ref)

        ds_ref[...] += jnp.sum(ds.reshape(G, W, 1), axis=1)

    q_spec, prev, cur, sk = _attn_specs(G, W, Dh)
    part = pl.BlockSpec((None, None, W, Dh), lambda j, n: (j, n, 0, 0))
    ps = jax.ShapeDtypeStruct((NKV, nb, W, Dh), F32)
    res = _call(body, name=name, grid=(NKV, nb), in_specs=[q_spec, prev, cur, prev, cur, sk, q_spec],
                out_specs=[q_spec, part, part, part, part, pl.BlockSpec((None, G, 1), lambda j, n: (j, 0, 0))],
                out_shape=[jax.ShapeDtypeStruct(q4.shape, F32), ps, ps, ps, ps,
                           jax.ShapeDtypeStruct((NKV, G, 1), F32)],
                args=[q4, k3, k3, v3, v3, sink, do4], semantics=("parallel", "arbitrary"), side=side)
    outs = res if side is None else res[0]
    ans = (outs[0], outs[1:5], outs[5])
    return ans if side is None else (ans, res[1])


def band_combine(cur, prev, name):
    NKV, nb, W, Dh = cur.shape

    def body(c_ref, p_ref, o_ref):
        keep = (pl.program_id(1) < nb - 1).astype(F32)
        o_ref[...] = c_ref[...] + keep * p_ref[...]

    return pl.pallas_call(
        body, name=name, grid=(NKV, nb),
        in_specs=[pl.BlockSpec((None, None, W, Dh), lambda j, n: (j, n, 0, 0)),
                  pl.BlockSpec((None, None, W, Dh), lambda j, n: (j, jnp.minimum(n + 1, nb - 1), 0, 0))],
        out_specs=pl.BlockSpec((None, W, Dh), lambda j, n: (j, n, 0)),
        out_shape=jax.ShapeDtypeStruct((NKV, nb * W, Dh), F32), compiler_params=_params("parallel", "parallel"),
    )(cur, prev)


def rope_tables(S):
    half = ROT_DIM // 2
    inv_freq = jnp.power(jnp.float32(ROPE_THETA), -jnp.arange(0, ROT_DIM, 2, dtype=F32) / ROT_DIM)
    ang = jnp.arange(S, dtype=F32)[:, None] * inv_freq[None, :]
    sin, cos = jnp.sin(ang), jnp.cos(ang)
    zeros = jnp.zeros((S, ATT_HEAD - ROT_DIM), F32)
    z8 = jnp.zeros((S, half), F32)
    cfull = jnp.concatenate([cos, cos, jnp.ones((S, ATT_HEAD - ROT_DIM), F32)], axis=1)
    s_next = jnp.concatenate([-sin, z8, zeros], axis=1)
    s_prev = jnp.concatenate([z8, sin, zeros], axis=1)
    two = lambda t: jnp.concatenate([t, t], axis=1)
    return two(cfull), two(s_next), two(s_prev)


def rope(x, tables, sign, name):
    S, Wd = x.shape
    tm = _row_tile(S)
    rep = Wd // LANES
    half = ROT_DIM // 2

    def body(x_ref, c_ref, sn_ref, sp_ref, o_ref):
        xv = x_ref[...]
        c = jnp.tile(c_ref[...], (1, rep))
        sn = jnp.tile(sn_ref[...], (1, rep))
        sp = jnp.tile(sp_ref[...], (1, rep))
        if sign > 0:
            nxt = pltpu.roll(xv, Wd - half, 1)
            prv = pltpu.roll(xv, half, 1)
            o_ref[...] = xv * c + nxt * sn + prv * sp
        else:
            o_ref[...] = xv * c + pltpu.roll(xv * sn, half, 1) + pltpu.roll(xv * sp, Wd - half, 1)

    row = pl.BlockSpec((tm, Wd), lambda i: (i, 0))
    tab = pl.BlockSpec((tm, LANES), lambda i: (i, 0))
    return pl.pallas_call(
        body, name=name, grid=(S // tm,), in_specs=[row, tab, tab, tab], out_specs=row,
        out_shape=jax.ShapeDtypeStruct((S, Wd), F32), compiler_params=_params("parallel"),
    )(x, *tables)


def cond_proj(c_all, w, bias, name):
    B, D = c_all.shape
    N = w.shape[1]
    tn = _tile(N, 512)

    def body(c_ref, w_ref, b_ref, o_ref):
        cv = c_ref[...]
        cs = (cv * jax.nn.sigmoid(cv)).astype(BF16)
        o_ref[...] = jnp.dot(cs, w_ref[...].astype(BF16), preferred_element_type=F32) + b_ref[...]

    return pl.pallas_call(
        body, name=name, grid=(N // tn,),
        in_specs=[pl.BlockSpec((B, D), lambda j: (0, 0)), pl.BlockSpec((D, tn), lambda j: (0, j)),
                  pl.BlockSpec((1, tn), lambda j: (0, j))],
        out_specs=pl.BlockSpec((B, tn), lambda j: (0, j)), out_shape=jax.ShapeDtypeStruct((B, N), F32),
        compiler_params=_params("parallel"),
    )(c_all, w, bias)


def cond_grad(c_all, dmod, name):
    B, D = c_all.shape
    N = dmod.shape[1]
    tn = _tile(N, 512)

    def body(c_ref, d_ref, o_ref):
        cv = c_ref[...]
        cs = (cv * jax.nn.sigmoid(cv)).astype(BF16)
        o_ref[...] = lax.dot_general(cs, d_ref[...].astype(BF16), (((0,), (0,)), ((), ())),
                                     preferred_element_type=F32)

    return pl.pallas_call(
        body, name=name, grid=(N // tn,),
        in_specs=[pl.BlockSpec((B, D), lambda j: (0, 0)), pl.BlockSpec((B, tn), lambda j: (0, j))],
        out_specs=pl.BlockSpec((D, tn), lambda j: (0, j)), out_shape=jax.ShapeDtypeStruct((D, N), F32),
        compiler_params=_params("parallel"),
    )(c_all, dmod)


def rowsum(g, name):
    B, N = g.shape
    tn = _tile(N, 8192)

    def body(g_ref, o_ref):
        acc = g_ref[0:1, :]
        for r in range(1, B):
            acc = acc + g_ref[r:r + 1, :]
        o_ref[...] = acc

    return pl.pallas_call(
        body, name=name, grid=(N // tn,), in_specs=[pl.BlockSpec((B, tn), lambda j: (0, j))],
        out_specs=pl.BlockSpec((1, tn), lambda j: (0, j)), out_shape=jax.ShapeDtypeStruct((1, N), F32),
        compiler_params=_params("parallel"),
    )(g)


def _adam_rows(R, C):
    if R * C * 4 <= (1 << 20) or R % 8:
        return R
    best = 8
    for t in range(8, R + 1, 8):
        if R % t == 0 and t * C * 4 <= (1 << 20):
            best = t
    return best


def adamw(w3, m3, v3, j, g3, name, into=None):
    n, R, C = w3.shape
    P = g3.shape[0]
    tr = _adam_rows(R, C * max(P // 2, 1))

    def body(*refs):
        w_ref, m_ref, v_ref, g_ref = refs[:4]
        go_ref, d_ref, mo_ref, vo_ref = refs[-4:]
        g = g_ref[0].astype(F32)
        for p in range(1, P):
            g = g + g_ref[p].astype(F32)
        mn = ADAM_B1 * m_ref[...] + (1.0 - ADAM_B1) * g
        vn = ADAM_B2 * v_ref[...] + (1.0 - ADAM_B2) * jnp.square(g)
        m_hat = mn / (1.0 - ADAM_B1 ** ADAM_STEP)
        v_hat = vn / (1.0 - ADAM_B2 ** ADAM_STEP)
        go_ref[...] = g
        d_ref[...] = -ADAM_LR * (m_hat / (jnp.sqrt(v_hat) + ADAM_EPS) + ADAM_WD * w_ref[...])
        mo_ref[...] = mn
        vo_ref[...] = vn

    spec = pl.BlockSpec((None, tr, C), lambda i: (j, i, 0))
    g_spec = pl.BlockSpec((P, tr, C), lambda i: (0, i, 0))
    sd = jax.ShapeDtypeStruct((n, R, C), F32)
    in_specs, args, aliases = [spec, spec, spec, g_spec], [w3, m3, v3, g3], {}
    if into is not None:
        in_specs += [_ANY] * 4
        args += list(into)
        aliases = {4 + k: k for k in range(4)}
    return pl.pallas_call(
        body, name=name, grid=(R // tr,), in_specs=in_specs, out_specs=[spec] * 4, out_shape=[sd] * 4,
        input_output_aliases=aliases, compiler_params=_params("parallel"),
    )(*args)


def _place():
    return lax.axis_index("x"), lax.axis_index("y"), lax.axis_index("c")


def _slot(p):
    return 4 * p[0] + 2 * p[1] + p[2]


def gather_side(xs):
    n = len(xs)

    def copy(ins, outs, sems, t, k, block, to, from_input=False):
        dst = outs[t].at[_slot(block)]
        return pltpu.make_async_remote_copy(
            src_ref=ins[t] if from_input else dst, dst_ref=dst, send_sem=sems[0].at[t, k],
            recv_sem=sems[1].at[t, k], device_id=to, device_id_type=MESH)

    def peers():
        x, y, c = _place()
        return (x, y, c), (x, y, 1 - c), [(1 - x, y), (x, 1 - y), (1 - x, 1 - y)]

    def start(ins, outs, sems):
        me, sibling, chips = peers()
        c = me[2]
        for t in range(n):
            pltpu.make_async_copy(ins[t], outs[t].at[_slot(me)], sems[2].at[t]).start()
            copy(ins, outs, sems, t, 0, me, sibling, True).start()
            for j, chip in enumerate(chips):
                copy(ins, outs, sems, t, 1 + j, me, (*chip, c), True).start()

    def finish(ins, outs, sems):
        me, sibling, chips = peers()
        c = me[2]
        for t in range(n):
            for j, chip in enumerate(chips):
                copy(ins, outs, sems, t, 1 + j, (*chip, c), me).wait_recv()
                copy(ins, outs, sems, t, 4 + j, (*chip, c), sibling).start()
        for t in range(n):
            copy(ins, outs, sems, t, 0, sibling, me).wait_recv()
            for j, chip in enumerate(chips):
                copy(ins, outs, sems, t, 4 + j, (*chip, 1 - c), me).wait_recv()
        for t in range(n):
            copy(ins, outs, sems, t, 0, me, sibling, True).wait_send()
            for j, chip in enumerate(chips):
                copy(ins, outs, sems, t, 1 + j, me, (*chip, c), True).wait_send()
                copy(ins, outs, sems, t, 4 + j, (*chip, c), sibling).wait_send()
            pltpu.make_async_copy(ins[t], outs[t].at[_slot(me)], sems[2].at[t]).wait()

    return _Side(xs, [jax.ShapeDtypeStruct((N_DEV,) + a.shape, a.dtype) for a in xs],
                 [pltpu.SemaphoreType.DMA((n, 7)), pltpu.SemaphoreType.DMA((n, 7)), pltpu.SemaphoreType.DMA((n,))],
                 start, finish)


def scatter_side(gs):
    n = len(gs)

    def copies(ins, outs, sems):
        x, y, c = _place()
        me = (x, y, c)
        out = []
        for t in range(n):
            out.append(pltpu.make_async_copy(ins[t].at[_slot(me)], outs[t].at[_slot(me)], sems[2].at[t]))
            for r in range(1, N_DEV):
                peer = (1 - x if r & 4 else x, 1 - y if r & 2 else y, 1 - c if r & 1 else c)
                out.append(pltpu.make_async_remote_copy(
                    src_ref=ins[t].at[_slot(peer)], dst_ref=outs[t].at[_slot(me)], send_sem=sems[0].at[t, r - 1],
                    recv_sem=sems[1].at[t, r - 1], device_id=peer, device_id_type=MESH))
        return out

    def start(ins, outs, sems):
        for cp in copies(ins, outs, sems):
            cp.start()

    def finish(ins, outs, sems):
        for cp in copies(ins, outs, sems):
            cp.wait()

    return _Side(gs, [jax.ShapeDtypeStruct(g.shape, g.dtype) for g in gs],
                 [pltpu.SemaphoreType.DMA((n, 7)), pltpu.SemaphoreType.DMA((n, 7)), pltpu.SemaphoreType.DMA((n,))],
                 start, finish)


def exchange(side, name):
    def body(*refs):
        n_in, n_out = len(side.inputs), len(side.out_shapes)
        ins, outs, sems = refs[:n_in], refs[n_in:n_in + n_out], refs[n_in + n_out:]
        side.start(ins, outs, sems)
        side.finish(ins, outs, sems)

    return pl.pallas_call(
        body, name=name, in_specs=[_ANY] * len(side.inputs), out_specs=[_ANY] * len(side.out_shapes),
        out_shape=side.out_shapes, scratch_shapes=side.sem_shapes,
    )(*side.inputs)


class _Plan:
    def __init__(self, plan, make_side):
        self.plan, self.make_side, self.source, self.got = plan, make_side, {}, {}

    def run(self, fn, *args, name, **kw):
        keys = self.plan.get(name)
        if not keys:
            return fn(*args, name=name, **kw)
        result, outs = fn(*args, name=name, side=self.make_side([self.source[k] for k in keys]), **kw)
        self.got.update(zip(keys, outs))
        return result


GATHER_PLAN = {
    "l0s0_in": [("out", 0, 0), "hin"],
    "l0s0_out": ["hout", "kv"],
    "l0s1_scan": [("in", 0, 1), ("out", 0, 1), ("in", 1, 0)],
    "l0s2_in": [("out", 1, 0), "q", "o"],
    "l1s0_in": [("in", 1, 1)],
    "l1s0_out": [("out", 1, 1)],
}
GATHER_FIRST = [("in", 0, 0)]
SCATTER_PLAN = {
    "l1s2_dwin": [("out", 1, 1)],
    "l1s1_dattn": [("in", 1, 1)],
    "l1s0_dact": ["o"],
    "l1s0_dab": ["q"],
    "l1s0_dwin": [("out", 1, 0)],
    "l0s2_dact": ["kv"],
    "l0s2_dwin": [("out", 0, 1)],
    "l0s1_dscan": [("in", 1, 0), ("in", 0, 1), "hout"],
    "l0s0_dab": ["hin"],
    "l0s0_dwin": [("out", 0, 0)],
    "l0s0_du": [("in", 0, 0)],
}


def _ffn_fwd(gp, W, u, l, i, tag):
    ab3 = gp.run(mm_nn, u[None], W[("in", l, i)][None], F32, name=tag + "_in")
    h3 = gp.run(swiglu_fwd, ab3, name=tag + "_act")
    w_out = W[("out", l, i)]
    J = h3.shape[0]
    y = gp.run(mm_nn, h3, w_out.reshape(J, 1, -1, w_out.shape[-1]), F32, name=tag + "_out", natural=True)
    return y, (ab3, h3)


def _ffn_bwd(sp, W, dy, u, ab3, h3, l, i, tag):
    w_in, w_out = W[("in", l, i)], W[("out", l, i)]
    J = h3.shape[0]
    D = w_out.shape[-1]
    dh3 = sp.run(mm_nt, dy[None], w_out.reshape(1, J, -1, D), F32, name=tag + "_dact")
    dab3 = sp.run(swiglu_bwd, ab3, dh3, name=tag + "_dab")
    sp.source[("out", l, i)] = sp.run(mm_tn, h3, dy[None], name=tag + "_dwout", out_dtype=BF16).reshape(w_out.shape)
    sp.source[("in", l, i)] = sp.run(mm_tn, u[None], dab3, name=tag + "_dwin", out_dtype=BF16).reshape(w_in.shape)
    return sp.run(mm_nt, dab3, w_in[:, None], F32, name=tag + "_du", natural=True)


def kernel(x, c, norm_gain, w_ada, b_ada, w_ffn_in, w_ffn_out, w_hgrn_in, hgrn_lb_logits, hgrn_head_gain, w_hgrn_out, kv_gain, w_ada_kv, b_ada_kv, w_kv, b_kv, w_q, b_q, attn_sinks, w_attn_out, final_gain, loss_target, m_norm_gain, m_w_ada, m_b_ada, m_w_ffn_in, m_w_ffn_out, m_w_hgrn_in, m_hgrn_lb_logits, m_hgrn_head_gain, m_w_hgrn_out, m_kv_gain, m_w_ada_kv, m_b_ada_kv, m_w_kv, m_b_kv, m_w_q, m_b_q, m_attn_sinks, m_w_attn_out, m_final_gain, v_norm_gain, v_w_ada, v_b_ada, v_w_ffn_in, v_w_ffn_out, v_w_hgrn_in, v_hgrn_lb_logits, v_hgrn_head_gain, v_w_hgrn_out, v_kv_gain, v_w_ada_kv, v_b_ada_kv, v_w_kv, v_b_kv, v_w_q, v_b_q, v_attn_sinks, v_w_attn_out, v_final_gain):
    xi, yi, ci = _place()
    me = 4 * xi + 2 * yi + ci
    _, S, D = x.shape
    L = norm_gain.shape[0]
    dsh = D // N_DEV
    ada_n = w_ada.shape[2]
    kv_n = w_ada_kv.shape[1]
    NQ = D // ATT_HEAD
    NKV = NQ // ATT_GROUP
    kvd = NKV * ATT_HEAD
    h0 = x[0]
    target = loss_target[0]

    def my_cols(a, n):
        return lax.dynamic_slice_in_dim(a, me * n, n, axis=a.ndim - 1)

    gp = _Plan(GATHER_PLAN, gather_side)
    bf = lambda a: a.astype(BF16)
    for l in range(L):
        for i in range(2):
            gp.source[("in", l, i)] = bf(w_ffn_in[l, i])
            gp.source[("out", l, i)] = bf(w_ffn_out[l, i])
    gp.source.update(hin=bf(w_hgrn_in[0]), hout=bf(w_hgrn_out[0]), kv=bf(w_kv), q=bf(w_q[0]), o=bf(w_attn_out[0]))
    W = gp.got
    W.update(zip(GATHER_FIRST, exchange(gather_side([gp.source[k] for k in GATHER_FIRST]), "gather_first")))
    full = lambda w: w.reshape(1, 1, -1, w.shape[-1])

    lb_sh = lower_bound_fwd(hgrn_lb_logits[0:1], hgrn_lb_logits[1:2], "lb_fwd")
    small = jnp.concatenate([c, norm_gain.reshape(1, L * 3 * dsh), hgrn_head_gain, lb_sh], axis=1)
    (g1,) = exchange(gather_side([small]), "gather_cond")
    g1 = g1.reshape(N_DEV, -1)
    c_all = g1[:, :D]
    gains = g1[:, D:D + L * 3 * dsh].reshape(N_DEV, L * 3, dsh).transpose(1, 0, 2).reshape(L, 3, 1, D)
    head_gain = g1[:, D + L * 3 * dsh:D + (L * 3 + 1) * dsh].reshape(1, D)
    lb0 = g1[:, D + (L * 3 + 1) * dsh:].reshape(1, D)

    parts = [cond_proj(c_all, w_ada[l], my_cols(b_ada[l:l + 1], ada_n), f"mod{l}") for l in range(L)]
    parts.append(cond_proj(c_all, w_ada_kv, my_cols(b_ada_kv[None], kv_n), "mod_kv"))
    (g2,) = exchange(gather_side([jnp.concatenate(parts, axis=1)]), "gather_mod")
    mine2 = lax.dynamic_index_in_dim(g2, me, axis=1, keepdims=False)
    mod = [mine2[:, l * ada_n:(l + 1) * ada_n].reshape(3, 3, 1, D) for l in range(L)]
    mod_kv = mine2[:, L * ada_n:].reshape(2, 1, D)

    tables = rope_tables(S)
    sink_col = jnp.broadcast_to(attn_sinks.reshape(NKV, ATT_GROUP, 1, 1), (NKV, ATT_GROUP, ATT_WINDOW, 1))
    sink_col = sink_col.reshape(NKV, ATT_GROUP * ATT_WINDOW, 1)

    def to_heads(t, n):
        return t.reshape(S, n, ATT_HEAD).transpose(1, 0, 2)

    def from_heads(t):
        return t.transpose(1, 0, 2).reshape(S, -1)

    h = h0
    saved = {}
    for l in range(L):
        for s in (0, 1, 2):
            tag = f"l{l}s{s}"
            shift, scale, gate = mod[l][s, 0], mod[l][s, 1], mod[l][s, 2]
            u = adaln_fwd(h, gains[l, s], shift, scale, tag + "_norm")
            if s != 1:
                y, res = _ffn_fwd(gp, W, u, l, s // 2, tag)
                coef = 0.5
            elif l == 0:
                proj = mm_nn(u[None], W["hin"][None], F32, tag + "_proj", natural=True)
                o, states = gp.run(hgrn_scan_fwd, proj, lb0, name=tag + "_scan")
                z = hgrn_post_fwd(o, proj, head_gain, tag + "_post")
                y = mm_nn(z[None], full(W["hout"]), F32, tag + "_out", natural=True)
                res = (proj, o, states, z)
                coef = 1.0
            else:
                q = mm_nn(u[None], full(W["q"]), F32, tag + "_q", natural=True, bias=b_q)
                q4 = to_heads(rope(q, tables, 1, tag + "_rope"), NQ).reshape(NKV, ATT_GROUP, S, ATT_HEAD)
                att4 = attn_fwd(q4, k3, v3, sink_col, tag + "_attn")
                att = from_heads(att4.reshape(NQ, S, ATT_HEAD))
                y = mm_nn(att[None], full(W["o"]), F32, tag + "_out", natural=True)
                res = (q4, att)
                coef = 1.0
            saved[(l, s)] = (h, u, y, res)
            h = resid_fwd(h, y, gate, coef, tag + "_res")
        if l == 0:
            h_kv = h
            u_kv = adaln_fwd(h, kv_gain[None], mod_kv[0], mod_kv[1], "kv_norm")
            kvp = mm_nn(u_kv[None], full(W["kv"]), F32, "kv_proj", natural=True, bias=b_kv[None])
            k3 = to_heads(rope(kvp[:, :kvd], tables, 1, "kv_rope"), NKV)
            v3 = to_heads(kvp[:, kvd:], NKV)

    loss_row, dh, d_final_gain = final_loss_grad(h, final_gain[None], target, "final")
    loss = lax.psum(loss_row[0, 0], ("x", "y", "c"))

    sp = _Plan(SCATTER_PLAN, scatter_side)
    d_mod = [[None] * 3 for _ in range(L)]
    d_gain = [[None] * 3 for _ in range(L)]
    for l in reversed(range(L)):
        if l == 0:
            dkv = jnp.concatenate([rope(from_heads(dk3), tables, -1, "kv_drope"), from_heads(dv3)], axis=1)
            sp.source["kv"] = mm_tn(u_kv[None], dkv[None], "kv_dw", out_dtype=BF16).reshape(W["kv"].shape)
            db_kv = colsum(dkv, "kv_db")
            du_kv = mm_nt(dkv[None], full(W["kv"]), F32, "kv_du", natural=True)
            dh, d_kv_gain, d_kv_shift, d_kv_scale = adaln_bwd(h_kv, kv_gain[None], mod_kv[0], mod_kv[1], du_kv, dh,
                                                              "kv_dnorm")
        for s in (2, 1, 0):
            tag = f"l{l}s{s}"
            shift, scale, gate = mod[l][s, 0], mod[l][s, 1], mod[l][s, 2]
            h_in, u, y, res = saved[(l, s)]
            dy, d_gate = gate_bwd(dh, y, gate, 0.5 if s != 1 else 1.0, tag + "_dres")
            if s != 1:
                du = _ffn_bwd(sp, W, dy, u, res[0], res[1], l, s // 2, tag)
            elif l == 0:
                proj, o, states, z = res
                sp.source["hout"] = mm_tn(z[None], dy[None], tag + "_dwout", out_dtype=BF16).reshape(W["hout"].shape)
                dz = mm_nt(dy[None], full(W["hout"]), F32, tag + "_dz", natural=True)
                do, dg, d_head_gain = hgrn_post_bwd(o, proj, head_gain, dz, tag + "_dpost")
                dproj, d_lb0 = sp.run(hgrn_scan_bwd, proj, lb0, states, do, dg, name=tag + "_dscan")
                sp.source["hin"] = mm_tn(u[None], dproj, tag + "_dwin", b_natural=True, jb=N_DEV,
                                         out_dtype=BF16).reshape(W["hin"].shape)
                du = mm_nt(dproj, W["hin"][:, None], F32, tag + "_du", natural=True, a_natural=True)
            else:
                q4, att = res
                sp.source["o"] = mm_tn(att[None], dy[None], tag + "_dwout", out_dtype=BF16).reshape(W["o"].shape)
                datt = mm_nt(dy[None], full(W["o"]), F32, tag + "_datt", natural=True)
                datt4 = to_heads(datt, NQ).reshape(NKV, ATT_GROUP, S, ATT_HEAD)
                dq4, (dkp, dkc, dvp, dvc), d_sink = sp.run(attn_bwd, q4, k3, v3, sink_col, datt4, name=tag + "_dattn")
                dk3 = band_combine(dkc, dkp, tag + "_dk")
                dv3 = band_combine(dvc, dvp, tag + "_dv")
                dq = rope(from_heads(dq4.reshape(NQ, S, ATT_HEAD)), tables, -1, tag + "_drope")
                sp.source["q"] = mm_tn(u[None], dq[None], tag + "_dwq", out_dtype=BF16).reshape(W["q"].shape)
                db_q = colsum(dq, tag + "_dbq")
                du = mm_nt(dq[None], full(W["q"]), F32, tag + "_du", natural=True)
            dh, dg_, dsh_, dsc_ = adaln_bwd(h_in, gains[l, s], shift, scale, du, dh, tag + "_dnorm")
            d_gain[l][s] = dg_
            d_mod[l][s] = jnp.concatenate([dsh_, dsc_, d_gate], axis=1)
    grad_x = dh[None]
    G = sp.got

    pad = lambda a, n: jnp.pad(a, ((0, 0), (0, n - a.shape[1])))
    pieces = [jnp.concatenate(d_mod[l], axis=1) for l in range(L)]
    pieces += [d_kv_shift, d_kv_scale]
    pieces += [d_gain[l][s] for l in range(L) for s in range(3)]
    pieces += [d_head_gain, d_lb0, d_kv_gain, db_kv, db_q, pad(d_sink.reshape(1, NQ), LANES), d_final_gain]
    (g3,) = exchange(gather_side([jnp.concatenate(pieces, axis=1)]), "gather_small_grads")
    g3 = g3.reshape(N_DEV, -1)
    tot = rowsum(g3, "sum_small_grads")
    offs = [0]
    for p in pieces:
        offs.append(offs[-1] + p.shape[1])
    seg = lambda k: tot[:, offs[k]:offs[k + 1]]
    k0 = 0
    g_b_ada = jnp.concatenate([seg(l) for l in range(L)], axis=0)
    k0 += L
    g_b_ada_kv = jnp.concatenate([seg(k0), seg(k0 + 1)], axis=1)
    k0 += 2
    g_norm_gain = jnp.concatenate([my_cols(seg(k0 + j), dsh) for j in range(3 * L)], axis=0)
    k0 += 3 * L
    g_head_gain = my_cols(seg(k0), dsh)
    d_lb_sh = my_cols(seg(k0 + 1), dsh)
    g_kv_gain = seg(k0 + 2)
    g_b_kv = seg(k0 + 3)
    g_b_q = seg(k0 + 4)
    g_sinks = seg(k0 + 5)[:, :NQ]
    g_final_gain = seg(k0 + 6)
    dl0, dl1 = lower_bound_bwd(hgrn_lb_logits[0:1], hgrn_lb_logits[1:2], d_lb_sh, "lb_bwd")
    g_lb_logits = jnp.concatenate([dl0, dl1], axis=0)

    g_w_ada = jnp.stack([cond_grad(c_all, lax.dynamic_slice_in_dim(g3, offs[l] + me * ada_n, ada_n, axis=1),
                                   f"dw_ada{l}") for l in range(L)])
    g_w_ada_kv = cond_grad(c_all, lax.dynamic_slice_in_dim(g3, offs[L] + me * kv_n, kv_n, axis=1), "dw_ada_kv")

    def update(w, m, v, grads, name):
        n = len(grads)
        three = lambda a: a.reshape((n, -1, w.shape[-1]))
        outs = None
        for j, g in enumerate(grads):
            outs = adamw(three(w), three(m), three(v), j, g, f"{name}{j}", into=outs)
        return [o.reshape(w.shape) for o in outs]

    def one(w, m, v, g, name):
        return update(w, m, v, [g.reshape((1, -1, w.shape[-1]))], name)

    res = {}
    res["norm_gain"] = one(norm_gain, m_norm_gain, v_norm_gain, g_norm_gain, "adam_norm_gain")
    res["w_ada"] = one(w_ada, m_w_ada, v_w_ada, g_w_ada, "adam_w_ada")
    res["b_ada"] = one(b_ada, m_b_ada, v_b_ada, g_b_ada, "adam_b_ada")
    ffn_keys = lambda kind: [G[(kind, l, i)] for l in range(L) for i in range(2)]
    res["w_ffn_in"] = update(w_ffn_in, m_w_ffn_in, v_w_ffn_in, ffn_keys("in"), "adam_w_ffn_in")
    res["w_ffn_out"] = update(w_ffn_out, m_w_ffn_out, v_w_ffn_out, ffn_keys("out"), "adam_w_ffn_out")
    res["w_hgrn_in"] = update(w_hgrn_in, m_w_hgrn_in, v_w_hgrn_in, [G["hin"]], "adam_w_hgrn_in")
    res["hgrn_lb_logits"] = one(hgrn_lb_logits, m_hgrn_lb_logits, v_hgrn_lb_logits, g_lb_logits, "adam_lb")
    res["hgrn_head_gain"] = one(hgrn_head_gain, m_hgrn_head_gain, v_hgrn_head_gain, g_head_gain, "adam_head_gain")
    res["w_hgrn_out"] = update(w_hgrn_out, m_w_hgrn_out, v_w_hgrn_out, [G["hout"]], "adam_w_hgrn_out")
    res["kv_gain"] = one(kv_gain, m_kv_gain, v_kv_gain, g_kv_gain, "adam_kv_gain")
    res["w_ada_kv"] = one(w_ada_kv, m_w_ada_kv, v_w_ada_kv, g_w_ada_kv, "adam_w_ada_kv")
    res["b_ada_kv"] = one(b_ada_kv, m_b_ada_kv, v_b_ada_kv, g_b_ada_kv, "adam_b_ada_kv")
    res["w_kv"] = update(w_kv, m_w_kv, v_w_kv, [G["kv"]], "adam_w_kv")
    res["b_kv"] = one(b_kv, m_b_kv, v_b_kv, g_b_kv, "adam_b_kv")
    res["w_q"] = update(w_q, m_w_q, v_w_q, [G["q"]], "adam_w_q")
    res["b_q"] = one(b_q, m_b_q, v_b_q, g_b_q, "adam_b_q")
    res["attn_sinks"] = one(attn_sinks, m_attn_sinks, v_attn_sinks, g_sinks, "adam_sinks")
    res["w_attn_out"] = update(w_attn_out, m_w_attn_out, v_w_attn_out, [G["o"]], "adam_w_attn_out")
    res["final_gain"] = one(final_gain, m_final_gain, v_final_gain, g_final_gain, "adam_final_gain")

    names = ["norm_gain", "w_ada", "b_ada", "w_ffn_in", "w_ffn_out", "w_hgrn_in", "hgrn_lb_logits", "hgrn_head_gain",
             "w_hgrn_out", "kv_gain", "w_ada_kv", "b_ada_kv", "w_kv", "b_kv", "w_q", "b_q", "attn_sinks", "w_attn_out",
             "final_gain"]
    return (loss, grad_x, *[res[n][0] for n in names], *[res[n][1] for n in names], *[res[n][2] for n in names],
            *[res[n][3] for n in names])
```

```python
import functools

import jax
import jax.numpy as jnp
from jax import lax
from jax.experimental import pallas as pl
from jax.experimental.pallas import tpu as pltpu

F32 = jnp.float32
BF16 = jnp.bfloat16
MESH = pl.DeviceIdType.MESH

N_DEV = 8
V7X_VMEM_LIMIT_BYTES = 56 * 1024 * 1024
LANES = 128

NORM_EPS = 1e-6
NEG_INF = -1e30
HGRN_CHUNK = 32
HGRN_HEAD = 128
ATT_HEAD = 64
ATT_WINDOW = 128
ATT_GROUP = 8
ROT_DIM = 16
ROPE_THETA = 500000.0

ADAM_LR = 0.001
ADAM_B1 = 0.9
ADAM_B2 = 0.999
ADAM_EPS = 1e-08
ADAM_WD = 0.01
ADAM_STEP = 10


def _params(*sem):
    return pltpu.CompilerParams(dimension_semantics=sem, vmem_limit_bytes=V7X_VMEM_LIMIT_BYTES)


def _tile(n, pref, unit=LANES):
    t = (min(n, pref) // unit) * unit
    while t >= unit:
        if n % t == 0:
            return t
        t -= unit
    return n


_ANY = pl.BlockSpec(memory_space=pl.ANY)


class _Side:
    def __init__(self, inputs, out_shapes, sem_shapes, start, finish):
        self.inputs, self.out_shapes, self.sem_shapes = list(inputs), list(out_shapes), list(sem_shapes)
        self.start, self.finish = start, finish


def _call(body, *, name, grid, in_specs, out_specs, out_shape, args, semantics, scratch_shapes=(), side=None):
    in_specs, out_specs, out_shape = list(in_specs), list(out_specs), list(out_shape)
    scratch_shapes = list(scratch_shapes)
    if side is None:
        outs = pl.pallas_call(
            body, name=name, grid=grid, in_specs=in_specs, out_specs=out_specs, out_shape=out_shape,
            scratch_shapes=scratch_shapes, compiler_params=_params(*semantics))(*args)
        return list(outs)
    n_in, n_out, n_scr = len(in_specs), len(out_specs), len(scratch_shapes)
    s_in, s_out = len(side.inputs), len(side.out_shapes)

    def carried(*refs):
        ins, refs = refs[:n_in], refs[n_in:]
        side_ins, refs = refs[:s_in], refs[s_in:]
        outs, refs = refs[:n_out], refs[n_out:]
        side_outs, refs = refs[:s_out], refs[s_out:]
        scratch, sems = refs[:n_scr], refs[n_scr:]
        first = pl.program_id(0) == 0
        last = pl.program_id(0) == grid[0] - 1
        for d in range(1, len(grid)):
            first = jnp.logical_and(first, pl.program_id(d) == 0)
            last = jnp.logical_and(last, pl.program_id(d) == grid[d] - 1)

        @pl.when(first)
        def _():
            side.start(side_ins, side_outs, sems)

        body(*ins, *outs, *scratch)

        @pl.when(last)
        def _():
            side.finish(side_ins, side_outs, sems)

    outs = pl.pallas_call(
        carried, name=name, grid=grid, in_specs=in_specs + [_ANY] * s_in, out_specs=out_specs + [_ANY] * s_out,
        out_shape=out_shape + side.out_shapes, scratch_shapes=scratch_shapes + side.sem_shapes,
        compiler_params=_params(*(["arbitrary"] * len(grid))))(*args, *side.inputs)
    return list(outs[:n_out]), list(outs[n_out:])


def _accumulate(prod, o_ref, acc_ref, k, nk):
    if nk == 1:
        o_ref[...] = prod.astype(o_ref.dtype)
        return

    @pl.when(k == 0)
    def _():
        acc_ref[...] = prod

    @pl.when(k > 0)
    def _():
        acc_ref[...] += prod

    @pl.when(k == nk - 1)
    def _():
        o_ref[...] = acc_ref[...].astype(o_ref.dtype)


def mm_nn(a3, b4, out_dtype, name, natural=False, a_natural=False, b_natural=False, jn=None, a_sel=None, bias=None,
          tm=512, tk=None, side=None):
    if b_natural:
        JK, JN, kb = 1, jn, b4.shape[0]
        nb = b4.shape[1] // JN
    else:
        JK, JN, kb, nb = b4.shape
    M = a3.shape[0] if a_natural else a3.shape[1]
    tm = _tile(M, tm, 16)
    tn = _tile(nb, 1024)
    tk = kb if tk is None else _tile(kb, tk)
    ntn, nkt = nb // tn, kb // tk
    nk = JK * nkt

    def body(*refs):
        if bias is None:
            a_ref, b_ref, o_ref, acc_ref = refs
        else:
            a_ref, b_ref, bias_ref, o_ref, acc_ref = refs
        prod = jnp.dot(a_ref[...].astype(BF16), b_ref[...].astype(BF16), preferred_element_type=F32)
        if bias is not None:
            prod = prod + bias_ref[...]
        _accumulate(prod, o_ref, acc_ref, pl.program_id(2), nk)

    if a_natural:
        a_spec = pl.BlockSpec((tm, tk), lambda j, i, k: (i, k))
    elif a_sel is not None:
        a_spec = pl.BlockSpec((None, tm, tk), lambda j, i, k: (a_sel, i, k))
    else:
        a_spec = pl.BlockSpec((None, tm, tk), lambda j, i, k: (k // nkt, i, k % nkt))
    if b_natural:
        b_spec = pl.BlockSpec((tk, tn), lambda j, i, k: (k, j))
    else:
        b_spec = pl.BlockSpec((None, None, tk, tn), lambda j, i, k: (k // nkt, j // ntn, k % nkt, j % ntn))
    in_specs = [a_spec, b_spec]
    args = [a3, b4]
    if bias is not None:
        assert natural and nk == 1
        in_specs.append(pl.BlockSpec((1, tn), lambda j, i, k: (0, j)))
        args.append(bias)
    if natural:
        out_shape = jax.ShapeDtypeStruct((M, JN * nb), out_dtype)
        o_spec = pl.BlockSpec((tm, tn), lambda j, i, k: (i, j))
    else:
        out_shape = jax.ShapeDtypeStruct((JN, M, nb), out_dtype)
        o_spec = pl.BlockSpec((None, tm, tn), lambda j, i, k: (j // ntn, i, j % ntn))
    res = _call(body, name=name, grid=(JN * ntn, M // tm, nk), in_specs=in_specs, out_specs=[o_spec],
                out_shape=[out_shape], scratch_shapes=[pltpu.VMEM((tm, tn), F32)], args=args,
                semantics=("parallel", "parallel", "arbitrary"), side=side)
    return res[0] if side is None else (res[0][0], res[1])


def mm_nt(a3, b4, out_dtype, name, natural=False, a_natural=False, side=None):
    JK, JN, nb, kb = b4.shape
    M = a3.shape[0] if a_natural else a3.shape[1]
    tm = min(M, 512)
    tn = _tile(nb, 1024)
    ntn = nb // tn
    nk = JK

    def body(a_ref, b_ref, o_ref, acc_ref):
        prod = lax.dot_general(a_ref[...].astype(BF16), b_ref[...].astype(BF16), (((1,), (1,)), ((), ())),
                               preferred_element_type=F32)
        _accumulate(prod, o_ref, acc_ref, pl.program_id(2), nk)

    if a_natural:
        a_spec = pl.BlockSpec((tm, kb), lambda j, i, k: (i, k))
    else:
        a_spec = pl.BlockSpec((None, tm, kb), lambda j, i, k: (k, i, 0))
    b_spec = pl.BlockSpec((None, None, tn, kb), lambda j, i, k: (k, j // ntn, j % ntn, 0))
    if natural:
        out_shape = jax.ShapeDtypeStruct((M, JN * nb), out_dtype)
        o_spec = pl.BlockSpec((tm, tn), lambda j, i, k: (i, j))
    else:
        out_shape = jax.ShapeDtypeStruct((JN, M, nb), out_dtype)
        o_spec = pl.BlockSpec((None, tm, tn), lambda j, i, k: (j // ntn, i, j % ntn))
    res = _call(body, name=name, grid=(JN * ntn, M // tm, nk), in_specs=[a_spec, b_spec], out_specs=[o_spec],
                out_shape=[out_shape], scratch_shapes=[pltpu.VMEM((tm, tn), F32)], args=[a3, b4],
                semantics=("parallel", "parallel", "arbitrary"), side=side)
    return res[0] if side is None else (res[0][0], res[1])


def _row_tile(S):
    return min(S, 256)


def _adaln(h, gain, shift, scale):
    y = h * lax.rsqrt(jnp.mean(h * h, axis=-1, keepdims=True) + NORM_EPS) * gain
    return y * (1.0 + scale) + shift


def adaln_fwd(h, gain, shift, scale, name):
    S, D = h.shape
    tm = _row_tile(S)

    def body(h_ref, g_ref, sh_ref, sc_ref, u_ref, ut_ref):
        u = _adaln(h_ref[...], g_ref[...], sh_ref[...], sc_ref[...])
        u_ref[...] = u.astype(BF16)
        ut_ref[...] = u.T.astype(BF16)

    row = pl.BlockSpec((tm, D), lambda i: (i, 0))
    vec = pl.BlockSpec((1, D), lambda i: (0, 0))
    return pl.pallas_call(
        body, name=name, grid=(S // tm,), in_specs=[row, vec, vec, vec],
        out_specs=[row, pl.BlockSpec((D, tm), lambda i: (0, i))],
        out_shape=[jax.ShapeDtypeStruct((S, D), BF16), jax.ShapeDtypeStruct((D, S), BF16)],
        compiler_params=_params("parallel"),
    )(h, gain, shift, scale)


def adaln_bwd(h, gain, shift, scale, du, dres, name):
    S, D = h.shape
    tm = _row_tile(S)

    def body(h_ref, g_ref, sh_ref, sc_ref, du_ref, dres_ref, dh_ref, dg_ref, dsh_ref, dsc_ref):
        _, vjp = jax.vjp(_adaln, h_ref[...], g_ref[...], sh_ref[...], sc_ref[...])
        dh, dg, dsh, dsc = vjp(du_ref[...].astype(F32))
        dh_ref[...] = dres_ref[...] + dh

        @pl.when(pl.program_id(0) == 0)
        def _():
            dg_ref[...] = jnp.zeros_like(dg_ref)
            dsh_ref[...] = jnp.zeros_like(dsh_ref)
            dsc_ref[...] = jnp.zeros_like(dsc_ref)

        dg_ref[...] += dg
        dsh_ref[...] += dsh
        dsc_ref[...] += dsc

    row = pl.BlockSpec((tm, D), lambda i: (i, 0))
    vec = pl.BlockSpec((1, D), lambda i: (0, 0))
    vs = jax.ShapeDtypeStruct((1, D), F32)
    return pl.pallas_call(
        body, name=name, grid=(S // tm,), in_specs=[row, vec, vec, vec, row, row], out_specs=[row, vec, vec, vec],
        out_shape=[jax.ShapeDtypeStruct((S, D), F32), vs, vs, vs], compiler_params=_params("arbitrary"),
    )(h, gain, shift, scale, du, dres)


def resid_fwd(h, y, gate, coef, name):
    S, D = h.shape
    tm = _row_tile(S)

    def body(h_ref, y_ref, g_ref, o_ref):
        o_ref[...] = h_ref[...] + (coef * g_ref[...]) * y_ref[...]

    row = pl.BlockSpec((tm, D), lambda i: (i, 0))
    vec = pl.BlockSpec((1, D), lambda i: (0, 0))
    return pl.pallas_call(
        body, name=name, grid=(S // tm,), in_specs=[row, row, vec], out_specs=row,
        out_shape=jax.ShapeDtypeStruct((S, D), F32), compiler_params=_params("parallel"),
    )(h, y, gate)


def gate_bwd(dh, y, gate, coef, name):
    S, D = dh.shape
    tm = _row_tile(S)

    def body(dh_ref, y_ref, g_ref, dy_ref, dg_ref):
        dh = dh_ref[...]
        dy_ref[...] = ((coef * g_ref[...]) * dh).astype(BF16)

        @pl.when(pl.program_id(0) == 0)
        def _():
            dg_ref[...] = jnp.zeros_like(dg_ref)

        dg_ref[...] += coef * jnp.sum(dh * y_ref[...], axis=0, keepdims=True)

    row = pl.BlockSpec((tm, D), lambda i: (i, 0))
    vec = pl.BlockSpec((1, D), lambda i: (0, 0))
    return pl.pallas_call(
        body, name=name, grid=(S // tm,), in_specs=[row, row, vec], out_specs=[row, vec],
        out_shape=[jax.ShapeDtypeStruct((S, D), BF16), jax.ShapeDtypeStruct((1, D), F32)],
        compiler_params=_params("arbitrary"),
    )(dh, y, gate)


def _swiglu(a, b):
    return a * jax.nn.sigmoid(a) * b


def swiglu_fwd(ab3, name, side=None):
    J2, S, nb = ab3.shape
    J = J2 // 2
    tm = _row_tile(S)

    def body(a_ref, b_ref, o_ref, ot_ref):
        hv = _swiglu(a_ref[...], b_ref[...])
        o_ref[...] = hv.astype(BF16)
        ot_ref[...] = hv.T.astype(BF16)

    res = _call(body, name=name, grid=(J, S // tm),
                in_specs=[pl.BlockSpec((None, tm, nb), lambda j, i: (j, i, 0)),
                          pl.BlockSpec((None, tm, nb), lambda j, i: (j + J, i, 0))],
                out_specs=[pl.BlockSpec((None, tm, nb), lambda j, i: (j, i, 0)),
                           pl.BlockSpec((None, nb, tm), lambda j, i: (j, 0, i))],
                out_shape=[jax.ShapeDtypeStruct((J, S, nb), BF16), jax.ShapeDtypeStruct((J, nb, S), BF16)],
                args=[ab3, ab3], semantics=("parallel", "parallel"), side=side)
    return res if side is None else (res[0], res[1])


def swiglu_bwd(ab3, dh3, name, side=None):
    J2, S, nb = ab3.shape
    J = J2 // 2
    tm = _row_tile(S)

    def body(ab_ref, dh_ref, o_ref):
        _, vjp = jax.vjp(_swiglu, ab_ref[0], ab_ref[1])
        da, db = vjp(dh_ref[...].astype(F32))
        o_ref[0] = da.astype(BF16)
        o_ref[1] = db.astype(BF16)

    both = pl.BlockSpec((2, None, tm, nb), lambda j, i: (0, j, i, 0))
    res = _call(body, name=name, grid=(J, S // tm),
                in_specs=[both, pl.BlockSpec((None, tm, nb), lambda j, i: (j, i, 0))], out_specs=[both],
                out_shape=[jax.ShapeDtypeStruct((2, J, S, nb), BF16)], args=[ab3.reshape(2, J, S, nb), dh3],
                semantics=("parallel", "parallel"), side=side)
    if side is None:
        return res[0].reshape(J2, S, nb)
    return res[0][0].reshape(J2, S, nb), res[1]


def colsum(x, name):
    S, N = x.shape
    tm = _row_tile(S)

    def body(x_ref, o_ref):
        @pl.when(pl.program_id(0) == 0)
        def _():
            o_ref[...] = jnp.zeros_like(o_ref)

        o_ref[...] += jnp.sum(x_ref[...].astype(F32), axis=0, keepdims=True)

    return pl.pallas_call(
        body, name=name, grid=(S // tm,), in_specs=[pl.BlockSpec((tm, N), lambda i: (i, 0))],
        out_specs=pl.BlockSpec((1, N), lambda i: (0, 0)), out_shape=jax.ShapeDtypeStruct((1, N), F32),
        compiler_params=_params("arbitrary"),
    )(x)


def _final_loss(h, gain, target):
    y = h * lax.rsqrt(jnp.mean(h * h, axis=-1, keepdims=True) + NORM_EPS) * gain
    err = y - target
    return 0.5 * jnp.sum(jnp.mean(err * err, axis=-1))


def final_loss_grad(h, gain, target, name):
    S, D = h.shape
    tm = _row_tile(S)

    def body(h_ref, g_ref, t_ref, loss_ref, dh_ref, dg_ref):
        loss, (dh, dg) = jax.value_and_grad(_final_loss, argnums=(0, 1))(h_ref[...], g_ref[...], t_ref[...])
        dh_ref[...] = dh

        @pl.when(pl.program_id(0) == 0)
        def _():
            loss_ref[...] = jnp.zeros_like(loss_ref)
            dg_ref[...] = jnp.zeros_like(dg_ref)

        loss_ref[...] += jnp.full(loss_ref.shape, loss, F32)
        dg_ref[...] += dg

    row = pl.BlockSpec((tm, D), lambda i: (i, 0))
    vec = pl.BlockSpec((1, D), lambda i: (0, 0))
    return pl.pallas_call(
        body, name=name, grid=(S // tm,), in_specs=[row, vec, row],
        out_specs=[pl.BlockSpec((1, LANES), lambda i: (0, 0)), row, vec],
        out_shape=[jax.ShapeDtypeStruct((1, LANES), F32), jax.ShapeDtypeStruct((S, D), F32),
                   jax.ShapeDtypeStruct((1, D), F32)],
        compiler_params=_params("arbitrary"),
    )(h, gain, target)


def _chunk_consts(H):
    C = HGRN_CHUNK
    t = lax.broadcasted_iota(jnp.int32, (H, C, C), 1)
    s = lax.broadcasted_iota(jnp.int32, (H, C, C), 2)
    return (s <= t).astype(F32), s <= t


def _hgrn_chunk(q_raw, f_raw, i_raw, lb, st):
    H, C, _ = q_raw.shape
    lower, causal = _chunk_consts(H)
    forget = lb + (1.0 - lb) * jax.nn.sigmoid(f_raw)
    g = jnp.log(forget)
    kk = 1.0 - forget
    qs = q_raw * jax.nn.sigmoid(q_raw)
    bnn = (((2,), (1,)), ((0,), (0,)))
    bnt = (((2,), (2,)), ((0,), (0,)))
    btn = (((1,), (1,)), ((0,), (0,)))
    b = lax.dot_general(lower, g, bnn, precision=lax.Precision.HIGHEST, preferred_element_type=F32)
    bm = b[:, C // 2 - 1:C // 2, :]
    bl = b[:, C - 1:C, :]
    inter = lax.dot_general((qs * jnp.exp(b)).astype(BF16), st.astype(BF16), bnt, preferred_element_type=F32)
    qt = (qs * jnp.exp(b - bm)).astype(BF16)
    kt = (kk * jnp.exp(bm - b)).astype(BF16)
    scores = lax.dot_general(qt, kt, bnt, preferred_element_type=F32)
    scores = jnp.where(causal, scores, 0.0)
    vb = i_raw.astype(BF16)
    out = inter + lax.dot_general(scores.astype(BF16), vb, bnn, preferred_element_type=F32)
    kdec = (kk * jnp.exp(bl - b)).astype(BF16)
    new_st = st * jnp.exp(bl) + lax.dot_general(vb, kdec, btn, preferred_element_type=F32)
    return out, new_st


def _heads(ref, rows, H):
    return jnp.stack([ref[rows, pl.ds(h * HGRN_HEAD, HGRN_HEAD)] for h in range(H)])


def hgrn_scan_fwd(proj, lb, name, side=None):
    S, D4 = proj.shape
    D = D4 // 4
    H = D // HGRN_HEAD
    C = HGRN_CHUNK
    R = min(S, 128)
    ncr = R // C

    def body(q_ref, f_ref, i_ref, lb_ref, o_ref, st_ref, state):
        @pl.when(pl.program_id(0) == 0)
        def _():
            state[...] = jnp.zeros_like(state)

        lbh = _heads(lb_ref, slice(None), H)

        def chunk(cc, carry):
            rows = pl.ds(pl.multiple_of(cc * C, C), C)
            st = state[...]
            st_ref[cc] = st
            out, new_st = _hgrn_chunk(_heads(q_ref, rows, H), _heads(f_ref, rows, H), _heads(i_ref, rows, H), lbh, st)
            for h in range(H):
                o_ref[rows, pl.ds(h * HGRN_HEAD, HGRN_HEAD)] = out[h]
            state[...] = new_st
            return carry

        lax.fori_loop(0, ncr, chunk, 0)

    col = lambda j: pl.BlockSpec((R, D), lambda i: (i, j))
    res = _call(body, name=name, grid=(S // R,),
                in_specs=[col(0), col(1), col(2), pl.BlockSpec((1, D), lambda i: (0, 0))],
                out_specs=[pl.BlockSpec((R, D), lambda i: (i, 0)),
                           pl.BlockSpec((ncr, H, HGRN_HEAD, HGRN_HEAD), lambda i: (i, 0, 0, 0))],
                out_shape=[jax.ShapeDtypeStruct((S, D), F32),
                           jax.ShapeDtypeStruct((S // C, H, HGRN_HEAD, HGRN_HEAD), F32)],
                scratch_shapes=[pltpu.VMEM((H, HGRN_HEAD, HGRN_HEAD), F32)], args=[proj, proj, proj, lb],
                semantics=("arbitrary",), side=side)
    return res if side is None else (res[0], res[1])


def hgrn_scan_bwd(proj, lb, states, do, dg, name, side=None):
    S, D4 = proj.shape
    D = D4 // 4
    H = D // HGRN_HEAD
    C = HGRN_CHUNK
    R = min(S, 128)
    ncr = R // C
    ng = S // R

    def body(q_ref, f_ref, i_ref, lb_ref, st_ref, do_ref, dg_ref, dp_ref, dlb_ref, dstate):
        @pl.when(pl.program_id(0) == 0)
        def _():
            dstate[...] = jnp.zeros_like(dstate)
            dlb_ref[...] = jnp.zeros_like(dlb_ref)

        dp_ref[:, pl.ds(3 * D, D)] = dg_ref[...].astype(BF16)
        lbh = _heads(lb_ref, slice(None), H)

        def chunk(t, carry):
            cc = ncr - 1 - t
            rows = pl.ds(pl.multiple_of(cc * C, C), C)
            _, vjp = jax.vjp(_hgrn_chunk, _heads(q_ref, rows, H), _heads(f_ref, rows, H), _heads(i_ref, rows, H),
                             lbh, st_ref[cc])
            dq, df, di, dlb, dst = vjp((_heads(do_ref, rows, H), dstate[...]))
            for h in range(H):
                dp_ref[rows, pl.ds(h * HGRN_HEAD, HGRN_HEAD)] = dq[h].astype(BF16)
                dp_ref[rows, pl.ds(D + h * HGRN_HEAD, HGRN_HEAD)] = df[h].astype(BF16)
                dp_ref[rows, pl.ds(2 * D + h * HGRN_HEAD, HGRN_HEAD)] = di[h].astype(BF16)
                dlb_ref[:, pl.ds(h * HGRN_HEAD, HGRN_HEAD)] += dlb[h]
            dstate[...] = dst
            return carry

        lax.fori_loop(0, ncr, chunk, 0)

    col = lambda j: pl.BlockSpec((R, D), lambda i: (ng - 1 - i, j))
    res = _call(body, name=name, grid=(ng,),
                in_specs=[col(0), col(1), col(2), pl.BlockSpec((1, D), lambda i: (0, 0)),
                          pl.BlockSpec((ncr, H, HGRN_HEAD, HGRN_HEAD), lambda i: (ng - 1 - i, 0, 0, 0)),
                          pl.BlockSpec((R, D), lambda i: (ng - 1 - i, 0)),
                          pl.BlockSpec((R, D), lambda i: (ng - 1 - i, 0))],
                out_specs=[pl.BlockSpec((R, D4), lambda i: (ng - 1 - i, 0)), pl.BlockSpec((1, D), lambda i: (0, 0))],
                out_shape=[jax.ShapeDtypeStruct((S, D4), BF16), jax.ShapeDtypeStruct((1, D), F32)],
                scratch_shapes=[pltpu.VMEM((H, HGRN_HEAD, HGRN_HEAD), F32)],
                args=[proj, proj, proj, lb, states, do, dg], semantics=("arbitrary",), side=side)
    return res if side is None else (res[0], res[1])


def _head_out(o, g, gain):
    y = o * lax.rsqrt(jnp.mean(o * o, axis=-1, keepdims=True) + NORM_EPS) * gain
    return y * jax.nn.sigmoid(g)


def hgrn_post_fwd(o, proj, gain, name):
    S, D = o.shape
    H = D // HGRN_HEAD
    tm = _row_tile(S)

    def body(o_ref, g_ref, gain_ref, z_ref, zt_ref):
        for h in range(H):
            ls = pl.ds(h * HGRN_HEAD, HGRN_HEAD)
            z = _head_out(o_ref[:, ls], g_ref[:, ls], gain_ref[:, ls])
            z_ref[:, ls] = z.astype(BF16)
            zt_ref[ls, :] = z.T.astype(BF16)

    row = pl.BlockSpec((tm, D), lambda i: (i, 0))
    return pl.pallas_call(
        body, name=name, grid=(S // tm,),
        in_specs=[row, pl.BlockSpec((tm, D), lambda i: (i, 3)), pl.BlockSpec((1, D), lambda i: (0, 0))],
        out_specs=[row, pl.BlockSpec((D, tm), lambda i: (0, i))],
        out_shape=[jax.ShapeDtypeStruct((S, D), BF16), jax.ShapeDtypeStruct((D, S), BF16)],
        compiler_params=_params("parallel"),
    )(o, proj, gain)


def hgrn_post_bwd(o, proj, gain, dz, name):
    S, D = o.shape
    H = D // HGRN_HEAD
    tm = _row_tile(S)

    def body(o_ref, g_ref, gain_ref, dz_ref, do_ref, dg_ref, dgain_ref):
        @pl.when(pl.program_id(0) == 0)
        def _():
            dgain_ref[...] = jnp.zeros_like(dgain_ref)

        for h in range(H):
            ls = pl.ds(h * HGRN_HEAD, HGRN_HEAD)
            _, vjp = jax.vjp(_head_out, o_ref[:, ls], g_ref[:, ls], gain_ref[:, ls])
            do, dg, dgain = vjp(dz_ref[:, ls].astype(F32))
            do_ref[:, ls] = do
            dg_ref[:, ls] = dg
            dgain_ref[:, ls] += dgain

    row = pl.BlockSpec((tm, D), lambda i: (i, 0))
    vec = pl.BlockSpec((1, D), lambda i: (0, 0))
    return pl.pallas_call(
        body, name=name, grid=(S // tm,),
        in_specs=[row, pl.BlockSpec((tm, D), lambda i: (i, 3)), vec, row], out_specs=[row, row, vec],
        out_shape=[jax.ShapeDtypeStruct((S, D), F32), jax.ShapeDtypeStruct((S, D), F32),
                   jax.ShapeDtypeStruct((1, D), F32)],
        compiler_params=_params("arbitrary"),
    )(o, proj, gain, dz)


def lower_bound_fwd(l0, l1, name):
    def body(a_ref, b_ref, o_ref):
        o_ref[...] = jax.nn.sigmoid(a_ref[...] - b_ref[...])

    return pl.pallas_call(body, name=name, out_shape=jax.ShapeDtypeStruct(l0.shape, F32))(l0, l1)


def lower_bound_bwd(l0, l1, dlb, name):
    def body(a_ref, b_ref, d_ref, o0_ref, o1_ref):
        s = jax.nn.sigmoid(a_ref[...] - b_ref[...])
        d0 = d_ref[...] * s * (1.0 - s)
        o0_ref[...] = d0
        o1_ref[...] = -d0

    sd = jax.ShapeDtypeStruct(l0.shape, F32)
    return pl.pallas_call(body, name=name, out_shape=[sd, sd])(l0, l1, dlb)


def _attn_tile(q, kp, kc, vp, vc, sink, first):
    W = ATT_WINDOW
    nt = (((1,), (1,)), ((), ()))
    qb = q.astype(BF16)
    scale = ATT_HEAD ** -0.5
    sp = lax.dot_general(qb, kp.astype(BF16), nt, preferred_element_type=F32) * scale
    sc = lax.dot_general(qb, kc.astype(BF16), nt, preferred_element_type=F32) * scale
    qi = lax.broadcasted_iota(jnp.int32, sp.shape, 0) & (W - 1)
    kj = lax.broadcasted_iota(jnp.int32, sp.shape, 1)
    sp = jnp.where((kj > qi) & jnp.logical_not(first), sp, NEG_INF)
    sc = jnp.where(kj <= qi, sc, NEG_INF)
    m = jnp.maximum(jnp.maximum(jnp.max(sp, axis=-1, keepdims=True), jnp.max(sc, axis=-1, keepdims=True)), sink)
    pp = jnp.exp(sp - m)
    pc = jnp.exp(sc - m)
    denom = jnp.sum(pp, axis=-1, keepdims=True) + jnp.sum(pc, axis=-1, keepdims=True) + jnp.exp(sink - m)
    out = jnp.dot((pp / denom).astype(BF16), vp.astype(BF16), preferred_element_type=F32)
    return out + jnp.dot((pc / denom).astype(BF16), vc.astype(BF16), preferred_element_type=F32)


def _attn_specs(G, W, Dh):
    q_spec = pl.BlockSpec((None, G, W, Dh), lambda j, n: (j, 0, n, 0))
    prev = pl.BlockSpec((None, W, Dh), lambda j, n: (j, jnp.maximum(n - 1, 0), 0))
    cur = pl.BlockSpec((None, W, Dh), lambda j, n: (j, n, 0))
    sink = pl.BlockSpec((None, G * W, 1), lambda j, n: (j, 0, 0))
    return q_spec, prev, cur, sink


def attn_fwd(q4, k3, v3, sink, name):
    NKV, G, S, Dh = q4.shape
    W = ATT_WINDOW

    def body(q_ref, kp_ref, kc_ref, vp_ref, vc_ref, s_ref, o_ref):
        first = pl.program_id(1) == 0
        out = _attn_tile(q_ref[...].reshape(G * W, Dh), kp_ref[...], kc_ref[...], vp_ref[...], vc_ref[...],
                         s_ref[...], first)
        o_ref[...] = out.reshape(G, W, Dh)

    q_spec, prev, cur, sk = _attn_specs(G, W, Dh)
    return pl.pallas_call(
        body, name=name, grid=(NKV, S // W), in_specs=[q_spec, prev, cur, prev, cur, sk], out_specs=q_spec,
        out_shape=jax.ShapeDtypeStruct(q4.shape, F32), compiler_params=_params("parallel", "parallel"),
    )(q4, k3, k3, v3, v3, sink)


def attn_bwd(q4, k3, v3, sink, do4, name, side=None):
    NKV, G, S, Dh = q4.shape
    W = ATT_WINDOW
    nb = S // W

    def body(q_ref, kp_ref, kc_ref, vp_ref, vc_ref, s_ref, do_ref, dq_ref, dkp_ref, dkc_ref, dvp_ref, dvc_ref,
             ds_ref):
        first = pl.program_id(1) == 0
        _, vjp = jax.vjp(functools.partial(_attn_tile, first=first), q_ref[...].reshape(G * W, Dh), kp_ref[...],
                         kc_ref[...], vp_ref[...], vc_ref[...], s_ref[...])
        dq, dkp, dkc, dvp, dvc, ds = vjp(do_ref[...].reshape(G * W, Dh))
        dq_ref[...] = dq.reshape(G, W, Dh)
        dkp_ref[...] = dkp
        dkc_ref[...] = dkc
        dvp_ref[...] = dvp
        dvc_ref[...] = dvc

        @pl.when(first)
        def _():
            ds_ref[...] = jnp.zeros_like(ds_ref)

        ds_ref[...] += jnp.sum(ds.reshape(G, W, 1), axis=1)

    q_spec, prev, cur, sk = _attn_specs(G, W, Dh)
    part = pl.BlockSpec((None, None, W, Dh), lambda j, n: (j, n, 0, 0))
    ps = jax.ShapeDtypeStruct((NKV, nb, W, Dh), F32)
    res = _call(body, name=name, grid=(NKV, nb), in_specs=[q_spec, prev, cur, prev, cur, sk, q_spec],
                out_specs=[q_spec, part, part, part, part, pl.BlockSpec((None, G, 1), lambda j, n: (j, 0, 0))],
                out_shape=[jax.ShapeDtypeStruct(q4.shape, F32), ps, ps, ps, ps,
                           jax.ShapeDtypeStruct((NKV, G, 1), F32)],
                args=[q4, k3, k3, v3, v3, sink, do4], semantics=("parallel", "arbitrary"), side=side)
    outs = res if side is None else res[0]
    ans = (outs[0], outs[1:5], outs[5])
    return ans if side is None else (ans, res[1])


def band_combine(cur, prev, name):
    NKV, nb, W, Dh = cur.shape

    def body(c_ref, p_ref, o_ref):
        keep = (pl.program_id(1) < nb - 1).astype(F32)
        o_ref[...] = c_ref[...] + keep * p_ref[...]

    return pl.pallas_call(
        body, name=name, grid=(NKV, nb),
        in_specs=[pl.BlockSpec((None, None, W, Dh), lambda j, n: (j, n, 0, 0)),
                  pl.BlockSpec((None, None, W, Dh), lambda j, n: (j, jnp.minimum(n + 1, nb - 1), 0, 0))],
        out_specs=pl.BlockSpec((None, W, Dh), lambda j, n: (j, n, 0)),
        out_shape=jax.ShapeDtypeStruct((NKV, nb * W, Dh), F32), compiler_params=_params("parallel", "parallel"),
    )(cur, prev)


def rope_tables(S):
    half = ROT_DIM // 2
    inv_freq = jnp.power(jnp.float32(ROPE_THETA), -jnp.arange(0, ROT_DIM, 2, dtype=F32) / ROT_DIM)
    ang = jnp.arange(S, dtype=F32)[:, None] * inv_freq[None, :]
    sin, cos = jnp.sin(ang), jnp.cos(ang)
    zeros = jnp.zeros((S, ATT_HEAD - ROT_DIM), F32)
    z8 = jnp.zeros((S, half), F32)
    cfull = jnp.concatenate([cos, cos, jnp.ones((S, ATT_HEAD - ROT_DIM), F32)], axis=1)
    s_next = jnp.concatenate([-sin, z8, zeros], axis=1)
    s_prev = jnp.concatenate([z8, sin, zeros], axis=1)
    two = lambda t: jnp.concatenate([t, t], axis=1)
    return two(cfull), two(s_next), two(s_prev)


def rope(x, tables, sign, name):
    S, Wd = x.shape
    tm = _row_tile(S)
    rep = Wd // LANES
    half = ROT_DIM // 2

    def body(x_ref, c_ref, sn_ref, sp_ref, o_ref):
        xv = x_ref[...]
        c = jnp.tile(c_ref[...], (1, rep))
        sn = jnp.tile(sn_ref[...], (1, rep))
        sp = jnp.tile(sp_ref[...], (1, rep))
        if sign > 0:
            nxt = pltpu.roll(xv, Wd - half, 1)
            prv = pltpu.roll(xv, half, 1)
            o_ref[...] = xv * c + nxt * sn + prv * sp
        else:
            o_ref[...] = xv * c + pltpu.roll(xv * sn, half, 1) + pltpu.roll(xv * sp, Wd - half, 1)

    row = pl.BlockSpec((tm, Wd), lambda i: (i, 0))
    tab = pl.BlockSpec((tm, LANES), lambda i: (i, 0))
    return pl.pallas_call(
        body, name=name, grid=(S // tm,), in_specs=[row, tab, tab, tab], out_specs=row,
        out_shape=jax.ShapeDtypeStruct((S, Wd), F32), compiler_params=_params("parallel"),
    )(x, *tables)


def cond_proj(c_all, w, bias, name):
    B, D = c_all.shape
    N = w.shape[1]
    tn = _tile(N, 512)

    def body(c_ref, w_ref, b_ref, o_ref):
        cv = c_ref[...]
        cs = (cv * jax.nn.sigmoid(cv)).astype(BF16)
        o_ref[...] = jnp.dot(cs, w_ref[...].astype(BF16), preferred_element_type=F32) + b_ref[...]

    return pl.pallas_call(
        body, name=name, grid=(N // tn,),
        in_specs=[pl.BlockSpec((B, D), lambda j: (0, 0)), pl.BlockSpec((D, tn), lambda j: (0, j)),
                  pl.BlockSpec((1, tn), lambda j: (0, j))],
        out_specs=pl.BlockSpec((B, tn), lambda j: (0, j)), out_shape=jax.ShapeDtypeStruct((B, N), F32),
        compiler_params=_params("parallel"),
    )(c_all, w, bias)


def cond_grad(c_all, dmod, name):
    B, D = c_all.shape
    N = dmod.shape[1]
    tn = _tile(N, 512)

    def body(c_ref, d_ref, o_ref):
        cv = c_ref[...]
        cs = (cv * jax.nn.sigmoid(cv)).astype(BF16)
        o_ref[...] = lax.dot_general(cs, d_ref[...].astype(BF16), (((0,), (0,)), ((), ())),
                                     preferred_element_type=F32)

    return pl.pallas_call(
        body, name=name, grid=(N // tn,),
        in_specs=[pl.BlockSpec((B, D), lambda j: (0, 0)), pl.BlockSpec((B, tn), lambda j: (0, j))],
        out_specs=pl.BlockSpec((D, tn), lambda j: (0, j)), out_shape=jax.ShapeDtypeStruct((D, N), F32),
        compiler_params=_params("parallel"),
    )(c_all, dmod)


def rowsum(g, name):
    B, N = g.shape
    tn = _tile(N, 8192)

    def body(g_ref, o_ref):
        acc = g_ref[0:1, :]
        for r in range(1, B):
            acc = acc + g_ref[r:r + 1, :]
        o_ref[...] = acc

    return pl.pallas_call(
        body, name=name, grid=(N // tn,), in_specs=[pl.BlockSpec((B, tn), lambda j: (0, j))],
        out_specs=pl.BlockSpec((1, tn), lambda j: (0, j)), out_shape=jax.ShapeDtypeStruct((1, N), F32),
        compiler_params=_params("parallel"),
    )(g)


def _adam_rows(R, C):
    if R * C * 4 <= (1 << 20) or R % 8:
        return R
    best = 8
    for t in range(8, R + 1, 8):
        if R % t == 0 and t * C * 4 <= (1 << 20):
            best = t
    return best


def adamw(w3, m3, v3, j, g3, name, into=None):
    n, R, C = w3.shape
    P = g3.shape[0]
    tr = _adam_rows(R, C * max(P // 2, 1))

    def body(*refs):
        w_ref, m_ref, v_ref, g_ref = refs[:4]
        go_ref, d_ref, mo_ref, vo_ref = refs[-4:]
        g = g_ref[0].astype(F32)
        for p in range(1, P):
            g = g + g_ref[p].astype(F32)
        mn = ADAM_B1 * m_ref[...] + (1.0 - ADAM_B1) * g
        vn = ADAM_B2 * v_ref[...] + (1.0 - ADAM_B2) * jnp.square(g)
        m_hat = mn / (1.0 - ADAM_B1 ** ADAM_STEP)
        v_hat = vn / (1.0 - ADAM_B2 ** ADAM_STEP)
        go_ref[...] = g
        d_ref[...] = -ADAM_LR * (m_hat / (jnp.sqrt(v_hat) + ADAM_EPS) + ADAM_WD * w_ref[...])
        mo_ref[...] = mn
        vo_ref[...] = vn

    spec = pl.BlockSpec((None, tr, C), lambda i: (j, i, 0))
    g_spec = pl.BlockSpec((P, tr, C), lambda i: (0, i, 0))
    sd = jax.ShapeDtypeStruct((n, R, C), F32)
    in_specs, args, aliases = [spec, spec, spec, g_spec], [w3, m3, v3, g3], {}
    if into is not None:
        in_specs += [_ANY] * 4
        args += list(into)
        aliases = {4 + k: k for k in range(4)}
    return pl.pallas_call(
        body, name=name, grid=(R // tr,), in_specs=in_specs, out_specs=[spec] * 4, out_shape=[sd] * 4,
        input_output_aliases=aliases, compiler_params=_params("parallel"),
    )(*args)


def _place():
    return lax.axis_index("x"), lax.axis_index("y"), lax.axis_index("c")


def _slot(p):
    return 4 * p[0] + 2 * p[1] + p[2]


def gather_side(xs):
    n = len(xs)

    def copy(ins, outs, sems, t, k, block, to, from_input=False):
        dst = outs[t].at[_slot(block)]
        return pltpu.make_async_remote_copy(
            src_ref=ins[t] if from_input else dst, dst_ref=dst, send_sem=sems[0].at[t, k],
            recv_sem=sems[1].at[t, k], device_id=to, device_id_type=MESH)

    def peers():
        x, y, c = _place()
        return (x, y, c), (x, y, 1 - c), [(1 - x, y), (x, 1 - y), (1 - x, 1 - y)]

    def start(ins, outs, sems):
        me, sibling, chips = peers()
        c = me[2]
        for t in range(n):
            pltpu.make_async_copy(ins[t], outs[t].at[_slot(me)], sems[2].at[t]).start()
            copy(ins, outs, sems, t, 0, me, sibling, True).start()
            for j, chip in enumerate(chips):
                copy(ins, outs, sems, t, 1 + j, me, (*chip, c), True).start()

    def finish(ins, outs, sems):
        me, sibling, chips = peers()
        c = me[2]
        for t in range(n):
            for j, chip in enumerate(chips):
                copy(ins, outs, sems, t, 1 + j, (*chip, c), me).wait_recv()
                copy(ins, outs, sems, t, 4 + j, (*chip, c), sibling).start()
        for t in range(n):
            copy(ins, outs, sems, t, 0, sibling, me).wait_recv()
            for j, chip in enumerate(chips):
                copy(ins, outs, sems, t, 4 + j, (*chip, 1 - c), me).wait_recv()
        for t in range(n):
            copy(ins, outs, sems, t, 0, me, sibling, True).wait_send()
            for j, chip in enumerate(chips):
                copy(ins, outs, sems, t, 1 + j, me, (*chip, c), True).wait_send()
                copy(ins, outs, sems, t, 4 + j, (*chip, c), sibling).wait_send()
            pltpu.make_async_copy(ins[t], outs[t].at[_slot(me)], sems[2].at[t]).wait()

    return _Side(xs, [jax.ShapeDtypeStruct((N_DEV,) + a.shape, a.dtype) for a in xs],
                 [pltpu.SemaphoreType.DMA((n, 7)), pltpu.SemaphoreType.DMA((n, 7)), pltpu.SemaphoreType.DMA((n,))],
                 start, finish)


def scatter_side(gs):
    n = len(gs)

    def copies(ins, outs, sems):
        x, y, c = _place()
        me = (x, y, c)
        out = []
        for t in range(n):
            out.append(pltpu.make_async_copy(ins[t].at[_slot(me)], outs[t].at[_slot(me)], sems[2].at[t]))
            for r in range(1, N_DEV):
                peer = (1 - x if r & 4 else x, 1 - y if r & 2 else y, 1 - c if r & 1 else c)
                out.append(pltpu.make_async_remote_copy(
                    src_ref=ins[t].at[_slot(peer)], dst_ref=outs[t].at[_slot(me)], send_sem=sems[0].at[t, r - 1],
                    recv_sem=sems[1].at[t, r - 1], device_id=peer, device_id_type=MESH))
        return out

    def start(ins, outs, sems):
        for cp in copies(ins, outs, sems):
            cp.start()

    def finish(ins, outs, sems):
        for cp in copies(ins, outs, sems):
            cp.wait()

    return _Side(gs, [jax.ShapeDtypeStruct(g.shape, g.dtype) for g in gs],
                 [pltpu.SemaphoreType.DMA((n, 7)), pltpu.SemaphoreType.DMA((n, 7)), pltpu.SemaphoreType.DMA((n,))],
                 start, finish)


def exchange(side, name):
    def body(*refs):
        n_in, n_out = len(side.inputs), len(side.out_shapes)
        ins, outs, sems = refs[:n_in], refs[n_in:n_in + n_out], refs[n_in + n_out:]
        side.start(ins, outs, sems)
        side.finish(ins, outs, sems)

    return pl.pallas_call(
        body, name=name, in_specs=[_ANY] * len(side.inputs), out_specs=[_ANY] * len(side.out_shapes),
        out_shape=side.out_shapes, scratch_shapes=side.sem_shapes,
    )(*side.inputs)


class _Plan:
    def __init__(self, plan, make_side):
        self.plan, self.make_side, self.source, self.got = plan, make_side, {}, {}

    def run(self, fn, *args, name, **kw):
        keys = self.plan.get(name)
        if not keys:
            return fn(*args, name=name, **kw)
        result, outs = fn(*args, name=name, side=self.make_side([self.source[k] for k in keys]), **kw)
        self.got.update(zip(keys, outs))
        return result


GATHER_PLAN = {
    "l0s0_in": [("out", 0, 0), "hout", "kv"],
    "l0s0_out": ["hin"],
    "l0s1_proj": [("out", 0, 1)],
    "l0s1_scan": [("in", 0, 1)],
    "l0s2_in": [("in", 1, 0)],
    "l0s2_act": ["q", "o"],
    "l0s2_out": [("out", 1, 0)],
    "l1s0_in": [("in", 1, 1)],
    "l1s0_out": [("out", 1, 1)],
}
GATHER_FIRST = [("in", 0, 0)]
SCATTER_PLAN = {
    "l1s2_du": [("out", 1, 1)],
    "l1s1_dattn": [("in", 1, 1, 0), ("in", 1, 1, 1)],
    "l1s0_dact": ["o"],
    "l1s0_dab": ["q"],
    "l1s0_du": [("out", 1, 0)],
    "l0s2_dact": ["kv"],
    "l0s2_du": [("in", 1, 0, 0)],
    "l0s1_dscan": [("in", 1, 0, 1), "hout"],
    "l0s1_du": [("out", 0, 1)],
    "l0s0_dact": [("hin", 0)],
    "l0s0_dab": [("hin", 1)],
    "l0s0_dwin0": [("in", 0, 1, 0)],
    "l0s0_dwin1": [("in", 0, 1, 1)],
    "l0s0_dwout": [("in", 0, 0, 0)],
    "l0s0_du": [("in", 0, 0, 1)],
}
GRAD_TM, GRAD_TK = 1024, 1024


def _ffn_fwd(gp, W, u, l, i, tag):
    ab3 = gp.run(mm_nn, u[None], W[("in", l, i)][None], F32, name=tag + "_in")
    h3, h3t = gp.run(swiglu_fwd, ab3, name=tag + "_act")
    w_out = W[("out", l, i)]
    J = h3.shape[0]
    y = gp.run(mm_nn, h3, w_out.reshape(J, 1, -1, w_out.shape[-1]), F32, name=tag + "_out", natural=True)
    return y, (ab3, h3, h3t)


def _ffn_bwd(sp, W, dy, ut, ab3, h3t, l, i, tag):
    w_in, w_out = W[("in", l, i)], W[("out", l, i)]
    J, nb, S = h3t.shape
    D = w_out.shape[-1]
    dh3 = sp.run(mm_nt, dy[None], w_out.reshape(1, J, -1, D), F32, name=tag + "_dact")
    dab3 = sp.run(swiglu_bwd, ab3, dh3, name=tag + "_dab")
    for hf in range(2):
        sp.source[("in", l, i, hf)] = sp.run(mm_nn, ut.reshape(2, D // 2, S), dab3[None], BF16, name=f"{tag}_dwin{hf}",
                                             a_sel=hf, tm=GRAD_TM, tk=GRAD_TK)
    dw_out = sp.run(mm_nn, h3t.reshape(1, J * nb, S), dy[None, None], BF16, name=tag + "_dwout", natural=True,
                    tm=w_out.shape[1], tk=GRAD_TK)
    sp.source[("out", l, i)] = dw_out.reshape(w_out.shape)
    return sp.run(mm_nt, dab3, w_in[:, None], F32, name=tag + "_du", natural=True)


def kernel(x, c, norm_gain, w_ada, b_ada, w_ffn_in, w_ffn_out, w_hgrn_in, hgrn_lb_logits, hgrn_head_gain, w_hgrn_out, kv_gain, w_ada_kv, b_ada_kv, w_kv, b_kv, w_q, b_q, attn_sinks, w_attn_out, final_gain, loss_target, m_norm_gain, m_w_ada, m_b_ada, m_w_ffn_in, m_w_ffn_out, m_w_hgrn_in, m_hgrn_lb_logits, m_hgrn_head_gain, m_w_hgrn_out, m_kv_gain, m_w_ada_kv, m_b_ada_kv, m_w_kv, m_b_kv, m_w_q, m_b_q, m_attn_sinks, m_w_attn_out, m_final_gain, v_norm_gain, v_w_ada, v_b_ada, v_w_ffn_in, v_w_ffn_out, v_w_hgrn_in, v_hgrn_lb_logits, v_hgrn_head_gain, v_w_hgrn_out, v_kv_gain, v_w_ada_kv, v_b_ada_kv, v_w_kv, v_b_kv, v_w_q, v_b_q, v_attn_sinks, v_w_attn_out, v_final_gain):
    xi, yi, ci = _place()
    me = 4 * xi + 2 * yi + ci
    _, S, D = x.shape
    L = norm_gain.shape[0]
    dsh = D // N_DEV
    ada_n = w_ada.shape[2]
    kv_n = w_ada_kv.shape[1]
    NQ = D // ATT_HEAD
    NKV = NQ // ATT_GROUP
    kvd = NKV * ATT_HEAD
    h0 = x[0]
    target = loss_target[0]

    def my_cols(a, n):
        return lax.dynamic_slice_in_dim(a, me * n, n, axis=a.ndim - 1)

    gp = _Plan(GATHER_PLAN, gather_side)
    bf = lambda a: a.astype(BF16)
    for l in range(L):
        for i in range(2):
            gp.source[("in", l, i)] = bf(w_ffn_in[l, i])
            gp.source[("out", l, i)] = bf(w_ffn_out[l, i])
    gp.source.update(hin=bf(w_hgrn_in[0]), hout=bf(w_hgrn_out[0]), kv=bf(w_kv), q=bf(w_q[0]), o=bf(w_attn_out[0]))
    W = gp.got
    W.update(zip(GATHER_FIRST, exchange(gather_side([gp.source[k] for k in GATHER_FIRST]), "gather_first")))
    full = lambda w: w.reshape(1, 1, -1, w.shape[-1])

    lb_sh = lower_bound_fwd(hgrn_lb_logits[0:1], hgrn_lb_logits[1:2], "lb_fwd")
    small = jnp.concatenate([c, norm_gain.reshape(1, L * 3 * dsh), hgrn_head_gain, lb_sh], axis=1)
    (g1,) = exchange(gather_side([small]), "gather_cond")
    g1 = g1.reshape(N_DEV, -1)
    c_all = g1[:, :D]
    gains = g1[:, D:D + L * 3 * dsh].reshape(N_DEV, L * 3, dsh).transpose(1, 0, 2).reshape(L, 3, 1, D)
    head_gain = g1[:, D + L * 3 * dsh:D + (L * 3 + 1) * dsh].reshape(1, D)
    lb0 = g1[:, D + (L * 3 + 1) * dsh:].reshape(1, D)

    parts = [cond_proj(c_all, w_ada[l], my_cols(b_ada[l:l + 1], ada_n), f"mod{l}") for l in range(L)]
    parts.append(cond_proj(c_all, w_ada_kv, my_cols(b_ada_kv[None], kv_n), "mod_kv"))
    (g2,) = exchange(gather_side([jnp.concatenate(parts, axis=1)]), "gather_mod")
    mine2 = lax.dynamic_index_in_dim(g2, me, axis=1, keepdims=False)
    mod = [mine2[:, l * ada_n:(l + 1) * ada_n].reshape(3, 3, 1, D) for l in range(L)]
    mod_kv = mine2[:, L * ada_n:].reshape(2, 1, D)

    tables = rope_tables(S)
    sink_col = jnp.broadcast_to(attn_sinks.reshape(NKV, ATT_GROUP, 1, 1), (NKV, ATT_GROUP, ATT_WINDOW, 1))
    sink_col = sink_col.reshape(NKV, ATT_GROUP * ATT_WINDOW, 1)

    def to_heads(t, n):
        return t.reshape(S, n, ATT_HEAD).transpose(1, 0, 2)

    def from_heads(t):
        return t.transpose(1, 0, 2).reshape(S, -1)

    h = h0
    saved = {}
    for l in range(L):
        for s in (0, 1, 2):
            tag = f"l{l}s{s}"
            shift, scale, gate = mod[l][s, 0], mod[l][s, 1], mod[l][s, 2]
            u, ut = adaln_fwd(h, gains[l, s], shift, scale, tag + "_norm")
            if s != 1:
                y, res = _ffn_fwd(gp, W, u, l, s // 2, tag)
                coef = 0.5
            elif l == 0:
                proj = gp.run(mm_nn, u[None], W["hin"][None], F32, name=tag + "_proj", natural=True)
                o, states = gp.run(hgrn_scan_fwd, proj, lb0, name=tag + "_scan")
                z, zt = hgrn_post_fwd(o, proj, head_gain, tag + "_post")
                y = mm_nn(z[None], full(W["hout"]), F32, tag + "_out", natural=True)
                res = (proj, o, states, zt)
                coef = 1.0
            else:
                q = mm_nn(u[None], full(W["q"]), F32, tag + "_q", natural=True, bias=b_q)
                q4 = to_heads(rope(q, tables, 1, tag + "_rope"), NQ).reshape(NKV, ATT_GROUP, S, ATT_HEAD)
                att4 = attn_fwd(q4, k3, v3, sink_col, tag + "_attn")
                att = from_heads(att4.reshape(NQ, S, ATT_HEAD))
                att_t = att4.reshape(NQ, S, ATT_HEAD).transpose(0, 2, 1).reshape(D, S)
                y = mm_nn(att[None], full(W["o"]), F32, tag + "_out", natural=True)
                res = (q4, att_t)
                coef = 1.0
            saved[(l, s)] = (h, ut, y, res)
            h = resid_fwd(h, y, gate, coef, tag + "_res")
        if l == 0:
            h_kv = h
            u_kv, u_kv_t = adaln_fwd(h, kv_gain[None], mod_kv[0], mod_kv[1], "kv_norm")
            kvp = mm_nn(u_kv[None], full(W["kv"]), F32, "kv_proj", natural=True, bias=b_kv[None])
            k3 = to_heads(rope(kvp[:, :kvd], tables, 1, "kv_rope"), NKV)
            v3 = to_heads(kvp[:, kvd:], NKV)

    loss_row, dh, d_final_gain = final_loss_grad(h, final_gain[None], target, "final")
    loss = lax.psum(loss_row[0, 0], ("x", "y", "c"))

    sp = _Plan(SCATTER_PLAN, scatter_side)

    def grad_w(a_t, b, name):
        return sp.run(mm_nn, a_t[None], b[None, None], BF16, name=name, natural=True, tm=GRAD_TM, tk=GRAD_TK)

    d_mod = [[None] * 3 for _ in range(L)]
    d_gain = [[None] * 3 for _ in range(L)]
    for l in reversed(range(L)):
        if l == 0:
            dkv = jnp.concatenate([rope(from_heads(dk3), tables, -1, "kv_drope"), from_heads(dv3)], axis=1)
            sp.source["kv"] = grad_w(u_kv_t, dkv, "kv_dw").reshape(W["kv"].shape)
            db_kv = colsum(dkv, "kv_db")
            du_kv = mm_nt(dkv[None], full(W["kv"]), F32, "kv_du", natural=True)
            dh, d_kv_gain, d_kv_shift, d_kv_scale = adaln_bwd(h_kv, kv_gain[None], mod_kv[0], mod_kv[1], du_kv, dh,
                                                              "kv_dnorm")
        for s in (2, 1, 0):
            tag = f"l{l}s{s}"
            shift, scale, gate = mod[l][s, 0], mod[l][s, 1], mod[l][s, 2]
            h_in, ut, y, res = saved[(l, s)]
            dy, d_gate = gate_bwd(dh, y, gate, 0.5 if s != 1 else 1.0, tag + "_dres")
            if s != 1:
                du = _ffn_bwd(sp, W, dy, ut, res[0], res[2], l, s // 2, tag)
            elif l == 0:
                proj, o, states, zt = res
                sp.source["hout"] = grad_w(zt, dy, tag + "_dwout").reshape(W["hout"].shape)
                dz = mm_nt(dy[None], full(W["hout"]), F32, tag + "_dz", natural=True)
                do, dg, d_head_gain = hgrn_post_bwd(o, proj, head_gain, dz, tag + "_dpost")
                dproj, d_lb0 = sp.run(hgrn_scan_bwd, proj, lb0, states, do, dg, name=tag + "_dscan")
                for hf in range(2):
                    sp.source[("hin", hf)] = sp.run(mm_nn, ut.reshape(2, D // 2, S), dproj, BF16, name=f"{tag}_dwin{hf}",
                                                    b_natural=True, jn=N_DEV, a_sel=hf, tm=GRAD_TM, tk=GRAD_TK)
                du = sp.run(mm_nt, dproj, W["hin"][:, None], F32, name=tag + "_du", natural=True, a_natural=True)
            else:
                q4, att_t = res
                sp.source["o"] = grad_w(att_t, dy, tag + "_dwout").reshape(W["o"].shape)
                datt = mm_nt(dy[None], full(W["o"]), F32, tag + "_datt", natural=True)
                datt4 = to_heads(datt, NQ).reshape(NKV, ATT_GROUP, S, ATT_HEAD)
                dq4, (dkp, dkc, dvp, dvc), d_sink = sp.run(attn_bwd, q4, k3, v3, sink_col, datt4, name=tag + "_dattn")
                dk3 = band_combine(dkc, dkp, tag + "_dk")
                dv3 = band_combine(dvc, dvp, tag + "_dv")
                dq = rope(from_heads(dq4.reshape(NQ, S, ATT_HEAD)), tables, -1, tag + "_drope")
                sp.source["q"] = grad_w(ut, dq, tag + "_dwq").reshape(W["q"].shape)
                db_q = colsum(dq, tag + "_dbq")
                du = mm_nt(dq[None], full(W["q"]), F32, tag + "_du", natural=True)
            dh, dg_, dsh_, dsc_ = adaln_bwd(h_in, gains[l, s], shift, scale, du, dh, tag + "_dnorm")
            d_gain[l][s] = dg_
            d_mod[l][s] = jnp.concatenate([dsh_, dsc_, d_gate], axis=1)
    grad_x = dh[None]
    G = sp.got
    rest = [k for k in sp.source if k not in G]
    if rest:
        G.update(zip(rest, exchange(scatter_side([sp.source[k] for k in rest]), "scatter_rest")))

    pad = lambda a, n: jnp.pad(a, ((0, 0), (0, n - a.shape[1])))
    pieces = [jnp.concatenate(d_mod[l], axis=1) for l in range(L)]
    pieces += [d_kv_shift, d_kv_scale]
    pieces += [d_gain[l][s] for l in range(L) for s in range(3)]
    pieces += [d_head_gain, d_lb0, d_kv_gain, db_kv, db_q, pad(d_sink.reshape(1, NQ), LANES), d_final_gain]
    (g3,) = exchange(gather_side([jnp.concatenate(pieces, axis=1)]), "gather_small_grads")
    g3 = g3.reshape(N_DEV, -1)
    tot = rowsum(g3, "sum_small_grads")
    offs = [0]
    for p in pieces:
        offs.append(offs[-1] + p.shape[1])
    seg = lambda k: tot[:, offs[k]:offs[k + 1]]
    k0 = 0
    g_b_ada = jnp.concatenate([seg(l) for l in range(L)], axis=0)
    k0 += L
    g_b_ada_kv = jnp.concatenate([seg(k0), seg(k0 + 1)], axis=1)
    k0 += 2
    g_norm_gain = jnp.concatenate([my_cols(seg(k0 + j), dsh) for j in range(3 * L)], axis=0)
    k0 += 3 * L
    g_head_gain = my_cols(seg(k0), dsh)
    d_lb_sh = my_cols(seg(k0 + 1), dsh)
    g_kv_gain = seg(k0 + 2)
    g_b_kv = seg(k0 + 3)
    g_b_q = seg(k0 + 4)
    g_sinks = seg(k0 + 5)[:, :NQ]
    g_final_gain = seg(k0 + 6)
    dl0, dl1 = lower_bound_bwd(hgrn_lb_logits[0:1], hgrn_lb_logits[1:2], d_lb_sh, "lb_bwd")
    g_lb_logits = jnp.concatenate([dl0, dl1], axis=0)

    g_w_ada = jnp.stack([cond_grad(c_all, lax.dynamic_slice_in_dim(g3, offs[l] + me * ada_n, ada_n, axis=1),
                                   f"dw_ada{l}") for l in range(L)])
    g_w_ada_kv = cond_grad(c_all, lax.dynamic_slice_in_dim(g3, offs[L] + me * kv_n, kv_n, axis=1), "dw_ada_kv")

    def update(w, m, v, grads, name):
        n = len(grads)
        three = lambda a: a.reshape((n, -1, w.shape[-1]))
        outs = None
        for j, g in enumerate(grads):
            outs = adamw(three(w), three(m), three(v), j, g, f"{name}{j}", into=outs)
        return [o.reshape(w.shape) for o in outs]

    def one(w, m, v, g, name):
        return update(w, m, v, [g.reshape((1, -1, w.shape[-1]))], name)

    res = {}
    res["norm_gain"] = one(norm_gain, m_norm_gain, v_norm_gain, g_norm_gain, "adam_norm_gain")
    res["w_ada"] = one(w_ada, m_w_ada, v_w_ada, g_w_ada, "adam_w_ada")
    res["b_ada"] = one(b_ada, m_b_ada, v_b_ada, g_b_ada, "adam_b_ada")
    g_ffn_in = [G[("in", l, i, hf)] for l in range(L) for i in range(2) for hf in range(2)]
    g_ffn_out = [G[("out", l, i)] for l in range(L) for i in range(2)]
    res["w_ffn_in"] = update(w_ffn_in, m_w_ffn_in, v_w_ffn_in, g_ffn_in, "adam_w_ffn_in")
    res["w_ffn_out"] = update(w_ffn_out, m_w_ffn_out, v_w_ffn_out, g_ffn_out, "adam_w_ffn_out")
    res["w_hgrn_in"] = update(w_hgrn_in, m_w_hgrn_in, v_w_hgrn_in, [G[("hin", 0)], G[("hin", 1)]], "adam_w_hgrn_in")
    res["hgrn_lb_logits"] = one(hgrn_lb_logits, m_hgrn_lb_logits, v_hgrn_lb_logits, g_lb_logits, "adam_lb")
    res["hgrn_head_gain"] = one(hgrn_head_gain, m_hgrn_head_gain, v_hgrn_head_gain, g_head_gain, "adam_head_gain")
    res["w_hgrn_out"] = update(w_hgrn_out, m_w_hgrn_out, v_w_hgrn_out, [G["hout"]], "adam_w_hgrn_out")
    res["kv_gain"] = one(kv_gain, m_kv_gain, v_kv_gain, g_kv_gain, "adam_kv_gain")
    res["w_ada_kv"] = one(w_ada_kv, m_w_ada_kv, v_w_ada_kv, g_w_ada_kv, "adam_w_ada_kv")
    res["b_ada_kv"] = one(b_ada_kv, m_b_ada_kv, v_b_ada_kv, g_b_ada_kv, "adam_b_ada_kv")
    res["w_kv"] = update(w_kv, m_w_kv, v_w_kv, [G["kv"]], "adam_w_kv")
    res["b_kv"] = one(b_kv, m_b_kv, v_b_kv, g_b_kv, "adam_b_kv")
    res["w_q"] = update(w_q, m_w_q, v_w_q, [G["q"]], "adam_w_q")
    res["b_q"] = one(b_q, m_b_q, v_b_q, g_b_q, "adam_b_q")
    res["attn_sinks"] = one(attn_sinks, m_attn_sinks, v_attn_sinks, g_sinks, "adam_sinks")
    res["w_attn_out"] = update(w_attn_out, m_w_attn_out, v_w_attn_out, [G["o"]], "adam_w_attn_out")
    res["final_gain"] = one(final_gain, m_final_gain, v_final_gain, g_final_gain, "adam_final_gain")

    names = ["norm_gain", "w_ada", "b_ada", "w_ffn_in", "w_ffn_out", "w_hgrn_in", "hgrn_lb_logits", "hgrn_head_gain",
             "w_hgrn_out", "kv_gain", "w_ada_kv", "b_ada_kv", "w_kv", "b_kv", "w_q", "b_q", "attn_sinks", "w_attn_out",
             "final_gain"]
    return (loss, grad_x, *[res[n][0] for n in names], *[res[n][1] for n in names], *[res[n][2] for n in names],
            *[res[n][3] for n in names])
```

```python
import functools

import jax
import jax.numpy as jnp
from jax import lax
from jax.experimental import pallas as pl
from jax.experimental.pallas import tpu as pltpu

F32 = jnp.float32
BF16 = jnp.bfloat16
MESH = pl.DeviceIdType.MESH

N_DEV = 8
V7X_VMEM_LIMIT_BYTES = 56 * 1024 * 1024
LANES = 128

NORM_EPS = 1e-6
NEG_INF = -1e30
HGRN_CHUNK = 32
HGRN_HEAD = 128
ATT_HEAD = 64
ATT_WINDOW = 128
ATT_GROUP = 8
ROT_DIM = 16
ROPE_THETA = 500000.0

ADAM_LR = 0.001
ADAM_B1 = 0.9
ADAM_B2 = 0.999
ADAM_EPS = 1e-08
ADAM_WD = 0.01
ADAM_STEP = 10


def _params(*sem):
    return pltpu.CompilerParams(dimension_semantics=sem, vmem_limit_bytes=V7X_VMEM_LIMIT_BYTES)


def _tile(n, pref, unit=LANES):
    t = (min(n, pref) // unit) * unit
    while t >= unit:
        if n % t == 0:
            return t
        t -= unit
    return n


_ANY = pl.BlockSpec(memory_space=pl.ANY)


class _Side:
    def __init__(self, inputs, out_shapes, sem_shapes, start, finish):
        self.inputs, self.out_shapes, self.sem_shapes = list(inputs), list(out_shapes), list(sem_shapes)
        self.start, self.finish = start, finish


def _call(body, *, name, grid, in_specs, out_specs, out_shape, args, semantics, scratch_shapes=(), side=None):
    in_specs, out_specs, out_shape = list(in_specs), list(out_specs), list(out_shape)
    scratch_shapes = list(scratch_shapes)
    if side is None:
        outs = pl.pallas_call(
            body, name=name, grid=grid, in_specs=in_specs, out_specs=out_specs, out_shape=out_shape,
            scratch_shapes=scratch_shapes, compiler_params=_params(*semantics))(*args)
        return list(outs)
    n_in, n_out, n_scr = len(in_specs), len(out_specs), len(scratch_shapes)
    s_in, s_out = len(side.inputs), len(side.out_shapes)

    def carried(*refs):
        ins, refs = refs[:n_in], refs[n_in:]
        side_ins, refs = refs[:s_in], refs[s_in:]
        outs, refs = refs[:n_out], refs[n_out:]
        side_outs, refs = refs[:s_out], refs[s_out:]
        scratch, sems = refs[:n_scr], refs[n_scr:]
        first = pl.program_id(0) == 0
        last = pl.program_id(0) == grid[0] - 1
        for d in range(1, len(grid)):
            first = jnp.logical_and(first, pl.program_id(d) == 0)
            last = jnp.logical_and(last, pl.program_id(d) == grid[d] - 1)

        @pl.when(first)
        def _():
            side.start(side_ins, side_outs, sems)

        body(*ins, *outs, *scratch)

        @pl.when(last)
        def _():
            side.finish(side_ins, side_outs, sems)

    outs = pl.pallas_call(
        carried, name=name, grid=grid, in_specs=in_specs + [_ANY] * s_in, out_specs=out_specs + [_ANY] * s_out,
        out_shape=out_shape + side.out_shapes, scratch_shapes=scratch_shapes + side.sem_shapes,
        compiler_params=_params(*(["arbitrary"] * len(grid))))(*args, *side.inputs)
    return list(outs[:n_out]), list(outs[n_out:])


def _accumulate(prod, o_ref, acc_ref, k, nk):
    if nk == 1:
        o_ref[...] = prod.astype(o_ref.dtype)
        return

    @pl.when(k == 0)
    def _():
        acc_ref[...] = prod

    @pl.when(k > 0)
    def _():
        acc_ref[...] += prod

    @pl.when(k == nk - 1)
    def _():
        o_ref[...] = acc_ref[...].astype(o_ref.dtype)


def mm_nn(a3, b4, out_dtype, name, natural=False, a_natural=False, b_natural=False, jn=None, a_sel=None, bias=None,
          tm=512, tk=None, side=None):
    if b_natural:
        JK, JN, kb = 1, jn, b4.shape[0]
        nb = b4.shape[1] // JN
    else:
        JK, JN, kb, nb = b4.shape
    M = a3.shape[0] if a_natural else a3.shape[1]
    tm = _tile(M, tm, 16)
    tn = _tile(nb, 1024)
    tk = kb if tk is None else _tile(kb, tk)
    ntn, nkt = nb // tn, kb // tk
    nk = JK * nkt

    def body(*refs):
        if bias is None:
            a_ref, b_ref, o_ref, acc_ref = refs
        else:
            a_ref, b_ref, bias_ref, o_ref, acc_ref = refs
        prod = jnp.dot(a_ref[...].astype(BF16), b_ref[...].astype(BF16), preferred_element_type=F32)
        if bias is not None:
            prod = prod + bias_ref[...]
        _accumulate(prod, o_ref, acc_ref, pl.program_id(2), nk)

    if a_natural:
        a_spec = pl.BlockSpec((tm, tk), lambda j, i, k: (i, k))
    elif a_sel is not None:
        a_spec = pl.BlockSpec((None, tm, tk), lambda j, i, k: (a_sel, i, k))
    else:
        a_spec = pl.BlockSpec((None, tm, tk), lambda j, i, k: (k // nkt, i, k % nkt))
    if b_natural:
        b_spec = pl.BlockSpec((tk, tn), lambda j, i, k: (k, j))
    else:
        b_spec = pl.BlockSpec((None, None, tk, tn), lambda j, i, k: (k // nkt, j // ntn, k % nkt, j % ntn))
    in_specs = [a_spec, b_spec]
    args = [a3, b4]
    if bias is not None:
        assert natural and nk == 1
        in_specs.append(pl.BlockSpec((1, tn), lambda j, i, k: (0, j)))
        args.append(bias)
    if natural:
        out_shape = jax.ShapeDtypeStruct((M, JN * nb), out_dtype)
        o_spec = pl.BlockSpec((tm, tn), lambda j, i, k: (i, j))
    else:
        out_shape = jax.ShapeDtypeStruct((JN, M, nb), out_dtype)
        o_spec = pl.BlockSpec((None, tm, tn), lambda j, i, k: (j // ntn, i, j % ntn))
    res = _call(body, name=name, grid=(JN * ntn, M // tm, nk), in_specs=in_specs, out_specs=[o_spec],
                out_shape=[out_shape], scratch_shapes=[pltpu.VMEM((tm, tn), F32)], args=args,
                semantics=("parallel", "parallel", "arbitrary"), side=side)
    return res[0] if side is None else (res[0][0], res[1])


def mm_nt(a3, b4, out_dtype, name, natural=False, a_natural=False, kgroup=1, side=None):
    JK, JN, nb, kb = b4.shape
    M = a3.shape[0] if a_natural else a3.shape[1]
    tm = min(M, 512)
    tn = _tile(nb, 1024)
    ntn = nb // tn
    nk = JK // kgroup

    def body(a_ref, b_ref, o_ref, acc_ref):
        nt = (((1,), (1,)), ((), ()))
        if kgroup == 1:
            prod = lax.dot_general(a_ref[...].astype(BF16), b_ref[...].astype(BF16), nt, preferred_element_type=F32)
        else:
            prod = lax.dot_general(a_ref[0].astype(BF16), b_ref[0].astype(BF16), nt, preferred_element_type=F32)
            for g in range(1, kgroup):
                prod += lax.dot_general(a_ref[g].astype(BF16), b_ref[g].astype(BF16), nt, preferred_element_type=F32)
        _accumulate(prod, o_ref, acc_ref, pl.program_id(2), nk)

    if kgroup > 1:
        assert JN == 1 and not a_natural
        a3 = a3.reshape(nk, kgroup, M, kb)
        b4 = b4.reshape(nk, kgroup, nb, kb)
        a_spec = pl.BlockSpec((None, kgroup, tm, kb), lambda j, i, k: (k, 0, i, 0))
        b_spec = pl.BlockSpec((None, kgroup, tn, kb), lambda j, i, k: (k, 0, j, 0))
    elif a_natural:
        a_spec = pl.BlockSpec((tm, kb), lambda j, i, k: (i, k))
        b_spec = pl.BlockSpec((None, None, tn, kb), lambda j, i, k: (k, j // ntn, j % ntn, 0))
    else:
        a_spec = pl.BlockSpec((None, tm, kb), lambda j, i, k: (k, i, 0))
        b_spec = pl.BlockSpec((None, None, tn, kb), lambda j, i, k: (k, j // ntn, j % ntn, 0))
    if natural:
        out_shape = jax.ShapeDtypeStruct((M, JN * nb), out_dtype)
        o_spec = pl.BlockSpec((tm, tn), lambda j, i, k: (i, j))
    else:
        out_shape = jax.ShapeDtypeStruct((JN, M, nb), out_dtype)
        o_spec = pl.BlockSpec((None, tm, tn), lambda j, i, k: (j // ntn, i, j % ntn))
    res = _call(body, name=name, grid=(JN * ntn, M // tm, nk), in_specs=[a_spec, b_spec], out_specs=[o_spec],
                out_shape=[out_shape], scratch_shapes=[pltpu.VMEM((tm, tn), F32)], args=[a3, b4],
                semantics=("parallel", "parallel", "arbitrary"), side=side)
    return res[0] if side is None else (res[0][0], res[1])


def _row_tile(S):
    return min(S, 256)


def _adaln(h, gain, shift, scale):
    y = h * lax.rsqrt(jnp.mean(h * h, axis=-1, keepdims=True) + NORM_EPS) * gain
    return y * (1.0 + scale) + shift


def adaln_fwd(h, gain, shift, scale, name):
    S, D = h.shape
    tm = _row_tile(S)

    def body(h_ref, g_ref, sh_ref, sc_ref, u_ref, ut_ref):
        u = _adaln(h_ref[...], g_ref[...], sh_ref[...], sc_ref[...])
        u_ref[...] = u.astype(BF16)
        ut_ref[...] = u.T.astype(BF16)

    row = pl.BlockSpec((tm, D), lambda i: (i, 0))
    vec = pl.BlockSpec((1, D), lambda i: (0, 0))
    return pl.pallas_call(
        body, name=name, grid=(S // tm,), in_specs=[row, vec, vec, vec],
        out_specs=[row, pl.BlockSpec((D, tm), lambda i: (0, i))],
        out_shape=[jax.ShapeDtypeStruct((S, D), BF16), jax.ShapeDtypeStruct((D, S), BF16)],
        compiler_params=_params("parallel"),
    )(h, gain, shift, scale)


def adaln_bwd(h, gain, shift, scale, du, dres, name):
    S, D = h.shape
    tm = _row_tile(S)

    def body(h_ref, g_ref, sh_ref, sc_ref, du_ref, dres_ref, dh_ref, dg_ref, dsh_ref, dsc_ref):
        _, vjp = jax.vjp(_adaln, h_ref[...], g_ref[...], sh_ref[...], sc_ref[...])
        dh, dg, dsh, dsc = vjp(du_ref[...].astype(F32))
        dh_ref[...] = dres_ref[...] + dh

        @pl.when(pl.program_id(0) == 0)
        def _():
            dg_ref[...] = jnp.zeros_like(dg_ref)
            dsh_ref[...] = jnp.zeros_like(dsh_ref)
            dsc_ref[...] = jnp.zeros_like(dsc_ref)

        dg_ref[...] += dg
        dsh_ref[...] += dsh
        dsc_ref[...] += dsc

    row = pl.BlockSpec((tm, D), lambda i: (i, 0))
    vec = pl.BlockSpec((1, D), lambda i: (0, 0))
    vs = jax.ShapeDtypeStruct((1, D), F32)
    return pl.pallas_call(
        body, name=name, grid=(S // tm,), in_specs=[row, vec, vec, vec, row, row], out_specs=[row, vec, vec, vec],
        out_shape=[jax.ShapeDtypeStruct((S, D), F32), vs, vs, vs], compiler_params=_params("arbitrary"),
    )(h, gain, shift, scale, du, dres)


def resid_fwd(h, y, gate, coef, name):
    S, D = h.shape
    tm = _row_tile(S)

    def body(h_ref, y_ref, g_ref, o_ref):
        o_ref[...] = h_ref[...] + (coef * g_ref[...]) * y_ref[...]

    row = pl.BlockSpec((tm, D), lambda i: (i, 0))
    vec = pl.BlockSpec((1, D), lambda i: (0, 0))
    return pl.pallas_call(
        body, name=name, grid=(S // tm,), in_specs=[row, row, vec], out_specs=row,
        out_shape=jax.ShapeDtypeStruct((S, D), F32), compiler_params=_params("parallel"),
    )(h, y, gate)


def gate_bwd(dh, y, gate, coef, name):
    S, D = dh.shape
    tm = _row_tile(S)

    def body(dh_ref, y_ref, g_ref, dy_ref, dg_ref):
        dh = dh_ref[...]
        dy_ref[...] = ((coef * g_ref[...]) * dh).astype(BF16)

        @pl.when(pl.program_id(0) == 0)
        def _():
            dg_ref[...] = jnp.zeros_like(dg_ref)

        dg_ref[...] += coef * jnp.sum(dh * y_ref[...], axis=0, keepdims=True)

    row = pl.BlockSpec((tm, D), lambda i: (i, 0))
    vec = pl.BlockSpec((1, D), lambda i: (0, 0))
    return pl.pallas_call(
        body, name=name, grid=(S // tm,), in_specs=[row, row, vec], out_specs=[row, vec],
        out_shape=[jax.ShapeDtypeStruct((S, D), BF16), jax.ShapeDtypeStruct((1, D), F32)],
        compiler_params=_params("arbitrary"),
    )(dh, y, gate)


def _swiglu(a, b):
    return a * jax.nn.sigmoid(a) * b


def ffn_in_act(u, w_in, name, side=None):
    J2, D, nb = w_in.shape
    J = J2 // 2
    S = u.shape[0]
    tm = _tile(S, 512, 16)

    def body(u_ref, wa_ref, wb_ref, ab_ref, h_ref, ht_ref):
        uv = u_ref[...]
        a = jnp.dot(uv, wa_ref[...], preferred_element_type=F32)
        b = jnp.dot(uv, wb_ref[...], preferred_element_type=F32)
        ab_ref[0] = a
        ab_ref[1] = b
        hv = _swiglu(a, b)
        h_ref[...] = hv.astype(BF16)
        ht_ref[...] = hv.T.astype(BF16)

    res = _call(body, name=name, grid=(J, S // tm),
                in_specs=[pl.BlockSpec((tm, D), lambda j, i: (i, 0)),
                          pl.BlockSpec((None, D, nb), lambda j, i: (j, 0, 0)),
                          pl.BlockSpec((None, D, nb), lambda j, i: (j + J, 0, 0))],
                out_specs=[pl.BlockSpec((2, None, tm, nb), lambda j, i: (0, j, i, 0)),
                           pl.BlockSpec((None, tm, nb), lambda j, i: (j, i, 0)),
                           pl.BlockSpec((None, nb, tm), lambda j, i: (j, 0, i))],
                out_shape=[jax.ShapeDtypeStruct((2, J, S, nb), F32), jax.ShapeDtypeStruct((J, S, nb), BF16),
                           jax.ShapeDtypeStruct((J, nb, S), BF16)],
                args=[u, w_in, w_in], semantics=("parallel", "parallel"), side=side)
    return res if side is None else (res[0], res[1])


def ffn_dact(dy, w_out4, ab4, name, side=None):
    J, nb, D = w_out4.shape
    S = dy.shape[0]
    tm = _tile(S, 512, 16)

    def body(dy_ref, w_ref, ab_ref, o_ref):
        dh = lax.dot_general(dy_ref[...], w_ref[...], (((1,), (1,)), ((), ())), preferred_element_type=F32)
        _, vjp = jax.vjp(_swiglu, ab_ref[0], ab_ref[1])
        da, db = vjp(dh)
        o_ref[0] = da.astype(BF16)
        o_ref[1] = db.astype(BF16)

    both = pl.BlockSpec((2, None, tm, nb), lambda j, i: (0, j, i, 0))
    res = _call(body, name=name, grid=(J, S // tm),
                in_specs=[pl.BlockSpec((tm, D), lambda j, i: (i, 0)),
                          pl.BlockSpec((None, nb, D), lambda j, i: (j, 0, 0)), both],
                out_specs=[both], out_shape=[jax.ShapeDtypeStruct((2, J, S, nb), BF16)],
                args=[dy, w_out4, ab4], semantics=("parallel", "parallel"), side=side)
    if side is None:
        return res[0].reshape(2 * J, S, nb)
    return res[0][0].reshape(2 * J, S, nb), res[1]


def colsum(x, name):
    S, N = x.shape
    tm = _row_tile(S)

    def body(x_ref, o_ref):
        @pl.when(pl.program_id(0) == 0)
        def _():
            o_ref[...] = jnp.zeros_like(o_ref)

        o_ref[...] += jnp.sum(x_ref[...].astype(F32), axis=0, keepdims=True)

    return pl.pallas_call(
        body, name=name, grid=(S // tm,), in_specs=[pl.BlockSpec((tm, N), lambda i: (i, 0))],
        out_specs=pl.BlockSpec((1, N), lambda i: (0, 0)), out_shape=jax.ShapeDtypeStruct((1, N), F32),
        compiler_params=_params("arbitrary"),
    )(x)


def _final_loss(h, gain, target):
    y = h * lax.rsqrt(jnp.mean(h * h, axis=-1, keepdims=True) + NORM_EPS) * gain
    err = y - target
    return 0.5 * jnp.sum(jnp.mean(err * err, axis=-1))


def final_loss_grad(h, gain, target, name):
    S, D = h.shape
    tm = _row_tile(S)

    def body(h_ref, g_ref, t_ref, loss_ref, dh_ref, dg_ref):
        loss, (dh, dg) = jax.value_and_grad(_final_loss, argnums=(0, 1))(h_ref[...], g_ref[...], t_ref[...])
        dh_ref[...] = dh

        @pl.when(pl.program_id(0) == 0)
        def _():
            loss_ref[...] = jnp.zeros_like(loss_ref)
            dg_ref[...] = jnp.zeros_like(dg_ref)

        loss_ref[...] += jnp.full(loss_ref.shape, loss, F32)
        dg_ref[...] += dg

    row = pl.BlockSpec((tm, D), lambda i: (i, 0))
    vec = pl.BlockSpec((1, D), lambda i: (0, 0))
    return pl.pallas_call(
        body, name=name, grid=(S // tm,), in_specs=[row, vec, row],
        out_specs=[pl.BlockSpec((1, LANES), lambda i: (0, 0)), row, vec],
        out_shape=[jax.ShapeDtypeStruct((1, LANES), F32), jax.ShapeDtypeStruct((S, D), F32),
                   jax.ShapeDtypeStruct((1, D), F32)],
        compiler_params=_params("arbitrary"),
    )(h, gain, target)


def _chunk_consts(H):
    C = HGRN_CHUNK
    t = lax.broadcasted_iota(jnp.int32, (H, C, C), 1)
    s = lax.broadcasted_iota(jnp.int32, (H, C, C), 2)
    return (s <= t).astype(F32), s <= t


def _hgrn_chunk(q_raw, f_raw, i_raw, lb, st):
    H, C, _ = q_raw.shape
    lower, causal = _chunk_consts(H)
    forget = lb + (1.0 - lb) * jax.nn.sigmoid(f_raw)
    g = jnp.log(forget)
    kk = 1.0 - forget
    qs = q_raw * jax.nn.sigmoid(q_raw)
    bnn = (((2,), (1,)), ((0,), (0,)))
    bnt = (((2,), (2,)), ((0,), (0,)))
    btn = (((1,), (1,)), ((0,), (0,)))
    b = lax.dot_general(lower, g, bnn, precision=lax.Precision.HIGHEST, preferred_element_type=F32)
    bm = b[:, C // 2 - 1:C // 2, :]
    bl = b[:, C - 1:C, :]
    inter = lax.dot_general((qs * jnp.exp(b)).astype(BF16), st.astype(BF16), bnt, preferred_element_type=F32)
    qt = (qs * jnp.exp(b - bm)).astype(BF16)
    kt = (kk * jnp.exp(bm - b)).astype(BF16)
    scores = lax.dot_general(qt, kt, bnt, preferred_element_type=F32)
    scores = jnp.where(causal, scores, 0.0)
    vb = i_raw.astype(BF16)
    out = inter + lax.dot_general(scores.astype(BF16), vb, bnn, preferred_element_type=F32)
    kdec = (kk * jnp.exp(bl - b)).astype(BF16)
    new_st = st * jnp.exp(bl) + lax.dot_general(vb, kdec, btn, preferred_element_type=F32)
    return out, new_st


def _heads(ref, rows, H):
    return jnp.stack([ref[rows, pl.ds(h * HGRN_HEAD, HGRN_HEAD)] for h in range(H)])


def hgrn_scan_fwd(proj, lb, name, side=None):
    S, D4 = proj.shape
    D = D4 // 4
    H = D // HGRN_HEAD
    C = HGRN_CHUNK
    R = min(S, 128)
    ncr = R // C

    def body(q_ref, f_ref, i_ref, lb_ref, o_ref, st_ref, state):
        @pl.when(pl.program_id(0) == 0)
        def _():
            state[...] = jnp.zeros_like(state)

        lbh = _heads(lb_ref, slice(None), H)

        def chunk(cc, carry):
            rows = pl.ds(pl.multiple_of(cc * C, C), C)
            st = state[...]
            st_ref[cc] = st
            out, new_st = _hgrn_chunk(_heads(q_ref, rows, H), _heads(f_ref, rows, H), _heads(i_ref, rows, H), lbh, st)
            for h in range(H):
                o_ref[rows, pl.ds(h * HGRN_HEAD, HGRN_HEAD)] = out[h]
            state[...] = new_st
            return carry

        lax.fori_loop(0, ncr, chunk, 0)

    col = lambda j: pl.BlockSpec((R, D), lambda i: (i, j))
    res = _call(body, name=name, grid=(S // R,),
                in_specs=[col(0), col(1), col(2), pl.BlockSpec((1, D), lambda i: (0, 0))],
                out_specs=[pl.BlockSpec((R, D), lambda i: (i, 0)),
                           pl.BlockSpec((ncr, H, HGRN_HEAD, HGRN_HEAD), lambda i: (i, 0, 0, 0))],
                out_shape=[jax.ShapeDtypeStruct((S, D), F32),
                           jax.ShapeDtypeStruct((S // C, H, HGRN_HEAD, HGRN_HEAD), F32)],
                scratch_shapes=[pltpu.VMEM((H, HGRN_HEAD, HGRN_HEAD), F32)], args=[proj, proj, proj, lb],
                semantics=("arbitrary",), side=side)
    return res if side is None else (res[0], res[1])


def hgrn_scan_bwd(proj, lb, states, do, dg, name, side=None):
    S, D4 = proj.shape
    D = D4 // 4
    H = D // HGRN_HEAD
    C = HGRN_CHUNK
    R = min(S, 128)
    ncr = R // C
    ng = S // R

    def body(q_ref, f_ref, i_ref, lb_ref, st_ref, do_ref, dg_ref, dp_ref, dlb_ref, dstate):
        @pl.when(pl.program_id(0) == 0)
        def _():
            dstate[...] = jnp.zeros_like(dstate)
            dlb_ref[...] = jnp.zeros_like(dlb_ref)

        dp_ref[:, pl.ds(3 * D, D)] = dg_ref[...].astype(BF16)
        lbh = _heads(lb_ref, slice(None), H)

        def chunk(t, carry):
            cc = ncr - 1 - t
            rows = pl.ds(pl.multiple_of(cc * C, C), C)
            _, vjp = jax.vjp(_hgrn_chunk, _heads(q_ref, rows, H), _heads(f_ref, rows, H), _heads(i_ref, rows, H),
                             lbh, st_ref[cc])
            dq, df, di, dlb, dst = vjp((_heads(do_ref, rows, H), dstate[...]))
            for h in range(H):
                dp_ref[rows, pl.ds(h * HGRN_HEAD, HGRN_HEAD)] = dq[h].astype(BF16)
                dp_ref[rows, pl.ds(D + h * HGRN_HEAD, HGRN_HEAD)] = df[h].astype(BF16)
                dp_ref[rows, pl.ds(2 * D + h * HGRN_HEAD, HGRN_HEAD)] = di[h].astype(BF16)
                dlb_ref[:, pl.ds(h * HGRN_HEAD, HGRN_HEAD)] += dlb[h]
            dstate[...] = dst
            return carry

        lax.fori_loop(0, ncr, chunk, 0)

    col = lambda j: pl.BlockSpec((R, D), lambda i: (ng - 1 - i, j))
    res = _call(body, name=name, grid=(ng,),
                in_specs=[col(0), col(1), col(2), pl.BlockSpec((1, D), lambda i: (0, 0)),
                          pl.BlockSpec((ncr, H, HGRN_HEAD, HGRN_HEAD), lambda i: (ng - 1 - i, 0, 0, 0)),
                          pl.BlockSpec((R, D), lambda i: (ng - 1 - i, 0)),
                          pl.BlockSpec((R, D), lambda i: (ng - 1 - i, 0))],
                out_specs=[pl.BlockSpec((R, D4), lambda i: (ng - 1 - i, 0)), pl.BlockSpec((1, D), lambda i: (0, 0))],
                out_shape=[jax.ShapeDtypeStruct((S, D4), BF16), jax.ShapeDtypeStruct((1, D), F32)],
                scratch_shapes=[pltpu.VMEM((H, HGRN_HEAD, HGRN_HEAD), F32)],
                args=[proj, proj, proj, lb, states, do, dg], semantics=("arbitrary",), side=side)
    return res if side is None else (res[0], res[1])


def _head_out(o, g, gain):
    y = o * lax.rsqrt(jnp.mean(o * o, axis=-1, keepdims=True) + NORM_EPS) * gain
    return y * jax.nn.sigmoid(g)


def hgrn_post_fwd(o, proj, gain, name):
    S, D = o.shape
    H = D // HGRN_HEAD
    tm = _row_tile(S)

    def body(o_ref, g_ref, gain_ref, z_ref, zt_ref):
        for h in range(H):
            ls = pl.ds(h * HGRN_HEAD, HGRN_HEAD)
            z = _head_out(o_ref[:, ls], g_ref[:, ls], gain_ref[:, ls])
            z_ref[:, ls] = z.astype(BF16)
            zt_ref[ls, :] = z.T.astype(BF16)

    row = pl.BlockSpec((tm, D), lambda i: (i, 0))
    return pl.pallas_call(
        body, name=name, grid=(S // tm,),
        in_specs=[row, pl.BlockSpec((tm, D), lambda i: (i, 3)), pl.BlockSpec((1, D), lambda i: (0, 0))],
        out_specs=[row, pl.BlockSpec((D, tm), lambda i: (0, i))],
        out_shape=[jax.ShapeDtypeStruct((S, D), BF16), jax.ShapeDtypeStruct((D, S), BF16)],
        compiler_params=_params("parallel"),
    )(o, proj, gain)


def hgrn_post_bwd(o, proj, gain, dz, name):
    S, D = o.shape
    H = D // HGRN_HEAD
    tm = _row_tile(S)

    def body(o_ref, g_ref, gain_ref, dz_ref, do_ref, dg_ref, dgain_ref):
        @pl.when(pl.program_id(0) == 0)
        def _():
            dgain_ref[...] = jnp.zeros_like(dgain_ref)

        for h in range(H):
            ls = pl.ds(h * HGRN_HEAD, HGRN_HEAD)
            _, vjp = jax.vjp(_head_out, o_ref[:, ls], g_ref[:, ls], gain_ref[:, ls])
            do, dg, dgain = vjp(dz_ref[:, ls].astype(F32))
            do_ref[:, ls] = do
            dg_ref[:, ls] = dg
            dgain_ref[:, ls] += dgain

    row = pl.BlockSpec((tm, D), lambda i: (i, 0))
    vec = pl.BlockSpec((1, D), lambda i: (0, 0))
    return pl.pallas_call(
        body, name=name, grid=(S // tm,),
        in_specs=[row, pl.BlockSpec((tm, D), lambda i: (i, 3)), vec, row], out_specs=[row, row, vec],
        out_shape=[jax.ShapeDtypeStruct((S, D), F32), jax.ShapeDtypeStruct((S, D), F32),
                   jax.ShapeDtypeStruct((1, D), F32)],
        compiler_params=_params("arbitrary"),
    )(o, proj, gain, dz)


def lower_bound_fwd(l0, l1, name):
    def body(a_ref, b_ref, o_ref):
        o_ref[...] = jax.nn.sigmoid(a_ref[...] - b_ref[...])

    return pl.pallas_call(body, name=name, out_shape=jax.ShapeDtypeStruct(l0.shape, F32))(l0, l1)


def lower_bound_bwd(l0, l1, dlb, name):
    def body(a_ref, b_ref, d_ref, o0_ref, o1_ref):
        s = jax.nn.sigmoid(a_ref[...] - b_ref[...])
        d0 = d_ref[...] * s * (1.0 - s)
        o0_ref[...] = d0
        o1_ref[...] = -d0

    sd = jax.ShapeDtypeStruct(l0.shape, F32)
    return pl.pallas_call(body, name=name, out_shape=[sd, sd])(l0, l1, dlb)


def _attn_tile(q, kp, kc, vp, vc, sink, first):
    W = ATT_WINDOW
    nt = (((1,), (1,)), ((), ()))
    qb = q.astype(BF16)
    scale = ATT_HEAD ** -0.5
    sp = lax.dot_general(qb, kp.astype(BF16), nt, preferred_element_type=F32) * scale
    sc = lax.dot_general(qb, kc.astype(BF16), nt, preferred_element_type=F32) * scale
    qi = lax.broadcasted_iota(jnp.int32, sp.shape, 0) & (W - 1)
    kj = lax.broadcasted_iota(jnp.int32, sp.shape, 1)
    sp = jnp.where((kj > qi) & jnp.logical_not(first), sp, NEG_INF)
    sc = jnp.where(kj <= qi, sc, NEG_INF)
    m = jnp.maximum(jnp.maximum(jnp.max(sp, axis=-1, keepdims=True), jnp.max(sc, axis=-1, keepdims=True)), sink)
    pp = jnp.exp(sp - m)
    pc = jnp.exp(sc - m)
    denom = jnp.sum(pp, axis=-1, keepdims=True) + jnp.sum(pc, axis=-1, keepdims=True) + jnp.exp(sink - m)
    out = jnp.dot((pp / denom).astype(BF16), vp.astype(BF16), preferred_element_type=F32)
    return out + jnp.dot((pc / denom).astype(BF16), vc.astype(BF16), preferred_element_type=F32)


def _attn_specs(G, W, Dh):
    q_spec = pl.BlockSpec((None, G, W, Dh), lambda j, n: (j, 0, n, 0))
    prev = pl.BlockSpec((None, W, Dh), lambda j, n: (j, jnp.maximum(n - 1, 0), 0))
    cur = pl.BlockSpec((None, W, Dh), lambda j, n: (j, n, 0))
    sink = pl.BlockSpec((None, G * W, 1), lambda j, n: (j, 0, 0))
    return q_spec, prev, cur, sink


def attn_fwd(q4, k3, v3, sink, name):
    NKV, G, S, Dh = q4.shape
    W = ATT_WINDOW

    def body(q_ref, kp_ref, kc_ref, vp_ref, vc_ref, s_ref, o_ref):
        first = pl.program_id(1) == 0
        out = _attn_tile(q_ref[...].reshape(G * W, Dh), kp_ref[...], kc_ref[...], vp_ref[...], vc_ref[...],
                         s_ref[...], first)
        o_ref[...] = out.reshape(G, W, Dh)

    q_spec, prev, cur, sk = _attn_specs(G, W, Dh)
    return pl.pallas_call(
        body, name=name, grid=(NKV, S // W), in_specs=[q_spec, prev, cur, prev, cur, sk], out_specs=q_spec,
        out_shape=jax.ShapeDtypeStruct(q4.shape, F32), compiler_params=_params("parallel", "parallel"),
    )(q4, k3, k3, v3, v3, sink)


def attn_bwd(q4, k3, v3, sink, do4, name, side=None):
    NKV, G, S, Dh = q4.shape
    W = ATT_WINDOW
    nb = S // W

    def body(q_ref, kp_ref, kc_ref, vp_ref, vc_ref, s_ref, do_ref, dq_ref, dkp_ref, dkc_ref, dvp_ref, dvc_ref,
             ds_ref):
        first = pl.program_id(1) == 0
        _, vjp = jax.vjp(functools.partial(_attn_tile, first=first), q_ref[...].reshape(G * W, Dh), kp_ref[...],
                         kc_ref[...], vp_ref[...], vc_ref[...], s_ref[...])
        dq, dkp, dkc, dvp, dvc, ds = vjp(do_ref[...].reshape(G * W, Dh))
        dq_ref[...] = dq.reshape(G, W, Dh)
        dkp_ref[...] = dkp
        dkc_ref[...] = dkc
        dvp_ref[...] = dvp
        dvc_ref[...] = dvc

        @pl.when(first)
        def _():
            ds_ref[...] = jnp.zeros_like(ds_ref)

        ds_ref[...] += jnp.sum(ds.reshape(G, W, 1), axis=1)

    q_spec, prev, cur, sk = _attn_specs(G, W, Dh)
    part = pl.BlockSpec((None, None, W, Dh), lambda j, n: (j, n, 0, 0))
    ps = jax.ShapeDtypeStruct((NKV, nb, W, Dh), F32)
    res = _call(body, name=name, grid=(NKV, nb), in_specs=[q_spec, prev, cur, prev, cur, sk, q_spec],
                out_specs=[q_spec, part, part, part, part, pl.BlockSpec((None, G, 1), lambda j, n: (j, 0, 0))],
                out_shape=[jax.ShapeDtypeStruct(q4.shape, F32), ps, ps, ps, ps,
                           jax.ShapeDtypeStruct((NKV, G, 1), F32)],
                args=[q4, k3, k3, v3, v3, sink, do4], semantics=("parallel", "arbitrary"), side=side)
    outs = res if side is None else res[0]
    ans = (outs[0], outs[1:5], outs[5])
    return ans if side is None else (ans, res[1])


def band_combine(cur, prev, name):
    NKV, nb, W, Dh = cur.shape

    def body(c_ref, p_ref, o_ref):
        keep = (pl.program_id(1) < nb - 1).astype(F32)
        o_ref[...] = c_ref[...] + keep * p_ref[...]

    return pl.pallas_call(
        body, name=name, grid=(NKV, nb),
        in_specs=[pl.BlockSpec((None, None, W, Dh), lambda j, n: (j, n, 0, 0)),
                  pl.BlockSpec((None, None, W, Dh), lambda j, n: (j, jnp.minimum(n + 1, nb - 1), 0, 0))],
        out_specs=pl.BlockSpec((None, W, Dh), lambda j, n: (j, n, 0)),
        out_shape=jax.ShapeDtypeStruct((NKV, nb * W, Dh), F32), compiler_params=_params("parallel", "parallel"),
    )(cur, prev)


def rope_tables(S):
    half = ROT_DIM // 2
    inv_freq = jnp.power(jnp.float32(ROPE_THETA), -jnp.arange(0, ROT_DIM, 2, dtype=F32) / ROT_DIM)
    ang = jnp.arange(S, dtype=F32)[:, None] * inv_freq[None, :]
    sin, cos = jnp.sin(ang), jnp.cos(ang)
    zeros = jnp.zeros((S, ATT_HEAD - ROT_DIM), F32)
    z8 = jnp.zeros((S, half), F32)
    cfull = jnp.concatenate([cos, cos, jnp.ones((S, ATT_HEAD - ROT_DIM), F32)], axis=1)
    s_next = jnp.concatenate([-sin, z8, zeros], axis=1)
    s_prev = jnp.concatenate([z8, sin, zeros], axis=1)
    two = lambda t: jnp.concatenate([t, t], axis=1)
    return two(cfull), two(s_next), two(s_prev)


def rope(x, tables, sign, name):
    S, Wd = x.shape
    tm = _row_tile(S)
    rep = Wd // LANES
    half = ROT_DIM // 2

    def body(x_ref, c_ref, sn_ref, sp_ref, o_ref):
        xv = x_ref[...]
        c = jnp.tile(c_ref[...], (1, rep))
        sn = jnp.tile(sn_ref[...], (1, rep))
        sp = jnp.tile(sp_ref[...], (1, rep))
        if sign > 0:
            nxt = pltpu.roll(xv, Wd - half, 1)
            prv = pltpu.roll(xv, half, 1)
            o_ref[...] = xv * c + nxt * sn + prv * sp
        else:
            o_ref[...] = xv * c + pltpu.roll(xv * sn, half, 1) + pltpu.roll(xv * sp, Wd - half, 1)

    row = pl.BlockSpec((tm, Wd), lambda i: (i, 0))
    tab = pl.BlockSpec((tm, LANES), lambda i: (i, 0))
    return pl.pallas_call(
        body, name=name, grid=(S // tm,), in_specs=[row, tab, tab, tab], out_specs=row,
        out_shape=jax.ShapeDtypeStruct((S, Wd), F32), compiler_params=_params("parallel"),
    )(x, *tables)


def cond_proj(c_all, w, bias, name):
    B, D = c_all.shape
    N = w.shape[1]
    tn = _tile(N, 512)

    def body(c_ref, w_ref, b_ref, o_ref):
        cv = c_ref[...]
        cs = (cv * jax.nn.sigmoid(cv)).astype(BF16)
        o_ref[...] = jnp.dot(cs, w_ref[...].astype(BF16), preferred_element_type=F32) + b_ref[...]

    return pl.pallas_call(
        body, name=name, grid=(N // tn,),
        in_specs=[pl.BlockSpec((B, D), lambda j: (0, 0)), pl.BlockSpec((D, tn), lambda j: (0, j)),
                  pl.BlockSpec((1, tn), lambda j: (0, j))],
        out_specs=pl.BlockSpec((B, tn), lambda j: (0, j)), out_shape=jax.ShapeDtypeStruct((B, N), F32),
        compiler_params=_params("parallel"),
    )(c_all, w, bias)


def cond_grad(c_all, dmod, name):
    B, D = c_all.shape
    N = dmod.shape[1]
    tn = _tile(N, 512)

    def body(c_ref, d_ref, o_ref):
        cv = c_ref[...]
        cs = (cv * jax.nn.sigmoid(cv)).astype(BF16)
        o_ref[...] = lax.dot_general(cs, d_ref[...].astype(BF16), (((0,), (0,)), ((), ())),
                                     preferred_element_type=F32)

    return pl.pallas_call(
        body, name=name, grid=(N // tn,),
        in_specs=[pl.BlockSpec((B, D), lambda j: (0, 0)), pl.BlockSpec((B, tn), lambda j: (0, j))],
        out_specs=pl.BlockSpec((D, tn), lambda j: (0, j)), out_shape=jax.ShapeDtypeStruct((D, N), F32),
        compiler_params=_params("parallel"),
    )(c_all, dmod)


def rowsum(g, name):
    B, N = g.shape
    tn = _tile(N, 8192)

    def body(g_ref, o_ref):
        acc = g_ref[0:1, :]
        for r in range(1, B):
            acc = acc + g_ref[r:r + 1, :]
        o_ref[...] = acc

    return pl.pallas_call(
        body, name=name, grid=(N // tn,), in_specs=[pl.BlockSpec((B, tn), lambda j: (0, j))],
        out_specs=pl.BlockSpec((1, tn), lambda j: (0, j)), out_shape=jax.ShapeDtypeStruct((1, N), F32),
        compiler_params=_params("parallel"),
    )(g)


def _adam_rows(R, C):
    if R * C * 4 <= (1 << 20) or R % 8:
        return R
    best = 8
    for t in range(8, R + 1, 8):
        if R % t == 0 and t * C * 4 <= (1 << 20):
            best = t
    return best


def adamw(w3, m3, v3, j, g3, name, into=None, side=None):
    n, R, C = w3.shape
    P = g3.shape[0]
    tr = _adam_rows(R, C)

    def body(*refs):
        w_ref, m_ref, v_ref, g_ref = refs[:4]
        go_ref, d_ref, mo_ref, vo_ref = refs[-4:]
        g = g_ref[0].astype(F32)
        for p in range(1, P):
            g = g + g_ref[p].astype(F32)
        mn = ADAM_B1 * m_ref[...] + (1.0 - ADAM_B1) * g
        vn = ADAM_B2 * v_ref[...] + (1.0 - ADAM_B2) * jnp.square(g)
        m_hat = mn / (1.0 - ADAM_B1 ** ADAM_STEP)
        v_hat = vn / (1.0 - ADAM_B2 ** ADAM_STEP)
        go_ref[...] = g
        d_ref[...] = -ADAM_LR * (m_hat / (jnp.sqrt(v_hat) + ADAM_EPS) + ADAM_WD * w_ref[...])
        mo_ref[...] = mn
        vo_ref[...] = vn

    spec = pl.BlockSpec((None, tr, C), lambda i: (j, i, 0))
    g_spec = pl.BlockSpec((P, tr, C), lambda i: (0, i, 0))
    sd = jax.ShapeDtypeStruct((n, R, C), F32)
    in_specs, args, aliases = [spec, spec, spec, g_spec], [w3, m3, v3, g3], {}
    if side is not None:
        assert into is None
        return _call(body, name=name, grid=(R // tr,), in_specs=in_specs, out_specs=[spec] * 4, out_shape=[sd] * 4,
                     args=args, semantics=("parallel",), side=side)
    if into is not None:
        in_specs += [_ANY] * 4
        args += list(into)
        aliases = {4 + k: k for k in range(4)}
    return pl.pallas_call(
        body, name=name, grid=(R // tr,), in_specs=in_specs, out_specs=[spec] * 4, out_shape=[sd] * 4,
        input_output_aliases=aliases, compiler_params=_params("parallel"),
    )(*args)


def _place():
    return lax.axis_index("x"), lax.axis_index("y"), lax.axis_index("c")


def _slot(p):
    return 4 * p[0] + 2 * p[1] + p[2]


def gather_side(xs):
    n = len(xs)

    def copy(ins, outs, sems, t, k, block, to, from_input=False):
        dst = outs[t].at[_slot(block)]
        return pltpu.make_async_remote_copy(
            src_ref=ins[t] if from_input else dst, dst_ref=dst, send_sem=sems[0].at[t, k],
            recv_sem=sems[1].at[t, k], device_id=to, device_id_type=MESH)

    def peers():
        x, y, c = _place()
        return (x, y, c), (x, y, 1 - c), [(1 - x, y), (x, 1 - y), (1 - x, 1 - y)]

    def start(ins, outs, sems):
        me, sibling, chips = peers()
        c = me[2]
        for t in range(n):
            pltpu.make_async_copy(ins[t], outs[t].at[_slot(me)], sems[2].at[t]).start()
            copy(ins, outs, sems, t, 0, me, sibling, True).start()
            for j, chip in enumerate(chips):
                copy(ins, outs, sems, t, 1 + j, me, (*chip, c), True).start()

    def finish(ins, outs, sems):
        me, sibling, chips = peers()
        c = me[2]
        for t in range(n):
            for j, chip in enumerate(chips):
                copy(ins, outs, sems, t, 1 + j, (*chip, c), me).wait_recv()
                copy(ins, outs, sems, t, 4 + j, (*chip, c), sibling).start()
        for t in range(n):
            copy(ins, outs, sems, t, 0, sibling, me).wait_recv()
            for j, chip in enumerate(chips):
                copy(ins, outs, sems, t, 4 + j, (*chip, 1 - c), me).wait_recv()
        for t in range(n):
            copy(ins, outs, sems, t, 0, me, sibling, True).wait_send()
            for j, chip in enumerate(chips):
                copy(ins, outs, sems, t, 1 + j, me, (*chip, c), True).wait_send()
                copy(ins, outs, sems, t, 4 + j, (*chip, c), sibling).wait_send()
            pltpu.make_async_copy(ins[t], outs[t].at[_slot(me)], sems[2].at[t]).wait()

    return _Side(xs, [jax.ShapeDtypeStruct((N_DEV,) + a.shape, a.dtype) for a in xs],
                 [pltpu.SemaphoreType.DMA((n, 7)), pltpu.SemaphoreType.DMA((n, 7)), pltpu.SemaphoreType.DMA((n,))],
                 start, finish)


def scatter_side(gs):
    n = len(gs)

    def copies(ins, outs, sems):
        x, y, c = _place()
        me = (x, y, c)
        out = []
        for t in range(n):
            out.append(pltpu.make_async_copy(ins[t].at[_slot(me)], outs[t].at[_slot(me)], sems[2].at[t]))
            for r in range(1, N_DEV):
                peer = (1 - x if r & 4 else x, 1 - y if r & 2 else y, 1 - c if r & 1 else c)
                out.append(pltpu.make_async_remote_copy(
                    src_ref=ins[t].at[_slot(peer)], dst_ref=outs[t].at[_slot(me)], send_sem=sems[0].at[t, r - 1],
                    recv_sem=sems[1].at[t, r - 1], device_id=peer, device_id_type=MESH))
        return out

    def start(ins, outs, sems):
        for cp in copies(ins, outs, sems):
            cp.start()

    def finish(ins, outs, sems):
        for cp in copies(ins, outs, sems):
            cp.wait()

    return _Side(gs, [jax.ShapeDtypeStruct(g.shape, g.dtype) for g in gs],
                 [pltpu.SemaphoreType.DMA((n, 7)), pltpu.SemaphoreType.DMA((n, 7)), pltpu.SemaphoreType.DMA((n,))],
                 start, finish)


def exchange(side, name):
    def body(*refs):
        n_in, n_out = len(side.inputs), len(side.out_shapes)
        ins, outs, sems = refs[:n_in], refs[n_in:n_in + n_out], refs[n_in + n_out:]
        side.start(ins, outs, sems)
        side.finish(ins, outs, sems)

    return pl.pallas_call(
        body, name=name, in_specs=[_ANY] * len(side.inputs), out_specs=[_ANY] * len(side.out_shapes),
        out_shape=side.out_shapes, scratch_shapes=side.sem_shapes,
    )(*side.inputs)


class _Plan:
    def __init__(self, plan, make_side):
        self.plan, self.make_side, self.source, self.got = plan, make_side, {}, {}

    def run(self, fn, *args, name, **kw):
        keys = self.plan.get(name)
        if not keys:
            return fn(*args, name=name, **kw)
        result, outs = fn(*args, name=name, side=self.make_side([self.source[k] for k in keys]), **kw)
        self.got.update(zip(keys, outs))
        return result


GATHER_PLAN = {
    "l0s0_in": [("out", 0, 0), "hout", "kv"],
    "l0s0_out": ["hin"],
    "l0s1_proj": [("out", 0, 1), "q"],
    "l0s1_scan": [("in", 0, 1)],
    "l0s2_in": [("in", 1, 0), "o"],
    "l0s2_out": [("out", 1, 0)],
    "l1s0_in": [("in", 1, 1)],
    "l1s0_out": [("out", 1, 1)],
}
GATHER_FIRST = [("in", 0, 0)]
SCATTER_PLAN = {
    "l1s2_dact": [("out", 1, 1)],
    "l1s2_du": [("in", 1, 1, 0)],
    "l1s1_dattn": [("in", 1, 1, 1), "o"],
    "l1s0_dwout": ["q"],
    "l1s0_dact": [("out", 1, 0)],
    "l1s0_du": [("in", 1, 0, 0)],
    "l0s2_dwout": ["kv"],
    "l0s2_dact": [("in", 1, 0, 1)],
    "l0s2_du": [("out", 0, 1)],
    "l0s1_dscan": [("in", 0, 1, 0), "hout"],
    "l0s1_du": [("in", 0, 1, 1)],
    "l0s0_dwout": [("hin", 0)],
    "l0s0_dact": [("hin", 1)],
    "l0s0_dwin1": [("out", 0, 0)],
    "l0s0_du": [("in", 0, 0, 0)],
    "adam_w_ada0": [("in", 0, 0, 1)],
}
GRAD_TM, GRAD_TK = 1024, 1024


def _ffn_fwd(gp, W, u, l, i, tag):
    ab4, h3, h3t = gp.run(ffn_in_act, u, W[("in", l, i)], name=tag + "_in")
    w_out = W[("out", l, i)]
    J = h3.shape[0]
    y = gp.run(mm_nn, h3, w_out.reshape(J, 1, -1, w_out.shape[-1]), F32, name=tag + "_out", natural=True)
    return y, (ab4, h3t)


def _ffn_bwd(sp, W, dy, ut, ab4, h3t, l, i, tag):
    w_in, w_out = W[("in", l, i)], W[("out", l, i)]
    J, nb, S = h3t.shape
    D = w_out.shape[-1]
    dw_out = sp.run(mm_nn, h3t.reshape(1, J * nb, S), dy[None, None], BF16, name=tag + "_dwout", natural=True,
                    tm=w_out.shape[1], tk=GRAD_TK)
    sp.source[("out", l, i)] = dw_out.reshape(w_out.shape)
    dab3 = sp.run(ffn_dact, dy, w_out.reshape(J, nb, D), ab4, name=tag + "_dact")
    for hf in range(2):
        sp.source[("in", l, i, hf)] = sp.run(mm_nn, ut.reshape(2, D // 2, S), dab3[None], BF16, name=f"{tag}_dwin{hf}",
                                             a_sel=hf, tm=GRAD_TM, tk=GRAD_TK)
    return sp.run(mm_nt, dab3, w_in[:, None], F32, name=tag + "_du", natural=True, kgroup=4)


def kernel(x, c, norm_gain, w_ada, b_ada, w_ffn_in, w_ffn_out, w_hgrn_in, hgrn_lb_logits, hgrn_head_gain, w_hgrn_out, kv_gain, w_ada_kv, b_ada_kv, w_kv, b_kv, w_q, b_q, attn_sinks, w_attn_out, final_gain, loss_target, m_norm_gain, m_w_ada, m_b_ada, m_w_ffn_in, m_w_ffn_out, m_w_hgrn_in, m_hgrn_lb_logits, m_hgrn_head_gain, m_w_hgrn_out, m_kv_gain, m_w_ada_kv, m_b_ada_kv, m_w_kv, m_b_kv, m_w_q, m_b_q, m_attn_sinks, m_w_attn_out, m_final_gain, v_norm_gain, v_w_ada, v_b_ada, v_w_ffn_in, v_w_ffn_out, v_w_hgrn_in, v_hgrn_lb_logits, v_hgrn_head_gain, v_w_hgrn_out, v_kv_gain, v_w_ada_kv, v_b_ada_kv, v_w_kv, v_b_kv, v_w_q, v_b_q, v_attn_sinks, v_w_attn_out, v_final_gain):
    xi, yi, ci = _place()
    me = 4 * xi + 2 * yi + ci
    _, S, D = x.shape
    L = norm_gain.shape[0]
    dsh = D // N_DEV
    ada_n = w_ada.shape[2]
    kv_n = w_ada_kv.shape[1]
    NQ = D // ATT_HEAD
    NKV = NQ // ATT_GROUP
    kvd = NKV * ATT_HEAD
    h0 = x[0]
    target = loss_target[0]

    def my_cols(a, n):
        return lax.dynamic_slice_in_dim(a, me * n, n, axis=a.ndim - 1)

    gp = _Plan(GATHER_PLAN, gather_side)
    bf = lambda a: a.astype(BF16)
    for l in range(L):
        for i in range(2):
            gp.source[("in", l, i)] = bf(w_ffn_in[l, i])
            gp.source[("out", l, i)] = bf(w_ffn_out[l, i])
    gp.source.update(hin=bf(w_hgrn_in[0]), hout=bf(w_hgrn_out[0]), kv=bf(w_kv), q=bf(w_q[0]), o=bf(w_attn_out[0]))
    W = gp.got
    W.update(zip(GATHER_FIRST, exchange(gather_side([gp.source[k] for k in GATHER_FIRST]), "gather_first")))
    full = lambda w: w.reshape(1, 1, -1, w.shape[-1])

    lb_sh = lower_bound_fwd(hgrn_lb_logits[0:1], hgrn_lb_logits[1:2], "lb_fwd")
    small = jnp.concatenate([c, norm_gain.reshape(1, L * 3 * dsh), hgrn_head_gain, lb_sh], axis=1)
    (g1,) = exchange(gather_side([small]), "gather_cond")
    g1 = g1.reshape(N_DEV, -1)
    c_all = g1[:, :D]
    gains = g1[:, D:D + L * 3 * dsh].reshape(N_DEV, L * 3, dsh).transpose(1, 0, 2).reshape(L, 3, 1, D)
    head_gain = g1[:, D + L * 3 * dsh:D + (L * 3 + 1) * dsh].reshape(1, D)
    lb0 = g1[:, D + (L * 3 + 1) * dsh:].reshape(1, D)

    parts = [cond_proj(c_all, w_ada[l], my_cols(b_ada[l:l + 1], ada_n), f"mod{l}") for l in range(L)]
    parts.append(cond_proj(c_all, w_ada_kv, my_cols(b_ada_kv[None], kv_n), "mod_kv"))
    (g2,) = exchange(gather_side([jnp.concatenate(parts, axis=1)]), "gather_mod")
    mine2 = lax.dynamic_index_in_dim(g2, me, axis=1, keepdims=False)
    mod = [mine2[:, l * ada_n:(l + 1) * ada_n].reshape(3, 3, 1, D) for l in range(L)]
    mod_kv = mine2[:, L * ada_n:].reshape(2, 1, D)

    tables = rope_tables(S)
    sink_col = jnp.broadcast_to(attn_sinks.reshape(NKV, ATT_GROUP, 1, 1), (NKV, ATT_GROUP, ATT_WINDOW, 1))
    sink_col = sink_col.reshape(NKV, ATT_GROUP * ATT_WINDOW, 1)

    def to_heads(t, n):
        return t.reshape(S, n, ATT_HEAD).transpose(1, 0, 2)

    def from_heads(t):
        return t.transpose(1, 0, 2).reshape(S, -1)

    h = h0
    saved = {}
    for l in range(L):
        for s in (0, 1, 2):
            tag = f"l{l}s{s}"
            shift, scale, gate = mod[l][s, 0], mod[l][s, 1], mod[l][s, 2]
            u, ut = adaln_fwd(h, gains[l, s], shift, scale, tag + "_norm")
            if s != 1:
                y, res = _ffn_fwd(gp, W, u, l, s // 2, tag)
                coef = 0.5
            elif l == 0:
                proj = gp.run(mm_nn, u[None], W["hin"][None], F32, name=tag + "_proj", natural=True)
                o, states = gp.run(hgrn_scan_fwd, proj, lb0, name=tag + "_scan")
                z, zt = hgrn_post_fwd(o, proj, head_gain, tag + "_post")
                y = mm_nn(z[None], full(W["hout"]), F32, tag + "_out", natural=True)
                res = (proj, o, states, zt)
                coef = 1.0
            else:
                q = mm_nn(u[None], full(W["q"]), F32, tag + "_q", natural=True, bias=b_q)
                q4 = to_heads(rope(q, tables, 1, tag + "_rope"), NQ).reshape(NKV, ATT_GROUP, S, ATT_HEAD)
                att4 = attn_fwd(q4, k3, v3, sink_col, tag + "_attn")
                att = from_heads(att4.reshape(NQ, S, ATT_HEAD))
                att_t = att4.reshape(NQ, S, ATT_HEAD).transpose(0, 2, 1).reshape(D, S)
                y = mm_nn(att[None], full(W["o"]), F32, tag + "_out", natural=True)
                res = (q4, att_t)
                coef = 1.0
            saved[(l, s)] = (h, ut, y, res)
            h = resid_fwd(h, y, gate, coef, tag + "_res")
        if l == 0:
            h_kv = h
            u_kv, u_kv_t = adaln_fwd(h, kv_gain[None], mod_kv[0], mod_kv[1], "kv_norm")
            kvp = mm_nn(u_kv[None], full(W["kv"]), F32, "kv_proj", natural=True, bias=b_kv[None])
            k3 = to_heads(rope(kvp[:, :kvd], tables, 1, "kv_rope"), NKV)
            v3 = to_heads(kvp[:, kvd:], NKV)

    loss_row, dh, d_final_gain = final_loss_grad(h, final_gain[None], target, "final")
    loss = lax.psum(loss_row[0, 0], ("x", "y", "c"))

    sp = _Plan(SCATTER_PLAN, scatter_side)

    def grad_w(a_t, b, name):
        return sp.run(mm_nn, a_t[None], b[None, None], BF16, name=name, natural=True, tm=GRAD_TM, tk=GRAD_TK)

    d_mod = [[None] * 3 for _ in range(L)]
    d_gain = [[None] * 3 for _ in range(L)]
    for l in reversed(range(L)):
        if l == 0:
            dkv = jnp.concatenate([rope(from_heads(dk3), tables, -1, "kv_drope"), from_heads(dv3)], axis=1)
            sp.source["kv"] = grad_w(u_kv_t, dkv, "kv_dw").reshape(W["kv"].shape)
            db_kv = colsum(dkv, "kv_db")
            du_kv = mm_nt(dkv[None], full(W["kv"]), F32, "kv_du", natural=True)
            dh, d_kv_gain, d_kv_shift, d_kv_scale = adaln_bwd(h_kv, kv_gain[None], mod_kv[0], mod_kv[1], du_kv, dh,
                                                              "kv_dnorm")
        for s in (2, 1, 0):
            tag = f"l{l}s{s}"
            shift, scale, gate = mod[l][s, 0], mod[l][s, 1], mod[l][s, 2]
            h_in, ut, y, res = saved[(l, s)]
            dy, d_gate = gate_bwd(dh, y, gate, 0.5 if s != 1 else 1.0, tag + "_dres")
            if s != 1:
                du = _ffn_bwd(sp, W, dy, ut, res[0], res[1], l, s // 2, tag)
            elif l == 0:
                proj, o, states, zt = res
                sp.source["hout"] = grad_w(zt, dy, tag + "_dwout").reshape(W["hout"].shape)
                dz = mm_nt(dy[None], full(W["hout"]), F32, tag + "_dz", natural=True)
                do, dg, d_head_gain = hgrn_post_bwd(o, proj, head_gain, dz, tag + "_dpost")
                dproj, d_lb0 = sp.run(hgrn_scan_bwd, proj, lb0, states, do, dg, name=tag + "_dscan")
                for hf in range(2):
                    sp.source[("hin", hf)] = sp.run(mm_nn, ut.reshape(2, D // 2, S), dproj, BF16, name=f"{tag}_dwin{hf}",
                                                    b_natural=True, jn=N_DEV, a_sel=hf, tm=GRAD_TM, tk=GRAD_TK)
                du = sp.run(mm_nt, dproj, W["hin"][:, None], F32, name=tag + "_du", natural=True, a_natural=True)
            else:
                q4, att_t = res
                sp.source["o"] = grad_w(att_t, dy, tag + "_dwout").reshape(W["o"].shape)
                datt = mm_nt(dy[None], full(W["o"]), F32, tag + "_datt", natural=True)
                datt4 = to_heads(datt, NQ).reshape(NKV, ATT_GROUP, S, ATT_HEAD)
                dq4, (dkp, dkc, dvp, dvc), d_sink = sp.run(attn_bwd, q4, k3, v3, sink_col, datt4, name=tag + "_dattn")
                dk3 = band_combine(dkc, dkp, tag + "_dk")
                dv3 = band_combine(dvc, dvp, tag + "_dv")
                dq = rope(from_heads(dq4.reshape(NQ, S, ATT_HEAD)), tables, -1, tag + "_drope")
                sp.source["q"] = grad_w(ut, dq, tag + "_dwq").reshape(W["q"].shape)
                db_q = colsum(dq, tag + "_dbq")
                du = mm_nt(dq[None], full(W["q"]), F32, tag + "_du", natural=True)
            dh, dg_, dsh_, dsc_ = adaln_bwd(h_in, gains[l, s], shift, scale, du, dh, tag + "_dnorm")
            d_gain[l][s] = dg_
            d_mod[l][s] = jnp.concatenate([dsh_, dsc_, d_gate], axis=1)
    grad_x = dh[None]
    G = sp.got

    pad = lambda a, n: jnp.pad(a, ((0, 0), (0, n - a.shape[1])))
    pieces = [jnp.concatenate(d_mod[l], axis=1) for l in range(L)]
    pieces += [d_kv_shift, d_kv_scale]
    pieces += [d_gain[l][s] for l in range(L) for s in range(3)]
    pieces += [d_head_gain, d_lb0, d_kv_gain, db_kv, db_q, pad(d_sink.reshape(1, NQ), LANES), d_final_gain]
    (g3,) = exchange(gather_side([jnp.concatenate(pieces, axis=1)]), "gather_small_grads")
    g3 = g3.reshape(N_DEV, -1)
    tot = rowsum(g3, "sum_small_grads")
    offs = [0]
    for p in pieces:
        offs.append(offs[-1] + p.shape[1])
    seg = lambda k: tot[:, offs[k]:offs[k + 1]]
    k0 = 0
    g_b_ada = jnp.concatenate([seg(l) for l in range(L)], axis=0)
    k0 += L
    g_b_ada_kv = jnp.concatenate([seg(k0), seg(k0 + 1)], axis=1)
    k0 += 2
    g_norm_gain = jnp.concatenate([my_cols(seg(k0 + j), dsh) for j in range(3 * L)], axis=0)
    k0 += 3 * L
    g_head_gain = my_cols(seg(k0), dsh)
    d_lb_sh = my_cols(seg(k0 + 1), dsh)
    g_kv_gain = seg(k0 + 2)
    g_b_kv = seg(k0 + 3)
    g_b_q = seg(k0 + 4)
    g_sinks = seg(k0 + 5)[:, :NQ]
    g_final_gain = seg(k0 + 6)
    dl0, dl1 = lower_bound_bwd(hgrn_lb_logits[0:1], hgrn_lb_logits[1:2], d_lb_sh, "lb_bwd")
    g_lb_logits = jnp.concatenate([dl0, dl1], axis=0)

    g_w_ada = jnp.stack([cond_grad(c_all, lax.dynamic_slice_in_dim(g3, offs[l] + me * ada_n, ada_n, axis=1),
                                   f"dw_ada{l}") for l in range(L)])
    g_w_ada_kv = cond_grad(c_all, lax.dynamic_slice_in_dim(g3, offs[L] + me * kv_n, kv_n, axis=1), "dw_ada_kv")

    def update(w, m, v, grads, name):
        n = len(grads)
        three = lambda a: a.reshape((n, -1, w.shape[-1]))
        outs = None
        for j, g in enumerate(grads):
            if n == 1:
                outs = sp.run(adamw, three(w), three(m), three(v), j, g, name=f"{name}{j}")
            else:
                outs = adamw(three(w), three(m), three(v), j, g, f"{name}{j}", into=outs)
        return [o.reshape(w.shape) for o in outs]

    def one(w, m, v, g, name):
        return update(w, m, v, [g.reshape((1, -1, w.shape[-1]))], name)

    res = {}
    res["norm_gain"] = one(norm_gain, m_norm_gain, v_norm_gain, g_norm_gain, "adam_norm_gain")
    res["w_ada"] = one(w_ada, m_w_ada, v_w_ada, g_w_ada, "adam_w_ada")
    res["b_ada"] = one(b_ada, m_b_ada, v_b_ada, g_b_ada, "adam_b_ada")
    rest = [k for k in sp.source if k not in G]
    if rest:
        G.update(zip(rest, exchange(scatter_side([sp.source[k] for k in rest]), "scatter_rest")))
    g_ffn_in = [G[("in", l, i, hf)] for l in range(L) for i in range(2) for hf in range(2)]
    g_ffn_out = [G[("out", l, i)] for l in range(L) for i in range(2)]
    res["w_ffn_in"] = update(w_ffn_in, m_w_ffn_in, v_w_ffn_in, g_ffn_in, "adam_w_ffn_in")
    res["w_ffn_out"] = update(w_ffn_out, m_w_ffn_out, v_w_ffn_out, g_ffn_out, "adam_w_ffn_out")
    res["w_hgrn_in"] = update(w_hgrn_in, m_w_hgrn_in, v_w_hgrn_in, [G[("hin", 0)], G[("hin", 1)]], "adam_w_hgrn_in")
    res["hgrn_lb_logits"] = one(hgrn_lb_logits, m_hgrn_lb_logits, v_hgrn_lb_logits, g_lb_logits, "adam_lb")
    res["hgrn_head_gain"] = one(hgrn_head_gain, m_hgrn_head_gain, v_hgrn_head_gain, g_head_gain, "adam_head_gain")
    res["w_hgrn_out"] = update(w_hgrn_out, m_w_hgrn_out, v_w_hgrn_out, [G["hout"]], "adam_w_hgrn_out")
    res["kv_gain"] = one(kv_gain, m_kv_gain, v_kv_gain, g_kv_gain, "adam_kv_gain")
    res["w_ada_kv"] = one(w_ada_kv, m_w_ada_kv, v_w_ada_kv, g_w_ada_kv, "adam_w_ada_kv")
    res["b_ada_kv"] = one(b_ada_kv, m_b_ada_kv, v_b_ada_kv, g_b_ada_kv, "adam_b_ada_kv")
    res["w_kv"] = update(w_kv, m_w_kv, v_w_kv, [G["kv"]], "adam_w_kv")
    res["b_kv"] = one(b_kv, m_b_kv, v_b_kv, g_b_kv, "adam_b_kv")
    res["w_q"] = update(w_q, m_w_q, v_w_q, [G["q"]], "adam_w_q")
    res["b_q"] = one(b_q, m_b_q, v_b_q, g_b_q, "adam_b_q")
    res["attn_sinks"] = one(attn_sinks, m_attn_sinks, v_attn_sinks, g_sinks, "adam_sinks")
    res["w_attn_out"] = update(w_attn_out, m_w_attn_out, v_w_attn_out, [G["o"]], "adam_w_attn_out")
    res["final_gain"] = one(final_gain, m_final_gain, v_final_gain, g_final_gain, "adam_final_gain")

    names = ["norm_gain", "w_ada", "b_ada", "w_ffn_in", "w_ffn_out", "w_hgrn_in", "hgrn_lb_logits", "hgrn_head_gain",
             "w_hgrn_out", "kv_gain", "w_ada_kv", "b_ada_kv", "w_kv", "b_kv", "w_q", "b_q", "attn_sinks", "w_attn_out",
             "final_gain"]
    return (loss, grad_x, *[res[n][0] for n in names], *[res[n][1] for n in names], *[res[n][2] for n in names],
            *[res[n][3] for n in names])
```

```python
import functools

import jax
import jax.numpy as jnp
from jax import lax
from jax.experimental import pallas as pl
from jax.experimental.pallas import tpu as pltpu

F32 = jnp.float32
BF16 = jnp.bfloat16
MESH = pl.DeviceIdType.MESH

N_DEV = 8
V7X_VMEM_LIMIT_BYTES = 56 * 1024 * 1024
LANES = 128

NORM_EPS = 1e-6
NEG_INF = -1e30
HGRN_CHUNK = 32
HGRN_HEAD = 128
ATT_HEAD = 64
ATT_WINDOW = 128
ATT_GROUP = 8
ROT_DIM = 16
ROPE_THETA = 500000.0

ADAM_LR = 0.001
ADAM_B1 = 0.9
ADAM_B2 = 0.999
ADAM_EPS = 1e-08
ADAM_WD = 0.01
ADAM_STEP = 10


def _params(*sem):
    return pltpu.CompilerParams(dimension_semantics=sem, vmem_limit_bytes=V7X_VMEM_LIMIT_BYTES)


def _tile(n, pref, unit=LANES):
    t = (min(n, pref) // unit) * unit
    while t >= unit:
        if n % t == 0:
            return t
        t -= unit
    return n


_ANY = pl.BlockSpec(memory_space=pl.ANY)


class _Side:
    def __init__(self, inputs, out_shapes, sem_shapes, start, finish):
        self.inputs, self.out_shapes, self.sem_shapes = list(inputs), list(out_shapes), list(sem_shapes)
        self.start, self.finish = start, finish


def _call(body, *, name, grid, in_specs, out_specs, out_shape, args, semantics, scratch_shapes=(), side=None):
    in_specs, out_specs, out_shape = list(in_specs), list(out_specs), list(out_shape)
    scratch_shapes = list(scratch_shapes)
    if side is None:
        outs = pl.pallas_call(
            body, name=name, grid=grid, in_specs=in_specs, out_specs=out_specs, out_shape=out_shape,
            scratch_shapes=scratch_shapes, compiler_params=_params(*semantics))(*args)
        return list(outs)
    n_in, n_out, n_scr = len(in_specs), len(out_specs), len(scratch_shapes)
    s_in, s_out = len(side.inputs), len(side.out_shapes)

    def carried(*refs):
        ins, refs = refs[:n_in], refs[n_in:]
        side_ins, refs = refs[:s_in], refs[s_in:]
        outs, refs = refs[:n_out], refs[n_out:]
        side_outs, refs = refs[:s_out], refs[s_out:]
        scratch, sems = refs[:n_scr], refs[n_scr:]
        first = pl.program_id(0) == 0
        last = pl.program_id(0) == grid[0] - 1
        for d in range(1, len(grid)):
            first = jnp.logical_and(first, pl.program_id(d) == 0)
            last = jnp.logical_and(last, pl.program_id(d) == grid[d] - 1)

        @pl.when(first)
        def _():
            side.start(side_ins, side_outs, sems)

        body(*ins, *outs, *scratch)

        @pl.when(last)
        def _():
            side.finish(side_ins, side_outs, sems)

    outs = pl.pallas_call(
        carried, name=name, grid=grid, in_specs=in_specs + [_ANY] * s_in, out_specs=out_specs + [_ANY] * s_out,
        out_shape=out_shape + side.out_shapes, scratch_shapes=scratch_shapes + side.sem_shapes,
        compiler_params=_params(*(["arbitrary"] * len(grid))))(*args, *side.inputs)
    return list(outs[:n_out]), list(outs[n_out:])


def _accumulate(prod, o_ref, acc_ref, k, nk):
    if nk == 1:
        o_ref[...] = prod.astype(o_ref.dtype)
        return

    @pl.when(k == 0)
    def _():
        acc_ref[...] = prod

    @pl.when(k > 0)
    def _():
        acc_ref[...] += prod

    @pl.when(k == nk - 1)
    def _():
        o_ref[...] = acc_ref[...].astype(o_ref.dtype)


def mm_nn(a3, b4, out_dtype, name, natural=False, a_natural=False, b_natural=False, jn=None, a_sel=None, bias=None,
          resid=None, tm=512, tk=None, side=None):
    if b_natural:
        JK, JN, kb = 1, jn, b4.shape[0]
        nb = b4.shape[1] // JN
    else:
        JK, JN, kb, nb = b4.shape
    M = a3.shape[0] if a_natural else a3.shape[1]
    tm = _tile(M, tm, 16)
    tn = _tile(nb, 1024)
    tk = kb if tk is None else _tile(kb, tk)
    ntn, nkt = nb // tn, kb // tk
    nk = JK * nkt

    def body(*refs):
        a_ref, b_ref = refs[:2]
        acc_ref = refs[-1]
        prod = jnp.dot(a_ref[...].astype(BF16), b_ref[...].astype(BF16), preferred_element_type=F32)
        if bias is not None:
            prod = prod + refs[2][...]
        if resid is None:
            _accumulate(prod, refs[-2], acc_ref, pl.program_id(2), nk)
            return
        h_ref, g_ref, y_ref, o_ref = refs[2:6]
        k = pl.program_id(2)

        @pl.when(k == 0)
        def _():
            acc_ref[...] = prod

        @pl.when(k > 0)
        def _():
            acc_ref[...] += prod

        @pl.when(k == nk - 1)
        def _():
            y = acc_ref[...]
            y_ref[...] = y
            o_ref[...] = h_ref[...] + (resid[2] * g_ref[...]) * y

    if a_natural:
        a_spec = pl.BlockSpec((tm, tk), lambda j, i, k: (i, k))
    elif a_sel is not None:
        a_spec = pl.BlockSpec((None, tm, tk), lambda j, i, k: (a_sel, i, k))
    else:
        a_spec = pl.BlockSpec((None, tm, tk), lambda j, i, k: (k // nkt, i, k % nkt))
    if b_natural:
        b_spec = pl.BlockSpec((tk, tn), lambda j, i, k: (k, j))
    else:
        b_spec = pl.BlockSpec((None, None, tk, tn), lambda j, i, k: (k // nkt, j // ntn, k % nkt, j % ntn))
    in_specs = [a_spec, b_spec]
    args = [a3, b4]
    if bias is not None:
        assert natural and nk == 1
        in_specs.append(pl.BlockSpec((1, tn), lambda j, i, k: (0, j)))
        args.append(bias)
    if natural:
        out_shape = jax.ShapeDtypeStruct((M, JN * nb), out_dtype)
        o_spec = pl.BlockSpec((tm, tn), lambda j, i, k: (i, j))
    else:
        out_shape = jax.ShapeDtypeStruct((JN, M, nb), out_dtype)
        o_spec = pl.BlockSpec((None, tm, tn), lambda j, i, k: (j // ntn, i, j % ntn))
    out_specs, out_shapes = [o_spec], [out_shape]
    if resid is not None:
        assert natural and bias is None and out_dtype == F32
        in_specs += [o_spec, pl.BlockSpec((1, tn), lambda j, i, k: (0, j))]
        args += [resid[0], resid[1]]
        out_specs, out_shapes = [o_spec, o_spec], [out_shape, out_shape]
    res = _call(body, name=name, grid=(JN * ntn, M // tm, nk), in_specs=in_specs, out_specs=out_specs,
                out_shape=out_shapes, scratch_shapes=[pltpu.VMEM((tm, tn), F32)], args=args,
                semantics=("parallel", "parallel", "arbitrary"), side=side)
    outs = res if side is None else res[0]
    ans = outs[0] if resid is None else (outs[0], outs[1])
    return ans if side is None else (ans, res[1])


def mm_nt(a3, b4, out_dtype, name, natural=False, a_natural=False, kgroup=1, side=None):
    JK, JN, nb, kb = b4.shape
    M = a3.shape[0] if a_natural else a3.shape[1]
    tm = min(M, 512)
    tn = _tile(nb, 1024)
    ntn = nb // tn
    nk = JK // kgroup

    def body(a_ref, b_ref, o_ref, acc_ref):
        nt = (((1,), (1,)), ((), ()))
        if kgroup == 1:
            prod = lax.dot_general(a_ref[...].astype(BF16), b_ref[...].astype(BF16), nt, preferred_element_type=F32)
        else:
            prod = lax.dot_general(a_ref[0].astype(BF16), b_ref[0].astype(BF16), nt, preferred_element_type=F32)
            for g in range(1, kgroup):
                prod += lax.dot_general(a_ref[g].astype(BF16), b_ref[g].astype(BF16), nt, preferred_element_type=F32)
        _accumulate(prod, o_ref, acc_ref, pl.program_id(2), nk)

    if kgroup > 1:
        assert JN == 1 and not a_natural
        a3 = a3.reshape(nk, kgroup, M, kb)
        b4 = b4.reshape(nk, kgroup, nb, kb)
        a_spec = pl.BlockSpec((None, kgroup, tm, kb), lambda j, i, k: (k, 0, i, 0))
        b_spec = pl.BlockSpec((None, kgroup, tn, kb), lambda j, i, k: (k, 0, j, 0))
    elif a_natural:
        a_spec = pl.BlockSpec((tm, kb), lambda j, i, k: (i, k))
        b_spec = pl.BlockSpec((None, None, tn, kb), lambda j, i, k: (k, j // ntn, j % ntn, 0))
    else:
        a_spec = pl.BlockSpec((None, tm, kb), lambda j, i, k: (k, i, 0))
        b_spec = pl.BlockSpec((None, None, tn, kb), lambda j, i, k: (k, j // ntn, j % ntn, 0))
    if natural:
        out_shape = jax.ShapeDtypeStruct((M, JN * nb), out_dtype)
        o_spec = pl.BlockSpec((tm, tn), lambda j, i, k: (i, j))
    else:
        out_shape = jax.ShapeDtypeStruct((JN, M, nb), out_dtype)
        o_spec = pl.BlockSpec((None, tm, tn), lambda j, i, k: (j // ntn, i, j % ntn))
    res = _call(body, name=name, grid=(JN * ntn, M // tm, nk), in_specs=[a_spec, b_spec], out_specs=[o_spec],
                out_shape=[out_shape], scratch_shapes=[pltpu.VMEM((tm, tn), F32)], args=[a3, b4],
                semantics=("parallel", "parallel", "arbitrary"), side=side)
    return res[0] if side is None else (res[0][0], res[1])


def _row_tile(S):
    return min(S, 256)


def _adaln(h, gain, shift, scale):
    y = h * lax.rsqrt(jnp.mean(h * h, axis=-1, keepdims=True) + NORM_EPS) * gain
    return y * (1.0 + scale) + shift


def adaln_fwd(h, gain, shift, scale, name):
    S, D = h.shape
    tm = _row_tile(S)

    def body(h_ref, g_ref, sh_ref, sc_ref, u_ref, ut_ref):
        u = _adaln(h_ref[...], g_ref[...], sh_ref[...], sc_ref[...])
        u_ref[...] = u.astype(BF16)
        ut_ref[...] = u.T.astype(BF16)

    row = pl.BlockSpec((tm, D), lambda i: (i, 0))
    vec = pl.BlockSpec((1, D), lambda i: (0, 0))
    return pl.pallas_call(
        body, name=name, grid=(S // tm,), in_specs=[row, vec, vec, vec],
        out_specs=[row, pl.BlockSpec((D, tm), lambda i: (0, i))],
        out_shape=[jax.ShapeDtypeStruct((S, D), BF16), jax.ShapeDtypeStruct((D, S), BF16)],
        compiler_params=_params("parallel"),
    )(h, gain, shift, scale)


def adaln_bwd(h, gain, shift, scale, du, dres, name):
    S, D = h.shape
    tm = _row_tile(S)

    def body(h_ref, g_ref, sh_ref, sc_ref, du_ref, dres_ref, dh_ref, dg_ref, dsh_ref, dsc_ref):
        _, vjp = jax.vjp(_adaln, h_ref[...], g_ref[...], sh_ref[...], sc_ref[...])
        dh, dg, dsh, dsc = vjp(du_ref[...].astype(F32))
        dh_ref[...] = dres_ref[...] + dh

        @pl.when(pl.program_id(0) == 0)
        def _():
            dg_ref[...] = jnp.zeros_like(dg_ref)
            dsh_ref[...] = jnp.zeros_like(dsh_ref)
            dsc_ref[...] = jnp.zeros_like(dsc_ref)

        dg_ref[...] += dg
        dsh_ref[...] += dsh
        dsc_ref[...] += dsc

    row = pl.BlockSpec((tm, D), lambda i: (i, 0))
    vec = pl.BlockSpec((1, D), lambda i: (0, 0))
    vs = jax.ShapeDtypeStruct((1, D), F32)
    return pl.pallas_call(
        body, name=name, grid=(S // tm,), in_specs=[row, vec, vec, vec, row, row], out_specs=[row, vec, vec, vec],
        out_shape=[jax.ShapeDtypeStruct((S, D), F32), vs, vs, vs], compiler_params=_params("arbitrary"),
    )(h, gain, shift, scale, du, dres)


def gate_bwd(dh, y, gate, coef, name):
    S, D = dh.shape
    tm = _row_tile(S)

    def body(dh_ref, y_ref, g_ref, dy_ref, dg_ref):
        dh = dh_ref[...]
        dy_ref[...] = ((coef * g_ref[...]) * dh).astype(BF16)

        @pl.when(pl.program_id(0) == 0)
        def _():
            dg_ref[...] = jnp.zeros_like(dg_ref)

        dg_ref[...] += coef * jnp.sum(dh * y_ref[...], axis=0, keepdims=True)

    row = pl.BlockSpec((tm, D), lambda i: (i, 0))
    vec = pl.BlockSpec((1, D), lambda i: (0, 0))
    return pl.pallas_call(
        body, name=name, grid=(S // tm,), in_specs=[row, row, vec], out_specs=[row, vec],
        out_shape=[jax.ShapeDtypeStruct((S, D), BF16), jax.ShapeDtypeStruct((1, D), F32)],
        compiler_params=_params("arbitrary"),
    )(dh, y, gate)


def _swiglu(a, b):
    return a * jax.nn.sigmoid(a) * b


def ffn_in_act(u, w_in, name, side=None):
    J2, D, nb = w_in.shape
    J = J2 // 2
    S = u.shape[0]
    tm = _tile(S, 512, 16)

    def body(u_ref, wa_ref, wb_ref, ab_ref, h_ref, ht_ref):
        uv = u_ref[...]
        a = jnp.dot(uv, wa_ref[...], preferred_element_type=F32)
        b = jnp.dot(uv, wb_ref[...], preferred_element_type=F32)
        ab_ref[0] = a
        ab_ref[1] = b
        hv = _swiglu(a, b)
        h_ref[...] = hv.astype(BF16)
        ht_ref[...] = hv.T.astype(BF16)

    res = _call(body, name=name, grid=(J, S // tm),
                in_specs=[pl.BlockSpec((tm, D), lambda j, i: (i, 0)),
                          pl.BlockSpec((None, D, nb), lambda j, i: (j, 0, 0)),
                          pl.BlockSpec((None, D, nb), lambda j, i: (j + J, 0, 0))],
                out_specs=[pl.BlockSpec((2, None, tm, nb), lambda j, i: (0, j, i, 0)),
                           pl.BlockSpec((None, tm, nb), lambda j, i: (j, i, 0)),
                           pl.BlockSpec((None, nb, tm), lambda j, i: (j, 0, i))],
                out_shape=[jax.ShapeDtypeStruct((2, J, S, nb), F32), jax.ShapeDtypeStruct((J, S, nb), BF16),
                           jax.ShapeDtypeStruct((J, nb, S), BF16)],
                args=[u, w_in, w_in], semantics=("parallel", "parallel"), side=side)
    return res if side is None else (res[0], res[1])


def ffn_dact(dy, w_out4, ab4, name, side=None):
    J, nb, D = w_out4.shape
    S = dy.shape[0]
    tm = _tile(S, 512, 16)

    def body(dy_ref, w_ref, ab_ref, o_ref):
        dh = lax.dot_general(dy_ref[...], w_ref[...], (((1,), (1,)), ((), ())), preferred_element_type=F32)
        _, vjp = jax.vjp(_swiglu, ab_ref[0], ab_ref[1])
        da, db = vjp(dh)
        o_ref[0] = da.astype(BF16)
        o_ref[1] = db.astype(BF16)

    both = pl.BlockSpec((2, None, tm, nb), lambda j, i: (0, j, i, 0))
    res = _call(body, name=name, grid=(J, S // tm),
                in_specs=[pl.BlockSpec((tm, D), lambda j, i: (i, 0)),
                          pl.BlockSpec((None, nb, D), lambda j, i: (j, 0, 0)), both],
                out_specs=[both], out_shape=[jax.ShapeDtypeStruct((2, J, S, nb), BF16)],
                args=[dy, w_out4, ab4], semantics=("parallel", "parallel"), side=side)
    if side is None:
        return res[0].reshape(2 * J, S, nb)
    return res[0][0].reshape(2 * J, S, nb), res[1]


def colsum(x, name):
    S, N = x.shape
    tm = _row_tile(S)

    def body(x_ref, o_ref):
        @pl.when(pl.program_id(0) == 0)
        def _():
            o_ref[...] = jnp.zeros_like(o_ref)

        o_ref[...] += jnp.sum(x_ref[...].astype(F32), axis=0, keepdims=True)

    return pl.pallas_call(
        body, name=name, grid=(S // tm,), in_specs=[pl.BlockSpec((tm, N), lambda i: (i, 0))],
        out_specs=pl.BlockSpec((1, N), lambda i: (0, 0)), out_shape=jax.ShapeDtypeStruct((1, N), F32),
        compiler_params=_params("arbitrary"),
    )(x)


def _final_loss(h, gain, target):
    y = h * lax.rsqrt(jnp.mean(h * h, axis=-1, keepdims=True) + NORM_EPS) * gain
    err = y - target
    return 0.5 * jnp.sum(jnp.mean(err * err, axis=-1))


def final_loss_grad(h, gain, target, name):
    S, D = h.shape
    tm = _row_tile(S)

    def body(h_ref, g_ref, t_ref, loss_ref, dh_ref, dg_ref):
        loss, (dh, dg) = jax.value_and_grad(_final_loss, argnums=(0, 1))(h_ref[...], g_ref[...], t_ref[...])
        dh_ref[...] = dh

        @pl.when(pl.program_id(0) == 0)
        def _():
            loss_ref[...] = jnp.zeros_like(loss_ref)
            dg_ref[...] = jnp.zeros_like(dg_ref)

        loss_ref[...] += jnp.full(loss_ref.shape, loss, F32)
        dg_ref[...] += dg

    row = pl.BlockSpec((tm, D), lambda i: (i, 0))
    vec = pl.BlockSpec((1, D), lambda i: (0, 0))
    return pl.pallas_call(
        body, name=name, grid=(S // tm,), in_specs=[row, vec, row],
        out_specs=[pl.BlockSpec((1, LANES), lambda i: (0, 0)), row, vec],
        out_shape=[jax.ShapeDtypeStruct((1, LANES), F32), jax.ShapeDtypeStruct((S, D), F32),
                   jax.ShapeDtypeStruct((1, D), F32)],
        compiler_params=_params("arbitrary"),
    )(h, gain, target)


def _chunk_consts(H):
    C = HGRN_CHUNK
    t = lax.broadcasted_iota(jnp.int32, (H, C, C), 1)
    s = lax.broadcasted_iota(jnp.int32, (H, C, C), 2)
    return (s <= t).astype(F32), s <= t


def _hgrn_chunk(q_raw, f_raw, i_raw, lb, st):
    H, C, _ = q_raw.shape
    lower, causal = _chunk_consts(H)
    forget = lb + (1.0 - lb) * jax.nn.sigmoid(f_raw)
    g = jnp.log(forget)
    kk = 1.0 - forget
    qs = q_raw * jax.nn.sigmoid(q_raw)
    bnn = (((2,), (1,)), ((0,), (0,)))
    bnt = (((2,), (2,)), ((0,), (0,)))
    btn = (((1,), (1,)), ((0,), (0,)))
    b = lax.dot_general(lower, g, bnn, precision=lax.Precision.HIGHEST, preferred_element_type=F32)
    bm = b[:, C // 2 - 1:C // 2, :]
    bl = b[:, C - 1:C, :]
    inter = lax.dot_general((qs * jnp.exp(b)).astype(BF16), st.astype(BF16), bnt, preferred_element_type=F32)
    qt = (qs * jnp.exp(b - bm)).astype(BF16)
    kt = (kk * jnp.exp(bm - b)).astype(BF16)
    scores = lax.dot_general(qt, kt, bnt, preferred_element_type=F32)
    scores = jnp.where(causal, scores, 0.0)
    vb = i_raw.astype(BF16)
    out = inter + lax.dot_general(scores.astype(BF16), vb, bnn, preferred_element_type=F32)
    kdec = (kk * jnp.exp(bl - b)).astype(BF16)
    new_st = st * jnp.exp(bl) + lax.dot_general(vb, kdec, btn, preferred_element_type=F32)
    return out, new_st


def _heads(ref, rows, H):
    return jnp.stack([ref[rows, pl.ds(h * HGRN_HEAD, HGRN_HEAD)] for h in range(H)])


def hgrn_scan_fwd(proj, lb, name, side=None):
    S, D4 = proj.shape
    D = D4 // 4
    H = D // HGRN_HEAD
    C = HGRN_CHUNK
    R = min(S, 128)
    ncr = R // C

    def body(q_ref, f_ref, i_ref, lb_ref, o_ref, st_ref, state):
        @pl.when(pl.program_id(0) == 0)
        def _():
            state[...] = jnp.zeros_like(state)

        lbh = _heads(lb_ref, slice(None), H)

        def chunk(cc, carry):
            rows = pl.ds(pl.multiple_of(cc * C, C), C)
            st = state[...]
            st_ref[cc] = st
            out, new_st = _hgrn_chunk(_heads(q_ref, rows, H), _heads(f_ref, rows, H), _heads(i_ref, rows, H), lbh, st)
            for h in range(H):
                o_ref[rows, pl.ds(h * HGRN_HEAD, HGRN_HEAD)] = out[h]
            state[...] = new_st
            return carry

        lax.fori_loop(0, ncr, chunk, 0)

    col = lambda j: pl.BlockSpec((R, D), lambda i: (i, j))
    res = _call(body, name=name, grid=(S // R,),
                in_specs=[col(0), col(1), col(2), pl.BlockSpec((1, D), lambda i: (0, 0))],
                out_specs=[pl.BlockSpec((R, D), lambda i: (i, 0)),
                           pl.BlockSpec((ncr, H, HGRN_HEAD, HGRN_HEAD), lambda i: (i, 0, 0, 0))],
                out_shape=[jax.ShapeDtypeStruct((S, D), F32),
                           jax.ShapeDtypeStruct((S // C, H, HGRN_HEAD, HGRN_HEAD), F32)],
                scratch_shapes=[pltpu.VMEM((H, HGRN_HEAD, HGRN_HEAD), F32)], args=[proj, proj, proj, lb],
                semantics=("arbitrary",), side=side)
    return res if side is None else (res[0], res[1])


def hgrn_scan_bwd(proj, lb, states, do, dg, name, side=None):
    S, D4 = proj.shape
    D = D4 // 4
    H = D // HGRN_HEAD
    C = HGRN_CHUNK
    R = min(S, 128)
    ncr = R // C
    ng = S // R

    def body(q_ref, f_ref, i_ref, lb_ref, st_ref, do_ref, dg_ref, dp_ref, dlb_ref, dstate):
        @pl.when(pl.program_id(0) == 0)
        def _():
            dstate[...] = jnp.zeros_like(dstate)
            dlb_ref[...] = jnp.zeros_like(dlb_ref)

        dp_ref[:, pl.ds(3 * D, D)] = dg_ref[...].astype(BF16)
        lbh = _heads(lb_ref, slice(None), H)

        def chunk(t, carry):
            cc = ncr - 1 - t
            rows = pl.ds(pl.multiple_of(cc * C, C), C)
            _, vjp = jax.vjp(_hgrn_chunk, _heads(q_ref, rows, H), _heads(f_ref, rows, H), _heads(i_ref, rows, H),
                             lbh, st_ref[cc])
            dq, df, di, dlb, dst = vjp((_heads(do_ref, rows, H), dstate[...]))
            for h in range(H):
                dp_ref[rows, pl.ds(h * HGRN_HEAD, HGRN_HEAD)] = dq[h].astype(BF16)
                dp_ref[rows, pl.ds(D + h * HGRN_HEAD, HGRN_HEAD)] = df[h].astype(BF16)
                dp_ref[rows, pl.ds(2 * D + h * HGRN_HEAD, HGRN_HEAD)] = di[h].astype(BF16)
                dlb_ref[:, pl.ds(h * HGRN_HEAD, HGRN_HEAD)] += dlb[h]
            dstate[...] = dst
            return carry

        lax.fori_loop(0, ncr, chunk, 0)

    col = lambda j: pl.BlockSpec((R, D), lambda i: (ng - 1 - i, j))
    res = _call(body, name=name, grid=(ng,),
                in_specs=[col(0), col(1), col(2), pl.BlockSpec((1, D), lambda i: (0, 0)),
                          pl.BlockSpec((ncr, H, HGRN_HEAD, HGRN_HEAD), lambda i: (ng - 1 - i, 0, 0, 0)),
                          pl.BlockSpec((R, D), lambda i: (ng - 1 - i, 0)),
                          pl.BlockSpec((R, D), lambda i: (ng - 1 - i, 0))],
                out_specs=[pl.BlockSpec((R, D4), lambda i: (ng - 1 - i, 0)), pl.BlockSpec((1, D), lambda i: (0, 0))],
                out_shape=[jax.ShapeDtypeStruct((S, D4), BF16), jax.ShapeDtypeStruct((1, D), F32)],
                scratch_shapes=[pltpu.VMEM((H, HGRN_HEAD, HGRN_HEAD), F32)],
                args=[proj, proj, proj, lb, states, do, dg], semantics=("arbitrary",), side=side)
    return res if side is None else (res[0], res[1])


def _head_out(o, g, gain):
    y = o * lax.rsqrt(jnp.mean(o * o, axis=-1, keepdims=True) + NORM_EPS) * gain
    return y * jax.nn.sigmoid(g)


def hgrn_post_fwd(o, proj, gain, name):
    S, D = o.shape
    H = D // HGRN_HEAD
    tm = _row_tile(S)

    def body(o_ref, g_ref, gain_ref, z_ref, zt_ref):
        for h in range(H):
            ls = pl.ds(h * HGRN_HEAD, HGRN_HEAD)
            z = _head_out(o_ref[:, ls], g_ref[:, ls], gain_ref[:, ls])
            z_ref[:, ls] = z.astype(BF16)
            zt_ref[ls, :] = z.T.astype(BF16)

    row = pl.BlockSpec((tm, D), lambda i: (i, 0))
    return pl.pallas_call(
        body, name=name, grid=(S // tm,),
        in_specs=[row, pl.BlockSpec((tm, D), lambda i: (i, 3)), pl.BlockSpec((1, D), lambda i: (0, 0))],
        out_specs=[row, pl.BlockSpec((D, tm), lambda i: (0, i))],
        out_shape=[jax.ShapeDtypeStruct((S, D), BF16), jax.ShapeDtypeStruct((D, S), BF16)],
        compiler_params=_params("parallel"),
    )(o, proj, gain)


def hgrn_post_bwd(o, proj, gain, dz, name):
    S, D = o.shape
    H = D // HGRN_HEAD
    tm = _row_tile(S)

    def body(o_ref, g_ref, gain_ref, dz_ref, do_ref, dg_ref, dgain_ref):
        @pl.when(pl.program_id(0) == 0)
        def _():
            dgain_ref[...] = jnp.zeros_like(dgain_ref)

        for h in range(H):
            ls = pl.ds(h * HGRN_HEAD, HGRN_HEAD)
            _, vjp = jax.vjp(_head_out, o_ref[:, ls], g_ref[:, ls], gain_ref[:, ls])
            do, dg, dgain = vjp(dz_ref[:, ls].astype(F32))
            do_ref[:, ls] = do
            dg_ref[:, ls] = dg
            dgain_ref[:, ls] += dgain

    row = pl.BlockSpec((tm, D), lambda i: (i, 0))
    vec = pl.BlockSpec((1, D), lambda i: (0, 0))
    return pl.pallas_call(
        body, name=name, grid=(S // tm,),
        in_specs=[row, pl.BlockSpec((tm, D), lambda i: (i, 3)), vec, row], out_specs=[row, row, vec],
        out_shape=[jax.ShapeDtypeStruct((S, D), F32), jax.ShapeDtypeStruct((S, D), F32),
                   jax.ShapeDtypeStruct((1, D), F32)],
        compiler_params=_params("arbitrary"),
    )(o, proj, gain, dz)


def lower_bound_fwd(l0, l1, name):
    def body(a_ref, b_ref, o_ref):
        o_ref[...] = jax.nn.sigmoid(a_ref[...] - b_ref[...])

    return pl.pallas_call(body, name=name, out_shape=jax.ShapeDtypeStruct(l0.shape, F32))(l0, l1)


def lower_bound_bwd(l0, l1, dlb, name):
    def body(a_ref, b_ref, d_ref, o0_ref, o1_ref):
        s = jax.nn.sigmoid(a_ref[...] - b_ref[...])
        d0 = d_ref[...] * s * (1.0 - s)
        o0_ref[...] = d0
        o1_ref[...] = -d0

    sd = jax.ShapeDtypeStruct(l0.shape, F32)
    return pl.pallas_call(body, name=name, out_shape=[sd, sd])(l0, l1, dlb)


def _attn_tile(q, kp, kc, vp, vc, sink, first):
    W = ATT_WINDOW
    nt = (((1,), (1,)), ((), ()))
    qb = q.astype(BF16)
    scale = ATT_HEAD ** -0.5
    sp = lax.dot_general(qb, kp.astype(BF16), nt, preferred_element_type=F32) * scale
    sc = lax.dot_general(qb, kc.astype(BF16), nt, preferred_element_type=F32) * scale
    qi = lax.broadcasted_iota(jnp.int32, sp.shape, 0) & (W - 1)
    kj = lax.broadcasted_iota(jnp.int32, sp.shape, 1)
    sp = jnp.where((kj > qi) & jnp.logical_not(first), sp, NEG_INF)
    sc = jnp.where(kj <= qi, sc, NEG_INF)
    m = jnp.maximum(jnp.maximum(jnp.max(sp, axis=-1, keepdims=True), jnp.max(sc, axis=-1, keepdims=True)), sink)
    pp = jnp.exp(sp - m)
    pc = jnp.exp(sc - m)
    denom = jnp.sum(pp, axis=-1, keepdims=True) + jnp.sum(pc, axis=-1, keepdims=True) + jnp.exp(sink - m)
    out = jnp.dot((pp / denom).astype(BF16), vp.astype(BF16), preferred_element_type=F32)
    return out + jnp.dot((pc / denom).astype(BF16), vc.astype(BF16), preferred_element_type=F32)


def _attn_specs(G, W, Dh):
    q_spec = pl.BlockSpec((None, G, W, Dh), lambda j, n: (j, 0, n, 0))
    prev = pl.BlockSpec((None, W, Dh), lambda j, n: (j, jnp.maximum(n - 1, 0), 0))
    cur = pl.BlockSpec((None, W, Dh), lambda j, n: (j, n, 0))
    sink = pl.BlockSpec((None, G * W, 1), lambda j, n: (j, 0, 0))
    return q_spec, prev, cur, sink


def attn_fwd(q4, k3, v3, sink, name):
    NKV, G, S, Dh = q4.shape
    W = ATT_WINDOW

    def body(q_ref, kp_ref, kc_ref, vp_ref, vc_ref, s_ref, o_ref):
        first = pl.program_id(1) == 0
        out = _attn_tile(q_ref[...].reshape(G * W, Dh), kp_ref[...], kc_ref[...], vp_ref[...], vc_ref[...],
                         s_ref[...], first)
        o_ref[...] = out.reshape(G, W, Dh)

    q_spec, prev, cur, sk = _attn_specs(G, W, Dh)
    return pl.pallas_call(
        body, name=name, grid=(NKV, S // W), in_specs=[q_spec, prev, cur, prev, cur, sk], out_specs=q_spec,
        out_shape=jax.ShapeDtypeStruct(q4.shape, F32), compiler_params=_params("parallel", "parallel"),
    )(q4, k3, k3, v3, v3, sink)


def attn_bwd(q4, k3, v3, sink, do4, name, side=None):
    NKV, G, S, Dh = q4.shape
    W = ATT_WINDOW
    nb = S // W

    def body(q_ref, kp_ref, kc_ref, vp_ref, vc_ref, s_ref, do_ref, dq_ref, dkp_ref, dkc_ref, dvp_ref, dvc_ref,
             ds_ref):
        first = pl.program_id(1) == 0
        _, vjp = jax.vjp(functools.partial(_attn_tile, first=first), q_ref[...].reshape(G * W, Dh), kp_ref[...],
                         kc_ref[...], vp_ref[...], vc_ref[...], s_ref[...])
        dq, dkp, dkc, dvp, dvc, ds = vjp(do_ref[...].reshape(G * W, Dh))
        dq_ref[...] = dq.reshape(G, W, Dh)
        dkp_ref[...] = dkp
        dkc_ref[...] = dkc
        dvp_ref[...] = dvp
        dvc_ref[...] = dvc

        @pl.when(first)
        def _():
            ds_ref[...] = jnp.zeros_like(ds_ref)

        ds_ref[...] += jnp.sum(ds.reshape(G, W, 1), axis=1)

    q_spec, prev, cur, sk = _attn_specs(G, W, Dh)
    part = pl.BlockSpec((None, None, W, Dh), lambda j, n: (j, n, 0, 0))
    ps = jax.ShapeDtypeStruct((NKV, nb, W, Dh), F32)
    res = _call(body, name=name, grid=(NKV, nb), in_specs=[q_spec, prev, cur, prev, cur, sk, q_spec],
                out_specs=[q_spec, part, part, part, part, pl.BlockSpec((None, G, 1), lambda j, n: (j, 0, 0))],
                out_shape=[jax.ShapeDtypeStruct(q4.shape, F32), ps, ps, ps, ps,
                           jax.ShapeDtypeStruct((NKV, G, 1), F32)],
                args=[q4, k3, k3, v3, v3, sink, do4], semantics=("parallel", "arbitrary"), side=side)
    outs = res if side is None else res[0]
    ans = (outs[0], outs[1:5], outs[5])
    return ans if side is None else (ans, res[1])


def band_combine(cur, prev, name):
    NKV, nb, W, Dh = cur.shape

    def body(c_ref, p_ref, o_ref):
        keep = (pl.program_id(1) < nb - 1).astype(F32)
        o_ref[...] = c_ref[...] + keep * p_ref[...]

    return pl.pallas_call(
        body, name=name, grid=(NKV, nb),
        in_specs=[pl.BlockSpec((None, None, W, Dh), lambda j, n: (j, n, 0, 0)),
                  pl.BlockSpec((None, None, W, Dh), lambda j, n: (j, jnp.minimum(n + 1, nb - 1), 0, 0))],
        out_specs=pl.BlockSpec((None, W, Dh), lambda j, n: (j, n, 0)),
        out_shape=jax.ShapeDtypeStruct((NKV, nb * W, Dh), F32), compiler_params=_params("parallel", "parallel"),
    )(cur, prev)


def rope_tables(S):
    half = ROT_DIM // 2
    inv_freq = jnp.power(jnp.float32(ROPE_THETA), -jnp.arange(0, ROT_DIM, 2, dtype=F32) / ROT_DIM)
    ang = jnp.arange(S, dtype=F32)[:, None] * inv_freq[None, :]
    sin, cos = jnp.sin(ang), jnp.cos(ang)
    zeros = jnp.zeros((S, ATT_HEAD - ROT_DIM), F32)
    z8 = jnp.zeros((S, half), F32)
    cfull = jnp.concatenate([cos, cos, jnp.ones((S, ATT_HEAD - ROT_DIM), F32)], axis=1)
    s_next = jnp.concatenate([-sin, z8, zeros], axis=1)
    s_prev = jnp.concatenate([z8, sin, zeros], axis=1)
    two = lambda t: jnp.concatenate([t, t], axis=1)
    return two(cfull), two(s_next), two(s_prev)


def rope(x, tables, sign, name):
    S, Wd = x.shape
    tm = _row_tile(S)
    rep = Wd // LANES
    half = ROT_DIM // 2

    def body(x_ref, c_ref, sn_ref, sp_ref, o_ref):
        xv = x_ref[...]
        c = jnp.tile(c_ref[...], (1, rep))
        sn = jnp.tile(sn_ref[...], (1, rep))
        sp = jnp.tile(sp_ref[...], (1, rep))
        if sign > 0:
            nxt = pltpu.roll(xv, Wd - half, 1)
            prv = pltpu.roll(xv, half, 1)
            o_ref[...] = xv * c + nxt * sn + prv * sp
        else:
            o_ref[...] = xv * c + pltpu.roll(xv * sn, half, 1) + pltpu.roll(xv * sp, Wd - half, 1)

    row = pl.BlockSpec((tm, Wd), lambda i: (i, 0))
    tab = pl.BlockSpec((tm, LANES), lambda i: (i, 0))
    return pl.pallas_call(
        body, name=name, grid=(S // tm,), in_specs=[row, tab, tab, tab], out_specs=row,
        out_shape=jax.ShapeDtypeStruct((S, Wd), F32), compiler_params=_params("parallel"),
    )(x, *tables)


def cond_proj(c_all, w, bias, name):
    B, D = c_all.shape
    N = w.shape[1]
    tn = _tile(N, 512)

    def body(c_ref, w_ref, b_ref, o_ref):
        cv = c_ref[...]
        cs = (cv * jax.nn.sigmoid(cv)).astype(BF16)
        o_ref[...] = jnp.dot(cs, w_ref[...].astype(BF16), preferred_element_type=F32) + b_ref[...]

    return pl.pallas_call(
        body, name=name, grid=(N // tn,),
        in_specs=[pl.BlockSpec((B, D), lambda j: (0, 0)), pl.BlockSpec((D, tn), lambda j: (0, j)),
                  pl.BlockSpec((1, tn), lambda j: (0, j))],
        out_specs=pl.BlockSpec((B, tn), lambda j: (0, j)), out_shape=jax.ShapeDtypeStruct((B, N), F32),
        compiler_params=_params("parallel"),
    )(c_all, w, bias)


def cond_grad(c_all, dmod, name):
    B, D = c_all.shape
    N = dmod.shape[1]
    tn = _tile(N, 512)

    def body(c_ref, d_ref, o_ref):
        cv = c_ref[...]
        cs = (cv * jax.nn.sigmoid(cv)).astype(BF16)
        o_ref[...] = lax.dot_general(cs, d_ref[...].astype(BF16), (((0,), (0,)), ((), ())),
                                     preferred_element_type=F32)

    return pl.pallas_call(
        body, name=name, grid=(N // tn,),
        in_specs=[pl.BlockSpec((B, D), lambda j: (0, 0)), pl.BlockSpec((B, tn), lambda j: (0, j))],
        out_specs=pl.BlockSpec((D, tn), lambda j: (0, j)), out_shape=jax.ShapeDtypeStruct((D, N), F32),
        compiler_params=_params("parallel"),
    )(c_all, dmod)


def rowsum(g, name):
    B, N = g.shape
    tn = _tile(N, 8192)

    def body(g_ref, o_ref):
        acc = g_ref[0:1, :]
        for r in range(1, B):
            acc = acc + g_ref[r:r + 1, :]
        o_ref[...] = acc

    return pl.pallas_call(
        body, name=name, grid=(N // tn,), in_specs=[pl.BlockSpec((B, tn), lambda j: (0, j))],
        out_specs=pl.BlockSpec((1, tn), lambda j: (0, j)), out_shape=jax.ShapeDtypeStruct((1, N), F32),
        compiler_params=_params("parallel"),
    )(g)


def _adam_rows(R, C):
    limit = 3 << 19
    if R * C * 4 <= limit or R % 8:
        return R
    best = 8
    for t in range(8, R + 1, 8):
        if R % t == 0 and t * C * 4 <= limit:
            best = t
    return best


def adamw(w3, m3, v3, j, g3, name, col=(0, 1), into=None, side=None):
    n, R, C = w3.shape
    P = g3.shape[0]
    cp, ncol = col
    Cp = C // ncol
    tr = _adam_rows(R, Cp)

    def body(*refs):
        w_ref, m_ref, v_ref, g_ref = refs[:4]
        go_ref, d_ref, mo_ref, vo_ref = refs[-4:]
        g = g_ref[0].astype(F32)
        for p in range(1, P):
            g = g + g_ref[p].astype(F32)
        mn = ADAM_B1 * m_ref[...] + (1.0 - ADAM_B1) * g
        vn = ADAM_B2 * v_ref[...] + (1.0 - ADAM_B2) * jnp.square(g)
        m_hat = mn / (1.0 - ADAM_B1 ** ADAM_STEP)
        v_hat = vn / (1.0 - ADAM_B2 ** ADAM_STEP)
        go_ref[...] = g
        d_ref[...] = -ADAM_LR * (m_hat / (jnp.sqrt(v_hat) + ADAM_EPS) + ADAM_WD * w_ref[...])
        mo_ref[...] = mn
        vo_ref[...] = vn

    spec = pl.BlockSpec((None, tr, Cp), lambda i: (j, i, cp))
    g_spec = pl.BlockSpec((P, tr, Cp), lambda i: (0, i, 0))
    sd = jax.ShapeDtypeStruct((n, R, C), F32)
    in_specs, args, aliases = [spec, spec, spec, g_spec], [w3, m3, v3, g3], {}
    if side is not None:
        assert into is None
        return _call(body, name=name, grid=(R // tr,), in_specs=in_specs, out_specs=[spec] * 4, out_shape=[sd] * 4,
                     args=args, semantics=("parallel",), side=side)
    if into is not None:
        in_specs += [_ANY] * 4
        args += list(into)
        aliases = {4 + k: k for k in range(4)}
    return pl.pallas_call(
        body, name=name, grid=(R // tr,), in_specs=in_specs, out_specs=[spec] * 4, out_shape=[sd] * 4,
        input_output_aliases=aliases, compiler_params=_params("parallel"),
    )(*args)


def _place():
    return lax.axis_index("x"), lax.axis_index("y"), lax.axis_index("c")


def _slot(p):
    return 4 * p[0] + 2 * p[1] + p[2]


def gather_side(items):
    xs = [a for a, _ in items]
    n = len(xs)

    def copy(ins, outs, sems, t, k, block, to, from_input=False):
        dst = outs[t].at[_slot(block)]
        return pltpu.make_async_remote_copy(
            src_ref=ins[t] if from_input else dst, dst_ref=dst, send_sem=sems[0].at[t, k],
            recv_sem=sems[1].at[t, k], device_id=to, device_id_type=MESH)

    def peers():
        x, y, c = _place()
        return (x, y, c), (x, y, 1 - c), [(1 - x, y), (x, 1 - y), (1 - x, 1 - y)]

    def start(ins, outs, sems):
        me, sibling, chips = peers()
        c = me[2]
        for t in range(n):
            pltpu.make_async_copy(ins[t], outs[t].at[_slot(me)], sems[2].at[t]).start()
            copy(ins, outs, sems, t, 0, me, sibling, True).start()
            for j, chip in enumerate(chips):
                copy(ins, outs, sems, t, 1 + j, me, (*chip, c), True).start()

    def finish(ins, outs, sems):
        me, sibling, chips = peers()
        c = me[2]
        for t in range(n):
            for j, chip in enumerate(chips):
                copy(ins, outs, sems, t, 1 + j, (*chip, c), me).wait_recv()
                copy(ins, outs, sems, t, 4 + j, (*chip, c), sibling).start()
        for t in range(n):
            copy(ins, outs, sems, t, 0, sibling, me).wait_recv()
            for j, chip in enumerate(chips):
                copy(ins, outs, sems, t, 4 + j, (*chip, 1 - c), me).wait_recv()
        for t in range(n):
            copy(ins, outs, sems, t, 0, me, sibling, True).wait_send()
            for j, chip in enumerate(chips):
                copy(ins, outs, sems, t, 1 + j, me, (*chip, c), True).wait_send()
                copy(ins, outs, sems, t, 4 + j, (*chip, c), sibling).wait_send()
            pltpu.make_async_copy(ins[t], outs[t].at[_slot(me)], sems[2].at[t]).wait()

    return _Side(xs, [jax.ShapeDtypeStruct((N_DEV,) + a.shape, a.dtype) for a in xs],
                 [pltpu.SemaphoreType.DMA((n, 7)), pltpu.SemaphoreType.DMA((n, 7)), pltpu.SemaphoreType.DMA((n,))],
                 start, finish)


def scatter_side(items):
    gs = [g for g, _ in items]
    n = len(gs)

    def part_shape(g, part):
        _, R, C = g.shape
        if part is None:
            return R, C
        axis, _, cnt = part
        return (R // cnt, C) if axis == 0 else (R, C // cnt)

    def block(ref, slot, g, part):
        if part is None:
            return ref.at[slot]
        axis, idx, cnt = part
        R, C = part_shape(g, part)
        return ref.at[slot, pl.ds(idx * R, R)] if axis == 0 else ref.at[slot, :, pl.ds(idx * C, C)]

    def copies(ins, outs, sems):
        x, y, c = _place()
        me = (x, y, c)
        out = []
        for t, (g, part) in enumerate(items):
            out.append(pltpu.make_async_copy(block(ins[t], _slot(me), g, part), outs[t].at[_slot(me)], sems[2].at[t]))
            for r in range(1, N_DEV):
                peer = (1 - x if r & 4 else x, 1 - y if r & 2 else y, 1 - c if r & 1 else c)
                out.append(pltpu.make_async_remote_copy(
                    src_ref=block(ins[t], _slot(peer), g, part), dst_ref=outs[t].at[_slot(me)],
                    send_sem=sems[0].at[t, r - 1], recv_sem=sems[1].at[t, r - 1], device_id=peer, device_id_type=MESH))
        return out

    def start(ins, outs, sems):
        for cp in copies(ins, outs, sems):
            cp.start()

    def finish(ins, outs, sems):
        for cp in copies(ins, outs, sems):
            cp.wait()

    return _Side(gs, [jax.ShapeDtypeStruct((N_DEV,) + part_shape(g, part), g.dtype) for g, part in items],
                 [pltpu.SemaphoreType.DMA((n, 7)), pltpu.SemaphoreType.DMA((n, 7)), pltpu.SemaphoreType.DMA((n,))],
                 start, finish)


def exchange(side, name):
    def body(*refs):
        n_in, n_out = len(side.inputs), len(side.out_shapes)
        ins, outs, sems = refs[:n_in], refs[n_in:n_in + n_out], refs[n_in + n_out:]
        side.start(ins, outs, sems)
        side.finish(ins, outs, sems)

    return pl.pallas_call(
        body, name=name, in_specs=[_ANY] * len(side.inputs), out_specs=[_ANY] * len(side.out_shapes),
        out_shape=side.out_shapes, scratch_shapes=side.sem_shapes,
    )(*side.inputs)


class _Part:
    def __init__(self, key, axis, index, count):
        self.key, self.axis, self.index, self.count = key, axis, index, count

    def __hash__(self):
        return hash((self.key, self.axis, self.index, self.count))

    def __eq__(self, other):
        return isinstance(other, _Part) and (self.key, self.axis, self.index, self.count) == (
            other.key, other.axis, other.index, other.count)


def _rows(key, p):
    return _Part(key, 0, p, 2)


def _cols(key, p):
    return _Part(key, 1, p, 2)


class _Plan:
    def __init__(self, plan, make_side):
        self.plan, self.make_side, self.source, self.got = plan, make_side, {}, {}

    def item(self, k):
        return (self.source[k.key], (k.axis, k.index, k.count)) if isinstance(k, _Part) else (self.source[k], None)

    def run(self, fn, *args, name, **kw):
        keys = self.plan.get(name)
        if not keys:
            return fn(*args, name=name, **kw)
        result, outs = fn(*args, name=name, side=self.make_side([self.item(k) for k in keys]), **kw)
        self.got.update(zip(keys, outs))
        return result


GATHER_PLAN = {
    "l0s0_in": [("out", 0, 0), "hout", "kv"],
    "l0s0_out": ["hin"],
    "l0s1_proj": [("out", 0, 1), "q"],
    "l0s1_scan": [("in", 0, 1)],
    "l0s2_in": [("in", 1, 0), "o"],
    "l0s2_out": [("out", 1, 0)],
    "l1s0_in": [("in", 1, 1)],
    "l1s0_out": [("out", 1, 1)],
}
GATHER_FIRST = [("in", 0, 0)]
SCATTER_PLAN = {
    "l1s2_dact": [_cols(("out", 1, 1), 0)],
    "l1s2_dwin0": [_cols(("out", 1, 1), 1)],
    "l1s2_dwin1": [_rows(("in", 1, 1, 0), 0)],
    "l1s2_du": [_rows(("in", 1, 1, 0), 1), _rows(("in", 1, 1, 1), 0)],
    "l1s1_dattn": [_rows(("in", 1, 1, 1), 1), "o"],
    "l1s0_dwout": ["q"],
    "l1s0_dact": [_cols(("out", 1, 0), 0)],
    "l1s0_dwin0": [_cols(("out", 1, 0), 1)],
    "l1s0_dwin1": [_rows(("in", 1, 0, 0), 0)],
    "l1s0_du": [_rows(("in", 1, 0, 0), 1), _rows(("in", 1, 0, 1), 0)],
    "l0s2_dwout": [_rows(("in", 1, 0, 1), 1)],
    "l0s2_dact": [_cols(("out", 0, 1), 0)],
    "l0s2_dwin0": [_cols(("out", 0, 1), 1)],
    "l0s2_dwin1": [_rows(("in", 0, 1, 0), 0)],
    "l0s2_du": [_rows(("in", 0, 1, 0), 1), _rows(("in", 0, 1, 1), 0)],
    "l0s1_dwout": ["kv"],
    "l0s1_dscan": [_rows(("in", 0, 1, 1), 1), "hout"],
    "l0s1_dwin1": [_rows(("hin", 0), 0)],
    "l0s1_du": [_rows(("hin", 0), 1), _rows(("hin", 1), 0)],
    "l0s0_dwout": [_rows(("hin", 1), 1)],
    "l0s0_dact": [_cols(("out", 0, 0), 0)],
    "l0s0_dwin0": [_cols(("out", 0, 0), 1)],
    "l0s0_dwin1": [_rows(("in", 0, 0, 0), 0)],
    "l0s0_du": [_rows(("in", 0, 0, 0), 1), _rows(("in", 0, 0, 1), 0)],
    "adam_w_ada0": [_rows(("in", 0, 0, 1), 1)],
}
GRAD_TM, GRAD_TK = 1024, 1024


def _ffn_fwd(gp, W, h, gate, u, l, i, tag):
    ab4, h3, h3t = gp.run(ffn_in_act, u, W[("in", l, i)], name=tag + "_in")
    w_out = W[("out", l, i)]
    J = h3.shape[0]
    y, h_new = gp.run(mm_nn, h3, w_out.reshape(J, 1, -1, w_out.shape[-1]), F32, name=tag + "_out", natural=True,
                      resid=(h, gate, 0.5))
    return y, h_new, (ab4, h3t)


def _ffn_bwd(sp, W, dy, ut, ab4, h3t, l, i, tag):
    w_in, w_out = W[("in", l, i)], W[("out", l, i)]
    J, nb, S = h3t.shape
    D = w_out.shape[-1]
    dw_out = sp.run(mm_nn, h3t.reshape(1, J * nb, S), dy[None, None], BF16, name=tag + "_dwout", natural=True,
                    tm=w_out.shape[1], tk=GRAD_TK)
    sp.source[("out", l, i)] = dw_out.reshape(w_out.shape)
    dab3 = sp.run(ffn_dact, dy, w_out.reshape(J, nb, D), ab4, name=tag + "_dact")
    for hf in range(2):
        sp.source[("in", l, i, hf)] = sp.run(mm_nn, ut.reshape(2, D // 2, S), dab3[None], BF16, name=f"{tag}_dwin{hf}",
                                             a_sel=hf, tm=GRAD_TM, tk=GRAD_TK)
    return sp.run(mm_nt, dab3, w_in[:, None], F32, name=tag + "_du", natural=True, kgroup=4)


def kernel(x, c, norm_gain, w_ada, b_ada, w_ffn_in, w_ffn_out, w_hgrn_in, hgrn_lb_logits, hgrn_head_gain, w_hgrn_out, kv_gain, w_ada_kv, b_ada_kv, w_kv, b_kv, w_q, b_q, attn_sinks, w_attn_out, final_gain, loss_target, m_norm_gain, m_w_ada, m_b_ada, m_w_ffn_in, m_w_ffn_out, m_w_hgrn_in, m_hgrn_lb_logits, m_hgrn_head_gain, m_w_hgrn_out, m_kv_gain, m_w_ada_kv, m_b_ada_kv, m_w_kv, m_b_kv, m_w_q, m_b_q, m_attn_sinks, m_w_attn_out, m_final_gain, v_norm_gain, v_w_ada, v_b_ada, v_w_ffn_in, v_w_ffn_out, v_w_hgrn_in, v_hgrn_lb_logits, v_hgrn_head_gain, v_w_hgrn_out, v_kv_gain, v_w_ada_kv, v_b_ada_kv, v_w_kv, v_b_kv, v_w_q, v_b_q, v_attn_sinks, v_w_attn_out, v_final_gain):
    xi, yi, ci = _place()
    me = 4 * xi + 2 * yi + ci
    _, S, D = x.shape
    L = norm_gain.shape[0]
    dsh = D // N_DEV
    ada_n = w_ada.shape[2]
    kv_n = w_ada_kv.shape[1]
    NQ = D // ATT_HEAD
    NKV = NQ // ATT_GROUP
    kvd = NKV * ATT_HEAD
    h0 = x[0]
    target = loss_target[0]

    def my_cols(a, n):
        return lax.dynamic_slice_in_dim(a, me * n, n, axis=a.ndim - 1)

    gp = _Plan(GATHER_PLAN, gather_side)
    bf = lambda a: a.astype(BF16)
    for l in range(L):
        for i in range(2):
            gp.source[("in", l, i)] = bf(w_ffn_in[l, i])
            gp.source[("out", l, i)] = bf(w_ffn_out[l, i])
    gp.source.update(hin=bf(w_hgrn_in[0]), hout=bf(w_hgrn_out[0]), kv=bf(w_kv), q=bf(w_q[0]), o=bf(w_attn_out[0]))
    W = gp.got
    W.update(zip(GATHER_FIRST, exchange(gather_side([(gp.source[k], None) for k in GATHER_FIRST]), "gather_first")))
    full = lambda w: w.reshape(1, 1, -1, w.shape[-1])

    lb_sh = lower_bound_fwd(hgrn_lb_logits[0:1], hgrn_lb_logits[1:2], "lb_fwd")
    small = jnp.concatenate([c, norm_gain.reshape(1, L * 3 * dsh), hgrn_head_gain, lb_sh], axis=1)
    (g1,) = exchange(gather_side([(small, None)]), "gather_cond")
    g1 = g1.reshape(N_DEV, -1)
    c_all = g1[:, :D]
    gains = g1[:, D:D + L * 3 * dsh].reshape(N_DEV, L * 3, dsh).transpose(1, 0, 2).reshape(L, 3, 1, D)
    head_gain = g1[:, D + L * 3 * dsh:D + (L * 3 + 1) * dsh].reshape(1, D)
    lb0 = g1[:, D + (L * 3 + 1) * dsh:].reshape(1, D)

    parts = [cond_proj(c_all, w_ada[l], my_cols(b_ada[l:l + 1], ada_n), f"mod{l}") for l in range(L)]
    parts.append(cond_proj(c_all, w_ada_kv, my_cols(b_ada_kv[None], kv_n), "mod_kv"))
    (g2,) = exchange(gather_side([(jnp.concatenate(parts, axis=1), None)]), "gather_mod")
    mine2 = lax.dynamic_index_in_dim(g2, me, axis=1, keepdims=False)
    mod = [mine2[:, l * ada_n:(l + 1) * ada_n].reshape(3, 3, 1, D) for l in range(L)]
    mod_kv = mine2[:, L * ada_n:].reshape(2, 1, D)

    tables = rope_tables(S)
    sink_col = jnp.broadcast_to(attn_sinks.reshape(NKV, ATT_GROUP, 1, 1), (NKV, ATT_GROUP, ATT_WINDOW, 1))
    sink_col = sink_col.reshape(NKV, ATT_GROUP * ATT_WINDOW, 1)

    def to_heads(t, n):
        return t.reshape(S, n, ATT_HEAD).transpose(1, 0, 2)

    def from_heads(t):
        return t.transpose(1, 0, 2).reshape(S, -1)

    h = h0
    saved = {}
    for l in range(L):
        for s in (0, 1, 2):
            tag = f"l{l}s{s}"
            shift, scale, gate = mod[l][s, 0], mod[l][s, 1], mod[l][s, 2]
            u, ut = adaln_fwd(h, gains[l, s], shift, scale, tag + "_norm")
            if s != 1:
                y, h_new, res = _ffn_fwd(gp, W, h, gate, u, l, s // 2, tag)
            elif l == 0:
                proj = gp.run(mm_nn, u[None], W["hin"][None], F32, name=tag + "_proj", natural=True)
                o, states = gp.run(hgrn_scan_fwd, proj, lb0, name=tag + "_scan")
                z, zt = hgrn_post_fwd(o, proj, head_gain, tag + "_post")
                y, h_new = mm_nn(z[None], full(W["hout"]), F32, tag + "_out", natural=True, resid=(h, gate, 1.0))
                res = (proj, o, states, zt)
            else:
                q = mm_nn(u[None], full(W["q"]), F32, tag + "_q", natural=True, bias=b_q)
                q4 = to_heads(rope(q, tables, 1, tag + "_rope"), NQ).reshape(NKV, ATT_GROUP, S, ATT_HEAD)
                att4 = attn_fwd(q4, k3, v3, sink_col, tag + "_attn")
                att = from_heads(att4.reshape(NQ, S, ATT_HEAD))
                att_t = att4.reshape(NQ, S, ATT_HEAD).transpose(0, 2, 1).reshape(D, S)
                y, h_new = mm_nn(att[None], full(W["o"]), F32, tag + "_out", natural=True, resid=(h, gate, 1.0))
                res = (q4, att_t)
            saved[(l, s)] = (h, ut, y, res)
            h = h_new
        if l == 0:
            h_kv = h
            u_kv, u_kv_t = adaln_fwd(h, kv_gain[None], mod_kv[0], mod_kv[1], "kv_norm")
            kvp = mm_nn(u_kv[None], full(W["kv"]), F32, "kv_proj", natural=True, bias=b_kv[None])
            k3 = to_heads(rope(kvp[:, :kvd], tables, 1, "kv_rope"), NKV)
            v3 = to_heads(kvp[:, kvd:], NKV)

    loss_row, dh, d_final_gain = final_loss_grad(h, final_gain[None], target, "final")
    loss = lax.psum(loss_row[0, 0], ("x", "y", "c"))

    sp = _Plan(SCATTER_PLAN, scatter_side)

    def grad_w(a_t, b, name):
        return sp.run(mm_nn, a_t[None], b[None, None], BF16, name=name, natural=True, tm=GRAD_TM, tk=GRAD_TK)

    d_mod = [[None] * 3 for _ in range(L)]
    d_gain = [[None] * 3 for _ in range(L)]
    for l in reversed(range(L)):
        if l == 0:
            dkv = jnp.concatenate([rope(from_heads(dk3), tables, -1, "kv_drope"), from_heads(dv3)], axis=1)
            sp.source["kv"] = grad_w(u_kv_t, dkv, "kv_dw").reshape(W["kv"].shape)
            db_kv = colsum(dkv, "kv_db")
            du_kv = mm_nt(dkv[None], full(W["kv"]), F32, "kv_du", natural=True)
            dh, d_kv_gain, d_kv_shift, d_kv_scale = adaln_bwd(h_kv, kv_gain[None], mod_kv[0], mod_kv[1], du_kv, dh,
                                                              "kv_dnorm")
        for s in (2, 1, 0):
            tag = f"l{l}s{s}"
            shift, scale, gate = mod[l][s, 0], mod[l][s, 1], mod[l][s, 2]
            h_in, ut, y, res = saved[(l, s)]
            dy, d_gate = gate_bwd(dh, y, gate, 0.5 if s != 1 else 1.0, tag + "_dres")
            if s != 1:
                du = _ffn_bwd(sp, W, dy, ut, res[0], res[1], l, s // 2, tag)
            elif l == 0:
                proj, o, states, zt = res
                sp.source["hout"] = grad_w(zt, dy, tag + "_dwout").reshape(W["hout"].shape)
                dz = mm_nt(dy[None], full(W["hout"]), F32, tag + "_dz", natural=True)
                do, dg, d_head_gain = hgrn_post_bwd(o, proj, head_gain, dz, tag + "_dpost")
                dproj, d_lb0 = sp.run(hgrn_scan_bwd, proj, lb0, states, do, dg, name=tag + "_dscan")
                for hf in range(2):
                    sp.source[("hin", hf)] = sp.run(mm_nn, ut.reshape(2, D // 2, S), dproj, BF16, name=f"{tag}_dwin{hf}",
                                                    b_natural=True, jn=N_DEV, a_sel=hf, tm=GRAD_TM, tk=GRAD_TK)
                du = sp.run(mm_nt, dproj, W["hin"][:, None], F32, name=tag + "_du", natural=True, a_natural=True)
            else:
                q4, att_t = res
                sp.source["o"] = grad_w(att_t, dy, tag + "_dwout").reshape(W["o"].shape)
                datt = mm_nt(dy[None], full(W["o"]), F32, tag + "_datt", natural=True)
                datt4 = to_heads(datt, NQ).reshape(NKV, ATT_GROUP, S, ATT_HEAD)
                dq4, (dkp, dkc, dvp, dvc), d_sink = sp.run(attn_bwd, q4, k3, v3, sink_col, datt4, name=tag + "_dattn")
                dk3 = band_combine(dkc, dkp, tag + "_dk")
                dv3 = band_combine(dvc, dvp, tag + "_dv")
                dq = rope(from_heads(dq4.reshape(NQ, S, ATT_HEAD)), tables, -1, tag + "_drope")
                sp.source["q"] = grad_w(ut, dq, tag + "_dwq").reshape(W["q"].shape)
                db_q = colsum(dq, tag + "_dbq")
                du = mm_nt(dq[None], full(W["q"]), F32, tag + "_du", natural=True)
            dh, dg_, dsh_, dsc_ = adaln_bwd(h_in, gains[l, s], shift, scale, du, dh, tag + "_dnorm")
            d_gain[l][s] = dg_
            d_mod[l][s] = jnp.concatenate([dsh_, dsc_, d_gate], axis=1)
    grad_x = dh[None]
    G = sp.got

    pad = lambda a, n: jnp.pad(a, ((0, 0), (0, n - a.shape[1])))
    pieces = [jnp.concatenate(d_mod[l], axis=1) for l in range(L)]
    pieces += [d_kv_shift, d_kv_scale]
    pieces += [d_gain[l][s] for l in range(L) for s in range(3)]
    pieces += [d_head_gain, d_lb0, d_kv_gain, db_kv, db_q, pad(d_sink.reshape(1, NQ), LANES), d_final_gain]
    (g3,) = exchange(gather_side([(jnp.concatenate(pieces, axis=1), None)]), "gather_small_grads")
    g3 = g3.reshape(N_DEV, -1)
    tot = rowsum(g3, "sum_small_grads")
    offs = [0]
    for p in pieces:
        offs.append(offs[-1] + p.shape[1])
    seg = lambda k: tot[:, offs[k]:offs[k + 1]]
    k0 = 0
    g_b_ada = jnp.concatenate([seg(l) for l in range(L)], axis=0)
    k0 += L
    g_b_ada_kv = jnp.concatenate([seg(k0), seg(k0 + 1)], axis=1)
    k0 += 2
    g_norm_gain = jnp.concatenate([my_cols(seg(k0 + j), dsh) for j in range(3 * L)], axis=0)
    k0 += 3 * L
    g_head_gain = my_cols(seg(k0), dsh)
    d_lb_sh = my_cols(seg(k0 + 1), dsh)
    g_kv_gain = seg(k0 + 2)
    g_b_kv = seg(k0 + 3)
    g_b_q = seg(k0 + 4)
    g_sinks = seg(k0 + 5)[:, :NQ]
    g_final_gain = seg(k0 + 6)
    dl0, dl1 = lower_bound_bwd(hgrn_lb_logits[0:1], hgrn_lb_logits[1:2], d_lb_sh, "lb_bwd")
    g_lb_logits = jnp.concatenate([dl0, dl1], axis=0)

    g_w_ada = jnp.stack([cond_grad(c_all, lax.dynamic_slice_in_dim(g3, offs[l] + me * ada_n, ada_n, axis=1),
                                   f"dw_ada{l}") for l in range(L)])
    g_w_ada_kv = cond_grad(c_all, lax.dynamic_slice_in_dim(g3, offs[L] + me * kv_n, kv_n, axis=1), "dw_ada_kv")

    def update(w, m, v, grads, name, ncol=1):
        n = len(grads) // ncol
        width = w.shape[-1]
        if ncol == 1 and (w.size // n) % (8 * 1024) == 0:
            width = 1024
        three = lambda a: a.reshape((n, -1, width))
        outs = None
        for k, g in enumerate(grads):
            j, cp = divmod(k, ncol)
            g = g.reshape((g.shape[0], -1, width // ncol))
            if len(grads) == 1:
                outs = sp.run(adamw, three(w), three(m), three(v), j, g, name=f"{name}{k}")
            else:
                outs = adamw(three(w), three(m), three(v), j, g, f"{name}{k}", col=(cp, ncol), into=outs)
        return [o.reshape(w.shape) for o in outs]

    def one(w, m, v, g, name):
        return update(w, m, v, [g.reshape((1, -1, w.shape[-1]))], name)

    res = {}
    res["norm_gain"] = one(norm_gain, m_norm_gain, v_norm_gain, g_norm_gain, "adam_norm_gain")
    res["w_ada"] = one(w_ada, m_w_ada, v_w_ada, g_w_ada, "adam_w_ada")
    res["b_ada"] = one(b_ada, m_b_ada, v_b_ada, g_b_ada, "adam_b_ada")
    sent = {k.key if isinstance(k, _Part) else k for k in G}
    rest = [k for k in sp.source if k not in sent]
    if rest:
        G.update(zip(rest, exchange(scatter_side([(sp.source[k], None) for k in rest]), "scatter_rest")))
    g_ffn_in = [G[_rows(("in", l, i, hf), p)] for l in range(L) for i in range(2) for hf in range(2) for p in range(2)]
    g_ffn_out = [G[_cols(("out", l, i), p)] for l in range(L) for i in range(2) for p in range(2)]
    g_hgrn_in = [G[_rows(("hin", hf), p)] for hf in range(2) for p in range(2)]
    res["w_ffn_in"] = update(w_ffn_in, m_w_ffn_in, v_w_ffn_in, g_ffn_in, "adam_w_ffn_in")
    res["w_ffn_out"] = update(w_ffn_out, m_w_ffn_out, v_w_ffn_out, g_ffn_out, "adam_w_ffn_out", ncol=2)
    res["w_hgrn_in"] = update(w_hgrn_in, m_w_hgrn_in, v_w_hgrn_in, g_hgrn_in, "adam_w_hgrn_in")
    res["hgrn_lb_logits"] = one(hgrn_lb_logits, m_hgrn_lb_logits, v_hgrn_lb_logits, g_lb_logits, "adam_lb")
    res["hgrn_head_gain"] = one(hgrn_head_gain, m_hgrn_head_gain, v_hgrn_head_gain, g_head_gain, "adam_head_gain")
    res["w_hgrn_out"] = update(w_hgrn_out, m_w_hgrn_out, v_w_hgrn_out, [G["hout"]], "adam_w_hgrn_out")
    res["kv_gain"] = one(kv_gain, m_kv_gain, v_kv_gain, g_kv_gain, "adam_kv_gain")
    res["w_ada_kv"] = one(w_ada_kv, m_w_ada_kv, v_w_ada_kv, g_w_ada_kv, "adam_w_ada_kv")
    res["b_ada_kv"] = one(b_ada_kv, m_b_ada_kv, v_b_ada_kv, g_b_ada_kv, "adam_b_ada_kv")
    res["w_kv"] = update(w_kv, m_w_kv, v_w_kv, [G["kv"]], "adam_w_kv")
    res["b_kv"] = one(b_kv, m_b_kv, v_b_kv, g_b_kv, "adam_b_kv")
    res["w_q"] = update(w_q, m_w_q, v_w_q, [G["q"]], "adam_w_q")
    res["b_q"] = one(b_q, m_b_q, v_b_q, g_b_q, "adam_b_q")
    res["attn_sinks"] = one(attn_sinks, m_attn_sinks, v_attn_sinks, g_sinks, "adam_sinks")
    res["w_attn_out"] = update(w_attn_out, m_w_attn_out, v_w_attn_out, [G["o"]], "adam_w_attn_out")
    res["final_gain"] = one(final_gain, m_final_gain, v_final_gain, g_final_gain, "adam_final_gain")

    names = ["norm_gain", "w_ada", "b_ada", "w_ffn_in", "w_ffn_out", "w_hgrn_in", "hgrn_lb_logits", "hgrn_head_gain",
             "w_hgrn_out", "kv_gain", "w_ada_kv", "b_ada_kv", "w_kv", "b_kv", "w_q", "b_q", "attn_sinks", "w_attn_out",
             "final_gain"]
    return (loss, grad_x, *[res[n][0] for n in names], *[res[n][1] for n in names], *[res[n][2] for n in names],
            *[res[n][3] for n in names])
```

```python
import functools

import jax
import jax.numpy as jnp
from jax import lax
from jax.experimental import pallas as pl
from jax.experimental.pallas import tpu as pltpu

F32 = jnp.float32
BF16 = jnp.bfloat16
MESH = pl.DeviceIdType.MESH

N_DEV = 8
V7X_VMEM_LIMIT_BYTES = 56 * 1024 * 1024
LANES = 128

NORM_EPS = 1e-6
NEG_INF = -1e30
HGRN_CHUNK = 32
HGRN_HEAD = 128
ATT_HEAD = 64
ATT_WINDOW = 128
ATT_GROUP = 8
ROT_DIM = 16
ROPE_THETA = 500000.0

ADAM_LR = 0.001
ADAM_B1 = 0.9
ADAM_B2 = 0.999
ADAM_EPS = 1e-08
ADAM_WD = 0.01
ADAM_STEP = 10


def _params(*sem):
    return pltpu.CompilerParams(dimension_semantics=sem, vmem_limit_bytes=V7X_VMEM_LIMIT_BYTES)


def _tile(n, pref, unit=LANES):
    t = (min(n, pref) // unit) * unit
    while t >= unit:
        if n % t == 0:
            return t
        t -= unit
    return n


_ANY = pl.BlockSpec(memory_space=pl.ANY)


class _Side:
    def __init__(self, inputs, out_shapes, sem_shapes, start, finish, middle=None):
        self.inputs, self.out_shapes, self.sem_shapes = list(inputs), list(out_shapes), list(sem_shapes)
        self.start, self.finish, self.middle = start, finish, middle


def _call(body, *, name, grid, in_specs, out_specs, out_shape, args, semantics, scratch_shapes=(), side=None):
    in_specs, out_specs, out_shape = list(in_specs), list(out_specs), list(out_shape)
    scratch_shapes = list(scratch_shapes)
    if side is None:
        outs = pl.pallas_call(
            body, name=name, grid=grid, in_specs=in_specs, out_specs=out_specs, out_shape=out_shape,
            scratch_shapes=scratch_shapes, compiler_params=_params(*semantics))(*args)
        return list(outs)
    n_in, n_out, n_scr = len(in_specs), len(out_specs), len(scratch_shapes)
    s_in, s_out = len(side.inputs), len(side.out_shapes)

    def carried(*refs):
        ins, refs = refs[:n_in], refs[n_in:]
        side_ins, refs = refs[:s_in], refs[s_in:]
        outs, refs = refs[:n_out], refs[n_out:]
        side_outs, refs = refs[:s_out], refs[s_out:]
        scratch, sems = refs[:n_scr], refs[n_scr:]
        step, steps = pl.program_id(0), grid[0]
        for d in range(1, len(grid)):
            step, steps = step * grid[d] + pl.program_id(d), steps * grid[d]

        @pl.when(step == 0)
        def _():
            side.start(side_ins, side_outs, sems)

        body(*ins, *outs, *scratch)

        if side.middle is not None:
            @pl.when(step == max(3 * steps // 4 - 1, 0))
            def _():
                side.middle(side_ins, side_outs, sems)

        @pl.when(step == steps - 1)
        def _():
            side.finish(side_ins, side_outs, sems)

    outs = pl.pallas_call(
        carried, name=name, grid=grid, in_specs=in_specs + [_ANY] * s_in, out_specs=out_specs + [_ANY] * s_out,
        out_shape=out_shape + side.out_shapes, scratch_shapes=scratch_shapes + side.sem_shapes,
        compiler_params=_params(*(["arbitrary"] * len(grid))))(*args, *side.inputs)
    return list(outs[:n_out]), list(outs[n_out:])


def _accumulate(prod, o_ref, acc_ref, k, nk):
    if nk == 1:
        o_ref[...] = prod.astype(o_ref.dtype)
        return

    @pl.when(k == 0)
    def _():
        acc_ref[...] = prod

    @pl.when(k > 0)
    def _():
        acc_ref[...] += prod

    @pl.when(k == nk - 1)
    def _():
        o_ref[...] = acc_ref[...].astype(o_ref.dtype)


def mm_nn(a3, b4, out_dtype, name, natural=False, a_natural=False, b_natural=False, jn=None, a_sel=None, bias=None,
          resid=None, tm=512, tk=None, side=None):
    if b_natural:
        JK, JN, kb = 1, jn, b4.shape[0]
        nb = b4.shape[1] // JN
    else:
        JK, JN, kb, nb = b4.shape
    M = a3.shape[0] if a_natural else a3.shape[1]
    tm = _tile(M, tm, 16)
    tn = _tile(nb, 1024)
    tk = kb if tk is None else _tile(kb, tk)
    ntn, nkt = nb // tn, kb // tk
    nk = JK * nkt

    def body(*refs):
        a_ref, b_ref = refs[:2]
        acc_ref = refs[-1]
        prod = jnp.dot(a_ref[...].astype(BF16), b_ref[...].astype(BF16), preferred_element_type=F32)
        if bias is not None:
            prod = prod + refs[2][...]
        if resid is None:
            _accumulate(prod, refs[-2], acc_ref, pl.program_id(2), nk)
            return
        h_ref, g_ref, y_ref, o_ref = refs[2:6]
        k = pl.program_id(2)

        @pl.when(k == 0)
        def _():
            acc_ref[...] = prod

        @pl.when(k > 0)
        def _():
            acc_ref[...] += prod

        @pl.when(k == nk - 1)
        def _():
            y = acc_ref[...]
            y_ref[...] = y
            o_ref[...] = h_ref[...] + (resid[2] * g_ref[...]) * y

    if a_natural:
        a_spec = pl.BlockSpec((tm, tk), lambda j, i, k: (i, k))
    elif a_sel is not None:
        a_spec = pl.BlockSpec((None, tm, tk), lambda j, i, k: (a_sel, i, k))
    else:
        a_spec = pl.BlockSpec((None, tm, tk), lambda j, i, k: (k // nkt, i, k % nkt))
    if b_natural:
        b_spec = pl.BlockSpec((tk, tn), lambda j, i, k: (k, j))
    else:
        b_spec = pl.BlockSpec((None, None, tk, tn), lambda j, i, k: (k // nkt, j // ntn, k % nkt, j % ntn))
    in_specs = [a_spec, b_spec]
    args = [a3, b4]
    if bias is not None:
        assert natural and nk == 1
        in_specs.append(pl.BlockSpec((1, tn), lambda j, i, k: (0, j)))
        args.append(bias)
    if natural:
        out_shape = jax.ShapeDtypeStruct((M, JN * nb), out_dtype)
        o_spec = pl.BlockSpec((tm, tn), lambda j, i, k: (i, j))
    else:
        out_shape = jax.ShapeDtypeStruct((JN, M, nb), out_dtype)
        o_spec = pl.BlockSpec((None, tm, tn), lambda j, i, k: (j // ntn, i, j % ntn))
    out_specs, out_shapes = [o_spec], [out_shape]
    if resid is not None:
        assert natural and bias is None and out_dtype == F32
        in_specs += [o_spec, pl.BlockSpec((1, tn), lambda j, i, k: (0, j))]
        args += [resid[0], resid[1]]
        out_specs, out_shapes = [o_spec, o_spec], [out_shape, out_shape]
    res = _call(body, name=name, grid=(JN * ntn, M // tm, nk), in_specs=in_specs, out_specs=out_specs,
                out_shape=out_shapes, scratch_shapes=[pltpu.VMEM((tm, tn), F32)], args=args,
                semantics=("parallel", "parallel", "arbitrary"), side=side)
    outs = res if side is None else res[0]
    ans = outs[0] if resid is None else (outs[0], outs[1])
    return ans if side is None else (ans, res[1])


def mm_nt(a3, b4, out_dtype, name, natural=False, a_natural=False, kgroup=1, side=None):
    JK, JN, nb, kb = b4.shape
    M = a3.shape[0] if a_natural else a3.shape[1]
    tm = min(M, 512)
    tn = _tile(nb, 1024)
    ntn = nb // tn
    nk = JK // kgroup

    def body(a_ref, b_ref, o_ref, acc_ref):
        nt = (((1,), (1,)), ((), ()))
        if kgroup == 1:
            prod = lax.dot_general(a_ref[...].astype(BF16), b_ref[...].astype(BF16), nt, preferred_element_type=F32)
        else:
            prod = lax.dot_general(a_ref[0].astype(BF16), b_ref[0].astype(BF16), nt, preferred_element_type=F32)
            for g in range(1, kgroup):
                prod += lax.dot_general(a_ref[g].astype(BF16), b_ref[g].astype(BF16), nt, preferred_element_type=F32)
        _accumulate(prod, o_ref, acc_ref, pl.program_id(2), nk)

    if kgroup > 1:
        assert JN == 1 and not a_natural
        a3 = a3.reshape(nk, kgroup, M, kb)
        b4 = b4.reshape(nk, kgroup, nb, kb)
        a_spec = pl.BlockSpec((None, kgroup, tm, kb), lambda j, i, k: (k, 0, i, 0))
        b_spec = pl.BlockSpec((None, kgroup, tn, kb), lambda j, i, k: (k, 0, j, 0))
    elif a_natural:
        a_spec = pl.BlockSpec((tm, kb), lambda j, i, k: (i, k))
        b_spec = pl.BlockSpec((None, None, tn, kb), lambda j, i, k: (k, j // ntn, j % ntn, 0))
    else:
        a_spec = pl.BlockSpec((None, tm, kb), lambda j, i, k: (k, i, 0))
        b_spec = pl.BlockSpec((None, None, tn, kb), lambda j, i, k: (k, j // ntn, j % ntn, 0))
    if natural:
        out_shape = jax.ShapeDtypeStruct((M, JN * nb), out_dtype)
        o_spec = pl.BlockSpec((tm, tn), lambda j, i, k: (i, j))
    else:
        out_shape = jax.ShapeDtypeStruct((JN, M, nb), out_dtype)
        o_spec = pl.BlockSpec((None, tm, tn), lambda j, i, k: (j // ntn, i, j % ntn))
    res = _call(body, name=name, grid=(JN * ntn, M // tm, nk), in_specs=[a_spec, b_spec], out_specs=[o_spec],
                out_shape=[out_shape], scratch_shapes=[pltpu.VMEM((tm, tn), F32)], args=[a3, b4],
                semantics=("parallel", "parallel", "arbitrary"), side=side)
    return res[0] if side is None else (res[0][0], res[1])


def _row_tile(S):
    return min(S, 256)


def _adaln(h, gain, shift, scale):
    y = h * lax.rsqrt(jnp.mean(h * h, axis=-1, keepdims=True) + NORM_EPS) * gain
    return y * (1.0 + scale) + shift


def adaln_fwd(h, gain, shift, scale, name):
    S, D = h.shape
    tm = _row_tile(S)

    def body(h_ref, g_ref, sh_ref, sc_ref, u_ref, ut_ref):
        u = _adaln(h_ref[...], g_ref[...], sh_ref[...], sc_ref[...])
        u_ref[...] = u.astype(BF16)
        ut_ref[...] = u.T.astype(BF16)

    row = pl.BlockSpec((tm, D), lambda i: (i, 0))
    vec = pl.BlockSpec((1, D), lambda i: (0, 0))
    return pl.pallas_call(
        body, name=name, grid=(S // tm,), in_specs=[row, vec, vec, vec],
        out_specs=[row, pl.BlockSpec((D, tm), lambda i: (0, i))],
        out_shape=[jax.ShapeDtypeStruct((S, D), BF16), jax.ShapeDtypeStruct((D, S), BF16)],
        compiler_params=_params("parallel"),
    )(h, gain, shift, scale)


def _gate_specs(S, D, tm, nxt):
    if nxt is None:
        return [], [], [], []
    row = pl.BlockSpec((tm, D), lambda i: (i, 0))
    vec = pl.BlockSpec((1, D), lambda i: (0, 0))
    return ([row, vec], [nxt[0], nxt[1]], [row, vec],
            [jax.ShapeDtypeStruct((S, D), BF16), jax.ShapeDtypeStruct((1, D), F32)])


def _gate_grads(dh, nxt, y_ref, g_ref, dy_ref, dg_ref):
    coef = nxt[2]
    dy_ref[...] = ((coef * g_ref[...]) * dh).astype(BF16)

    @pl.when(pl.program_id(0) == 0)
    def _():
        dg_ref[...] = jnp.zeros_like(dg_ref)

    dg_ref[...] += coef * jnp.sum(dh * y_ref[...], axis=0, keepdims=True)


def adaln_bwd(h, gain, shift, scale, du, dres, name, nxt=None):
    S, D = h.shape
    tm = _row_tile(S)

    def body(h_ref, g_ref, sh_ref, sc_ref, du_ref, dres_ref, *rest):
        dh_ref, dg_ref, dsh_ref, dsc_ref = rest[-6:-2] if nxt is not None else rest
        _, vjp = jax.vjp(_adaln, h_ref[...], g_ref[...], sh_ref[...], sc_ref[...])
        dh, dg, dsh, dsc = vjp(du_ref[...].astype(F32))
        dh = dres_ref[...] + dh
        dh_ref[...] = dh

        @pl.when(pl.program_id(0) == 0)
        def _():
            dg_ref[...] = jnp.zeros_like(dg_ref)
            dsh_ref[...] = jnp.zeros_like(dsh_ref)
            dsc_ref[...] = jnp.zeros_like(dsc_ref)

        dg_ref[...] += dg
        dsh_ref[...] += dsh
        dsc_ref[...] += dsc
        if nxt is not None:
            _gate_grads(dh, nxt, rest[0], rest[1], rest[-2], rest[-1])

    row = pl.BlockSpec((tm, D), lambda i: (i, 0))
    vec = pl.BlockSpec((1, D), lambda i: (0, 0))
    vs = jax.ShapeDtypeStruct((1, D), F32)
    xin, xargs, xout, xshape = _gate_specs(S, D, tm, nxt)
    return pl.pallas_call(
        body, name=name, grid=(S // tm,), in_specs=[row, vec, vec, vec, row, row] + xin,
        out_specs=[row, vec, vec, vec] + xout, out_shape=[jax.ShapeDtypeStruct((S, D), F32), vs, vs, vs] + xshape,
        compiler_params=_params("arbitrary"),
    )(h, gain, shift, scale, du, dres, *xargs)


def _swiglu(a, b):
    return a * jax.nn.sigmoid(a) * b


def ffn_in_act(u, w_in, name, side=None):
    J2, D, nb = w_in.shape
    J = J2 // 2
    S = u.shape[0]
    tm = _tile(S, 512, 16)

    def body(u_ref, wa_ref, wb_ref, ab_ref, h_ref, ht_ref):
        uv = u_ref[...]
        a = jnp.dot(uv, wa_ref[...], preferred_element_type=F32)
        b = jnp.dot(uv, wb_ref[...], preferred_element_type=F32)
        ab_ref[0] = a
        ab_ref[1] = b
        hv = _swiglu(a, b)
        h_ref[...] = hv.astype(BF16)
        ht_ref[...] = hv.T.astype(BF16)

    res = _call(body, name=name, grid=(J, S // tm),
                in_specs=[pl.BlockSpec((tm, D), lambda j, i: (i, 0)),
                          pl.BlockSpec((None, D, nb), lambda j, i: (j, 0, 0)),
                          pl.BlockSpec((None, D, nb), lambda j, i: (j + J, 0, 0))],
                out_specs=[pl.BlockSpec((2, None, tm, nb), lambda j, i: (0, j, i, 0)),
                           pl.BlockSpec((None, tm, nb), lambda j, i: (j, i, 0)),
                           pl.BlockSpec((None, nb, tm), lambda j, i: (j, 0, i))],
                out_shape=[jax.ShapeDtypeStruct((2, J, S, nb), F32), jax.ShapeDtypeStruct((J, S, nb), BF16),
                           jax.ShapeDtypeStruct((J, nb, S), BF16)],
                args=[u, w_in, w_in], semantics=("parallel", "parallel"), side=side)
    return res if side is None else (res[0], res[1])


def ffn_dact(dy, w_out4, ab4, name, side=None):
    J, nb, D = w_out4.shape
    S = dy.shape[0]
    tm = _tile(S, 512, 16)

    def body(dy_ref, w_ref, ab_ref, o_ref):
        dh = lax.dot_general(dy_ref[...], w_ref[...], (((1,), (1,)), ((), ())), preferred_element_type=F32)
        _, vjp = jax.vjp(_swiglu, ab_ref[0], ab_ref[1])
        da, db = vjp(dh)
        o_ref[0] = da.astype(BF16)
        o_ref[1] = db.astype(BF16)

    both = pl.BlockSpec((2, None, tm, nb), lambda j, i: (0, j, i, 0))
    res = _call(body, name=name, grid=(J, S // tm),
                in_specs=[pl.BlockSpec((tm, D), lambda j, i: (i, 0)),
                          pl.BlockSpec((None, nb, D), lambda j, i: (j, 0, 0)), both],
                out_specs=[both], out_shape=[jax.ShapeDtypeStruct((2, J, S, nb), BF16)],
                args=[dy, w_out4, ab4], semantics=("parallel", "parallel"), side=side)
    if side is None:
        return res[0].reshape(2 * J, S, nb)
    return res[0][0].reshape(2 * J, S, nb), res[1]


def colsum(x, name):
    S, N = x.shape
    tm = _row_tile(S)

    def body(x_ref, o_ref):
        @pl.when(pl.program_id(0) == 0)
        def _():
            o_ref[...] = jnp.zeros_like(o_ref)

        o_ref[...] += jnp.sum(x_ref[...].astype(F32), axis=0, keepdims=True)

    return pl.pallas_call(
        body, name=name, grid=(S // tm,), in_specs=[pl.BlockSpec((tm, N), lambda i: (i, 0))],
        out_specs=pl.BlockSpec((1, N), lambda i: (0, 0)), out_shape=jax.ShapeDtypeStruct((1, N), F32),
        compiler_params=_params("arbitrary"),
    )(x)


def _final_loss(h, gain, target):
    y = h * lax.rsqrt(jnp.mean(h * h, axis=-1, keepdims=True) + NORM_EPS) * gain
    err = y - target
    return 0.5 * jnp.sum(jnp.mean(err * err, axis=-1))


def final_loss_grad(h, gain, target, name, nxt):
    S, D = h.shape
    tm = _row_tile(S)

    def body(h_ref, g_ref, t_ref, y_ref, gate_ref, loss_ref, dh_ref, dg_ref, dy_ref, dgate_ref):
        loss, (dh, dg) = jax.value_and_grad(_final_loss, argnums=(0, 1))(h_ref[...], g_ref[...], t_ref[...])
        dh_ref[...] = dh

        @pl.when(pl.program_id(0) == 0)
        def _():
            loss_ref[...] = jnp.zeros_like(loss_ref)
            dg_ref[...] = jnp.zeros_like(dg_ref)

        loss_ref[...] += jnp.full(loss_ref.shape, loss, F32)
        dg_ref[...] += dg
        _gate_grads(dh, nxt, y_ref, gate_ref, dy_ref, dgate_ref)

    row = pl.BlockSpec((tm, D), lambda i: (i, 0))
    vec = pl.BlockSpec((1, D), lambda i: (0, 0))
    xin, xargs, xout, xshape = _gate_specs(S, D, tm, nxt)
    return pl.pallas_call(
        body, name=name, grid=(S // tm,), in_specs=[row, vec, row] + xin,
        out_specs=[pl.BlockSpec((1, LANES), lambda i: (0, 0)), row, vec] + xout,
        out_shape=[jax.ShapeDtypeStruct((1, LANES), F32), jax.ShapeDtypeStruct((S, D), F32),
                   jax.ShapeDtypeStruct((1, D), F32)] + xshape,
        compiler_params=_params("arbitrary"),
    )(h, gain, target, *xargs)


def _chunk_consts(H):
    C = HGRN_CHUNK
    t = lax.broadcasted_iota(jnp.int32, (H, C, C), 1)
    s = lax.broadcasted_iota(jnp.int32, (H, C, C), 2)
    return (s <= t).astype(F32), s <= t


def _hgrn_chunk(q_raw, f_raw, i_raw, lb, st):
    H, C, _ = q_raw.shape
    lower, causal = _chunk_consts(H)
    forget = lb + (1.0 - lb) * jax.nn.sigmoid(f_raw)
    g = jnp.log(forget)
    kk = 1.0 - forget
    qs = q_raw * jax.nn.sigmoid(q_raw)
    bnn = (((2,), (1,)), ((0,), (0,)))
    bnt = (((2,), (2,)), ((0,), (0,)))
    btn = (((1,), (1,)), ((0,), (0,)))
    b = lax.dot_general(lower, g, bnn, precision=lax.Precision.HIGHEST, preferred_element_type=F32)
    bm = b[:, C // 2 - 1:C // 2, :]
    bl = b[:, C - 1:C, :]
    inter = lax.dot_general((qs * jnp.exp(b)).astype(BF16), st.astype(BF16), bnt, preferred_element_type=F32)
    qt = (qs * jnp.exp(b - bm)).astype(BF16)
    kt = (kk * jnp.exp(bm - b)).astype(BF16)
    scores = lax.dot_general(qt, kt, bnt, preferred_element_type=F32)
    scores = jnp.where(causal, scores, 0.0)
    vb = i_raw.astype(BF16)
    out = inter + lax.dot_general(scores.astype(BF16), vb, bnn, preferred_element_type=F32)
    kdec = (kk * jnp.exp(bl - b)).astype(BF16)
    new_st = st * jnp.exp(bl) + lax.dot_general(vb, kdec, btn, preferred_element_type=F32)
    return out, new_st


def _heads(ref, rows, H):
    return jnp.stack([ref[rows, pl.ds(h * HGRN_HEAD, HGRN_HEAD)] for h in range(H)])


def hgrn_scan_fwd(proj, lb, name, side=None):
    S, D4 = proj.shape
    D = D4 // 4
    H = D // HGRN_HEAD
    C = HGRN_CHUNK
    R = min(S, 128)
    ncr = R // C

    def body(q_ref, f_ref, i_ref, lb_ref, o_ref, st_ref, state):
        @pl.when(pl.program_id(0) == 0)
        def _():
            state[...] = jnp.zeros_like(state)

        lbh = _heads(lb_ref, slice(None), H)

        def chunk(cc, carry):
            rows = pl.ds(pl.multiple_of(cc * C, C), C)
            st = state[...]
            st_ref[cc] = st
            out, new_st = _hgrn_chunk(_heads(q_ref, rows, H), _heads(f_ref, rows, H), _heads(i_ref, rows, H), lbh, st)
            for h in range(H):
                o_ref[rows, pl.ds(h * HGRN_HEAD, HGRN_HEAD)] = out[h]
            state[...] = new_st
            return carry

        lax.fori_loop(0, ncr, chunk, 0)

    col = lambda j: pl.BlockSpec((R, D), lambda i: (i, j))
    res = _call(body, name=name, grid=(S // R,),
                in_specs=[col(0), col(1), col(2), pl.BlockSpec((1, D), lambda i: (0, 0))],
                out_specs=[pl.BlockSpec((R, D), lambda i: (i, 0)),
                           pl.BlockSpec((ncr, H, HGRN_HEAD, HGRN_HEAD), lambda i: (i, 0, 0, 0))],
                out_shape=[jax.ShapeDtypeStruct((S, D), F32),
                           jax.ShapeDtypeStruct((S // C, H, HGRN_HEAD, HGRN_HEAD), F32)],
                scratch_shapes=[pltpu.VMEM((H, HGRN_HEAD, HGRN_HEAD), F32)], args=[proj, proj, proj, lb],
                semantics=("arbitrary",), side=side)
    return res if side is None else (res[0], res[1])


def hgrn_scan_bwd(proj, lb, states, do, dg, name, side=None):
    S, D4 = proj.shape
    D = D4 // 4
    H = D // HGRN_HEAD
    C = HGRN_CHUNK
    R = min(S, 128)
    ncr = R // C
    ng = S // R

    def body(q_ref, f_ref, i_ref, lb_ref, st_ref, do_ref, dg_ref, dp_ref, dlb_ref, dstate):
        @pl.when(pl.program_id(0) == 0)
        def _():
            dstate[...] = jnp.zeros_like(dstate)
            dlb_ref[...] = jnp.zeros_like(dlb_ref)

        dp_ref[:, pl.ds(3 * D, D)] = dg_ref[...].astype(BF16)
        lbh = _heads(lb_ref, slice(None), H)

        def chunk(t, carry):
            cc = ncr - 1 - t
            rows = pl.ds(pl.multiple_of(cc * C, C), C)
            _, vjp = jax.vjp(_hgrn_chunk, _heads(q_ref, rows, H), _heads(f_ref, rows, H), _heads(i_ref, rows, H),
                             lbh, st_ref[cc])
            dq, df, di, dlb, dst = vjp((_heads(do_ref, rows, H), dstate[...]))
            for h in range(H):
                dp_ref[rows, pl.ds(h * HGRN_HEAD, HGRN_HEAD)] = dq[h].astype(BF16)
                dp_ref[rows, pl.ds(D + h * HGRN_HEAD, HGRN_HEAD)] = df[h].astype(BF16)
                dp_ref[rows, pl.ds(2 * D + h * HGRN_HEAD, HGRN_HEAD)] = di[h].astype(BF16)
                dlb_ref[:, pl.ds(h * HGRN_HEAD, HGRN_HEAD)] += dlb[h]
            dstate[...] = dst
            return carry

        lax.fori_loop(0, ncr, chunk, 0)

    col = lambda j: pl.BlockSpec((R, D), lambda i: (ng - 1 - i, j))
    res = _call(body, name=name, grid=(ng,),
                in_specs=[col(0), col(1), col(2), pl.BlockSpec((1, D), lambda i: (0, 0)),
                          pl.BlockSpec((ncr, H, HGRN_HEAD, HGRN_HEAD), lambda i: (ng - 1 - i, 0, 0, 0)),
                          pl.BlockSpec((R, D), lambda i: (ng - 1 - i, 0)),
                          pl.BlockSpec((R, D), lambda i: (ng - 1 - i, 0))],
                out_specs=[pl.BlockSpec((R, D4), lambda i: (ng - 1 - i, 0)), pl.BlockSpec((1, D), lambda i: (0, 0))],
                out_shape=[jax.ShapeDtypeStruct((S, D4), BF16), jax.ShapeDtypeStruct((1, D), F32)],
                scratch_shapes=[pltpu.VMEM((H, HGRN_HEAD, HGRN_HEAD), F32)],
                args=[proj, proj, proj, lb, states, do, dg], semantics=("arbitrary",), side=side)
    return res if side is None else (res[0], res[1])


def _head_out(o, g, gain):
    y = o * lax.rsqrt(jnp.mean(o * o, axis=-1, keepdims=True) + NORM_EPS) * gain
    return y * jax.nn.sigmoid(g)


def hgrn_post_fwd(o, proj, gain, name):
    S, D = o.shape
    H = D // HGRN_HEAD
    tm = _row_tile(S)

    def body(o_ref, g_ref, gain_ref, z_ref, zt_ref):
        for h in range(H):
            ls = pl.ds(h * HGRN_HEAD, HGRN_HEAD)
            z = _head_out(o_ref[:, ls], g_ref[:, ls], gain_ref[:, ls])
            z_ref[:, ls] = z.astype(BF16)
            zt_ref[ls, :] = z.T.astype(BF16)

    row = pl.BlockSpec((tm, D), lambda i: (i, 0))
    return pl.pallas_call(
        body, name=name, grid=(S // tm,),
        in_specs=[row, pl.BlockSpec((tm, D), lambda i: (i, 3)), pl.BlockSpec((1, D), lambda i: (0, 0))],
        out_specs=[row, pl.BlockSpec((D, tm), lambda i: (0, i))],
        out_shape=[jax.ShapeDtypeStruct((S, D), BF16), jax.ShapeDtypeStruct((D, S), BF16)],
        compiler_params=_params("parallel"),
    )(o, proj, gain)


def hgrn_post_bwd(o, proj, gain, dz, name):
    S, D = o.shape
    H = D // HGRN_HEAD
    tm = _row_tile(S)

    def body(o_ref, g_ref, gain_ref, dz_ref, do_ref, dg_ref, dgain_ref):
        @pl.when(pl.program_id(0) == 0)
        def _():
            dgain_ref[...] = jnp.zeros_like(dgain_ref)

        for h in range(H):
            ls = pl.ds(h * HGRN_HEAD, HGRN_HEAD)
            _, vjp = jax.vjp(_head_out, o_ref[:, ls], g_ref[:, ls], gain_ref[:, ls])
            do, dg, dgain = vjp(dz_ref[:, ls].astype(F32))
            do_ref[:, ls] = do
            dg_ref[:, ls] = dg
            dgain_ref[:, ls] += dgain

    row = pl.BlockSpec((tm, D), lambda i: (i, 0))
    vec = pl.BlockSpec((1, D), lambda i: (0, 0))
    return pl.pallas_call(
        body, name=name, grid=(S // tm,),
        in_specs=[row, pl.BlockSpec((tm, D), lambda i: (i, 3)), vec, row], out_specs=[row, row, vec],
        out_shape=[jax.ShapeDtypeStruct((S, D), F32), jax.ShapeDtypeStruct((S, D), F32),
                   jax.ShapeDtypeStruct((1, D), F32)],
        compiler_params=_params("arbitrary"),
    )(o, proj, gain, dz)


def lower_bound_fwd(l0, l1, name):
    def body(a_ref, b_ref, o_ref):
        o_ref[...] = jax.nn.sigmoid(a_ref[...] - b_ref[...])

    return pl.pallas_call(body, name=name, out_shape=jax.ShapeDtypeStruct(l0.shape, F32))(l0, l1)


def lower_bound_bwd(l0, l1, dlb, name):
    def body(a_ref, b_ref, d_ref, o0_ref, o1_ref):
        s = jax.nn.sigmoid(a_ref[...] - b_ref[...])
        d0 = d_ref[...] * s * (1.0 - s)
        o0_ref[...] = d0
        o1_ref[...] = -d0

    sd = jax.ShapeDtypeStruct(l0.shape, F32)
    return pl.pallas_call(body, name=name, out_shape=[sd, sd])(l0, l1, dlb)


def _attn_tile(q, kp, kc, vp, vc, sink, first):
    W = ATT_WINDOW
    nt = (((1,), (1,)), ((), ()))
    qb = q.astype(BF16)
    scale = ATT_HEAD ** -0.5
    sp = lax.dot_general(qb, kp.astype(BF16), nt, preferred_element_type=F32) * scale
    sc = lax.dot_general(qb, kc.astype(BF16), nt, preferred_element_type=F32) * scale
    qi = lax.broadcasted_iota(jnp.int32, sp.shape, 0) & (W - 1)
    kj = lax.broadcasted_iota(jnp.int32, sp.shape, 1)
    sp = jnp.where((kj > qi) & jnp.logical_not(first), sp, NEG_INF)
    sc = jnp.where(kj <= qi, sc, NEG_INF)
    m = jnp.maximum(jnp.maximum(jnp.max(sp, axis=-1, keepdims=True), jnp.max(sc, axis=-1, keepdims=True)), sink)
    pp = jnp.exp(sp - m)
    pc = jnp.exp(sc - m)
    denom = jnp.sum(pp, axis=-1, keepdims=True) + jnp.sum(pc, axis=-1, keepdims=True) + jnp.exp(sink - m)
    out = jnp.dot((pp / denom).astype(BF16), vp.astype(BF16), preferred_element_type=F32)
    return out + jnp.dot((pc / denom).astype(BF16), vc.astype(BF16), preferred_element_type=F32)


def _attn_specs(G, W, Dh):
    q_spec = pl.BlockSpec((None, G, W, Dh), lambda j, n: (j, 0, n, 0))
    prev = pl.BlockSpec((None, W, Dh), lambda j, n: (j, jnp.maximum(n - 1, 0), 0))
    cur = pl.BlockSpec((None, W, Dh), lambda j, n: (j, n, 0))
    sink = pl.BlockSpec((None, G * W, 1), lambda j, n: (j, 0, 0))
    return q_spec, prev, cur, sink


def attn_fwd(q4, k3, v3, sink, name):
    NKV, G, S, Dh = q4.shape
    W = ATT_WINDOW

    def body(q_ref, kp_ref, kc_ref, vp_ref, vc_ref, s_ref, o_ref):
        first = pl.program_id(1) == 0
        out = _attn_tile(q_ref[...].reshape(G * W, Dh), kp_ref[...], kc_ref[...], vp_ref[...], vc_ref[...],
                         s_ref[...], first)
        o_ref[...] = out.reshape(G, W, Dh)

    q_spec, prev, cur, sk = _attn_specs(G, W, Dh)
    return pl.pallas_call(
        body, name=name, grid=(NKV, S // W), in_specs=[q_spec, prev, cur, prev, cur, sk], out_specs=q_spec,
        out_shape=jax.ShapeDtypeStruct(q4.shape, F32), compiler_params=_params("parallel", "parallel"),
    )(q4, k3, k3, v3, v3, sink)


def attn_bwd(q4, k3, v3, sink, do4, name, side=None):
    NKV, G, S, Dh = q4.shape
    W = ATT_WINDOW
    nb = S // W

    def body(q_ref, kp_ref, kc_ref, vp_ref, vc_ref, s_ref, do_ref, dq_ref, dkp_ref, dkc_ref, dvp_ref, dvc_ref,
             ds_ref):
        first = pl.program_id(1) == 0
        _, vjp = jax.vjp(functools.partial(_attn_tile, first=first), q_ref[...].reshape(G * W, Dh), kp_ref[...],
                         kc_ref[...], vp_ref[...], vc_ref[...], s_ref[...])
        dq, dkp, dkc, dvp, dvc, ds = vjp(do_ref[...].reshape(G * W, Dh))
        dq_ref[...] = dq.reshape(G, W, Dh)
        dkp_ref[...] = dkp
        dkc_ref[...] = dkc
        dvp_ref[...] = dvp
        dvc_ref[...] = dvc

        @pl.when(first)
        def _():
            ds_ref[...] = jnp.zeros_like(ds_ref)

        ds_ref[...] += jnp.sum(ds.reshape(G, W, 1), axis=1)

    q_spec, prev, cur, sk = _attn_specs(G, W, Dh)
    part = pl.BlockSpec((None, None, W, Dh), lambda j, n: (j, n, 0, 0))
    ps = jax.ShapeDtypeStruct((NKV, nb, W, Dh), F32)
    res = _call(body, name=name, grid=(NKV, nb), in_specs=[q_spec, prev, cur, prev, cur, sk, q_spec],
                out_specs=[q_spec, part, part, part, part, pl.BlockSpec((None, G, 1), lambda j, n: (j, 0, 0))],
                out_shape=[jax.ShapeDtypeStruct(q4.shape, F32), ps, ps, ps, ps,
                           jax.ShapeDtypeStruct((NKV, G, 1), F32)],
                args=[q4, k3, k3, v3, v3, sink, do4], semantics=("parallel", "arbitrary"), side=side)
    outs = res if side is None else res[0]
    ans = (outs[0], outs[1:5], outs[5])
    return ans if side is None else (ans, res[1])


def band_combine(cur, prev, name):
    NKV, nb, W, Dh = cur.shape

    def body(c_ref, p_ref, o_ref):
        keep = (pl.program_id(1) < nb - 1).astype(F32)
        o_ref[...] = c_ref[...] + keep * p_ref[...]

    return pl.pallas_call(
        body, name=name, grid=(NKV, nb),
        in_specs=[pl.BlockSpec((None, None, W, Dh), lambda j, n: (j, n, 0, 0)),
                  pl.BlockSpec((None, None, W, Dh), lambda j, n: (j, jnp.minimum(n + 1, nb - 1), 0, 0))],
        out_specs=pl.BlockSpec((None, W, Dh), lambda j, n: (j, n, 0)),
        out_shape=jax.ShapeDtypeStruct((NKV, nb * W, Dh), F32), compiler_params=_params("parallel", "parallel"),
    )(cur, prev)


def rope_tables(S):
    half = ROT_DIM // 2
    inv_freq = jnp.power(jnp.float32(ROPE_THETA), -jnp.arange(0, ROT_DIM, 2, dtype=F32) / ROT_DIM)
    ang = jnp.arange(S, dtype=F32)[:, None] * inv_freq[None, :]
    sin, cos = jnp.sin(ang), jnp.cos(ang)
    zeros = jnp.zeros((S, ATT_HEAD - ROT_DIM), F32)
    z8 = jnp.zeros((S, half), F32)
    cfull = jnp.concatenate([cos, cos, jnp.ones((S, ATT_HEAD - ROT_DIM), F32)], axis=1)
    s_next = jnp.concatenate([-sin, z8, zeros], axis=1)
    s_prev = jnp.concatenate([z8, sin, zeros], axis=1)
    two = lambda t: jnp.concatenate([t, t], axis=1)
    return two(cfull), two(s_next), two(s_prev)


def rope(x, tables, sign, name):
    S, Wd = x.shape
    tm = _row_tile(S)
    rep = Wd // LANES
    half = ROT_DIM // 2

    def body(x_ref, c_ref, sn_ref, sp_ref, o_ref):
        xv = x_ref[...]
        c = jnp.tile(c_ref[...], (1, rep))
        sn = jnp.tile(sn_ref[...], (1, rep))
        sp = jnp.tile(sp_ref[...], (1, rep))
        if sign > 0:
            nxt = pltpu.roll(xv, Wd - half, 1)
            prv = pltpu.roll(xv, half, 1)
            o_ref[...] = xv * c + nxt * sn + prv * sp
        else:
            o_ref[...] = xv * c + pltpu.roll(xv * sn, half, 1) + pltpu.roll(xv * sp, Wd - half, 1)

    row = pl.BlockSpec((tm, Wd), lambda i: (i, 0))
    tab = pl.BlockSpec((tm, LANES), lambda i: (i, 0))
    return pl.pallas_call(
        body, name=name, grid=(S // tm,), in_specs=[row, tab, tab, tab], out_specs=row,
        out_shape=jax.ShapeDtypeStruct((S, Wd), F32), compiler_params=_params("parallel"),
    )(x, *tables)


def cond_proj(c_all, w, bias, name):
    B, D = c_all.shape
    N = w.shape[1]
    tn = _tile(N, 512)

    def body(c_ref, w_ref, b_ref, o_ref):
        cv = c_ref[...]
        cs = (cv * jax.nn.sigmoid(cv)).astype(BF16)
        o_ref[...] = jnp.dot(cs, w_ref[...].astype(BF16), preferred_element_type=F32) + b_ref[...]

    return pl.pallas_call(
        body, name=name, grid=(N // tn,),
        in_specs=[pl.BlockSpec((B, D), lambda j: (0, 0)), pl.BlockSpec((D, tn), lambda j: (0, j)),
                  pl.BlockSpec((1, tn), lambda j: (0, j))],
        out_specs=pl.BlockSpec((B, tn), lambda j: (0, j)), out_shape=jax.ShapeDtypeStruct((B, N), F32),
        compiler_params=_params("parallel"),
    )(c_all, w, bias)


def cond_grad(c_all, dmod, name):
    B, D = c_all.shape
    N = dmod.shape[1]
    tn = _tile(N, 512)

    def body(c_ref, d_ref, o_ref):
        cv = c_ref[...]
        cs = (cv * jax.nn.sigmoid(cv)).astype(BF16)
        o_ref[...] = lax.dot_general(cs, d_ref[...].astype(BF16), (((0,), (0,)), ((), ())),
                                     preferred_element_type=F32)

    return pl.pallas_call(
        body, name=name, grid=(N // tn,),
        in_specs=[pl.BlockSpec((B, D), lambda j: (0, 0)), pl.BlockSpec((B, tn), lambda j: (0, j))],
        out_specs=pl.BlockSpec((D, tn), lambda j: (0, j)), out_shape=jax.ShapeDtypeStruct((D, N), F32),
        compiler_params=_params("parallel"),
    )(c_all, dmod)


def rowsum(g, name):
    B, N = g.shape
    tn = _tile(N, 8192)

    def body(g_ref, o_ref):
        acc = g_ref[0:1, :]
        for r in range(1, B):
            acc = acc + g_ref[r:r + 1, :]
        o_ref[...] = acc

    return pl.pallas_call(
        body, name=name, grid=(N // tn,), in_specs=[pl.BlockSpec((B, tn), lambda j: (0, j))],
        out_specs=pl.BlockSpec((1, tn), lambda j: (0, j)), out_shape=jax.ShapeDtypeStruct((1, N), F32),
        compiler_params=_params("parallel"),
    )(g)


def _adam_rows(R, C):
    limit = 3 << 19
    if R * C * 4 <= limit or R % 8:
        return R
    best = 8
    for t in range(8, R + 1, 8):
        if R % t == 0 and t * C * 4 <= limit:
            best = t
    return best


def adamw(w3, m3, v3, j, g3, name, col=(0, 1), into=None, side=None):
    n, R, C = w3.shape
    P = g3.shape[0]
    cp, ncol = col
    Cp = C // ncol
    tr = _adam_rows(R, Cp)

    def body(*refs):
        w_ref, m_ref, v_ref, g_ref = refs[:4]
        go_ref, d_ref, mo_ref, vo_ref = refs[-4:]
        g = g_ref[0].astype(F32)
        for p in range(1, P):
            g = g + g_ref[p].astype(F32)
        mn = ADAM_B1 * m_ref[...] + (1.0 - ADAM_B1) * g
        vn = ADAM_B2 * v_ref[...] + (1.0 - ADAM_B2) * jnp.square(g)
        m_hat = mn / (1.0 - ADAM_B1 ** ADAM_STEP)
        v_hat = vn / (1.0 - ADAM_B2 ** ADAM_STEP)
        go_ref[...] = g
        d_ref[...] = -ADAM_LR * (m_hat / (jnp.sqrt(v_hat) + ADAM_EPS) + ADAM_WD * w_ref[...])
        mo_ref[...] = mn
        vo_ref[...] = vn

    spec = pl.BlockSpec((None, tr, Cp), lambda i: (j, i, cp))
    g_spec = pl.BlockSpec((P, tr, Cp), lambda i: (0, i, 0))
    sd = jax.ShapeDtypeStruct((n, R, C), F32)
    in_specs, args, aliases = [spec, spec, spec, g_spec], [w3, m3, v3, g3], {}
    if side is not None:
        assert into is None
        return _call(body, name=name, grid=(R // tr,), in_specs=in_specs, out_specs=[spec] * 4, out_shape=[sd] * 4,
                     args=args, semantics=("parallel",), side=side)
    if into is not None:
        in_specs += [_ANY] * 4
        args += list(into)
        aliases = {4 + k: k for k in range(4)}
    return pl.pallas_call(
        body, name=name, grid=(R // tr,), in_specs=in_specs, out_specs=[spec] * 4, out_shape=[sd] * 4,
        input_output_aliases=aliases, compiler_params=_params("parallel"),
    )(*args)


def _place():
    return lax.axis_index("x"), lax.axis_index("y"), lax.axis_index("c")


def _slot(p):
    return 4 * p[0] + 2 * p[1] + p[2]


def gather_side(items):
    xs = [a for a, _ in items]
    n = len(xs)

    def copy(ins, outs, sems, t, k, block, to, from_input=False):
        dst = outs[t].at[_slot(block)]
        return pltpu.make_async_remote_copy(
            src_ref=ins[t] if from_input else dst, dst_ref=dst, send_sem=sems[0].at[t, k],
            recv_sem=sems[1].at[t, k], device_id=to, device_id_type=MESH)

    def peers():
        x, y, c = _place()
        return (x, y, c), (x, y, 1 - c), [(1 - x, y), (x, 1 - y), (1 - x, 1 - y)]

    def start(ins, outs, sems):
        me, sibling, chips = peers()
        c = me[2]
        for t in range(n):
            pltpu.make_async_copy(ins[t], outs[t].at[_slot(me)], sems[2].at[t]).start()
            copy(ins, outs, sems, t, 0, me, sibling, True).start()
            for j, chip in enumerate(chips):
                copy(ins, outs, sems, t, 1 + j, me, (*chip, c), True).start()

    def middle(ins, outs, sems):
        me, sibling, chips = peers()
        c = me[2]
        for t in range(n):
            for j, chip in enumerate(chips):
                copy(ins, outs, sems, t, 1 + j, (*chip, c), me).wait_recv()
                copy(ins, outs, sems, t, 4 + j, (*chip, c), sibling).start()

    def finish(ins, outs, sems):
        me, sibling, chips = peers()
        c = me[2]
        for t in range(n):
            copy(ins, outs, sems, t, 0, sibling, me).wait_recv()
            for j, chip in enumerate(chips):
                copy(ins, outs, sems, t, 4 + j, (*chip, 1 - c), me).wait_recv()
        for t in range(n):
            copy(ins, outs, sems, t, 0, me, sibling, True).wait_send()
            for j, chip in enumerate(chips):
                copy(ins, outs, sems, t, 1 + j, me, (*chip, c), True).wait_send()
                copy(ins, outs, sems, t, 4 + j, (*chip, c), sibling).wait_send()
            pltpu.make_async_copy(ins[t], outs[t].at[_slot(me)], sems[2].at[t]).wait()

    return _Side(xs, [jax.ShapeDtypeStruct((N_DEV,) + a.shape, a.dtype) for a in xs],
                 [pltpu.SemaphoreType.DMA((n, 7)), pltpu.SemaphoreType.DMA((n, 7)), pltpu.SemaphoreType.DMA((n,))],
                 start, finish, middle)


def scatter_side(items):
    gs = [g for g, _ in items]
    n = len(gs)

    def part_shape(g, part):
        _, R, C = g.shape
        if part is None:
            return R, C
        axis, _, cnt = part
        return (R // cnt, C) if axis == 0 else (R, C // cnt)

    def block(ref, slot, g, part):
        if part is None:
            return ref.at[slot]
        axis, idx, cnt = part
        R, C = part_shape(g, part)
        return ref.at[slot, pl.ds(idx * R, R)] if axis == 0 else ref.at[slot, :, pl.ds(idx * C, C)]

    def copies(ins, outs, sems):
        x, y, c = _place()
        me = (x, y, c)
        out = []
        for t, (g, part) in enumerate(items):
            out.append(pltpu.make_async_copy(block(ins[t], _slot(me), g, part), outs[t].at[_slot(me)], sems[2].at[t]))
            for r in range(1, N_DEV):
                peer = (1 - x if r & 4 else x, 1 - y if r & 2 else y, 1 - c if r & 1 else c)
                out.append(pltpu.make_async_remote_copy(
                    src_ref=block(ins[t], _slot(peer), g, part), dst_ref=outs[t].at[_slot(me)],
                    send_sem=sems[0].at[t, r - 1], recv_sem=sems[1].at[t, r - 1], device_id=peer, device_id_type=MESH))
        return out

    def start(ins, outs, sems):
        for cp in copies(ins, outs, sems):
            cp.start()

    def finish(ins, outs, sems):
        for cp in copies(ins, outs, sems):
            cp.wait()

    return _Side(gs, [jax.ShapeDtypeStruct((N_DEV,) + part_shape(g, part), g.dtype) for g, part in items],
                 [pltpu.SemaphoreType.DMA((n, 7)), pltpu.SemaphoreType.DMA((n, 7)), pltpu.SemaphoreType.DMA((n,))],
                 start, finish)


def exchange(side, name):
    def body(*refs):
        n_in, n_out = len(side.inputs), len(side.out_shapes)
        ins, outs, sems = refs[:n_in], refs[n_in:n_in + n_out], refs[n_in + n_out:]
        side.start(ins, outs, sems)
        if side.middle is not None:
            side.middle(ins, outs, sems)
        side.finish(ins, outs, sems)

    return pl.pallas_call(
        body, name=name, in_specs=[_ANY] * len(side.inputs), out_specs=[_ANY] * len(side.out_shapes),
        out_shape=side.out_shapes, scratch_shapes=side.sem_shapes,
    )(*side.inputs)


class _Part:
    def __init__(self, key, axis, index, count):
        self.key, self.axis, self.index, self.count = key, axis, index, count

    def __hash__(self):
        return hash((self.key, self.axis, self.index, self.count))

    def __eq__(self, other):
        return isinstance(other, _Part) and (self.key, self.axis, self.index, self.count) == (
            other.key, other.axis, other.index, other.count)


def _rows(key, p):
    return _Part(key, 0, p, 2)


def _cols(key, p):
    return _Part(key, 1, p, 2)


class _Plan:
    def __init__(self, plan, make_side):
        self.plan, self.make_side, self.source, self.got = plan, make_side, {}, {}

    def item(self, k):
        return (self.source[k.key], (k.axis, k.index, k.count)) if isinstance(k, _Part) else (self.source[k], None)

    def run(self, fn, *args, name, **kw):
        keys = self.plan.get(name)
        if not keys:
            return fn(*args, name=name, **kw)
        result, outs = fn(*args, name=name, side=self.make_side([self.item(k) for k in keys]), **kw)
        self.got.update(zip(keys, outs))
        return result


GATHER_PLAN = {
    "l0s0_in": [("out", 0, 0), "hout", "kv"],
    "l0s0_out": ["hin"],
    "l0s1_proj": [("out", 0, 1), "q"],
    "l0s1_scan": [("in", 0, 1)],
    "l0s2_in": [("in", 1, 0), "o"],
    "l0s2_out": [("out", 1, 0)],
    "l1s0_in": [("in", 1, 1)],
    "l1s0_out": [("out", 1, 1)],
}
GATHER_FIRST = [("in", 0, 0)]
SCATTER_PLAN = {
    "l1s2_dact": [_cols(("out", 1, 1), 0)],
    "l1s2_dwin0": [_cols(("out", 1, 1), 1)],
    "l1s2_dwin1": [_rows(("in", 1, 1, 0), 0)],
    "l1s2_du": [_rows(("in", 1, 1, 0), 1), _rows(("in", 1, 1, 1), 0)],
    "l1s1_dattn": [_rows(("in", 1, 1, 1), 1), "o"],
    "l1s0_dwout": ["q"],
    "l1s0_dact": [_cols(("out", 1, 0), 0)],
    "l1s0_dwin0": [_cols(("out", 1, 0), 1)],
    "l1s0_dwin1": [_rows(("in", 1, 0, 0), 0)],
    "l1s0_du": [_rows(("in", 1, 0, 0), 1), _rows(("in", 1, 0, 1), 0)],
    "l0s2_dwout": [_rows(("in", 1, 0, 1), 1)],
    "l0s2_dact": [_cols(("out", 0, 1), 0)],
    "l0s2_dwin0": [_cols(("out", 0, 1), 1)],
    "l0s2_dwin1": [_rows(("in", 0, 1, 0), 0)],
    "l0s2_du": [_rows(("in", 0, 1, 0), 1), _rows(("in", 0, 1, 1), 0)],
    "l0s1_dwout": ["kv"],
    "l0s1_dscan": [_rows(("in", 0, 1, 1), 1), "hout"],
    "l0s1_dwin1": [_rows(("hin", 0), 0)],
    "l0s1_du": [_rows(("hin", 0), 1), _rows(("hin", 1), 0)],
    "l0s0_dwout": [_rows(("hin", 1), 1)],
    "l0s0_dact": [_cols(("out", 0, 0), 0)],
    "l0s0_dwin0": [_cols(("out", 0, 0), 1)],
    "l0s0_dwin1": [_rows(("in", 0, 0, 0), 0)],
    "l0s0_du": [_rows(("in", 0, 0, 0), 1), _rows(("in", 0, 0, 1), 0)],
    "adam_w_ada0": [_rows(("in", 0, 0, 1), 1)],
}
GRAD_TM, GRAD_TK = 1024, 1024


def _ffn_fwd(gp, W, h, gate, u, l, i, tag):
    ab4, h3, h3t = gp.run(ffn_in_act, u, W[("in", l, i)], name=tag + "_in")
    w_out = W[("out", l, i)]
    J = h3.shape[0]
    y, h_new = gp.run(mm_nn, h3, w_out.reshape(J, 1, -1, w_out.shape[-1]), F32, name=tag + "_out", natural=True,
                      resid=(h, gate, 0.5))
    return y, h_new, (ab4, h3t)


def _ffn_bwd(sp, W, dy, ut, ab4, h3t, l, i, tag):
    w_in, w_out = W[("in", l, i)], W[("out", l, i)]
    J, nb, S = h3t.shape
    D = w_out.shape[-1]
    dw_out = sp.run(mm_nn, h3t.reshape(1, J * nb, S), dy[None, None], BF16, name=tag + "_dwout", natural=True,
                    tm=w_out.shape[1], tk=GRAD_TK)
    sp.source[("out", l, i)] = dw_out.reshape(w_out.shape)
    dab3 = sp.run(ffn_dact, dy, w_out.reshape(J, nb, D), ab4, name=tag + "_dact")
    for hf in range(2):
        sp.source[("in", l, i, hf)] = sp.run(mm_nn, ut.reshape(2, D // 2, S), dab3[None], BF16, name=f"{tag}_dwin{hf}",
                                             a_sel=hf, tm=GRAD_TM, tk=GRAD_TK)
    return sp.run(mm_nt, dab3, w_in[:, None], F32, name=tag + "_du", natural=True, kgroup=4)


def kernel(x, c, norm_gain, w_ada, b_ada, w_ffn_in, w_ffn_out, w_hgrn_in, hgrn_lb_logits, hgrn_head_gain, w_hgrn_out, kv_gain, w_ada_kv, b_ada_kv, w_kv, b_kv, w_q, b_q, attn_sinks, w_attn_out, final_gain, loss_target, m_norm_gain, m_w_ada, m_b_ada, m_w_ffn_in, m_w_ffn_out, m_w_hgrn_in, m_hgrn_lb_logits, m_hgrn_head_gain, m_w_hgrn_out, m_kv_gain, m_w_ada_kv, m_b_ada_kv, m_w_kv, m_b_kv, m_w_q, m_b_q, m_attn_sinks, m_w_attn_out, m_final_gain, v_norm_gain, v_w_ada, v_b_ada, v_w_ffn_in, v_w_ffn_out, v_w_hgrn_in, v_hgrn_lb_logits, v_hgrn_head_gain, v_w_hgrn_out, v_kv_gain, v_w_ada_kv, v_b_ada_kv, v_w_kv, v_b_kv, v_w_q, v_b_q, v_attn_sinks, v_w_attn_out, v_final_gain):
    xi, yi, ci = _place()
    me = 4 * xi + 2 * yi + ci
    _, S, D = x.shape
    L = norm_gain.shape[0]
    dsh = D // N_DEV
    ada_n = w_ada.shape[2]
    kv_n = w_ada_kv.shape[1]
    NQ = D // ATT_HEAD
    NKV = NQ // ATT_GROUP
    kvd = NKV * ATT_HEAD
    h0 = x[0]
    target = loss_target[0]

    def my_cols(a, n):
        return lax.dynamic_slice_in_dim(a, me * n, n, axis=a.ndim - 1)

    gp = _Plan(GATHER_PLAN, gather_side)
    bf = lambda a: a.astype(BF16)
    for l in range(L):
        for i in range(2):
            gp.source[("in", l, i)] = bf(w_ffn_in[l, i])
            gp.source[("out", l, i)] = bf(w_ffn_out[l, i])
    gp.source.update(hin=bf(w_hgrn_in[0]), hout=bf(w_hgrn_out[0]), kv=bf(w_kv), q=bf(w_q[0]), o=bf(w_attn_out[0]))
    W = gp.got
    W.update(zip(GATHER_FIRST, exchange(gather_side([(gp.source[k], None) for k in GATHER_FIRST]), "gather_first")))
    full = lambda w: w.reshape(1, 1, -1, w.shape[-1])

    lb_sh = lower_bound_fwd(hgrn_lb_logits[0:1], hgrn_lb_logits[1:2], "lb_fwd")
    small = jnp.concatenate([c, norm_gain.reshape(1, L * 3 * dsh), hgrn_head_gain, lb_sh], axis=1)
    (g1,) = exchange(gather_side([(small, None)]), "gather_cond")
    g1 = g1.reshape(N_DEV, -1)
    c_all = g1[:, :D]
    gains = g1[:, D:D + L * 3 * dsh].reshape(N_DEV, L * 3, dsh).transpose(1, 0, 2).reshape(L, 3, 1, D)
    head_gain = g1[:, D + L * 3 * dsh:D + (L * 3 + 1) * dsh].reshape(1, D)
    lb0 = g1[:, D + (L * 3 + 1) * dsh:].reshape(1, D)

    parts = [cond_proj(c_all, w_ada[l], my_cols(b_ada[l:l + 1], ada_n), f"mod{l}") for l in range(L)]
    parts.append(cond_proj(c_all, w_ada_kv, my_cols(b_ada_kv[None], kv_n), "mod_kv"))
    (g2,) = exchange(gather_side([(jnp.concatenate(parts, axis=1), None)]), "gather_mod")
    mine2 = lax.dynamic_index_in_dim(g2, me, axis=1, keepdims=False)
    mod = [mine2[:, l * ada_n:(l + 1) * ada_n].reshape(3, 3, 1, D) for l in range(L)]
    mod_kv = mine2[:, L * ada_n:].reshape(2, 1, D)

    tables = rope_tables(S)
    sink_col = jnp.broadcast_to(attn_sinks.reshape(NKV, ATT_GROUP, 1, 1), (NKV, ATT_GROUP, ATT_WINDOW, 1))
    sink_col = sink_col.reshape(NKV, ATT_GROUP * ATT_WINDOW, 1)

    def to_heads(t, n):
        return t.reshape(S, n, ATT_HEAD).transpose(1, 0, 2)

    def from_heads(t):
        return t.transpose(1, 0, 2).reshape(S, -1)

    h = h0
    saved = {}
    for l in range(L):
        for s in (0, 1, 2):
            tag = f"l{l}s{s}"
            shift, scale, gate = mod[l][s, 0], mod[l][s, 1], mod[l][s, 2]
            u, ut = adaln_fwd(h, gains[l, s], shift, scale, tag + "_norm")
            if s != 1:
                y, h_new, res = _ffn_fwd(gp, W, h, gate, u, l, s // 2, tag)
            elif l == 0:
                proj = gp.run(mm_nn, u[None], W["hin"][None], F32, name=tag + "_proj", natural=True)
                o, states = gp.run(hgrn_scan_fwd, proj, lb0, name=tag + "_scan")
                z, zt = hgrn_post_fwd(o, proj, head_gain, tag + "_post")
                y, h_new = mm_nn(z[None], full(W["hout"]), F32, tag + "_out", natural=True, resid=(h, gate, 1.0))
                res = (proj, o, states, zt)
            else:
                q = mm_nn(u[None], full(W["q"]), F32, tag + "_q", natural=True, bias=b_q)
                q4 = to_heads(rope(q, tables, 1, tag + "_rope"), NQ).reshape(NKV, ATT_GROUP, S, ATT_HEAD)
                att4 = attn_fwd(q4, k3, v3, sink_col, tag + "_attn")
                att = from_heads(att4.reshape(NQ, S, ATT_HEAD))
                att_t = att4.reshape(NQ, S, ATT_HEAD).transpose(0, 2, 1).reshape(D, S)
                y, h_new = mm_nn(att[None], full(W["o"]), F32, tag + "_out", natural=True, resid=(h, gate, 1.0))
                res = (q4, att_t)
            saved[(l, s)] = (h, ut, y, res)
            h = h_new
        if l == 0:
            h_kv = h
            u_kv, u_kv_t = adaln_fwd(h, kv_gain[None], mod_kv[0], mod_kv[1], "kv_norm")
            kvp = mm_nn(u_kv[None], full(W["kv"]), F32, "kv_proj", natural=True, bias=b_kv[None])
            k3 = to_heads(rope(kvp[:, :kvd], tables, 1, "kv_rope"), NKV)
            v3 = to_heads(kvp[:, kvd:], NKV)

    def branch(l, s):
        return saved[(l, s)][2], mod[l][s, 2], 0.5 if s != 1 else 1.0

    loss_row, dh, d_final_gain, dy, d_gate = final_loss_grad(h, final_gain[None], target, "final", branch(L - 1, 2))
    loss = lax.psum(loss_row[0, 0], ("x", "y", "c"))

    sp = _Plan(SCATTER_PLAN, scatter_side)

    def grad_w(a_t, b, name):
        return sp.run(mm_nn, a_t[None], b[None, None], BF16, name=name, natural=True, tm=GRAD_TM, tk=GRAD_TK)

    d_mod = [[None] * 3 for _ in range(L)]
    d_gain = [[None] * 3 for _ in range(L)]
    for l in reversed(range(L)):
        if l == 0:
            dkv = jnp.concatenate([rope(from_heads(dk3), tables, -1, "kv_drope"), from_heads(dv3)], axis=1)
            sp.source["kv"] = grad_w(u_kv_t, dkv, "kv_dw").reshape(W["kv"].shape)
            db_kv = colsum(dkv, "kv_db")
            du_kv = mm_nt(dkv[None], full(W["kv"]), F32, "kv_du", natural=True)
            dh, d_kv_gain, d_kv_shift, d_kv_scale, dy, d_gate = adaln_bwd(
                h_kv, kv_gain[None], mod_kv[0], mod_kv[1], du_kv, dh, "kv_dnorm", nxt=branch(0, 2))
        for s in (2, 1, 0):
            tag = f"l{l}s{s}"
            shift, scale, gate = mod[l][s, 0], mod[l][s, 1], mod[l][s, 2]
            h_in, ut, y, res = saved[(l, s)]
            if s != 1:
                du = _ffn_bwd(sp, W, dy, ut, res[0], res[1], l, s // 2, tag)
            elif l == 0:
                proj, o, states, zt = res
                sp.source["hout"] = grad_w(zt, dy, tag + "_dwout").reshape(W["hout"].shape)
                dz = mm_nt(dy[None], full(W["hout"]), F32, tag + "_dz", natural=True)
                do, dg, d_head_gain = hgrn_post_bwd(o, proj, head_gain, dz, tag + "_dpost")
                dproj, d_lb0 = sp.run(hgrn_scan_bwd, proj, lb0, states, do, dg, name=tag + "_dscan")
                for hf in range(2):
                    sp.source[("hin", hf)] = sp.run(mm_nn, ut.reshape(2, D // 2, S), dproj, BF16, name=f"{tag}_dwin{hf}",
                                                    b_natural=True, jn=N_DEV, a_sel=hf, tm=GRAD_TM, tk=GRAD_TK)
                du = sp.run(mm_nt, dproj, W["hin"][:, None], F32, name=tag + "_du", natural=True, a_natural=True)
            else:
                q4, att_t = res
                sp.source["o"] = grad_w(att_t, dy, tag + "_dwout").reshape(W["o"].shape)
                datt = mm_nt(dy[None], full(W["o"]), F32, tag + "_datt", natural=True)
                datt4 = to_heads(datt, NQ).reshape(NKV, ATT_GROUP, S, ATT_HEAD)
                dq4, (dkp, dkc, dvp, dvc), d_sink = sp.run(attn_bwd, q4, k3, v3, sink_col, datt4, name=tag + "_dattn")
                dk3 = band_combine(dkc, dkp, tag + "_dk")
                dv3 = band_combine(dvc, dvp, tag + "_dv")
                dq = rope(from_heads(dq4.reshape(NQ, S, ATT_HEAD)), tables, -1, tag + "_drope")
                sp.source["q"] = grad_w(ut, dq, tag + "_dwq").reshape(W["q"].shape)
                db_q = colsum(dq, tag + "_dbq")
                du = mm_nt(dq[None], full(W["q"]), F32, tag + "_du", natural=True)
            outs = adaln_bwd(h_in, gains[l, s], shift, scale, du, dh, tag + "_dnorm",
                             nxt=branch(l, s - 1) if s > 0 else None)
            dh, d_gain[l][s], dsh_, dsc_ = outs[:4]
            d_mod[l][s] = jnp.concatenate([dsh_, dsc_, d_gate], axis=1)
            if s > 0:
                dy, d_gate = outs[4:]
    grad_x = dh[None]
    G = sp.got

    pad = lambda a, n: jnp.pad(a, ((0, 0), (0, n - a.shape[1])))
    pieces = [jnp.concatenate(d_mod[l], axis=1) for l in range(L)]
    pieces += [d_kv_shift, d_kv_scale]
    pieces += [d_gain[l][s] for l in range(L) for s in range(3)]
    pieces += [d_head_gain, d_lb0, d_kv_gain, db_kv, db_q, pad(d_sink.reshape(1, NQ), LANES), d_final_gain]
    (g3,) = exchange(gather_side([(jnp.concatenate(pieces, axis=1), None)]), "gather_small_grads")
    g3 = g3.reshape(N_DEV, -1)
    tot = rowsum(g3, "sum_small_grads")
    offs = [0]
    for p in pieces:
        offs.append(offs[-1] + p.shape[1])
    seg = lambda k: tot[:, offs[k]:offs[k + 1]]
    k0 = 0
    g_b_ada = jnp.concatenate([seg(l) for l in range(L)], axis=0)
    k0 += L
    g_b_ada_kv = jnp.concatenate([seg(k0), seg(k0 + 1)], axis=1)
    k0 += 2
    g_norm_gain = jnp.concatenate([my_cols(seg(k0 + j), dsh) for j in range(3 * L)], axis=0)
    k0 += 3 * L
    g_head_gain = my_cols(seg(k0), dsh)
    d_lb_sh = my_cols(seg(k0 + 1), dsh)
    g_kv_gain = seg(k0 + 2)
    g_b_kv = seg(k0 + 3)
    g_b_q = seg(k0 + 4)
    g_sinks = seg(k0 + 5)[:, :NQ]
    g_final_gain = seg(k0 + 6)
    dl0, dl1 = lower_bound_bwd(hgrn_lb_logits[0:1], hgrn_lb_logits[1:2], d_lb_sh, "lb_bwd")
    g_lb_logits = jnp.concatenate([dl0, dl1], axis=0)

    g_w_ada = jnp.stack([cond_grad(c_all, lax.dynamic_slice_in_dim(g3, offs[l] + me * ada_n, ada_n, axis=1),
                                   f"dw_ada{l}") for l in range(L)])
    g_w_ada_kv = cond_grad(c_all, lax.dynamic_slice_in_dim(g3, offs[L] + me * kv_n, kv_n, axis=1), "dw_ada_kv")

    def update(w, m, v, grads, name, ncol=1):
        n = len(grads) // ncol
        width = w.shape[-1]
        three = lambda a: a.reshape((n, -1, width))
        outs = None
        for k, g in enumerate(grads):
            j, cp = divmod(k, ncol)
            g = g.reshape((g.shape[0], -1, width // ncol))
            if len(grads) == 1:
                outs = sp.run(adamw, three(w), three(m), three(v), j, g, name=f"{name}{k}")
            else:
                outs = adamw(three(w), three(m), three(v), j, g, f"{name}{k}", col=(cp, ncol), into=outs)
        return [o.reshape(w.shape) for o in outs]

    def one(w, m, v, g, name):
        return update(w, m, v, [g.reshape((1, -1, w.shape[-1]))], name)

    res = {}
    res["norm_gain"] = one(norm_gain, m_norm_gain, v_norm_gain, g_norm_gain, "adam_norm_gain")
    res["w_ada"] = one(w_ada, m_w_ada, v_w_ada, g_w_ada, "adam_w_ada")
    res["b_ada"] = one(b_ada, m_b_ada, v_b_ada, g_b_ada, "adam_b_ada")
    sent = {k.key if isinstance(k, _Part) else k for k in G}
    rest = [k for k in sp.source if k not in sent]
    if rest:
        G.update(zip(rest, exchange(scatter_side([(sp.source[k], None) for k in rest]), "scatter_rest")))
    g_ffn_in = [G[_rows(("in", l, i, hf), p)] for l in range(L) for i in range(2) for hf in range(2) for p in range(2)]
    g_ffn_out = [G[_cols(("out", l, i), p)] for l in range(L) for i in range(2) for p in range(2)]
    g_hgrn_in = [G[_rows(("hin", hf), p)] for hf in range(2) for p in range(2)]
    res["w_ffn_in"] = update(w_ffn_in, m_w_ffn_in, v_w_ffn_in, g_ffn_in, "adam_w_ffn_in")
    res["w_ffn_out"] = update(w_ffn_out, m_w_ffn_out, v_w_ffn_out, g_ffn_out, "adam_w_ffn_out", ncol=2)
    res["w_hgrn_in"] = update(w_hgrn_in, m_w_hgrn_in, v_w_hgrn_in, g_hgrn_in, "adam_w_hgrn_in")
    res["hgrn_lb_logits"] = one(hgrn_lb_logits, m_hgrn_lb_logits, v_hgrn_lb_logits, g_lb_logits, "adam_lb")
    res["hgrn_head_gain"] = one(hgrn_head_gain, m_hgrn_head_gain, v_hgrn_head_gain, g_head_gain, "adam_head_gain")
    res["w_hgrn_out"] = update(w_hgrn_out, m_w_hgrn_out, v_w_hgrn_out, [G["hout"]], "adam_w_hgrn_out")
    res["kv_gain"] = one(kv_gain, m_kv_gain, v_kv_gain, g_kv_gain, "adam_kv_gain")
    res["w_ada_kv"] = one(w_ada_kv, m_w_ada_kv, v_w_ada_kv, g_w_ada_kv, "adam_w_ada_kv")
    res["b_ada_kv"] = one(b_ada_kv, m_b_ada_kv, v_b_ada_kv, g_b_ada_kv, "adam_b_ada_kv")
    res["w_kv"] = update(w_kv, m_w_kv, v_w_kv, [G["kv"]], "adam_w_kv")
    res["b_kv"] = one(b_kv, m_b_kv, v_b_kv, g_b_kv, "adam_b_kv")
    res["w_q"] = update(w_q, m_w_q, v_w_q, [G["q"]], "adam_w_q")
    res["b_q"] = one(b_q, m_b_q, v_b_q, g_b_q, "adam_b_q")
    res["attn_sinks"] = one(attn_sinks, m_attn_sinks, v_attn_sinks, g_sinks, "adam_sinks")
    res["w_attn_out"] = update(w_attn_out, m_w_attn_out, v_w_attn_out, [G["o"]], "adam_w_attn_out")
    res["final_gain"] = one(final_gain, m_final_gain, v_final_gain, g_final_gain, "adam_final_gain")

    names = ["norm_gain", "w_ada", "b_ada", "w_ffn_in", "w_ffn_out", "w_hgrn_in", "hgrn_lb_logits", "hgrn_head_gain",
             "w_hgrn_out", "kv_gain", "w_ada_kv", "b_ada_kv", "w_kv", "b_kv", "w_q", "b_q", "attn_sinks", "w_attn_out",
             "final_gain"]
    return (loss, grad_x, *[res[n][0] for n in names], *[res[n][1] for n in names], *[res[n][2] for n in names],
            *[res[n][3] for n in names])
```

```python
import functools

import jax
import jax.numpy as jnp
from jax import lax
from jax.experimental import pallas as pl
from jax.experimental.pallas import tpu as pltpu

F32 = jnp.float32
BF16 = jnp.bfloat16
MESH = pl.DeviceIdType.MESH

N_DEV = 8
V7X_VMEM_LIMIT_BYTES = 56 * 1024 * 1024
LANES = 128

NORM_EPS = 1e-6
NEG_INF = -1e30
HGRN_CHUNK = 32
HGRN_HEAD = 128
ATT_HEAD = 64
ATT_WINDOW = 128
ATT_GROUP = 8
ROT_DIM = 16
ROPE_THETA = 500000.0

ADAM_LR = 0.001
ADAM_B1 = 0.9
ADAM_B2 = 0.999
ADAM_EPS = 1e-08
ADAM_WD = 0.01
ADAM_STEP = 10


def _params(*sem):
    return pltpu.CompilerParams(dimension_semantics=sem, vmem_limit_bytes=V7X_VMEM_LIMIT_BYTES)


def _tile(n, pref, unit=LANES):
    t = (min(n, pref) // unit) * unit
    while t >= unit:
        if n % t == 0:
            return t
        t -= unit
    return n


_ANY = pl.BlockSpec(memory_space=pl.ANY)


class _Side:
    def __init__(self, inputs, out_shapes, sem_shapes, start, finish):
        self.inputs, self.out_shapes, self.sem_shapes = list(inputs), list(out_shapes), list(sem_shapes)
        self.start, self.finish = start, finish


def _call(body, *, name, grid, in_specs, out_specs, out_shape, args, semantics, scratch_shapes=(), side=None):
    in_specs, out_specs, out_shape = list(in_specs), list(out_specs), list(out_shape)
    scratch_shapes = list(scratch_shapes)
    if side is None:
        outs = pl.pallas_call(
            body, name=name, grid=grid, in_specs=in_specs, out_specs=out_specs, out_shape=out_shape,
            scratch_shapes=scratch_shapes, compiler_params=_params(*semantics))(*args)
        return list(outs)
    n_in, n_out, n_scr = len(in_specs), len(out_specs), len(scratch_shapes)
    s_in, s_out = len(side.inputs), len(side.out_shapes)

    def carried(*refs):
        ins, refs = refs[:n_in], refs[n_in:]
        side_ins, refs = refs[:s_in], refs[s_in:]
        outs, refs = refs[:n_out], refs[n_out:]
        side_outs, refs = refs[:s_out], refs[s_out:]
        scratch, sems = refs[:n_scr], refs[n_scr:]
        step, steps = pl.program_id(0), grid[0]
        for d in range(1, len(grid)):
            step, steps = step * grid[d] + pl.program_id(d), steps * grid[d]

        @pl.when(step == 0)
        def _():
            side.start(side_ins, side_outs, sems)

        body(*ins, *outs, *scratch)

        @pl.when(step == steps - 1)
        def _():
            side.finish(side_ins, side_outs, sems)

    outs = pl.pallas_call(
        carried, name=name, grid=grid, in_specs=in_specs + [_ANY] * s_in, out_specs=out_specs + [_ANY] * s_out,
        out_shape=out_shape + side.out_shapes, scratch_shapes=scratch_shapes + side.sem_shapes,
        compiler_params=_params(*(["arbitrary"] * len(grid))))(*args, *side.inputs)
    return list(outs[:n_out]), list(outs[n_out:])


def _accumulate(prod, o_ref, acc_ref, k, nk):
    if nk == 1:
        o_ref[...] = prod.astype(o_ref.dtype)
        return

    @pl.when(k == 0)
    def _():
        acc_ref[...] = prod

    @pl.when(k > 0)
    def _():
        acc_ref[...] += prod

    @pl.when(k == nk - 1)
    def _():
        o_ref[...] = acc_ref[...].astype(o_ref.dtype)


def mm_nn(a3, b4, out_dtype, name, natural=False, a_natural=False, b_natural=False, jn=None, a_sel=None, bias=None,
          resid=None, tm=512, tk=None, kgroup=1, side=None):
    if b_natural:
        JK, JN, kb = 1, jn, b4.shape[0]
        nb = b4.shape[1] // JN
    else:
        JK, JN, kb, nb = b4.shape
    M = a3.shape[0] if a_natural else a3.shape[1]
    tm = _tile(M, tm, 16)
    tn = _tile(nb, 1024)
    tk = kb if tk is None else _tile(kb, tk)
    ntn, nkt = nb // tn, kb // tk
    nk = JK * nkt // kgroup

    def body(*refs):
        a_ref, b_ref = refs[:2]
        acc_ref = refs[-1]
        if kgroup == 1:
            prod = jnp.dot(a_ref[...].astype(BF16), b_ref[...].astype(BF16), preferred_element_type=F32)
        else:
            prod = jnp.dot(a_ref[0].astype(BF16), b_ref[0].astype(BF16), preferred_element_type=F32)
            for g in range(1, kgroup):
                prod += jnp.dot(a_ref[g].astype(BF16), b_ref[g].astype(BF16), preferred_element_type=F32)
        if bias is not None:
            prod = prod + refs[2][...]
        if resid is None:
            _accumulate(prod, refs[-2], acc_ref, pl.program_id(2), nk)
            return
        h_ref, g_ref, y_ref, o_ref = refs[2:6]
        k = pl.program_id(2)

        @pl.when(k == 0)
        def _():
            acc_ref[...] = prod

        @pl.when(k > 0)
        def _():
            acc_ref[...] += prod

        @pl.when(k == nk - 1)
        def _():
            y = acc_ref[...]
            y_ref[...] = y
            o_ref[...] = h_ref[...] + (resid[2] * g_ref[...]) * y

    if kgroup > 1:
        assert JN == 1 and nkt == 1 and not (a_natural or b_natural) and a_sel is None
        a3 = a3.reshape(nk, kgroup, M, kb)
        b4 = b4.reshape(nk, kgroup, kb, nb)
        a_spec = pl.BlockSpec((None, kgroup, tm, kb), lambda j, i, k: (k, 0, i, 0))
    elif a_natural:
        a_spec = pl.BlockSpec((tm, tk), lambda j, i, k: (i, k))
    elif a_sel is not None:
        a_spec = pl.BlockSpec((None, tm, tk), lambda j, i, k: (a_sel, i, k))
    else:
        a_spec = pl.BlockSpec((None, tm, tk), lambda j, i, k: (k // nkt, i, k % nkt))
    if kgroup > 1:
        b_spec = pl.BlockSpec((None, kgroup, kb, tn), lambda j, i, k: (k, 0, 0, j))
    elif b_natural:
        b_spec = pl.BlockSpec((tk, tn), lambda j, i, k: (k, j))
    else:
        b_spec = pl.BlockSpec((None, None, tk, tn), lambda j, i, k: (k // nkt, j // ntn, k % nkt, j % ntn))
    in_specs = [a_spec, b_spec]
    args = [a3, b4]
    if bias is not None:
        assert natural and nk == 1
        in_specs.append(pl.BlockSpec((1, tn), lambda j, i, k: (0, j)))
        args.append(bias)
    if natural:
        out_shape = jax.ShapeDtypeStruct((M, JN * nb), out_dtype)
        o_spec = pl.BlockSpec((tm, tn), lambda j, i, k: (i, j))
    else:
        out_shape = jax.ShapeDtypeStruct((JN, M, nb), out_dtype)
        o_spec = pl.BlockSpec((None, tm, tn), lambda j, i, k: (j // ntn, i, j % ntn))
    out_specs, out_shapes = [o_spec], [out_shape]
    if resid is not None:
        assert natural and bias is None and out_dtype == F32
        in_specs += [o_spec, pl.BlockSpec((1, tn), lambda j, i, k: (0, j))]
        args += [resid[0], resid[1]]
        out_specs, out_shapes = [o_spec, o_spec], [out_shape, out_shape]
    res = _call(body, name=name, grid=(JN * ntn, M // tm, nk), in_specs=in_specs, out_specs=out_specs,
                out_shape=out_shapes, scratch_shapes=[pltpu.VMEM((tm, tn), F32)], args=args,
                semantics=("parallel", "parallel", "arbitrary"), side=side)
    outs = res if side is None else res[0]
    ans = outs[0] if resid is None else (outs[0], outs[1])
    return ans if side is None else (ans, res[1])


def mm_nt(a3, b4, out_dtype, name, natural=False, a_natural=False, kgroup=1, side=None):
    JK, JN, nb, kb = b4.shape
    M = a3.shape[0] if a_natural else a3.shape[1]
    tm = min(M, 512)
    tn = _tile(nb, 1024)
    ntn = nb // tn
    nk = JK // kgroup

    def body(a_ref, b_ref, o_ref, acc_ref):
        nt = (((1,), (1,)), ((), ()))
        if kgroup == 1:
            prod = lax.dot_general(a_ref[...].astype(BF16), b_ref[...].astype(BF16), nt, preferred_element_type=F32)
        else:
            prod = lax.dot_general(a_ref[0].astype(BF16), b_ref[0].astype(BF16), nt, preferred_element_type=F32)
            for g in range(1, kgroup):
                prod += lax.dot_general(a_ref[g].astype(BF16), b_ref[g].astype(BF16), nt, preferred_element_type=F32)
        _accumulate(prod, o_ref, acc_ref, pl.program_id(2), nk)

    if kgroup > 1:
        assert JN == 1 and not a_natural
        a3 = a3.reshape(nk, kgroup, M, kb)
        b4 = b4.reshape(nk, kgroup, nb, kb)
        a_spec = pl.BlockSpec((None, kgroup, tm, kb), lambda j, i, k: (k, 0, i, 0))
        b_spec = pl.BlockSpec((None, kgroup, tn, kb), lambda j, i, k: (k, 0, j, 0))
    elif a_natural:
        a_spec = pl.BlockSpec((tm, kb), lambda j, i, k: (i, k))
        b_spec = pl.BlockSpec((None, None, tn, kb), lambda j, i, k: (k, j // ntn, j % ntn, 0))
    else:
        a_spec = pl.BlockSpec((None, tm, kb), lambda j, i, k: (k, i, 0))
        b_spec = pl.BlockSpec((None, None, tn, kb), lambda j, i, k: (k, j // ntn, j % ntn, 0))
    if natural:
        out_shape = jax.ShapeDtypeStruct((M, JN * nb), out_dtype)
        o_spec = pl.BlockSpec((tm, tn), lambda j, i, k: (i, j))
    else:
        out_shape = jax.ShapeDtypeStruct((JN, M, nb), out_dtype)
        o_spec = pl.BlockSpec((None, tm, tn), lambda j, i, k: (j // ntn, i, j % ntn))
    res = _call(body, name=name, grid=(JN * ntn, M // tm, nk), in_specs=[a_spec, b_spec], out_specs=[o_spec],
                out_shape=[out_shape], scratch_shapes=[pltpu.VMEM((tm, tn), F32)], args=[a3, b4],
                semantics=("parallel", "parallel", "arbitrary"), side=side)
    return res[0] if side is None else (res[0][0], res[1])


def _row_tile(S):
    return min(S, 256)


def _adaln(h, gain, shift, scale):
    y = h * lax.rsqrt(jnp.mean(h * h, axis=-1, keepdims=True) + NORM_EPS) * gain
    return y * (1.0 + scale) + shift


def adaln_fwd(h, gain, shift, scale, name):
    S, D = h.shape
    tm = _row_tile(S)

    def body(h_ref, g_ref, sh_ref, sc_ref, u_ref, ut_ref):
        u = _adaln(h_ref[...], g_ref[...], sh_ref[...], sc_ref[...])
        u_ref[...] = u.astype(BF16)
        ut_ref[...] = u.T.astype(BF16)

    row = pl.BlockSpec((tm, D), lambda i: (i, 0))
    vec = pl.BlockSpec((1, D), lambda i: (0, 0))
    return pl.pallas_call(
        body, name=name, grid=(S // tm,), in_specs=[row, vec, vec, vec],
        out_specs=[row, pl.BlockSpec((D, tm), lambda i: (0, i))],
        out_shape=[jax.ShapeDtypeStruct((S, D), BF16), jax.ShapeDtypeStruct((D, S), BF16)],
        compiler_params=_params("parallel"),
    )(h, gain, shift, scale)


def _gate_specs(S, D, tm, nxt):
    if nxt is None:
        return [], [], [], []
    row = pl.BlockSpec((tm, D), lambda i: (i, 0))
    vec = pl.BlockSpec((1, D), lambda i: (0, 0))
    return ([row, vec], [nxt[0], nxt[1]], [row, vec],
            [jax.ShapeDtypeStruct((S, D), BF16), jax.ShapeDtypeStruct((1, D), F32)])


def _gate_grads(dh, nxt, y_ref, g_ref, dy_ref, dg_ref):
    coef = nxt[2]
    dy_ref[...] = ((coef * g_ref[...]) * dh).astype(BF16)

    @pl.when(pl.program_id(0) == 0)
    def _():
        dg_ref[...] = jnp.zeros_like(dg_ref)

    dg_ref[...] += coef * jnp.sum(dh * y_ref[...], axis=0, keepdims=True)


def adaln_bwd(h, gain, shift, scale, du, dres, name, nxt=None):
    S, D = h.shape
    tm = _row_tile(S)

    def body(h_ref, g_ref, sh_ref, sc_ref, du_ref, dres_ref, *rest):
        dh_ref, dg_ref, dsh_ref, dsc_ref = rest[-6:-2] if nxt is not None else rest
        _, vjp = jax.vjp(_adaln, h_ref[...], g_ref[...], sh_ref[...], sc_ref[...])
        dh, dg, dsh, dsc = vjp(du_ref[...].astype(F32))
        dh = dres_ref[...] + dh
        dh_ref[...] = dh

        @pl.when(pl.program_id(0) == 0)
        def _():
            dg_ref[...] = jnp.zeros_like(dg_ref)
            dsh_ref[...] = jnp.zeros_like(dsh_ref)
            dsc_ref[...] = jnp.zeros_like(dsc_ref)

        dg_ref[...] += dg
        dsh_ref[...] += dsh
        dsc_ref[...] += dsc
        if nxt is not None:
            _gate_grads(dh, nxt, rest[0], rest[1], rest[-2], rest[-1])

    row = pl.BlockSpec((tm, D), lambda i: (i, 0))
    vec = pl.BlockSpec((1, D), lambda i: (0, 0))
    vs = jax.ShapeDtypeStruct((1, D), F32)
    xin, xargs, xout, xshape = _gate_specs(S, D, tm, nxt)
    return pl.pallas_call(
        body, name=name, grid=(S // tm,), in_specs=[row, vec, vec, vec, row, row] + xin,
        out_specs=[row, vec, vec, vec] + xout, out_shape=[jax.ShapeDtypeStruct((S, D), F32), vs, vs, vs] + xshape,
        compiler_params=_params("arbitrary"),
    )(h, gain, shift, scale, du, dres, *xargs)


def _swiglu(a, b):
    return a * jax.nn.sigmoid(a) * b


def ffn_in_act(u, w_in, name, side=None):
    J2, D, nb = w_in.shape
    J = J2 // 2
    S = u.shape[0]
    tm = _tile(S, 512, 16)

    def body(u_ref, wa_ref, wb_ref, ab_ref, h_ref, ht_ref):
        uv = u_ref[...]
        a = jnp.dot(uv, wa_ref[...], preferred_element_type=F32)
        b = jnp.dot(uv, wb_ref[...], preferred_element_type=F32)
        ab_ref[0] = a
        ab_ref[1] = b
        hv = _swiglu(a, b)
        h_ref[...] = hv.astype(BF16)
        ht_ref[...] = hv.T.astype(BF16)

    res = _call(body, name=name, grid=(J, S // tm),
                in_specs=[pl.BlockSpec((tm, D), lambda j, i: (i, 0)),
                          pl.BlockSpec((None, D, nb), lambda j, i: (j, 0, 0)),
                          pl.BlockSpec((None, D, nb), lambda j, i: (j + J, 0, 0))],
                out_specs=[pl.BlockSpec((2, None, tm, nb), lambda j, i: (0, j, i, 0)),
                           pl.BlockSpec((None, tm, nb), lambda j, i: (j, i, 0)),
                           pl.BlockSpec((None, nb, tm), lambda j, i: (j, 0, i))],
                out_shape=[jax.ShapeDtypeStruct((2, J, S, nb), F32), jax.ShapeDtypeStruct((J, S, nb), BF16),
                           jax.ShapeDtypeStruct((J, nb, S), BF16)],
                args=[u, w_in, w_in], semantics=("parallel", "parallel"), side=side)
    return res if side is None else (res[0], res[1])


def ffn_dact(dy, w_out4, ab4, name, side=None):
    J, nb, D = w_out4.shape
    S = dy.shape[0]
    tm = _tile(S, 512, 16)

    def body(dy_ref, w_ref, ab_ref, o_ref):
        dh = lax.dot_general(dy_ref[...], w_ref[...], (((1,), (1,)), ((), ())), preferred_element_type=F32)
        _, vjp = jax.vjp(_swiglu, ab_ref[0], ab_ref[1])
        da, db = vjp(dh)
        o_ref[0] = da.astype(BF16)
        o_ref[1] = db.astype(BF16)

    both = pl.BlockSpec((2, None, tm, nb), lambda j, i: (0, j, i, 0))
    res = _call(body, name=name, grid=(J, S // tm),
                in_specs=[pl.BlockSpec((tm, D), lambda j, i: (i, 0)),
                          pl.BlockSpec((None, nb, D), lambda j, i: (j, 0, 0)), both],
                out_specs=[both], out_shape=[jax.ShapeDtypeStruct((2, J, S, nb), BF16)],
                args=[dy, w_out4, ab4], semantics=("parallel", "parallel"), side=side)
    if side is None:
        return res[0].reshape(2 * J, S, nb)
    return res[0][0].reshape(2 * J, S, nb), res[1]


def colsum(x, name):
    S, N = x.shape
    tm = _row_tile(S)

    def body(x_ref, o_ref):
        @pl.when(pl.program_id(0) == 0)
        def _():
            o_ref[...] = jnp.zeros_like(o_ref)

        o_ref[...] += jnp.sum(x_ref[...].astype(F32), axis=0, keepdims=True)

    return pl.pallas_call(
        body, name=name, grid=(S // tm,), in_specs=[pl.BlockSpec((tm, N), lambda i: (i, 0))],
        out_specs=pl.BlockSpec((1, N), lambda i: (0, 0)), out_shape=jax.ShapeDtypeStruct((1, N), F32),
        compiler_params=_params("arbitrary"),
    )(x)


def _final_loss(h, gain, target):
    y = h * lax.rsqrt(jnp.mean(h * h, axis=-1, keepdims=True) + NORM_EPS) * gain
    err = y - target
    return 0.5 * jnp.sum(jnp.mean(err * err, axis=-1))


def final_loss_grad(h, gain, target, name, nxt):
    S, D = h.shape
    tm = _row_tile(S)

    def body(h_ref, g_ref, t_ref, y_ref, gate_ref, loss_ref, dh_ref, dg_ref, dy_ref, dgate_ref):
        loss, (dh, dg) = jax.value_and_grad(_final_loss, argnums=(0, 1))(h_ref[...], g_ref[...], t_ref[...])
        dh_ref[...] = dh

        @pl.when(pl.program_id(0) == 0)
        def _():
            loss_ref[...] = jnp.zeros_like(loss_ref)
            dg_ref[...] = jnp.zeros_like(dg_ref)

        loss_ref[...] += jnp.full(loss_ref.shape, loss, F32)
        dg_ref[...] += dg
        _gate_grads(dh, nxt, y_ref, gate_ref, dy_ref, dgate_ref)

    row = pl.BlockSpec((tm, D), lambda i: (i, 0))
    vec = pl.BlockSpec((1, D), lambda i: (0, 0))
    xin, xargs, xout, xshape = _gate_specs(S, D, tm, nxt)
    return pl.pallas_call(
        body, name=name, grid=(S // tm,), in_specs=[row, vec, row] + xin,
        out_specs=[pl.BlockSpec((1, LANES), lambda i: (0, 0)), row, vec] + xout,
        out_shape=[jax.ShapeDtypeStruct((1, LANES), F32), jax.ShapeDtypeStruct((S, D), F32),
                   jax.ShapeDtypeStruct((1, D), F32)] + xshape,
        compiler_params=_params("arbitrary"),
    )(h, gain, target, *xargs)


def _chunk_consts(H):
    C = HGRN_CHUNK
    t = lax.broadcasted_iota(jnp.int32, (H, C, C), 1)
    s = lax.broadcasted_iota(jnp.int32, (H, C, C), 2)
    return (s <= t).astype(F32), s <= t


def _hgrn_chunk(q_raw, f_raw, i_raw, lb, st):
    H, C, _ = q_raw.shape
    lower, causal = _chunk_consts(H)
    forget = lb + (1.0 - lb) * jax.nn.sigmoid(f_raw)
    g = jnp.log(forget)
    kk = 1.0 - forget
    qs = q_raw * jax.nn.sigmoid(q_raw)
    bnn = (((2,), (1,)), ((0,), (0,)))
    bnt = (((2,), (2,)), ((0,), (0,)))
    btn = (((1,), (1,)), ((0,), (0,)))
    b = lax.dot_general(lower, g, bnn, precision=lax.Precision.HIGHEST, preferred_element_type=F32)
    bm = b[:, C // 2 - 1:C // 2, :]
    bl = b[:, C - 1:C, :]
    inter = lax.dot_general((qs * jnp.exp(b)).astype(BF16), st.astype(BF16), bnt, preferred_element_type=F32)
    qt = (qs * jnp.exp(b - bm)).astype(BF16)
    kt = (kk * jnp.exp(bm - b)).astype(BF16)
    scores = lax.dot_general(qt, kt, bnt, preferred_element_type=F32)
    scores = jnp.where(causal, scores, 0.0)
    vb = i_raw.astype(BF16)
    out = inter + lax.dot_general(scores.astype(BF16), vb, bnn, preferred_element_type=F32)
    kdec = (kk * jnp.exp(bl - b)).astype(BF16)
    new_st = st * jnp.exp(bl) + lax.dot_general(vb, kdec, btn, preferred_element_type=F32)
    return out, new_st


def _heads(ref, rows, H):
    return jnp.stack([ref[rows, pl.ds(h * HGRN_HEAD, HGRN_HEAD)] for h in range(H)])


def hgrn_scan_fwd(proj, lb, name, side=None):
    S, D4 = proj.shape
    D = D4 // 4
    H = D // HGRN_HEAD
    C = HGRN_CHUNK
    R = min(S, 128)
    ncr = R // C

    def body(q_ref, f_ref, i_ref, lb_ref, o_ref, st_ref, state):
        @pl.when(pl.program_id(0) == 0)
        def _():
            state[...] = jnp.zeros_like(state)

        lbh = _heads(lb_ref, slice(None), H)

        def chunk(cc, carry):
            rows = pl.ds(pl.multiple_of(cc * C, C), C)
            st = state[...]
            st_ref[cc] = st
            out, new_st = _hgrn_chunk(_heads(q_ref, rows, H), _heads(f_ref, rows, H), _heads(i_ref, rows, H), lbh, st)
            for h in range(H):
                o_ref[rows, pl.ds(h * HGRN_HEAD, HGRN_HEAD)] = out[h]
            state[...] = new_st
            return carry

        lax.fori_loop(0, ncr, chunk, 0)

    col = lambda j: pl.BlockSpec((R, D), lambda i: (i, j))
    res = _call(body, name=name, grid=(S // R,),
                in_specs=[col(0), col(1), col(2), pl.BlockSpec((1, D), lambda i: (0, 0))],
                out_specs=[pl.BlockSpec((R, D), lambda i: (i, 0)),
                           pl.BlockSpec((ncr, H, HGRN_HEAD, HGRN_HEAD), lambda i: (i, 0, 0, 0))],
                out_shape=[jax.ShapeDtypeStruct((S, D), F32),
                           jax.ShapeDtypeStruct((S // C, H, HGRN_HEAD, HGRN_HEAD), F32)],
                scratch_shapes=[pltpu.VMEM((H, HGRN_HEAD, HGRN_HEAD), F32)], args=[proj, proj, proj, lb],
                semantics=("arbitrary",), side=side)
    return res if side is None else (res[0], res[1])


def hgrn_scan_bwd(proj, lb, states, do, dg, name, side=None):
    S, D4 = proj.shape
    D = D4 // 4
    H = D // HGRN_HEAD
    C = HGRN_CHUNK
    R = min(S, 128)
    ncr = R // C
    ng = S // R

    def body(q_ref, f_ref, i_ref, lb_ref, st_ref, do_ref, dg_ref, dp_ref, dlb_ref, dstate):
        @pl.when(pl.program_id(0) == 0)
        def _():
            dstate[...] = jnp.zeros_like(dstate)
            dlb_ref[...] = jnp.zeros_like(dlb_ref)

        dp_ref[:, pl.ds(3 * D, D)] = dg_ref[...].astype(BF16)
        lbh = _heads(lb_ref, slice(None), H)

        def chunk(t, carry):
            cc = ncr - 1 - t
            rows = pl.ds(pl.multiple_of(cc * C, C), C)
            _, vjp = jax.vjp(_hgrn_chunk, _heads(q_ref, rows, H), _heads(f_ref, rows, H), _heads(i_ref, rows, H),
                             lbh, st_ref[cc])
            dq, df, di, dlb, dst = vjp((_heads(do_ref, rows, H), dstate[...]))
            for h in range(H):
                dp_ref[rows, pl.ds(h * HGRN_HEAD, HGRN_HEAD)] = dq[h].astype(BF16)
                dp_ref[rows, pl.ds(D + h * HGRN_HEAD, HGRN_HEAD)] = df[h].astype(BF16)
                dp_ref[rows, pl.ds(2 * D + h * HGRN_HEAD, HGRN_HEAD)] = di[h].astype(BF16)
                dlb_ref[:, pl.ds(h * HGRN_HEAD, HGRN_HEAD)] += dlb[h]
            dstate[...] = dst
            return carry

        lax.fori_loop(0, ncr, chunk, 0)

    col = lambda j: pl.BlockSpec((R, D), lambda i: (ng - 1 - i, j))
    res = _call(body, name=name, grid=(ng,),
                in_specs=[col(0), col(1), col(2), pl.BlockSpec((1, D), lambda i: (0, 0)),
                          pl.BlockSpec((ncr, H, HGRN_HEAD, HGRN_HEAD), lambda i: (ng - 1 - i, 0, 0, 0)),
                          pl.BlockSpec((R, D), lambda i: (ng - 1 - i, 0)),
                          pl.BlockSpec((R, D), lambda i: (ng - 1 - i, 0))],
                out_specs=[pl.BlockSpec((R, D4), lambda i: (ng - 1 - i, 0)), pl.BlockSpec((1, D), lambda i: (0, 0))],
                out_shape=[jax.ShapeDtypeStruct((S, D4), BF16), jax.ShapeDtypeStruct((1, D), F32)],
                scratch_shapes=[pltpu.VMEM((H, HGRN_HEAD, HGRN_HEAD), F32)],
                args=[proj, proj, proj, lb, states, do, dg], semantics=("arbitrary",), side=side)
    return res if side is None else (res[0], res[1])


def _head_out(o, g, gain):
    y = o * lax.rsqrt(jnp.mean(o * o, axis=-1, keepdims=True) + NORM_EPS) * gain
    return y * jax.nn.sigmoid(g)


def hgrn_post_fwd(o, proj, gain, name):
    S, D = o.shape
    H = D // HGRN_HEAD
    tm = _row_tile(S)

    def body(o_ref, g_ref, gain_ref, z_ref, zt_ref):
        for h in range(H):
            ls = pl.ds(h * HGRN_HEAD, HGRN_HEAD)
            z = _head_out(o_ref[:, ls], g_ref[:, ls], gain_ref[:, ls])
            z_ref[:, ls] = z.astype(BF16)
            zt_ref[ls, :] = z.T.astype(BF16)

    row = pl.BlockSpec((tm, D), lambda i: (i, 0))
    return pl.pallas_call(
        body, name=name, grid=(S // tm,),
        in_specs=[row, pl.BlockSpec((tm, D), lambda i: (i, 3)), pl.BlockSpec((1, D), lambda i: (0, 0))],
        out_specs=[row, pl.BlockSpec((D, tm), lambda i: (0, i))],
        out_shape=[jax.ShapeDtypeStruct((S, D), BF16), jax.ShapeDtypeStruct((D, S), BF16)],
        compiler_params=_params("parallel"),
    )(o, proj, gain)


def hgrn_post_bwd(o, proj, gain, dz, name):
    S, D = o.shape
    H = D // HGRN_HEAD
    tm = _row_tile(S)

    def body(o_ref, g_ref, gain_ref, dz_ref, do_ref, dg_ref, dgain_ref):
        @pl.when(pl.program_id(0) == 0)
        def _():
            dgain_ref[...] = jnp.zeros_like(dgain_ref)

        for h in range(H):
            ls = pl.ds(h * HGRN_HEAD, HGRN_HEAD)
            _, vjp = jax.vjp(_head_out, o_ref[:, ls], g_ref[:, ls], gain_ref[:, ls])
            do, dg, dgain = vjp(dz_ref[:, ls].astype(F32))
            do_ref[:, ls] = do
            dg_ref[:, ls] = dg
            dgain_ref[:, ls] += dgain

    row = pl.BlockSpec((tm, D), lambda i: (i, 0))
    vec = pl.BlockSpec((1, D), lambda i: (0, 0))
    return pl.pallas_call(
        body, name=name, grid=(S // tm,),
        in_specs=[row, pl.BlockSpec((tm, D), lambda i: (i, 3)), vec, row], out_specs=[row, row, vec],
        out_shape=[jax.ShapeDtypeStruct((S, D), F32), jax.ShapeDtypeStruct((S, D), F32),
                   jax.ShapeDtypeStruct((1, D), F32)],
        compiler_params=_params("arbitrary"),
    )(o, proj, gain, dz)


def lower_bound_fwd(l0, l1, name):
    def body(a_ref, b_ref, o_ref):
        o_ref[...] = jax.nn.sigmoid(a_ref[...] - b_ref[...])

    return pl.pallas_call(body, name=name, out_shape=jax.ShapeDtypeStruct(l0.shape, F32))(l0, l1)


def lower_bound_bwd(l0, l1, dlb, name):
    def body(a_ref, b_ref, d_ref, o0_ref, o1_ref):
        s = jax.nn.sigmoid(a_ref[...] - b_ref[...])
        d0 = d_ref[...] * s * (1.0 - s)
        o0_ref[...] = d0
        o1_ref[...] = -d0

    sd = jax.ShapeDtypeStruct(l0.shape, F32)
    return pl.pallas_call(body, name=name, out_shape=[sd, sd])(l0, l1, dlb)


def _attn_tile(q, kp, kc, vp, vc, sink, first):
    W = ATT_WINDOW
    nt = (((1,), (1,)), ((), ()))
    qb = q.astype(BF16)
    scale = ATT_HEAD ** -0.5
    sp = lax.dot_general(qb, kp.astype(BF16), nt, preferred_element_type=F32) * scale
    sc = lax.dot_general(qb, kc.astype(BF16), nt, preferred_element_type=F32) * scale
    qi = lax.broadcasted_iota(jnp.int32, sp.shape, 0) & (W - 1)
    kj = lax.broadcasted_iota(jnp.int32, sp.shape, 1)
    sp = jnp.where((kj > qi) & jnp.logical_not(first), sp, NEG_INF)
    sc = jnp.where(kj <= qi, sc, NEG_INF)
    m = jnp.maximum(jnp.maximum(jnp.max(sp, axis=-1, keepdims=True), jnp.max(sc, axis=-1, keepdims=True)), sink)
    pp = jnp.exp(sp - m)
    pc = jnp.exp(sc - m)
    denom = jnp.sum(pp, axis=-1, keepdims=True) + jnp.sum(pc, axis=-1, keepdims=True) + jnp.exp(sink - m)
    out = jnp.dot((pp / denom).astype(BF16), vp.astype(BF16), preferred_element_type=F32)
    return out + jnp.dot((pc / denom).astype(BF16), vc.astype(BF16), preferred_element_type=F32)


def _attn_specs(G, W, Dh):
    q_spec = pl.BlockSpec((None, G, W, Dh), lambda j, n: (j, 0, n, 0))
    prev = pl.BlockSpec((None, W, Dh), lambda j, n: (j, jnp.maximum(n - 1, 0), 0))
    cur = pl.BlockSpec((None, W, Dh), lambda j, n: (j, n, 0))
    sink = pl.BlockSpec((None, G * W, 1), lambda j, n: (j, 0, 0))
    return q_spec, prev, cur, sink


def attn_fwd(q4, k3, v3, sink, name, side=None):
    NKV, G, S, Dh = q4.shape
    W = ATT_WINDOW

    def body(q_ref, kp_ref, kc_ref, vp_ref, vc_ref, s_ref, o_ref):
        first = pl.program_id(1) == 0
        out = _attn_tile(q_ref[...].reshape(G * W, Dh), kp_ref[...], kc_ref[...], vp_ref[...], vc_ref[...],
                         s_ref[...], first)
        o_ref[...] = out.reshape(G, W, Dh)

    q_spec, prev, cur, sk = _attn_specs(G, W, Dh)
    res = _call(body, name=name, grid=(NKV, S // W), in_specs=[q_spec, prev, cur, prev, cur, sk], out_specs=[q_spec],
                out_shape=[jax.ShapeDtypeStruct(q4.shape, F32)], args=[q4, k3, k3, v3, v3, sink],
                semantics=("parallel", "parallel"), side=side)
    return res[0] if side is None else (res[0][0], res[1])


def attn_bwd(q4, k3, v3, sink, do4, name, side=None):
    NKV, G, S, Dh = q4.shape
    W = ATT_WINDOW
    nb = S // W

    def body(q_ref, kp_ref, kc_ref, vp_ref, vc_ref, s_ref, do_ref, dq_ref, dkp_ref, dkc_ref, dvp_ref, dvc_ref,
             ds_ref):
        first = pl.program_id(1) == 0
        _, vjp = jax.vjp(functools.partial(_attn_tile, first=first), q_ref[...].reshape(G * W, Dh), kp_ref[...],
                         kc_ref[...], vp_ref[...], vc_ref[...], s_ref[...])
        dq, dkp, dkc, dvp, dvc, ds = vjp(do_ref[...].reshape(G * W, Dh))
        dq_ref[...] = dq.reshape(G, W, Dh)
        dkp_ref[...] = dkp
        dkc_ref[...] = dkc
        dvp_ref[...] = dvp
        dvc_ref[...] = dvc

        @pl.when(first)
        def _():
            ds_ref[...] = jnp.zeros_like(ds_ref)

        ds_ref[...] += jnp.sum(ds.reshape(G, W, 1), axis=1)

    q_spec, prev, cur, sk = _attn_specs(G, W, Dh)
    part = pl.BlockSpec((None, None, W, Dh), lambda j, n: (j, n, 0, 0))
    ps = jax.ShapeDtypeStruct((NKV, nb, W, Dh), F32)
    res = _call(body, name=name, grid=(NKV, nb), in_specs=[q_spec, prev, cur, prev, cur, sk, q_spec],
                out_specs=[q_spec, part, part, part, part, pl.BlockSpec((None, G, 1), lambda j, n: (j, 0, 0))],
                out_shape=[jax.ShapeDtypeStruct(q4.shape, F32), ps, ps, ps, ps,
                           jax.ShapeDtypeStruct((NKV, G, 1), F32)],
                args=[q4, k3, k3, v3, v3, sink, do4], semantics=("parallel", "arbitrary"), side=side)
    outs = res if side is None else res[0]
    ans = (outs[0], outs[1:5], outs[5])
    return ans if side is None else (ans, res[1])


def band_combine(cur, prev, name):
    NKV, nb, W, Dh = cur.shape

    def body(c_ref, p_ref, o_ref):
        keep = (pl.program_id(1) < nb - 1).astype(F32)
        o_ref[...] = c_ref[...] + keep * p_ref[...]

    return pl.pallas_call(
        body, name=name, grid=(NKV, nb),
        in_specs=[pl.BlockSpec((None, None, W, Dh), lambda j, n: (j, n, 0, 0)),
                  pl.BlockSpec((None, None, W, Dh), lambda j, n: (j, jnp.minimum(n + 1, nb - 1), 0, 0))],
        out_specs=pl.BlockSpec((None, W, Dh), lambda j, n: (j, n, 0)),
        out_shape=jax.ShapeDtypeStruct((NKV, nb * W, Dh), F32), compiler_params=_params("parallel", "parallel"),
    )(cur, prev)


def rope_tables(S):
    half = ROT_DIM // 2
    inv_freq = jnp.power(jnp.float32(ROPE_THETA), -jnp.arange(0, ROT_DIM, 2, dtype=F32) / ROT_DIM)
    ang = jnp.arange(S, dtype=F32)[:, None] * inv_freq[None, :]
    sin, cos = jnp.sin(ang), jnp.cos(ang)
    zeros = jnp.zeros((S, ATT_HEAD - ROT_DIM), F32)
    z8 = jnp.zeros((S, half), F32)
    cfull = jnp.concatenate([cos, cos, jnp.ones((S, ATT_HEAD - ROT_DIM), F32)], axis=1)
    s_next = jnp.concatenate([-sin, z8, zeros], axis=1)
    s_prev = jnp.concatenate([z8, sin, zeros], axis=1)
    two = lambda t: jnp.concatenate([t, t], axis=1)
    return two(cfull), two(s_next), two(s_prev)


def rope(x, tables, sign, name):
    S, Wd = x.shape
    tm = _row_tile(S)
    rep = Wd // LANES
    half = ROT_DIM // 2

    def body(x_ref, c_ref, sn_ref, sp_ref, o_ref):
        xv = x_ref[...]
        c = jnp.tile(c_ref[...], (1, rep))
        sn = jnp.tile(sn_ref[...], (1, rep))
        sp = jnp.tile(sp_ref[...], (1, rep))
        if sign > 0:
            nxt = pltpu.roll(xv, Wd - half, 1)
            prv = pltpu.roll(xv, half, 1)
            o_ref[...] = xv * c + nxt * sn + prv * sp
        else:
            o_ref[...] = xv * c + pltpu.roll(xv * sn, half, 1) + pltpu.roll(xv * sp, Wd - half, 1)

    row = pl.BlockSpec((tm, Wd), lambda i: (i, 0))
    tab = pl.BlockSpec((tm, LANES), lambda i: (i, 0))
    return pl.pallas_call(
        body, name=name, grid=(S // tm,), in_specs=[row, tab, tab, tab], out_specs=row,
        out_shape=jax.ShapeDtypeStruct((S, Wd), F32), compiler_params=_params("parallel"),
    )(x, *tables)


def cond_proj(c_all, w, bias, name):
    B, D = c_all.shape
    N = w.shape[1]
    tn = _tile(N, 512)

    def body(c_ref, w_ref, b_ref, o_ref):
        cv = c_ref[...]
        cs = (cv * jax.nn.sigmoid(cv)).astype(BF16)
        o_ref[...] = jnp.dot(cs, w_ref[...].astype(BF16), preferred_element_type=F32) + b_ref[...]

    return pl.pallas_call(
        body, name=name, grid=(N // tn,),
        in_specs=[pl.BlockSpec((B, D), lambda j: (0, 0)), pl.BlockSpec((D, tn), lambda j: (0, j)),
                  pl.BlockSpec((1, tn), lambda j: (0, j))],
        out_specs=pl.BlockSpec((B, tn), lambda j: (0, j)), out_shape=jax.ShapeDtypeStruct((B, N), F32),
        compiler_params=_params("parallel"),
    )(c_all, w, bias)


def cond_grad(c_all, dmod, name):
    B, D = c_all.shape
    N = dmod.shape[1]
    tn = _tile(N, 512)

    def body(c_ref, d_ref, o_ref):
        cv = c_ref[...]
        cs = (cv * jax.nn.sigmoid(cv)).astype(BF16)
        o_ref[...] = lax.dot_general(cs, d_ref[...].astype(BF16), (((0,), (0,)), ((), ())),
                                     preferred_element_type=F32)

    return pl.pallas_call(
        body, name=name, grid=(N // tn,),
        in_specs=[pl.BlockSpec((B, D), lambda j: (0, 0)), pl.BlockSpec((B, tn), lambda j: (0, j))],
        out_specs=pl.BlockSpec((D, tn), lambda j: (0, j)), out_shape=jax.ShapeDtypeStruct((D, N), F32),
        compiler_params=_params("parallel"),
    )(c_all, dmod)


def rowsum(g, name):
    B, N = g.shape
    tn = _tile(N, 8192)

    def body(g_ref, o_ref):
        acc = g_ref[0:1, :]
        for r in range(1, B):
            acc = acc + g_ref[r:r + 1, :]
        o_ref[...] = acc

    return pl.pallas_call(
        body, name=name, grid=(N // tn,), in_specs=[pl.BlockSpec((B, tn), lambda j: (0, j))],
        out_specs=pl.BlockSpec((1, tn), lambda j: (0, j)), out_shape=jax.ShapeDtypeStruct((1, N), F32),
        compiler_params=_params("parallel"),
    )(g)


def _adam_rows(R, C):
    limit = 3 << 19
    if R * C * 4 <= limit or R % 8:
        return R
    best = 8
    for t in range(8, R + 1, 8):
        if R % t == 0 and t * C * 4 <= limit:
            best = t
    return best


def adamw(w3, m3, v3, j, g3, name, col=(0, 1), into=None, side=None):
    n, R, C = w3.shape
    P = g3.shape[0]
    cp, ncol = col
    Cp = C // ncol
    tr = _adam_rows(R, Cp)

    def body(*refs):
        w_ref, m_ref, v_ref, g_ref = refs[:4]
        go_ref, d_ref, mo_ref, vo_ref = refs[-4:]
        g = g_ref[0].astype(F32)
        for p in range(1, P):
            g = g + g_ref[p].astype(F32)
        mn = ADAM_B1 * m_ref[...] + (1.0 - ADAM_B1) * g
        vn = ADAM_B2 * v_ref[...] + (1.0 - ADAM_B2) * jnp.square(g)
        m_hat = mn / (1.0 - ADAM_B1 ** ADAM_STEP)
        v_hat = vn / (1.0 - ADAM_B2 ** ADAM_STEP)
        go_ref[...] = g
        d_ref[...] = -ADAM_LR * (m_hat / (jnp.sqrt(v_hat) + ADAM_EPS) + ADAM_WD * w_ref[...])
        mo_ref[...] = mn
        vo_ref[...] = vn

    spec = pl.BlockSpec((None, tr, Cp), lambda i: (j, i, cp))
    g_spec = pl.BlockSpec((P, tr, Cp), lambda i: (0, i, 0))
    sd = jax.ShapeDtypeStruct((n, R, C), F32)
    in_specs, args, aliases = [spec, spec, spec, g_spec], [w3, m3, v3, g3], {}
    if side is not None:
        assert into is None
        return _call(body, name=name, grid=(R // tr,), in_specs=in_specs, out_specs=[spec] * 4, out_shape=[sd] * 4,
                     args=args, semantics=("parallel",), side=side)
    if into is not None:
        in_specs += [_ANY] * 4
        args += list(into)
        aliases = {4 + k: k for k in range(4)}
    return pl.pallas_call(
        body, name=name, grid=(R // tr,), in_specs=in_specs, out_specs=[spec] * 4, out_shape=[sd] * 4,
        input_output_aliases=aliases, compiler_params=_params("parallel"),
    )(*args)


def _place():
    return lax.axis_index("x"), lax.axis_index("y"), lax.axis_index("c")


def _slot(p):
    return 4 * p[0] + 2 * p[1] + p[2]


def gather_side(items):
    xs = [a for a, _ in items]
    n = len(xs)

    def copy(ins, outs, sems, t, k, block, to, from_input=False):
        dst = outs[t].at[_slot(block)]
        return pltpu.make_async_remote_copy(
            src_ref=ins[t] if from_input else dst, dst_ref=dst, send_sem=sems[0].at[t, k],
            recv_sem=sems[1].at[t, k], device_id=to, device_id_type=MESH)

    def peers():
        x, y, c = _place()
        return (x, y, c), (x, y, 1 - c), [(1 - x, y), (x, 1 - y), (1 - x, 1 - y)]

    def start(ins, outs, sems):
        me, sibling, chips = peers()
        c = me[2]
        for t in range(n):
            pltpu.make_async_copy(ins[t], outs[t].at[_slot(me)], sems[2].at[t]).start()
            copy(ins, outs, sems, t, 0, me, sibling, True).start()
            for j, chip in enumerate(chips):
                copy(ins, outs, sems, t, 1 + j, me, (*chip, c), True).start()

    def finish(ins, outs, sems):
        me, sibling, chips = peers()
        c = me[2]
        for t in range(n):
            for j, chip in enumerate(chips):
                copy(ins, outs, sems, t, 1 + j, (*chip, c), me).wait_recv()
                copy(ins, outs, sems, t, 4 + j, (*chip, c), sibling).start()
        for t in range(n):
            copy(ins, outs, sems, t, 0, sibling, me).wait_recv()
            for j, chip in enumerate(chips):
                copy(ins, outs, sems, t, 4 + j, (*chip, 1 - c), me).wait_recv()
        for t in range(n):
            copy(ins, outs, sems, t, 0, me, sibling, True).wait_send()
            for j, chip in enumerate(chips):
                copy(ins, outs, sems, t, 1 + j, me, (*chip, c), True).wait_send()
                copy(ins, outs, sems, t, 4 + j, (*chip, c), sibling).wait_send()
            pltpu.make_async_copy(ins[t], outs[t].at[_slot(me)], sems[2].at[t]).wait()

    return _Side(xs, [jax.ShapeDtypeStruct((N_DEV,) + a.shape, a.dtype) for a in xs],
                 [pltpu.SemaphoreType.DMA((n, 7)), pltpu.SemaphoreType.DMA((n, 7)), pltpu.SemaphoreType.DMA((n,))],
                 start, finish)


def scatter_side(items):
    gs = [g for g, _ in items]
    n = len(gs)

    def part_shape(g, part):
        _, R, C = g.shape
        if part is None:
            return R, C
        axis, _, cnt = part
        return (R // cnt, C) if axis == 0 else (R, C // cnt)

    def block(ref, slot, g, part):
        if part is None:
            return ref.at[slot]
        axis, idx, cnt = part
        R, C = part_shape(g, part)
        return ref.at[slot, pl.ds(idx * R, R)] if axis == 0 else ref.at[slot, :, pl.ds(idx * C, C)]

    def copies(ins, outs, sems):
        x, y, c = _place()
        me = (x, y, c)
        out = []
        for t, (g, part) in enumerate(items):
            out.append(pltpu.make_async_copy(block(ins[t], _slot(me), g, part), outs[t].at[_slot(me)], sems[2].at[t]))
            for r in range(1, N_DEV):
                peer = (1 - x if r & 4 else x, 1 - y if r & 2 else y, 1 - c if r & 1 else c)
                out.append(pltpu.make_async_remote_copy(
                    src_ref=block(ins[t], _slot(peer), g, part), dst_ref=outs[t].at[_slot(me)],
                    send_sem=sems[0].at[t, r - 1], recv_sem=sems[1].at[t, r - 1], device_id=peer, device_id_type=MESH))
        return out

    def start(ins, outs, sems):
        for cp in copies(ins, outs, sems):
            cp.start()

    def finish(ins, outs, sems):
        for cp in copies(ins, outs, sems):
            cp.wait()

    return _Side(gs, [jax.ShapeDtypeStruct((N_DEV,) + part_shape(g, part), g.dtype) for g, part in items],
                 [pltpu.SemaphoreType.DMA((n, 7)), pltpu.SemaphoreType.DMA((n, 7)), pltpu.SemaphoreType.DMA((n,))],
                 start, finish)


def exchange(side, name):
    def body(*refs):
        n_in, n_out = len(side.inputs), len(side.out_shapes)
        ins, outs, sems = refs[:n_in], refs[n_in:n_in + n_out], refs[n_in + n_out:]
        side.start(ins, outs, sems)
        side.finish(ins, outs, sems)

    return pl.pallas_call(
        body, name=name, in_specs=[_ANY] * len(side.inputs), out_specs=[_ANY] * len(side.out_shapes),
        out_shape=side.out_shapes, scratch_shapes=side.sem_shapes,
    )(*side.inputs)


class _Part:
    def __init__(self, key, axis, index, count):
        self.key, self.axis, self.index, self.count = key, axis, index, count

    def __hash__(self):
        return hash((self.key, self.axis, self.index, self.count))

    def __eq__(self, other):
        return isinstance(other, _Part) and (self.key, self.axis, self.index, self.count) == (
            other.key, other.axis, other.index, other.count)


def _rows(key, p):
    return _Part(key, 0, p, 2)


def _cols(key, p):
    return _Part(key, 1, p, 2)


class _Plan:
    def __init__(self, plan, make_side):
        self.plan, self.make_side, self.source, self.got = plan, make_side, {}, {}

    def item(self, k):
        return (self.source[k.key], (k.axis, k.index, k.count)) if isinstance(k, _Part) else (self.source[k], None)

    def run(self, fn, *args, name, **kw):
        keys = self.plan.get(name)
        if not keys:
            return fn(*args, name=name, **kw)
        result, outs = fn(*args, name=name, side=self.make_side([self.item(k) for k in keys]), **kw)
        self.got.update(zip(keys, outs))
        return result


GATHER_PLAN = {
    "l0s0_in": [("out", 0, 0), "kv"],
    "l0s0_out": ["hin"],
    "l0s1_proj": [("out", 0, 1), "hout"],
    "l0s1_scan": [("in", 0, 1)],
    "l0s2_in": [("in", 1, 0)],
    "l0s2_out": [("out", 1, 0), "q"],
    "l1s0_in": [("out", 1, 1), "o"],
    "l1s1_attn": [("in", 1, 1)],
}
GATHER_FIRST = [("in", 0, 0)]
SCATTER_PLAN = {
    "l1s2_dact": [_cols(("out", 1, 1), 0)],
    "l1s2_dwin0": [_cols(("out", 1, 1), 1)],
    "l1s2_dwin1": [_rows(("in", 1, 1, 0), 0)],
    "l1s2_du": [_rows(("in", 1, 1, 0), 1), _rows(("in", 1, 1, 1), 0)],
    "l1s1_dattn": [_rows(("in", 1, 1, 1), 1), "o"],
    "l1s0_dwout": ["q"],
    "l1s0_dact": [_cols(("out", 1, 0), 0)],
    "l1s0_dwin0": [_cols(("out", 1, 0), 1)],
    "l1s0_dwin1": [_rows(("in", 1, 0, 0), 0)],
    "l1s0_du": [_rows(("in", 1, 0, 0), 1), _rows(("in", 1, 0, 1), 0)],
    "l0s2_dwout": [_rows(("in", 1, 0, 1), 1)],
    "l0s2_dact": [_cols(("out", 0, 1), 0)],
    "l0s2_dwin0": [_cols(("out", 0, 1), 1)],
    "l0s2_dwin1": [_rows(("in", 0, 1, 0), 0)],
    "l0s2_du": [_rows(("in", 0, 1, 0), 1), _rows(("in", 0, 1, 1), 0)],
    "l0s1_dwout": ["kv"],
    "l0s1_dscan": [_rows(("in", 0, 1, 1), 1), "hout"],
    "l0s1_dwin1": [_rows(("hin", 0), 0)],
    "l0s1_du": [_rows(("hin", 0), 1), _rows(("hin", 1), 0)],
    "l0s0_dwout": [_rows(("hin", 1), 1)],
    "l0s0_dact": [_cols(("out", 0, 0), 0)],
    "l0s0_dwin0": [_cols(("out", 0, 0), 1)],
    "l0s0_dwin1": [_rows(("in", 0, 0, 0), 0)],
    "l0s0_du": [_rows(("in", 0, 0, 0), 1), _rows(("in", 0, 0, 1), 0)],
    "adam_w_ada0": [_rows(("in", 0, 0, 1), 1)],
}
GRAD_TM, GRAD_TK = 1024, 2048


def _ffn_fwd(gp, W, h, gate, u, l, i, tag):
    ab4, h3, h3t = gp.run(ffn_in_act, u, W[("in", l, i)], name=tag + "_in")
    w_out = W[("out", l, i)]
    J = h3.shape[0]
    y, h_new = gp.run(mm_nn, h3, w_out.reshape(J, 1, -1, w_out.shape[-1]), F32, name=tag + "_out", natural=True,
                      resid=(h, gate, 0.5), kgroup=2)
    return y, h_new, (ab4, h3t)


def _ffn_bwd(sp, W, dy, ut, ab4, h3t, l, i, tag):
    w_in, w_out = W[("in", l, i)], W[("out", l, i)]
    J, nb, S = h3t.shape
    D = w_out.shape[-1]
    dw_out = sp.run(mm_nn, h3t.reshape(1, J * nb, S), dy[None, None], BF16, name=tag + "_dwout", natural=True,
                    tm=w_out.shape[1], tk=GRAD_TK)
    sp.source[("out", l, i)] = dw_out.reshape(w_out.shape)
    dab3 = sp.run(ffn_dact, dy, w_out.reshape(J, nb, D), ab4, name=tag + "_dact")
    for hf in range(2):
        sp.source[("in", l, i, hf)] = sp.run(mm_nn, ut.reshape(2, D // 2, S), dab3[None], BF16, name=f"{tag}_dwin{hf}",
                                             a_sel=hf, tm=GRAD_TM, tk=GRAD_TK)
    return sp.run(mm_nt, dab3, w_in[:, None], F32, name=tag + "_du", natural=True, kgroup=4)


def kernel(x, c, norm_gain, w_ada, b_ada, w_ffn_in, w_ffn_out, w_hgrn_in, hgrn_lb_logits, hgrn_head_gain, w_hgrn_out, kv_gain, w_ada_kv, b_ada_kv, w_kv, b_kv, w_q, b_q, attn_sinks, w_attn_out, final_gain, loss_target, m_norm_gain, m_w_ada, m_b_ada, m_w_ffn_in, m_w_ffn_out, m_w_hgrn_in, m_hgrn_lb_logits, m_hgrn_head_gain, m_w_hgrn_out, m_kv_gain, m_w_ada_kv, m_b_ada_kv, m_w_kv, m_b_kv, m_w_q, m_b_q, m_attn_sinks, m_w_attn_out, m_final_gain, v_norm_gain, v_w_ada, v_b_ada, v_w_ffn_in, v_w_ffn_out, v_w_hgrn_in, v_hgrn_lb_logits, v_hgrn_head_gain, v_w_hgrn_out, v_kv_gain, v_w_ada_kv, v_b_ada_kv, v_w_kv, v_b_kv, v_w_q, v_b_q, v_attn_sinks, v_w_attn_out, v_final_gain):
    xi, yi, ci = _place()
    me = 4 * xi + 2 * yi + ci
    _, S, D = x.shape
    L = norm_gain.shape[0]
    dsh = D // N_DEV
    ada_n = w_ada.shape[2]
    kv_n = w_ada_kv.shape[1]
    NQ = D // ATT_HEAD
    NKV = NQ // ATT_GROUP
    kvd = NKV * ATT_HEAD
    h0 = x[0]
    target = loss_target[0]

    def my_cols(a, n):
        return lax.dynamic_slice_in_dim(a, me * n, n, axis=a.ndim - 1)

    gp = _Plan(GATHER_PLAN, gather_side)
    bf = lambda a: a.astype(BF16)
    for l in range(L):
        for i in range(2):
            gp.source[("in", l, i)] = bf(w_ffn_in[l, i])
            gp.source[("out", l, i)] = bf(w_ffn_out[l, i])
    gp.source.update(hin=bf(w_hgrn_in[0]), hout=bf(w_hgrn_out[0]), kv=bf(w_kv), q=bf(w_q[0]), o=bf(w_attn_out[0]))
    W = gp.got
    W.update(zip(GATHER_FIRST, exchange(gather_side([(gp.source[k], None) for k in GATHER_FIRST]), "gather_first")))
    full = lambda w: w.reshape(1, 1, -1, w.shape[-1])

    lb_sh = lower_bound_fwd(hgrn_lb_logits[0:1], hgrn_lb_logits[1:2], "lb_fwd")
    small = jnp.concatenate([c, norm_gain.reshape(1, L * 3 * dsh), hgrn_head_gain, lb_sh], axis=1)
    (g1,) = exchange(gather_side([(small, None)]), "gather_cond")
    g1 = g1.reshape(N_DEV, -1)
    c_all = g1[:, :D]
    gains = g1[:, D:D + L * 3 * dsh].reshape(N_DEV, L * 3, dsh).transpose(1, 0, 2).reshape(L, 3, 1, D)
    head_gain = g1[:, D + L * 3 * dsh:D + (L * 3 + 1) * dsh].reshape(1, D)
    lb0 = g1[:, D + (L * 3 + 1) * dsh:].reshape(1, D)

    parts = [cond_proj(c_all, w_ada[l], my_cols(b_ada[l:l + 1], ada_n), f"mod{l}") for l in range(L)]
    parts.append(cond_proj(c_all, w_ada_kv, my_cols(b_ada_kv[None], kv_n), "mod_kv"))
    (g2,) = exchange(gather_side([(jnp.concatenate(parts, axis=1), None)]), "gather_mod")
    mine2 = lax.dynamic_index_in_dim(g2, me, axis=1, keepdims=False)
    mod = [mine2[:, l * ada_n:(l + 1) * ada_n].reshape(3, 3, 1, D) for l in range(L)]
    mod_kv = mine2[:, L * ada_n:].reshape(2, 1, D)

    tables = rope_tables(S)
    sink_col = jnp.broadcast_to(attn_sinks.reshape(NKV, ATT_GROUP, 1, 1), (NKV, ATT_GROUP, ATT_WINDOW, 1))
    sink_col = sink_col.reshape(NKV, ATT_GROUP * ATT_WINDOW, 1)

    def to_heads(t, n):
        return t.reshape(S, n, ATT_HEAD).transpose(1, 0, 2)

    def from_heads(t):
        return t.transpose(1, 0, 2).reshape(S, -1)

    h = h0
    saved = {}
    for l in range(L):
        for s in (0, 1, 2):
            tag = f"l{l}s{s}"
            shift, scale, gate = mod[l][s, 0], mod[l][s, 1], mod[l][s, 2]
            u, ut = adaln_fwd(h, gains[l, s], shift, scale, tag + "_norm")
            if s != 1:
                y, h_new, res = _ffn_fwd(gp, W, h, gate, u, l, s // 2, tag)
            elif l == 0:
                proj = gp.run(mm_nn, u[None], W["hin"][None], F32, name=tag + "_proj", natural=True)
                o, states = gp.run(hgrn_scan_fwd, proj, lb0, name=tag + "_scan")
                z, zt = hgrn_post_fwd(o, proj, head_gain, tag + "_post")
                y, h_new = mm_nn(z[None], full(W["hout"]), F32, tag + "_out", natural=True, resid=(h, gate, 1.0))
                res = (proj, o, states, zt)
            else:
                q = mm_nn(u[None], full(W["q"]), F32, tag + "_q", natural=True, bias=b_q)
                q4 = to_heads(rope(q, tables, 1, tag + "_rope"), NQ).reshape(NKV, ATT_GROUP, S, ATT_HEAD)
                att4 = gp.run(attn_fwd, q4, k3, v3, sink_col, name=tag + "_attn")
                att = from_heads(att4.reshape(NQ, S, ATT_HEAD))
                att_t = att4.reshape(NQ, S, ATT_HEAD).transpose(0, 2, 1).reshape(D, S)
                y, h_new = mm_nn(att[None], full(W["o"]), F32, tag + "_out", natural=True, resid=(h, gate, 1.0))
                res = (q4, att_t)
            saved[(l, s)] = (h, ut, y, res)
            h = h_new
        if l == 0:
            h_kv = h
            u_kv, u_kv_t = adaln_fwd(h, kv_gain[None], mod_kv[0], mod_kv[1], "kv_norm")
            kvp = mm_nn(u_kv[None], full(W["kv"]), F32, "kv_proj", natural=True, bias=b_kv[None])
            k3 = to_heads(rope(kvp[:, :kvd], tables, 1, "kv_rope"), NKV)
            v3 = to_heads(kvp[:, kvd:], NKV)

    def branch(l, s):
        return saved[(l, s)][2], mod[l][s, 2], 0.5 if s != 1 else 1.0

    loss_row, dh, d_final_gain, dy, d_gate = final_loss_grad(h, final_gain[None], target, "final", branch(L - 1, 2))
    loss = lax.psum(loss_row[0, 0], ("x", "y", "c"))

    sp = _Plan(SCATTER_PLAN, scatter_side)

    def grad_w(a_t, b, name):
        return sp.run(mm_nn, a_t[None], b[None, None], BF16, name=name, natural=True, tm=GRAD_TM, tk=GRAD_TK)

    d_mod = [[None] * 3 for _ in range(L)]
    d_gain = [[None] * 3 for _ in range(L)]
    for l in reversed(range(L)):
        if l == 0:
            dkv = jnp.concatenate([rope(from_heads(dk3), tables, -1, "kv_drope"), from_heads(dv3)], axis=1)
            sp.source["kv"] = grad_w(u_kv_t, dkv, "kv_dw").reshape(W["kv"].shape)
            db_kv = colsum(dkv, "kv_db")
            du_kv = mm_nt(dkv[None], full(W["kv"]), F32, "kv_du", natural=True)
            dh, d_kv_gain, d_kv_shift, d_kv_scale, dy, d_gate = adaln_bwd(
                h_kv, kv_gain[None], mod_kv[0], mod_kv[1], du_kv, dh, "kv_dnorm", nxt=branch(0, 2))
        for s in (2, 1, 0):
            tag = f"l{l}s{s}"
            shift, scale, gate = mod[l][s, 0], mod[l][s, 1], mod[l][s, 2]
            h_in, ut, y, res = saved[(l, s)]
            if s != 1:
                du = _ffn_bwd(sp, W, dy, ut, res[0], res[1], l, s // 2, tag)
            elif l == 0:
                proj, o, states, zt = res
                sp.source["hout"] = grad_w(zt, dy, tag + "_dwout").reshape(W["hout"].shape)
                dz = mm_nt(dy[None], full(W["hout"]), F32, tag + "_dz", natural=True)
                do, dg, d_head_gain = hgrn_post_bwd(o, proj, head_gain, dz, tag + "_dpost")
                dproj, d_lb0 = sp.run(hgrn_scan_bwd, proj, lb0, states, do, dg, name=tag + "_dscan")
                for hf in range(2):
                    sp.source[("hin", hf)] = sp.run(mm_nn, ut.reshape(2, D // 2, S), dproj, BF16, name=f"{tag}_dwin{hf}",
                                                    b_natural=True, jn=N_DEV, a_sel=hf, tm=GRAD_TM, tk=GRAD_TK)
                du = sp.run(mm_nt, dproj, W["hin"][:, None], F32, name=tag + "_du", natural=True, a_natural=True)
            else:
                q4, att_t = res
                sp.source["o"] = grad_w(att_t, dy, tag + "_dwout").reshape(W["o"].shape)
                datt = mm_nt(dy[None], full(W["o"]), F32, tag + "_datt", natural=True)
                datt4 = to_heads(datt, NQ).reshape(NKV, ATT_GROUP, S, ATT_HEAD)
                dq4, (dkp, dkc, dvp, dvc), d_sink = sp.run(attn_bwd, q4, k3, v3, sink_col, datt4, name=tag + "_dattn")
                dk3 = band_combine(dkc, dkp, tag + "_dk")
                dv3 = band_combine(dvc, dvp, tag + "_dv")
                dq = rope(from_heads(dq4.reshape(NQ, S, ATT_HEAD)), tables, -1, tag + "_drope")
                sp.source["q"] = grad_w(ut, dq, tag + "_dwq").reshape(W["q"].shape)
                db_q = colsum(dq, tag + "_dbq")
                du = mm_nt(dq[None], full(W["q"]), F32, tag + "_du", natural=True)
            outs = adaln_bwd(h_in, gains[l, s], shift, scale, du, dh, tag + "_dnorm",
                             nxt=branch(l, s - 1) if s > 0 else None)
            dh, d_gain[l][s], dsh_, dsc_ = outs[:4]
            d_mod[l][s] = jnp.concatenate([dsh_, dsc_, d_gate], axis=1)
            if s > 0:
                dy, d_gate = outs[4:]
    grad_x = dh[None]
    G = sp.got

    pad = lambda a, n: jnp.pad(a, ((0, 0), (0, n - a.shape[1])))
    pieces = [jnp.concatenate(d_mod[l], axis=1) for l in range(L)]
    pieces += [d_kv_shift, d_kv_scale]
    pieces += [d_gain[l][s] for l in range(L) for s in range(3)]
    pieces += [d_head_gain, d_lb0, d_kv_gain, db_kv, db_q, pad(d_sink.reshape(1, NQ), LANES), d_final_gain]
    (g3,) = exchange(gather_side([(jnp.concatenate(pieces, axis=1), None)]), "gather_small_grads")
    g3 = g3.reshape(N_DEV, -1)
    tot = rowsum(g3, "sum_small_grads")
    offs = [0]
    for p in pieces:
        offs.append(offs[-1] + p.shape[1])
    seg = lambda k: tot[:, offs[k]:offs[k + 1]]
    k0 = 0
    g_b_ada = jnp.concatenate([seg(l) for l in range(L)], axis=0)
    k0 += L
    g_b_ada_kv = jnp.concatenate([seg(k0), seg(k0 + 1)], axis=1)
    k0 += 2
    g_norm_gain = jnp.concatenate([my_cols(seg(k0 + j), dsh) for j in range(3 * L)], axis=0)
    k0 += 3 * L
    g_head_gain = my_cols(seg(k0), dsh)
    d_lb_sh = my_cols(seg(k0 + 1), dsh)
    g_kv_gain = seg(k0 + 2)
    g_b_kv = seg(k0 + 3)
    g_b_q = seg(k0 + 4)
    g_sinks = seg(k0 + 5)[:, :NQ]
    g_final_gain = seg(k0 + 6)
    dl0, dl1 = lower_bound_bwd(hgrn_lb_logits[0:1], hgrn_lb_logits[1:2], d_lb_sh, "lb_bwd")
    g_lb_logits = jnp.concatenate([dl0, dl1], axis=0)

    g_w_ada = jnp.stack([cond_grad(c_all, lax.dynamic_slice_in_dim(g3, offs[l] + me * ada_n, ada_n, axis=1),
                                   f"dw_ada{l}") for l in range(L)])
    g_w_ada_kv = cond_grad(c_all, lax.dynamic_slice_in_dim(g3, offs[L] + me * kv_n, kv_n, axis=1), "dw_ada_kv")

    def update(w, m, v, grads, name, ncol=1):
        n = len(grads) // ncol
        width = w.shape[-1]
        three = lambda a: a.reshape((n, -1, width))
        outs = None
        for k, g in enumerate(grads):
            j, cp = divmod(k, ncol)
            g = g.reshape((g.shape[0], -1, width // ncol))
            if len(grads) == 1:
                outs = sp.run(adamw, three(w), three(m), three(v), j, g, name=f"{name}{k}")
            else:
                outs = adamw(three(w), three(m), three(v), j, g, f"{name}{k}", col=(cp, ncol), into=outs)
        return [o.reshape(w.shape) for o in outs]

    def one(w, m, v, g, name):
        return update(w, m, v, [g.reshape((1, -1, w.shape[-1]))], name)

    res = {}
    res["norm_gain"] = one(norm_gain, m_norm_gain, v_norm_gain, g_norm_gain, "adam_norm_gain")
    res["w_ada"] = one(w_ada, m_w_ada, v_w_ada, g_w_ada, "adam_w_ada")
    res["b_ada"] = one(b_ada, m_b_ada, v_b_ada, g_b_ada, "adam_b_ada")
    sent = {k.key if isinstance(k, _Part) else k for k in G}
    rest = [k for k in sp.source if k not in sent]
    if rest:
        G.update(zip(rest, exchange(scatter_side([(sp.source[k], None) for k in rest]), "scatter_rest")))
    g_ffn_in = [G[_rows(("in", l, i, hf), p)] for l in range(L) for i in range(2) for hf in range(2) for p in range(2)]
    g_ffn_out = [G[_cols(("out", l, i), p)] for l in range(L) for i in range(2) for p in range(2)]
    g_hgrn_in = [G[_rows(("hin", hf), p)] for hf in range(2) for p in range(2)]
    res["w_ffn_in"] = update(w_ffn_in, m_w_ffn_in, v_w_ffn_in, g_ffn_in, "adam_w_ffn_in")
    res["w_ffn_out"] = update(w_ffn_out, m_w_ffn_out, v_w_ffn_out, g_ffn_out, "adam_w_ffn_out", ncol=2)
    res["w_hgrn_in"] = update(w_hgrn_in, m_w_hgrn_in, v_w_hgrn_in, g_hgrn_in, "adam_w_hgrn_in")
    res["hgrn_lb_logits"] = one(hgrn_lb_logits, m_hgrn_lb_logits, v_hgrn_lb_logits, g_lb_logits, "adam_lb")
    res["hgrn_head_gain"] = one(hgrn_head_gain, m_hgrn_head_gain, v_hgrn_head_gain, g_head_gain, "adam_head_gain")
    res["w_hgrn_out"] = update(w_hgrn_out, m_w_hgrn_out, v_w_hgrn_out, [G["hout"]], "adam_w_hgrn_out")
    res["kv_gain"] = one(kv_gain, m_kv_gain, v_kv_gain, g_kv_gain, "adam_kv_gain")
    res["w_ada_kv"] = one(w_ada_kv, m_w_ada_kv, v_w_ada_kv, g_w_ada_kv, "adam_w_ada_kv")
    res["b_ada_kv"] = one(b_ada_kv, m_b_ada_kv, v_b_ada_kv, g_b_ada_kv, "adam_b_ada_kv")
    res["w_kv"] = update(w_kv, m_w_kv, v_w_kv, [G["kv"]], "adam_w_kv")
    res["b_kv"] = one(b_kv, m_b_kv, v_b_kv, g_b_kv, "adam_b_kv")
    res["w_q"] = update(w_q, m_w_q, v_w_q, [G["q"]], "adam_w_q")
    res["b_q"] = one(b_q, m_b_q, v_b_q, g_b_q, "adam_b_q")
    res["attn_sinks"] = one(attn_sinks, m_attn_sinks, v_attn_sinks, g_sinks, "adam_sinks")
    res["w_attn_out"] = update(w_attn_out, m_w_attn_out, v_w_attn_out, [G["o"]], "adam_w_attn_out")
    res["final_gain"] = one(final_gain, m_final_gain, v_final_gain, g_final_gain, "adam_final_gain")

    names = ["norm_gain", "w_ada", "b_ada", "w_ffn_in", "w_ffn_out", "w_hgrn_in", "hgrn_lb_logits", "hgrn_head_gain",
             "w_hgrn_out", "kv_gain", "w_ada_kv", "b_ada_kv", "w_kv", "b_kv", "w_q", "b_q", "attn_sinks", "w_attn_out",
             "final_gain"]
    return (loss, grad_x, *[res[n][0] for n in names], *[res[n][1] for n in names], *[res[n][2] for n in names],
            *[res[n][3] for n in names])
```

```python
import functools

import jax
import jax.numpy as jnp
from jax import lax
from jax.experimental import pallas as pl
from jax.experimental.pallas import tpu as pltpu

F32 = jnp.float32
BF16 = jnp.bfloat16
MESH = pl.DeviceIdType.MESH

N_DEV = 8
V7X_VMEM_LIMIT_BYTES = 56 * 1024 * 1024
LANES = 128

NORM_EPS = 1e-6
NEG_INF = -1e30
HGRN_CHUNK = 32
HGRN_HEAD = 128
ATT_HEAD = 64
ATT_WINDOW = 128
ATT_GROUP = 8
ROT_DIM = 16
ROPE_THETA = 500000.0

ADAM_LR = 0.001
ADAM_B1 = 0.9
ADAM_B2 = 0.999
ADAM_EPS = 1e-08
ADAM_WD = 0.01
ADAM_STEP = 10


def _params(*sem):
    return pltpu.CompilerParams(dimension_semantics=sem, vmem_limit_bytes=V7X_VMEM_LIMIT_BYTES)


def _tile(n, pref, unit=LANES):
    t = (min(n, pref) // unit) * unit
    while t >= unit:
        if n % t == 0:
            return t
        t -= unit
    return n


_ANY = pl.BlockSpec(memory_space=pl.ANY)


class _Side:
    def __init__(self, inputs, out_shapes, sem_shapes, start, finish):
        self.inputs, self.out_shapes, self.sem_shapes = list(inputs), list(out_shapes), list(sem_shapes)
        self.start, self.finish = start, finish


def _call(body, *, name, grid, in_specs, out_specs, out_shape, args, semantics, scratch_shapes=(), side=None):
    in_specs, out_specs, out_shape = list(in_specs), list(out_specs), list(out_shape)
    scratch_shapes = list(scratch_shapes)
    if side is None:
        outs = pl.pallas_call(
            body, name=name, grid=grid, in_specs=in_specs, out_specs=out_specs, out_shape=out_shape,
            scratch_shapes=scratch_shapes, compiler_params=_params(*semantics))(*args)
        return list(outs)
    n_in, n_out, n_scr = len(in_specs), len(out_specs), len(scratch_shapes)
    s_in, s_out = len(side.inputs), len(side.out_shapes)

    def carried(*refs):
        ins, refs = refs[:n_in], refs[n_in:]
        side_ins, refs = refs[:s_in], refs[s_in:]
        outs, refs = refs[:n_out], refs[n_out:]
        side_outs, refs = refs[:s_out], refs[s_out:]
        scratch, sems = refs[:n_scr], refs[n_scr:]
        step, steps = pl.program_id(0), grid[0]
        for d in range(1, len(grid)):
            step, steps = step * grid[d] + pl.program_id(d), steps * grid[d]

        @pl.when(step == 0)
        def _():
            side.start(side_ins, side_outs, sems)

        body(*ins, *outs, *scratch)

        @pl.when(step == steps - 1)
        def _():
            side.finish(side_ins, side_outs, sems)

    outs = pl.pallas_call(
        carried, name=name, grid=grid, in_specs=in_specs + [_ANY] * s_in, out_specs=out_specs + [_ANY] * s_out,
        out_shape=out_shape + side.out_shapes, scratch_shapes=scratch_shapes + side.sem_shapes,
        compiler_params=_params(*(["arbitrary"] * len(grid))))(*args, *side.inputs)
    return list(outs[:n_out]), list(outs[n_out:])


def _accumulate(prod, o_ref, acc_ref, k, nk):
    if nk == 1:
        o_ref[...] = prod.astype(o_ref.dtype)
        return

    @pl.when(k == 0)
    def _():
        acc_ref[...] = prod

    @pl.when(k > 0)
    def _():
        acc_ref[...] += prod

    @pl.when(k == nk - 1)
    def _():
        o_ref[...] = acc_ref[...].astype(o_ref.dtype)


def mm_nn(a3, b4, out_dtype, name, natural=False, a_natural=False, b_natural=False, jn=None, a_sel=None, bias=None,
          resid=None, tm=512, tk=None, kgroup=1, side=None):
    if b_natural:
        JK, JN, kb = 1, jn, b4.shape[0]
        nb = b4.shape[1] // JN
    else:
        JK, JN, kb, nb = b4.shape
    M = a3.shape[0] if a_natural else a3.shape[1]
    tm = _tile(M, tm, 16)
    tn = _tile(nb, 1024)
    tk = kb if tk is None else _tile(kb, tk)
    ntn, nkt = nb // tn, kb // tk
    nk = JK * nkt // kgroup

    def body(*refs):
        a_ref, b_ref = refs[:2]
        acc_ref = refs[-1]
        if kgroup == 1:
            prod = jnp.dot(a_ref[...].astype(BF16), b_ref[...].astype(BF16), preferred_element_type=F32)
        else:
            prod = jnp.dot(a_ref[0].astype(BF16), b_ref[0].astype(BF16), preferred_element_type=F32)
            for g in range(1, kgroup):
                prod += jnp.dot(a_ref[g].astype(BF16), b_ref[g].astype(BF16), preferred_element_type=F32)
        if bias is not None:
            prod = prod + refs[2][...]
        if resid is None:
            _accumulate(prod, refs[-2], acc_ref, pl.program_id(2), nk)
            return
        h_ref, g_ref, y_ref, o_ref = refs[2:6]
        k = pl.program_id(2)

        @pl.when(k == 0)
        def _():
            acc_ref[...] = prod

        @pl.when(k > 0)
        def _():
            acc_ref[...] += prod

        @pl.when(k == nk - 1)
        def _():
            y = acc_ref[...]
            y_ref[...] = y
            o_ref[...] = h_ref[...] + (resid[2] * g_ref[...]) * y

    if kgroup > 1:
        assert JN == 1 and nkt == 1 and not (a_natural or b_natural) and a_sel is None
        a3 = a3.reshape(nk, kgroup, M, kb)
        b4 = b4.reshape(nk, kgroup, kb, nb)
        a_spec = pl.BlockSpec((None, kgroup, tm, kb), lambda j, i, k: (k, 0, i, 0))
    elif a_natural:
        a_spec = pl.BlockSpec((tm, tk), lambda j, i, k: (i, k))
    elif a_sel is not None:
        a_spec = pl.BlockSpec((None, tm, tk), lambda j, i, k: (a_sel, i, k))
    else:
        a_spec = pl.BlockSpec((None, tm, tk), lambda j, i, k: (k // nkt, i, k % nkt))
    if kgroup > 1:
        b_spec = pl.BlockSpec((None, kgroup, kb, tn), lambda j, i, k: (k, 0, 0, j))
    elif b_natural:
        b_spec = pl.BlockSpec((tk, tn), lambda j, i, k: (k, j))
    else:
        b_spec = pl.BlockSpec((None, None, tk, tn), lambda j, i, k: (k // nkt, j // ntn, k % nkt, j % ntn))
    in_specs = [a_spec, b_spec]
    args = [a3, b4]
    if bias is not None:
        assert natural and nk == 1
        in_specs.append(pl.BlockSpec((1, tn), lambda j, i, k: (0, j)))
        args.append(bias)
    if natural:
        out_shape = jax.ShapeDtypeStruct((M, JN * nb), out_dtype)
        o_spec = pl.BlockSpec((tm, tn), lambda j, i, k: (i, j))
    else:
        out_shape = jax.ShapeDtypeStruct((JN, M, nb), out_dtype)
        o_spec = pl.BlockSpec((None, tm, tn), lambda j, i, k: (j // ntn, i, j % ntn))
    out_specs, out_shapes = [o_spec], [out_shape]
    if resid is not None:
        assert natural and bias is None and out_dtype == F32
        in_specs += [o_spec, pl.BlockSpec((1, tn), lambda j, i, k: (0, j))]
        args += [resid[0], resid[1]]
        out_specs, out_shapes = [o_spec, o_spec], [out_shape, out_shape]
    res = _call(body, name=name, grid=(JN * ntn, M // tm, nk), in_specs=in_specs, out_specs=out_specs,
                out_shape=out_shapes, scratch_shapes=[pltpu.VMEM((tm, tn), F32)], args=args,
                semantics=("parallel", "parallel", "arbitrary"), side=side)
    outs = res if side is None else res[0]
    ans = outs[0] if resid is None else (outs[0], outs[1])
    return ans if side is None else (ans, res[1])


def mm_nt(a3, b4, out_dtype, name, natural=False, a_natural=False, kgroup=1, side=None):
    JK, JN, nb, kb = b4.shape
    M = a3.shape[0] if a_natural else a3.shape[1]
    tm = min(M, 512)
    tn = _tile(nb, 1024)
    ntn = nb // tn
    nk = JK // kgroup

    def body(a_ref, b_ref, o_ref, acc_ref):
        nt = (((1,), (1,)), ((), ()))
        if kgroup == 1:
            prod = lax.dot_general(a_ref[...].astype(BF16), b_ref[...].astype(BF16), nt, preferred_element_type=F32)
        else:
            prod = lax.dot_general(a_ref[0].astype(BF16), b_ref[0].astype(BF16), nt, preferred_element_type=F32)
            for g in range(1, kgroup):
                prod += lax.dot_general(a_ref[g].astype(BF16), b_ref[g].astype(BF16), nt, preferred_element_type=F32)
        _accumulate(prod, o_ref, acc_ref, pl.program_id(2), nk)

    if kgroup > 1:
        assert JN == 1 and not a_natural
        a3 = a3.reshape(nk, kgroup, M, kb)
        b4 = b4.reshape(nk, kgroup, nb, kb)
        a_spec = pl.BlockSpec((None, kgroup, tm, kb), lambda j, i, k: (k, 0, i, 0))
        b_spec = pl.BlockSpec((None, kgroup, tn, kb), lambda j, i, k: (k, 0, j, 0))
    elif a_natural:
        a_spec = pl.BlockSpec((tm, kb), lambda j, i, k: (i, k))
        b_spec = pl.BlockSpec((None, None, tn, kb), lambda j, i, k: (k, j // ntn, j % ntn, 0))
    else:
        a_spec = pl.BlockSpec((None, tm, kb), lambda j, i, k: (k, i, 0))
        b_spec = pl.BlockSpec((None, None, tn, kb), lambda j, i, k: (k, j // ntn, j % ntn, 0))
    if natural:
        out_shape = jax.ShapeDtypeStruct((M, JN * nb), out_dtype)
        o_spec = pl.BlockSpec((tm, tn), lambda j, i, k: (i, j))
    else:
        out_shape = jax.ShapeDtypeStruct((JN, M, nb), out_dtype)
        o_spec = pl.BlockSpec((None, tm, tn), lambda j, i, k: (j // ntn, i, j % ntn))
    res = _call(body, name=name, grid=(JN * ntn, M // tm, nk), in_specs=[a_spec, b_spec], out_specs=[o_spec],
                out_shape=[out_shape], scratch_shapes=[pltpu.VMEM((tm, tn), F32)], args=[a3, b4],
                semantics=("parallel", "parallel", "arbitrary"), side=side)
    return res[0] if side is None else (res[0][0], res[1])


def _row_tile(S):
    return min(S, 256)


def _adaln(h, gain, shift, scale):
    y = h * lax.rsqrt(jnp.mean(h * h, axis=-1, keepdims=True) + NORM_EPS) * gain
    return y * (1.0 + scale) + shift


def adaln_fwd(h, gain, shift, scale, name):
    S, D = h.shape
    tm = _row_tile(S)

    def body(h_ref, g_ref, sh_ref, sc_ref, u_ref, ut_ref):
        u = _adaln(h_ref[...], g_ref[...], sh_ref[...], sc_ref[...])
        u_ref[...] = u.astype(BF16)
        ut_ref[...] = u.T.astype(BF16)

    row = pl.BlockSpec((tm, D), lambda i: (i, 0))
    vec = pl.BlockSpec((1, D), lambda i: (0, 0))
    return pl.pallas_call(
        body, name=name, grid=(S // tm,), in_specs=[row, vec, vec, vec],
        out_specs=[row, pl.BlockSpec((D, tm), lambda i: (0, i))],
        out_shape=[jax.ShapeDtypeStruct((S, D), BF16), jax.ShapeDtypeStruct((D, S), BF16)],
        compiler_params=_params("parallel"),
    )(h, gain, shift, scale)


def _gate_specs(S, D, tm, nxt):
    if nxt is None:
        return [], [], [], []
    row = pl.BlockSpec((tm, D), lambda i: (i, 0))
    vec = pl.BlockSpec((1, D), lambda i: (0, 0))
    return ([row, vec], [nxt[0], nxt[1]], [row, vec],
            [jax.ShapeDtypeStruct((S, D), BF16), jax.ShapeDtypeStruct((1, D), F32)])


def _gate_grads(dh, nxt, y_ref, g_ref, dy_ref, dg_ref):
    coef = nxt[2]
    dy_ref[...] = ((coef * g_ref[...]) * dh).astype(BF16)

    @pl.when(pl.program_id(0) == 0)
    def _():
        dg_ref[...] = jnp.zeros_like(dg_ref)

    dg_ref[...] += coef * jnp.sum(dh * y_ref[...], axis=0, keepdims=True)


def adaln_bwd(h, gain, shift, scale, du, dres, name, nxt=None):
    S, D = h.shape
    tm = _row_tile(S)

    def body(h_ref, g_ref, sh_ref, sc_ref, du_ref, dres_ref, *rest):
        dh_ref, dg_ref, dsh_ref, dsc_ref = rest[-6:-2] if nxt is not None else rest
        _, vjp = jax.vjp(_adaln, h_ref[...], g_ref[...], sh_ref[...], sc_ref[...])
        dh, dg, dsh, dsc = vjp(du_ref[...].astype(F32))
        dh = dres_ref[...] + dh
        dh_ref[...] = dh

        @pl.when(pl.program_id(0) == 0)
        def _():
            dg_ref[...] = jnp.zeros_like(dg_ref)
            dsh_ref[...] = jnp.zeros_like(dsh_ref)
            dsc_ref[...] = jnp.zeros_like(dsc_ref)

        dg_ref[...] += dg
        dsh_ref[...] += dsh
        dsc_ref[...] += dsc
        if nxt is not None:
            _gate_grads(dh, nxt, rest[0], rest[1], rest[-2], rest[-1])

    row = pl.BlockSpec((tm, D), lambda i: (i, 0))
    vec = pl.BlockSpec((1, D), lambda i: (0, 0))
    vs = jax.ShapeDtypeStruct((1, D), F32)
    xin, xargs, xout, xshape = _gate_specs(S, D, tm, nxt)
    return pl.pallas_call(
        body, name=name, grid=(S // tm,), in_specs=[row, vec, vec, vec, row, row] + xin,
        out_specs=[row, vec, vec, vec] + xout, out_shape=[jax.ShapeDtypeStruct((S, D), F32), vs, vs, vs] + xshape,
        compiler_params=_params("arbitrary"),
    )(h, gain, shift, scale, du, dres, *xargs)


def _swiglu(a, b):
    return a * jax.nn.sigmoid(a) * b


def ffn_in_act(u, w_in, name, side=None):
    J2, D, nb = w_in.shape
    J = J2 // 2
    S = u.shape[0]
    tm = _tile(S, 512, 16)

    def body(u_ref, wa_ref, wb_ref, ab_ref, h_ref, ht_ref):
        uv = u_ref[...]
        a = jnp.dot(uv, wa_ref[...], preferred_element_type=F32)
        b = jnp.dot(uv, wb_ref[...], preferred_element_type=F32)
        ab_ref[0] = a
        ab_ref[1] = b
        hv = _swiglu(a, b)
        h_ref[...] = hv.astype(BF16)
        ht_ref[...] = hv.T.astype(BF16)

    res = _call(body, name=name, grid=(J, S // tm),
                in_specs=[pl.BlockSpec((tm, D), lambda j, i: (i, 0)),
                          pl.BlockSpec((None, D, nb), lambda j, i: (j, 0, 0)),
                          pl.BlockSpec((None, D, nb), lambda j, i: (j + J, 0, 0))],
                out_specs=[pl.BlockSpec((2, None, tm, nb), lambda j, i: (0, j, i, 0)),
                           pl.BlockSpec((None, tm, nb), lambda j, i: (j, i, 0)),
                           pl.BlockSpec((None, nb, tm), lambda j, i: (j, 0, i))],
                out_shape=[jax.ShapeDtypeStruct((2, J, S, nb), F32), jax.ShapeDtypeStruct((J, S, nb), BF16),
                           jax.ShapeDtypeStruct((J, nb, S), BF16)],
                args=[u, w_in, w_in], semantics=("parallel", "parallel"), side=side)
    return res if side is None else (res[0], res[1])


def ffn_dact(dy, w_out4, ab4, name, side=None):
    J, nb, D = w_out4.shape
    S = dy.shape[0]
    tm = _tile(S, 512, 16)

    def body(dy_ref, w_ref, ab_ref, o_ref):
        dh = lax.dot_general(dy_ref[...], w_ref[...], (((1,), (1,)), ((), ())), preferred_element_type=F32)
        _, vjp = jax.vjp(_swiglu, ab_ref[0], ab_ref[1])
        da, db = vjp(dh)
        o_ref[0] = da.astype(BF16)
        o_ref[1] = db.astype(BF16)

    both = pl.BlockSpec((2, None, tm, nb), lambda j, i: (0, j, i, 0))
    res = _call(body, name=name, grid=(J, S // tm),
                in_specs=[pl.BlockSpec((tm, D), lambda j, i: (i, 0)),
                          pl.BlockSpec((None, nb, D), lambda j, i: (j, 0, 0)), both],
                out_specs=[both], out_shape=[jax.ShapeDtypeStruct((2, J, S, nb), BF16)],
                args=[dy, w_out4, ab4], semantics=("parallel", "parallel"), side=side)
    if side is None:
        return res[0].reshape(2 * J, S, nb)
    return res[0][0].reshape(2 * J, S, nb), res[1]


def colsum(x, name):
    S, N = x.shape
    tm = _row_tile(S)

    def body(x_ref, o_ref):
        @pl.when(pl.program_id(0) == 0)
        def _():
            o_ref[...] = jnp.zeros_like(o_ref)

        o_ref[...] += jnp.sum(x_ref[...].astype(F32), axis=0, keepdims=True)

    return pl.pallas_call(
        body, name=name, grid=(S // tm,), in_specs=[pl.BlockSpec((tm, N), lambda i: (i, 0))],
        out_specs=pl.BlockSpec((1, N), lambda i: (0, 0)), out_shape=jax.ShapeDtypeStruct((1, N), F32),
        compiler_params=_params("arbitrary"),
    )(x)


def _final_loss(h, gain, target):
    y = h * lax.rsqrt(jnp.mean(h * h, axis=-1, keepdims=True) + NORM_EPS) * gain
    err = y - target
    return 0.5 * jnp.sum(jnp.mean(err * err, axis=-1))


def final_loss_grad(h, gain, target, name, nxt):
    S, D = h.shape
    tm = _row_tile(S)

    def body(h_ref, g_ref, t_ref, y_ref, gate_ref, loss_ref, dh_ref, dg_ref, dy_ref, dgate_ref):
        loss, (dh, dg) = jax.value_and_grad(_final_loss, argnums=(0, 1))(h_ref[...], g_ref[...], t_ref[...])
        dh_ref[...] = dh

        @pl.when(pl.program_id(0) == 0)
        def _():
            loss_ref[...] = jnp.zeros_like(loss_ref)
            dg_ref[...] = jnp.zeros_like(dg_ref)

        loss_ref[...] += jnp.full(loss_ref.shape, loss, F32)
        dg_ref[...] += dg
        _gate_grads(dh, nxt, y_ref, gate_ref, dy_ref, dgate_ref)

    row = pl.BlockSpec((tm, D), lambda i: (i, 0))
    vec = pl.BlockSpec((1, D), lambda i: (0, 0))
    xin, xargs, xout, xshape = _gate_specs(S, D, tm, nxt)
    return pl.pallas_call(
        body, name=name, grid=(S // tm,), in_specs=[row, vec, row] + xin,
        out_specs=[pl.BlockSpec((1, LANES), lambda i: (0, 0)), row, vec] + xout,
        out_shape=[jax.ShapeDtypeStruct((1, LANES), F32), jax.ShapeDtypeStruct((S, D), F32),
                   jax.ShapeDtypeStruct((1, D), F32)] + xshape,
        compiler_params=_params("arbitrary"),
    )(h, gain, target, *xargs)


def _chunk_consts(H):
    C = HGRN_CHUNK
    t = lax.broadcasted_iota(jnp.int32, (H, C, C), 1)
    s = lax.broadcasted_iota(jnp.int32, (H, C, C), 2)
    return (s <= t).astype(F32), s <= t


def _hgrn_chunk(q_raw, f_raw, i_raw, lb, st):
    H, C, _ = q_raw.shape
    lower, causal = _chunk_consts(H)
    forget = lb + (1.0 - lb) * jax.nn.sigmoid(f_raw)
    g = jnp.log(forget)
    kk = 1.0 - forget
    qs = q_raw * jax.nn.sigmoid(q_raw)
    bnn = (((2,), (1,)), ((0,), (0,)))
    bnt = (((2,), (2,)), ((0,), (0,)))
    btn = (((1,), (1,)), ((0,), (0,)))
    b = lax.dot_general(lower, g, bnn, precision=lax.Precision.HIGHEST, preferred_element_type=F32)
    bm = b[:, C // 2 - 1:C // 2, :]
    bl = b[:, C - 1:C, :]
    inter = lax.dot_general((qs * jnp.exp(b)).astype(BF16), st.astype(BF16), bnt, preferred_element_type=F32)
    qt = (qs * jnp.exp(b - bm)).astype(BF16)
    kt = (kk * jnp.exp(bm - b)).astype(BF16)
    scores = lax.dot_general(qt, kt, bnt, preferred_element_type=F32)
    scores = jnp.where(causal, scores, 0.0)
    vb = i_raw.astype(BF16)
    out = inter + lax.dot_general(scores.astype(BF16), vb, bnn, preferred_element_type=F32)
    kdec = (kk * jnp.exp(bl - b)).astype(BF16)
    new_st = st * jnp.exp(bl) + lax.dot_general(vb, kdec, btn, preferred_element_type=F32)
    return out, new_st


def _heads(ref, rows, H):
    return jnp.stack([ref[rows, pl.ds(h * HGRN_HEAD, HGRN_HEAD)] for h in range(H)])


def hgrn_scan_fwd(proj, lb, name, side=None):
    S, D4 = proj.shape
    D = D4 // 4
    H = D // HGRN_HEAD
    C = HGRN_CHUNK
    R = min(S, 128)
    ncr = R // C

    def body(q_ref, f_ref, i_ref, lb_ref, o_ref, st_ref, state):
        @pl.when(pl.program_id(0) == 0)
        def _():
            state[...] = jnp.zeros_like(state)

        lbh = _heads(lb_ref, slice(None), H)

        def chunk(cc, carry):
            rows = pl.ds(pl.multiple_of(cc * C, C), C)
            st = state[...]
            st_ref[cc] = st
            out, new_st = _hgrn_chunk(_heads(q_ref, rows, H), _heads(f_ref, rows, H), _heads(i_ref, rows, H), lbh, st)
            for h in range(H):
                o_ref[rows, pl.ds(h * HGRN_HEAD, HGRN_HEAD)] = out[h]
            state[...] = new_st
            return carry

        lax.fori_loop(0, ncr, chunk, 0)

    col = lambda j: pl.BlockSpec((R, D), lambda i: (i, j))
    res = _call(body, name=name, grid=(S // R,),
                in_specs=[col(0), col(1), col(2), pl.BlockSpec((1, D), lambda i: (0, 0))],
                out_specs=[pl.BlockSpec((R, D), lambda i: (i, 0)),
                           pl.BlockSpec((ncr, H, HGRN_HEAD, HGRN_HEAD), lambda i: (i, 0, 0, 0))],
                out_shape=[jax.ShapeDtypeStruct((S, D), F32),
                           jax.ShapeDtypeStruct((S // C, H, HGRN_HEAD, HGRN_HEAD), F32)],
                scratch_shapes=[pltpu.VMEM((H, HGRN_HEAD, HGRN_HEAD), F32)], args=[proj, proj, proj, lb],
                semantics=("arbitrary",), side=side)
    return res if side is None else (res[0], res[1])


def hgrn_scan_bwd(proj, lb, states, do, dg, name, side=None):
    S, D4 = proj.shape
    D = D4 // 4
    H = D // HGRN_HEAD
    C = HGRN_CHUNK
    R = min(S, 128)
    ncr = R // C
    ng = S // R

    def body(q_ref, f_ref, i_ref, lb_ref, st_ref, do_ref, dg_ref, dp_ref, dlb_ref, dstate):
        @pl.when(pl.program_id(0) == 0)
        def _():
            dstate[...] = jnp.zeros_like(dstate)
            dlb_ref[...] = jnp.zeros_like(dlb_ref)

        dp_ref[:, pl.ds(3 * D, D)] = dg_ref[...].astype(BF16)
        lbh = _heads(lb_ref, slice(None), H)

        def chunk(t, carry):
            cc = ncr - 1 - t
            rows = pl.ds(pl.multiple_of(cc * C, C), C)
            _, vjp = jax.vjp(_hgrn_chunk, _heads(q_ref, rows, H), _heads(f_ref, rows, H), _heads(i_ref, rows, H),
                             lbh, st_ref[cc])
            dq, df, di, dlb, dst = vjp((_heads(do_ref, rows, H), dstate[...]))
            for h in range(H):
                dp_ref[rows, pl.ds(h * HGRN_HEAD, HGRN_HEAD)] = dq[h].astype(BF16)
                dp_ref[rows, pl.ds(D + h * HGRN_HEAD, HGRN_HEAD)] = df[h].astype(BF16)
                dp_ref[rows, pl.ds(2 * D + h * HGRN_HEAD, HGRN_HEAD)] = di[h].astype(BF16)
                dlb_ref[:, pl.ds(h * HGRN_HEAD, HGRN_HEAD)] += dlb[h]
            dstate[...] = dst
            return carry

        lax.fori_loop(0, ncr, chunk, 0)

    col = lambda j: pl.BlockSpec((R, D), lambda i: (ng - 1 - i, j))
    res = _call(body, name=name, grid=(ng,),
                in_specs=[col(0), col(1), col(2), pl.BlockSpec((1, D), lambda i: (0, 0)),
                          pl.BlockSpec((ncr, H, HGRN_HEAD, HGRN_HEAD), lambda i: (ng - 1 - i, 0, 0, 0)),
                          pl.BlockSpec((R, D), lambda i: (ng - 1 - i, 0)),
                          pl.BlockSpec((R, D), lambda i: (ng - 1 - i, 0))],
                out_specs=[pl.BlockSpec((R, D4), lambda i: (ng - 1 - i, 0)), pl.BlockSpec((1, D), lambda i: (0, 0))],
                out_shape=[jax.ShapeDtypeStruct((S, D4), BF16), jax.ShapeDtypeStruct((1, D), F32)],
                scratch_shapes=[pltpu.VMEM((H, HGRN_HEAD, HGRN_HEAD), F32)],
                args=[proj, proj, proj, lb, states, do, dg], semantics=("arbitrary",), side=side)
    return res if side is None else (res[0], res[1])


def _head_out(o, g, gain):
    y = o * lax.rsqrt(jnp.mean(o * o, axis=-1, keepdims=True) + NORM_EPS) * gain
    return y * jax.nn.sigmoid(g)


def hgrn_post_fwd(o, proj, gain, name):
    S, D = o.shape
    H = D // HGRN_HEAD
    tm = _row_tile(S)

    def body(o_ref, g_ref, gain_ref, z_ref, zt_ref):
        for h in range(H):
            ls = pl.ds(h * HGRN_HEAD, HGRN_HEAD)
            z = _head_out(o_ref[:, ls], g_ref[:, ls], gain_ref[:, ls])
            z_ref[:, ls] = z.astype(BF16)
            zt_ref[ls, :] = z.T.astype(BF16)

    row = pl.BlockSpec((tm, D), lambda i: (i, 0))
    return pl.pallas_call(
        body, name=name, grid=(S // tm,),
        in_specs=[row, pl.BlockSpec((tm, D), lambda i: (i, 3)), pl.BlockSpec((1, D), lambda i: (0, 0))],
        out_specs=[row, pl.BlockSpec((D, tm), lambda i: (0, i))],
        out_shape=[jax.ShapeDtypeStruct((S, D), BF16), jax.ShapeDtypeStruct((D, S), BF16)],
        compiler_params=_params("parallel"),
    )(o, proj, gain)


def hgrn_post_bwd(o, proj, gain, dz, name):
    S, D = o.shape
    H = D // HGRN_HEAD
    tm = _row_tile(S)

    def body(o_ref, g_ref, gain_ref, dz_ref, do_ref, dg_ref, dgain_ref):
        @pl.when(pl.program_id(0) == 0)
        def _():
            dgain_ref[...] = jnp.zeros_like(dgain_ref)

        for h in range(H):
            ls = pl.ds(h * HGRN_HEAD, HGRN_HEAD)
            _, vjp = jax.vjp(_head_out, o_ref[:, ls], g_ref[:, ls], gain_ref[:, ls])
            do, dg, dgain = vjp(dz_ref[:, ls].astype(F32))
            do_ref[:, ls] = do
            dg_ref[:, ls] = dg
            dgain_ref[:, ls] += dgain

    row = pl.BlockSpec((tm, D), lambda i: (i, 0))
    vec = pl.BlockSpec((1, D), lambda i: (0, 0))
    return pl.pallas_call(
        body, name=name, grid=(S // tm,),
        in_specs=[row, pl.BlockSpec((tm, D), lambda i: (i, 3)), vec, row], out_specs=[row, row, vec],
        out_shape=[jax.ShapeDtypeStruct((S, D), F32), jax.ShapeDtypeStruct((S, D), F32),
                   jax.ShapeDtypeStruct((1, D), F32)],
        compiler_params=_params("arbitrary"),
    )(o, proj, gain, dz)


def lower_bound_fwd(l0, l1, name):
    def body(a_ref, b_ref, o_ref):
        o_ref[...] = jax.nn.sigmoid(a_ref[...] - b_ref[...])

    return pl.pallas_call(body, name=name, out_shape=jax.ShapeDtypeStruct(l0.shape, F32))(l0, l1)


def lower_bound_bwd(l0, l1, dlb, name):
    def body(a_ref, b_ref, d_ref, o0_ref, o1_ref):
        s = jax.nn.sigmoid(a_ref[...] - b_ref[...])
        d0 = d_ref[...] * s * (1.0 - s)
        o0_ref[...] = d0
        o1_ref[...] = -d0

    sd = jax.ShapeDtypeStruct(l0.shape, F32)
    return pl.pallas_call(body, name=name, out_shape=[sd, sd])(l0, l1, dlb)


def _attn_tile(q, kp, kc, vp, vc, sink, first):
    W = ATT_WINDOW
    nt = (((1,), (1,)), ((), ()))
    qb = q.astype(BF16)
    scale = ATT_HEAD ** -0.5
    sp = lax.dot_general(qb, kp.astype(BF16), nt, preferred_element_type=F32) * scale
    sc = lax.dot_general(qb, kc.astype(BF16), nt, preferred_element_type=F32) * scale
    qi = lax.broadcasted_iota(jnp.int32, sp.shape, 0) & (W - 1)
    kj = lax.broadcasted_iota(jnp.int32, sp.shape, 1)
    sp = jnp.where((kj > qi) & jnp.logical_not(first), sp, NEG_INF)
    sc = jnp.where(kj <= qi, sc, NEG_INF)
    m = jnp.maximum(jnp.maximum(jnp.max(sp, axis=-1, keepdims=True), jnp.max(sc, axis=-1, keepdims=True)), sink)
    pp = jnp.exp(sp - m)
    pc = jnp.exp(sc - m)
    denom = jnp.sum(pp, axis=-1, keepdims=True) + jnp.sum(pc, axis=-1, keepdims=True) + jnp.exp(sink - m)
    out = jnp.dot((pp / denom).astype(BF16), vp.astype(BF16), preferred_element_type=F32)
    return out + jnp.dot((pc / denom).astype(BF16), vc.astype(BF16), preferred_element_type=F32)


def _attn_specs(G, W, Dh):
    q_spec = pl.BlockSpec((None, G, W, Dh), lambda j, n: (j, 0, n, 0))
    prev = pl.BlockSpec((None, W, Dh), lambda j, n: (j, jnp.maximum(n - 1, 0), 0))
    cur = pl.BlockSpec((None, W, Dh), lambda j, n: (j, n, 0))
    sink = pl.BlockSpec((None, G * W, 1), lambda j, n: (j, 0, 0))
    return q_spec, prev, cur, sink


def attn_fwd(q4, k3, v3, sink, name, side=None):
    NKV, G, S, Dh = q4.shape
    W = ATT_WINDOW

    def body(q_ref, kp_ref, kc_ref, vp_ref, vc_ref, s_ref, o_ref):
        first = pl.program_id(1) == 0
        out = _attn_tile(q_ref[...].reshape(G * W, Dh), kp_ref[...], kc_ref[...], vp_ref[...], vc_ref[...],
                         s_ref[...], first)
        o_ref[...] = out.reshape(G, W, Dh)

    q_spec, prev, cur, sk = _attn_specs(G, W, Dh)
    res = _call(body, name=name, grid=(NKV, S // W), in_specs=[q_spec, prev, cur, prev, cur, sk], out_specs=[q_spec],
                out_shape=[jax.ShapeDtypeStruct(q4.shape, F32)], args=[q4, k3, k3, v3, v3, sink],
                semantics=("parallel", "parallel"), side=side)
    return res[0] if side is None else (res[0][0], res[1])


def attn_bwd(q4, k3, v3, sink, do4, name, side=None):
    NKV, G, S, Dh = q4.shape
    W = ATT_WINDOW
    nb = S // W

    def body(q_ref, kp_ref, kc_ref, vp_ref, vc_ref, s_ref, do_ref, dq_ref, dkp_ref, dkc_ref, dvp_ref, dvc_ref,
             ds_ref):
        first = pl.program_id(1) == 0
        _, vjp = jax.vjp(functools.partial(_attn_tile, first=first), q_ref[...].reshape(G * W, Dh), kp_ref[...],
                         kc_ref[...], vp_ref[...], vc_ref[...], s_ref[...])
        dq, dkp, dkc, dvp, dvc, ds = vjp(do_ref[...].reshape(G * W, Dh))
        dq_ref[...] = dq.reshape(G, W, Dh)
        dkp_ref[...] = dkp
        dkc_ref[...] = dkc
        dvp_ref[...] = dvp
        dvc_ref[...] = dvc

        @pl.when(first)
        def _():
            ds_ref[...] = jnp.zeros_like(ds_ref)

        ds_ref[...] += jnp.sum(ds.reshape(G, W, 1), axis=1)

    q_spec, prev, cur, sk = _attn_specs(G, W, Dh)
    part = pl.BlockSpec((None, None, W, Dh), lambda j, n: (j, n, 0, 0))
    ps = jax.ShapeDtypeStruct((NKV, nb, W, Dh), F32)
    res = _call(body, name=name, grid=(NKV, nb), in_specs=[q_spec, prev, cur, prev, cur, sk, q_spec],
                out_specs=[q_spec, part, part, part, part, pl.BlockSpec((None, G, 1), lambda j, n: (j, 0, 0))],
                out_shape=[jax.ShapeDtypeStruct(q4.shape, F32), ps, ps, ps, ps,
                           jax.ShapeDtypeStruct((NKV, G, 1), F32)],
                args=[q4, k3, k3, v3, v3, sink, do4], semantics=("parallel", "arbitrary"), side=side)
    outs = res if side is None else res[0]
    ans = (outs[0], outs[1:5], outs[5])
    return ans if side is None else (ans, res[1])


def band_combine(kc, kp, vc, vp, name):
    NKV, nb, W, Dh = kc.shape

    def shifted_sum(c_ref, p_ref, o_ref):
        o_ref[pl.ds(0, nb - 1)] = c_ref[pl.ds(0, nb - 1)] + p_ref[pl.ds(1, nb - 1)]
        o_ref[nb - 1] = c_ref[nb - 1]

    def body(kc_ref, kp_ref, vc_ref, vp_ref, dk_ref, dv_ref):
        shifted_sum(kc_ref, kp_ref, dk_ref)
        shifted_sum(vc_ref, vp_ref, dv_ref)

    spec = pl.BlockSpec((None, nb, W, Dh), lambda j: (j, 0, 0, 0))
    sd = jax.ShapeDtypeStruct((NKV, nb, W, Dh), F32)
    dk, dv = pl.pallas_call(
        body, name=name, grid=(NKV,), in_specs=[spec] * 4, out_specs=[spec] * 2, out_shape=[sd, sd],
        compiler_params=_params("parallel"),
    )(kc, kp, vc, vp)
    return dk.reshape(NKV, nb * W, Dh), dv.reshape(NKV, nb * W, Dh)


def rope_tables(S):
    half = ROT_DIM // 2
    inv_freq = jnp.power(jnp.float32(ROPE_THETA), -jnp.arange(0, ROT_DIM, 2, dtype=F32) / ROT_DIM)
    ang = jnp.arange(S, dtype=F32)[:, None] * inv_freq[None, :]
    sin, cos = jnp.sin(ang), jnp.cos(ang)
    zeros = jnp.zeros((S, ATT_HEAD - ROT_DIM), F32)
    z8 = jnp.zeros((S, half), F32)
    cfull = jnp.concatenate([cos, cos, jnp.ones((S, ATT_HEAD - ROT_DIM), F32)], axis=1)
    s_next = jnp.concatenate([-sin, z8, zeros], axis=1)
    s_prev = jnp.concatenate([z8, sin, zeros], axis=1)
    two = lambda t: jnp.concatenate([t, t], axis=1)
    return two(cfull), two(s_next), two(s_prev)


def rope(x, tables, sign, name):
    S, Wd = x.shape
    tm = _row_tile(S)
    rep = Wd // LANES
    half = ROT_DIM // 2

    def body(x_ref, c_ref, sn_ref, sp_ref, o_ref):
        xv = x_ref[...]
        c = jnp.tile(c_ref[...], (1, rep))
        sn = jnp.tile(sn_ref[...], (1, rep))
        sp = jnp.tile(sp_ref[...], (1, rep))
        if sign > 0:
            nxt = pltpu.roll(xv, Wd - half, 1)
            prv = pltpu.roll(xv, half, 1)
            o_ref[...] = xv * c + nxt * sn + prv * sp
        else:
            o_ref[...] = xv * c + pltpu.roll(xv * sn, half, 1) + pltpu.roll(xv * sp, Wd - half, 1)

    row = pl.BlockSpec((tm, Wd), lambda i: (i, 0))
    tab = pl.BlockSpec((tm, LANES), lambda i: (i, 0))
    return pl.pallas_call(
        body, name=name, grid=(S // tm,), in_specs=[row, tab, tab, tab], out_specs=row,
        out_shape=jax.ShapeDtypeStruct((S, Wd), F32), compiler_params=_params("parallel"),
    )(x, *tables)


def cond_proj(c_all, w, bias, name):
    B, D = c_all.shape
    N = w.shape[1]
    tn = _tile(N, 512)

    def body(c_ref, w_ref, b_ref, o_ref):
        cv = c_ref[...]
        cs = (cv * jax.nn.sigmoid(cv)).astype(BF16)
        o_ref[...] = jnp.dot(cs, w_ref[...].astype(BF16), preferred_element_type=F32) + b_ref[...]

    return pl.pallas_call(
        body, name=name, grid=(N // tn,),
        in_specs=[pl.BlockSpec((B, D), lambda j: (0, 0)), pl.BlockSpec((D, tn), lambda j: (0, j)),
                  pl.BlockSpec((1, tn), lambda j: (0, j))],
        out_specs=pl.BlockSpec((B, tn), lambda j: (0, j)), out_shape=jax.ShapeDtypeStruct((B, N), F32),
        compiler_params=_params("parallel"),
    )(c_all, w, bias)


def cond_grad(c_all, dmod, name):
    B, D = c_all.shape
    N = dmod.shape[1]
    tn = _tile(N, 512)

    def body(c_ref, d_ref, o_ref):
        cv = c_ref[...]
        cs = (cv * jax.nn.sigmoid(cv)).astype(BF16)
        o_ref[...] = lax.dot_general(cs, d_ref[...].astype(BF16), (((0,), (0,)), ((), ())),
                                     preferred_element_type=F32)

    return pl.pallas_call(
        body, name=name, grid=(N // tn,),
        in_specs=[pl.BlockSpec((B, D), lambda j: (0, 0)), pl.BlockSpec((B, tn), lambda j: (0, j))],
        out_specs=pl.BlockSpec((D, tn), lambda j: (0, j)), out_shape=jax.ShapeDtypeStruct((D, N), F32),
        compiler_params=_params("parallel"),
    )(c_all, dmod)


def rowsum(g, name):
    B, N = g.shape
    tn = N if B * N * 4 <= (4 << 20) else _tile(N, 8192)

    def body(g_ref, o_ref):
        acc = g_ref[0:1, :]
        for r in range(1, B):
            acc = acc + g_ref[r:r + 1, :]
        o_ref[...] = acc

    return pl.pallas_call(
        body, name=name, grid=(N // tn,), in_specs=[pl.BlockSpec((B, tn), lambda j: (0, j))],
        out_specs=pl.BlockSpec((1, tn), lambda j: (0, j)), out_shape=jax.ShapeDtypeStruct((1, N), F32),
        compiler_params=_params("parallel"),
    )(g)


def _adam_rows(R, C):
    limit = 3 << 19
    if R * C * 4 <= limit or R % 8:
        return R
    best = 8
    for t in range(8, R + 1, 8):
        if R % t == 0 and t * C * 4 <= limit:
            best = t
    return best


def adamw(w3, m3, v3, j, g3, name, col=(0, 1), into=None, side=None):
    n, R, C = w3.shape
    P = g3.shape[0]
    cp, ncol = col
    Cp = C // ncol
    tr = _adam_rows(R, Cp)

    def body(*refs):
        w_ref, m_ref, v_ref, g_ref = refs[:4]
        go_ref, d_ref, mo_ref, vo_ref = refs[-4:]
        g = g_ref[0].astype(F32)
        for p in range(1, P):
            g = g + g_ref[p].astype(F32)
        mn = ADAM_B1 * m_ref[...] + (1.0 - ADAM_B1) * g
        vn = ADAM_B2 * v_ref[...] + (1.0 - ADAM_B2) * jnp.square(g)
        m_hat = mn / (1.0 - ADAM_B1 ** ADAM_STEP)
        v_hat = vn / (1.0 - ADAM_B2 ** ADAM_STEP)
        go_ref[...] = g
        d_ref[...] = -ADAM_LR * (m_hat / (jnp.sqrt(v_hat) + ADAM_EPS) + ADAM_WD * w_ref[...])
        mo_ref[...] = mn
        vo_ref[...] = vn

    spec = pl.BlockSpec((None, tr, Cp), lambda i: (j, i, cp))
    g_spec = pl.BlockSpec((P, tr, Cp), lambda i: (0, i, 0))
    sd = jax.ShapeDtypeStruct((n, R, C), F32)
    in_specs, args, aliases = [spec, spec, spec, g_spec], [w3, m3, v3, g3], {}
    if side is not None:
        assert into is None
        return _call(body, name=name, grid=(R // tr,), in_specs=in_specs, out_specs=[spec] * 4, out_shape=[sd] * 4,
                     args=args, semantics=("parallel",), side=side)
    if into is not None:
        in_specs += [_ANY] * 4
        args += list(into)
        aliases = {4 + k: k for k in range(4)}
    return pl.pallas_call(
        body, name=name, grid=(R // tr,), in_specs=in_specs, out_specs=[spec] * 4, out_shape=[sd] * 4,
        input_output_aliases=aliases, compiler_params=_params("parallel"),
    )(*args)


def _place():
    return lax.axis_index("x"), lax.axis_index("y"), lax.axis_index("c")


def _slot(p):
    return 4 * p[0] + 2 * p[1] + p[2]


def gather_side(items):
    xs = [a for a, _ in items]
    n = len(xs)

    def copy(ins, outs, sems, t, k, block, to, from_input=False):
        dst = outs[t].at[_slot(block)]
        return pltpu.make_async_remote_copy(
            src_ref=ins[t] if from_input else dst, dst_ref=dst, send_sem=sems[0].at[t, k],
            recv_sem=sems[1].at[t, k], device_id=to, device_id_type=MESH)

    def peers():
        x, y, c = _place()
        return (x, y, c), (x, y, 1 - c), [(1 - x, y), (x, 1 - y), (1 - x, 1 - y)]

    def start(ins, outs, sems):
        me, sibling, chips = peers()
        c = me[2]
        for t in range(n):
            pltpu.make_async_copy(ins[t], outs[t].at[_slot(me)], sems[2].at[t]).start()
            copy(ins, outs, sems, t, 0, me, sibling, True).start()
            for j, chip in enumerate(chips):
                copy(ins, outs, sems, t, 1 + j, me, (*chip, c), True).start()

    def finish(ins, outs, sems):
        me, sibling, chips = peers()
        c = me[2]
        for t in range(n):
            for j, chip in enumerate(chips):
                copy(ins, outs, sems, t, 1 + j, (*chip, c), me).wait_recv()
                copy(ins, outs, sems, t, 4 + j, (*chip, c), sibling).start()
        for t in range(n):
            copy(ins, outs, sems, t, 0, sibling, me).wait_recv()
            for j, chip in enumerate(chips):
                copy(ins, outs, sems, t, 4 + j, (*chip, 1 - c), me).wait_recv()
        for t in range(n):
            copy(ins, outs, sems, t, 0, me, sibling, True).wait_send()
            for j, chip in enumerate(chips):
                copy(ins, outs, sems, t, 1 + j, me, (*chip, c), True).wait_send()
                copy(ins, outs, sems, t, 4 + j, (*chip, c), sibling).wait_send()
            pltpu.make_async_copy(ins[t], outs[t].at[_slot(me)], sems[2].at[t]).wait()

    return _Side(xs, [jax.ShapeDtypeStruct((N_DEV,) + a.shape, a.dtype) for a in xs],
                 [pltpu.SemaphoreType.DMA((n, 7)), pltpu.SemaphoreType.DMA((n, 7)), pltpu.SemaphoreType.DMA((n,))],
                 start, finish)


def scatter_side(items):
    gs = [g for g, _ in items]
    n = len(gs)

    def part_shape(g, part):
        _, R, C = g.shape
        if part is None:
            return R, C
        axis, _, cnt = part
        return (R // cnt, C) if axis == 0 else (R, C // cnt)

    def block(ref, slot, g, part):
        if part is None:
            return ref.at[slot]
        axis, idx, cnt = part
        R, C = part_shape(g, part)
        return ref.at[slot, pl.ds(idx * R, R)] if axis == 0 else ref.at[slot, :, pl.ds(idx * C, C)]

    def copies(ins, outs, sems):
        x, y, c = _place()
        me = (x, y, c)
        out = []
        for t, (g, part) in enumerate(items):
            out.append(pltpu.make_async_copy(block(ins[t], _slot(me), g, part), outs[t].at[_slot(me)], sems[2].at[t]))
            for r in range(1, N_DEV):
                peer = (1 - x if r & 4 else x, 1 - y if r & 2 else y, 1 - c if r & 1 else c)
                out.append(pltpu.make_async_remote_copy(
                    src_ref=block(ins[t], _slot(peer), g, part), dst_ref=outs[t].at[_slot(me)],
                    send_sem=sems[0].at[t, r - 1], recv_sem=sems[1].at[t, r - 1], device_id=peer, device_id_type=MESH))
        return out

    def start(ins, outs, sems):
        for cp in copies(ins, outs, sems):
            cp.start()

    def finish(ins, outs, sems):
        for cp in copies(ins, outs, sems):
            cp.wait()

    return _Side(gs, [jax.ShapeDtypeStruct((N_DEV,) + part_shape(g, part), g.dtype) for g, part in items],
                 [pltpu.SemaphoreType.DMA((n, 7)), pltpu.SemaphoreType.DMA((n, 7)), pltpu.SemaphoreType.DMA((n,))],
                 start, finish)


def exchange(side, name):
    def body(*refs):
        n_in, n_out = len(side.inputs), len(side.out_shapes)
        ins, outs, sems = refs[:n_in], refs[n_in:n_in + n_out], refs[n_in + n_out:]
        side.start(ins, outs, sems)
        side.finish(ins, outs, sems)

    return pl.pallas_call(
        body, name=name, in_specs=[_ANY] * len(side.inputs), out_specs=[_ANY] * len(side.out_shapes),
        out_shape=side.out_shapes, scratch_shapes=side.sem_shapes,
    )(*side.inputs)


class _Part:
    def __init__(self, key, axis, index, count):
        self.key, self.axis, self.index, self.count = key, axis, index, count

    def __hash__(self):
        return hash((self.key, self.axis, self.index, self.count))

    def __eq__(self, other):
        return isinstance(other, _Part) and (self.key, self.axis, self.index, self.count) == (
            other.key, other.axis, other.index, other.count)


def _rows(key, p):
    return _Part(key, 0, p, 2)


def _cols(key, p):
    return _Part(key, 1, p, 2)


class _Plan:
    def __init__(self, plan, make_side):
        self.plan, self.make_side, self.source, self.got = plan, make_side, {}, {}

    def item(self, k):
        return (self.source[k.key], (k.axis, k.index, k.count)) if isinstance(k, _Part) else (self.source[k], None)

    def run(self, fn, *args, name, **kw):
        keys = self.plan.get(name)
        if not keys:
            return fn(*args, name=name, **kw)
        result, outs = fn(*args, name=name, side=self.make_side([self.item(k) for k in keys]), **kw)
        self.got.update(zip(keys, outs))
        return result


GATHER_PLAN = {
    "l0s0_in": [("out", 0, 0), "kv"],
    "l0s0_out": ["hin"],
    "l0s1_proj": [("out", 0, 1), "hout"],
    "l0s1_scan": [("in", 0, 1)],
    "l0s2_in": [("in", 1, 0)],
    "l0s2_out": [("out", 1, 0), "q"],
    "l1s0_in": [("out", 1, 1), "o"],
    "l1s1_attn": [("in", 1, 1)],
}
GATHER_FIRST = [("in", 0, 0)]
SCATTER_PLAN = {
    "l1s2_dact": [_cols(("out", 1, 1), 0)],
    "l1s2_dwin0": [_cols(("out", 1, 1), 1)],
    "l1s2_dwin1": [_rows(("in", 1, 1, 0), 0)],
    "l1s2_du": [_rows(("in", 1, 1, 0), 1), _rows(("in", 1, 1, 1), 0)],
    "l1s1_dattn": [_rows(("in", 1, 1, 1), 1), "o"],
    "l1s0_dwout": ["q"],
    "l1s0_dact": [_cols(("out", 1, 0), 0)],
    "l1s0_dwin0": [_cols(("out", 1, 0), 1)],
    "l1s0_dwin1": [_rows(("in", 1, 0, 0), 0)],
    "l1s0_du": [_rows(("in", 1, 0, 0), 1), _rows(("in", 1, 0, 1), 0)],
    "l0s2_dwout": [_rows(("in", 1, 0, 1), 1)],
    "l0s2_dact": [_cols(("out", 0, 1), 0)],
    "l0s2_dwin0": [_cols(("out", 0, 1), 1)],
    "l0s2_dwin1": [_rows(("in", 0, 1, 0), 0)],
    "l0s2_du": [_rows(("in", 0, 1, 0), 1), _rows(("in", 0, 1, 1), 0)],
    "l0s1_dwout": ["kv"],
    "l0s1_dscan": [_rows(("in", 0, 1, 1), 1), "hout"],
    "l0s1_dwin1": [_rows(("hin", 0), 0)],
    "l0s1_du": [_rows(("hin", 0), 1), _rows(("hin", 1), 0)],
    "l0s0_dwout": [_rows(("hin", 1), 1)],
    "l0s0_dact": [_cols(("out", 0, 0), 0)],
    "l0s0_dwin0": [_cols(("out", 0, 0), 1)],
    "l0s0_dwin1": [_rows(("in", 0, 0, 0), 0)],
    "l0s0_du": [_rows(("in", 0, 0, 0), 1), _rows(("in", 0, 0, 1), 0)],
    "adam_w_ada0": [_rows(("in", 0, 0, 1), 1)],
}
GRAD_TM, GRAD_TK = 512, 4096


def _ffn_fwd(gp, W, h, gate, u, l, i, tag):
    ab4, h3, h3t = gp.run(ffn_in_act, u, W[("in", l, i)], name=tag + "_in")
    w_out = W[("out", l, i)]
    J = h3.shape[0]
    y, h_new = gp.run(mm_nn, h3, w_out.reshape(J, 1, -1, w_out.shape[-1]), F32, name=tag + "_out", natural=True,
                      resid=(h, gate, 0.5), kgroup=J)
    return y, h_new, (ab4, h3t)


def _ffn_bwd(sp, W, dy, ut, ab4, h3t, l, i, tag):
    w_in, w_out = W[("in", l, i)], W[("out", l, i)]
    J, nb, S = h3t.shape
    D = w_out.shape[-1]
    dw_out = sp.run(mm_nn, h3t.reshape(1, J * nb, S), dy[None, None], BF16, name=tag + "_dwout", natural=True,
                    tm=w_out.shape[1], tk=GRAD_TK)
    sp.source[("out", l, i)] = dw_out.reshape(w_out.shape)
    dab3 = sp.run(ffn_dact, dy, w_out.reshape(J, nb, D), ab4, name=tag + "_dact")
    for hf in range(2):
        sp.source[("in", l, i, hf)] = sp.run(mm_nn, ut.reshape(2, D // 2, S), dab3[None], BF16, name=f"{tag}_dwin{hf}",
                                             a_sel=hf, tm=GRAD_TM, tk=GRAD_TK)
    return sp.run(mm_nt, dab3, w_in[:, None], F32, name=tag + "_du", natural=True, kgroup=4)


def kernel(x, c, norm_gain, w_ada, b_ada, w_ffn_in, w_ffn_out, w_hgrn_in, hgrn_lb_logits, hgrn_head_gain, w_hgrn_out, kv_gain, w_ada_kv, b_ada_kv, w_kv, b_kv, w_q, b_q, attn_sinks, w_attn_out, final_gain, loss_target, m_norm_gain, m_w_ada, m_b_ada, m_w_ffn_in, m_w_ffn_out, m_w_hgrn_in, m_hgrn_lb_logits, m_hgrn_head_gain, m_w_hgrn_out, m_kv_gain, m_w_ada_kv, m_b_ada_kv, m_w_kv, m_b_kv, m_w_q, m_b_q, m_attn_sinks, m_w_attn_out, m_final_gain, v_norm_gain, v_w_ada, v_b_ada, v_w_ffn_in, v_w_ffn_out, v_w_hgrn_in, v_hgrn_lb_logits, v_hgrn_head_gain, v_w_hgrn_out, v_kv_gain, v_w_ada_kv, v_b_ada_kv, v_w_kv, v_b_kv, v_w_q, v_b_q, v_attn_sinks, v_w_attn_out, v_final_gain):
    xi, yi, ci = _place()
    me = 4 * xi + 2 * yi + ci
    _, S, D = x.shape
    L = norm_gain.shape[0]
    dsh = D // N_DEV
    ada_n = w_ada.shape[2]
    kv_n = w_ada_kv.shape[1]
    NQ = D // ATT_HEAD
    NKV = NQ // ATT_GROUP
    kvd = NKV * ATT_HEAD
    h0 = x[0]
    target = loss_target[0]

    def my_cols(a, n):
        return lax.dynamic_slice_in_dim(a, me * n, n, axis=a.ndim - 1)

    gp = _Plan(GATHER_PLAN, gather_side)
    bf = lambda a: a.astype(BF16)
    for l in range(L):
        for i in range(2):
            gp.source[("in", l, i)] = bf(w_ffn_in[l, i])
            gp.source[("out", l, i)] = bf(w_ffn_out[l, i])
    gp.source.update(hin=bf(w_hgrn_in[0]), hout=bf(w_hgrn_out[0]), kv=bf(w_kv), q=bf(w_q[0]), o=bf(w_attn_out[0]))
    W = gp.got
    W.update(zip(GATHER_FIRST, exchange(gather_side([(gp.source[k], None) for k in GATHER_FIRST]), "gather_first")))
    full = lambda w: w.reshape(1, 1, -1, w.shape[-1])

    lb_sh = lower_bound_fwd(hgrn_lb_logits[0:1], hgrn_lb_logits[1:2], "lb_fwd")
    small = jnp.concatenate([c, norm_gain.reshape(1, L * 3 * dsh), hgrn_head_gain, lb_sh], axis=1)
    (g1,) = exchange(gather_side([(small, None)]), "gather_cond")
    g1 = g1.reshape(N_DEV, -1)
    c_all = g1[:, :D]
    gains = g1[:, D:D + L * 3 * dsh].reshape(N_DEV, L * 3, dsh).transpose(1, 0, 2).reshape(L, 3, 1, D)
    head_gain = g1[:, D + L * 3 * dsh:D + (L * 3 + 1) * dsh].reshape(1, D)
    lb0 = g1[:, D + (L * 3 + 1) * dsh:].reshape(1, D)

    parts = [cond_proj(c_all, w_ada[l], my_cols(b_ada[l:l + 1], ada_n), f"mod{l}") for l in range(L)]
    parts.append(cond_proj(c_all, w_ada_kv, my_cols(b_ada_kv[None], kv_n), "mod_kv"))
    (g2,) = exchange(gather_side([(jnp.concatenate(parts, axis=1), None)]), "gather_mod")
    mine2 = lax.dynamic_index_in_dim(g2, me, axis=1, keepdims=False)
    mod = [mine2[:, l * ada_n:(l + 1) * ada_n].reshape(3, 3, 1, D) for l in range(L)]
    mod_kv = mine2[:, L * ada_n:].reshape(2, 1, D)

    tables = rope_tables(S)
    sink_col = jnp.broadcast_to(attn_sinks.reshape(NKV, ATT_GROUP, 1, 1), (NKV, ATT_GROUP, ATT_WINDOW, 1))
    sink_col = sink_col.reshape(NKV, ATT_GROUP * ATT_WINDOW, 1)

    def to_heads(t, n):
        return t.reshape(S, n, ATT_HEAD).transpose(1, 0, 2)

    def from_heads(t):
        return t.transpose(1, 0, 2).reshape(S, -1)

    h = h0
    saved = {}
    for l in range(L):
        for s in (0, 1, 2):
            tag = f"l{l}s{s}"
            shift, scale, gate = mod[l][s, 0], mod[l][s, 1], mod[l][s, 2]
            u, ut = adaln_fwd(h, gains[l, s], shift, scale, tag + "_norm")
            if s != 1:
                y, h_new, res = _ffn_fwd(gp, W, h, gate, u, l, s // 2, tag)
            elif l == 0:
                proj = gp.run(mm_nn, u[None], W["hin"][None], F32, name=tag + "_proj", natural=True)
                o, states = gp.run(hgrn_scan_fwd, proj, lb0, name=tag + "_scan")
                z, zt = hgrn_post_fwd(o, proj, head_gain, tag + "_post")
                y, h_new = mm_nn(z[None], full(W["hout"]), F32, tag + "_out", natural=True, resid=(h, gate, 1.0))
                res = (proj, o, states, zt)
            else:
                q = mm_nn(u[None], full(W["q"]), F32, tag + "_q", natural=True, bias=b_q)
                q4 = to_heads(rope(q, tables, 1, tag + "_rope"), NQ).reshape(NKV, ATT_GROUP, S, ATT_HEAD)
                att4 = gp.run(attn_fwd, q4, k3, v3, sink_col, name=tag + "_attn")
                att = from_heads(att4.reshape(NQ, S, ATT_HEAD))
                att_t = att4.reshape(NQ, S, ATT_HEAD).transpose(0, 2, 1).reshape(D, S)
                y, h_new = mm_nn(att[None], full(W["o"]), F32, tag + "_out", natural=True, resid=(h, gate, 1.0))
                res = (q4, att_t)
            saved[(l, s)] = (h, ut, y, res)
            h = h_new
        if l == 0:
            h_kv = h
            u_kv, u_kv_t = adaln_fwd(h, kv_gain[None], mod_kv[0], mod_kv[1], "kv_norm")
            kvp = mm_nn(u_kv[None], full(W["kv"]), F32, "kv_proj", natural=True, bias=b_kv[None])
            k3 = to_heads(rope(kvp[:, :kvd], tables, 1, "kv_rope"), NKV)
            v3 = to_heads(kvp[:, kvd:], NKV)

    def branch(l, s):
        return saved[(l, s)][2], mod[l][s, 2], 0.5 if s != 1 else 1.0

    loss_row, dh, d_final_gain, dy, d_gate = final_loss_grad(h, final_gain[None], target, "final", branch(L - 1, 2))
    loss = lax.psum(loss_row[0, 0], ("x", "y", "c"))

    sp = _Plan(SCATTER_PLAN, scatter_side)

    def grad_w(a_t, b, name):
        return sp.run(mm_nn, a_t[None], b[None, None], BF16, name=name, natural=True, tm=GRAD_TM, tk=GRAD_TK)

    d_mod = [[None] * 3 for _ in range(L)]
    d_gain = [[None] * 3 for _ in range(L)]
    for l in reversed(range(L)):
        if l == 0:
            dkv = jnp.concatenate([rope(from_heads(dk3), tables, -1, "kv_drope"), from_heads(dv3)], axis=1)
            sp.source["kv"] = grad_w(u_kv_t, dkv, "kv_dw").reshape(W["kv"].shape)
            db_kv = colsum(dkv, "kv_db")
            du_kv = mm_nt(dkv[None], full(W["kv"]), F32, "kv_du", natural=True)
            dh, d_kv_gain, d_kv_shift, d_kv_scale, dy, d_gate = adaln_bwd(
                h_kv, kv_gain[None], mod_kv[0], mod_kv[1], du_kv, dh, "kv_dnorm", nxt=branch(0, 2))
        for s in (2, 1, 0):
            tag = f"l{l}s{s}"
            shift, scale, gate = mod[l][s, 0], mod[l][s, 1], mod[l][s, 2]
            h_in, ut, y, res = saved[(l, s)]
            if s != 1:
                du = _ffn_bwd(sp, W, dy, ut, res[0], res[1], l, s // 2, tag)
            elif l == 0:
                proj, o, states, zt = res
                sp.source["hout"] = grad_w(zt, dy, tag + "_dwout").reshape(W["hout"].shape)
                dz = mm_nt(dy[None], full(W["hout"]), F32, tag + "_dz", natural=True)
                do, dg, d_head_gain = hgrn_post_bwd(o, proj, head_gain, dz, tag + "_dpost")
                dproj, d_lb0 = sp.run(hgrn_scan_bwd, proj, lb0, states, do, dg, name=tag + "_dscan")
                for hf in range(2):
                    sp.source[("hin", hf)] = sp.run(mm_nn, ut.reshape(2, D // 2, S), dproj, BF16, name=f"{tag}_dwin{hf}",
                                                    b_natural=True, jn=N_DEV, a_sel=hf, tm=GRAD_TM, tk=GRAD_TK)
                du = sp.run(mm_nt, dproj, W["hin"][:, None], F32, name=tag + "_du", natural=True, a_natural=True)
            else:
                q4, att_t = res
                sp.source["o"] = grad_w(att_t, dy, tag + "_dwout").reshape(W["o"].shape)
                datt = mm_nt(dy[None], full(W["o"]), F32, tag + "_datt", natural=True)
                datt4 = to_heads(datt, NQ).reshape(NKV, ATT_GROUP, S, ATT_HEAD)
                dq4, (dkp, dkc, dvp, dvc), d_sink = sp.run(attn_bwd, q4, k3, v3, sink_col, datt4, name=tag + "_dattn")
                dk3, dv3 = band_combine(dkc, dkp, dvc, dvp, tag + "_dkv")
                dq = rope(from_heads(dq4.reshape(NQ, S, ATT_HEAD)), tables, -1, tag + "_drope")
                sp.source["q"] = grad_w(ut, dq, tag + "_dwq").reshape(W["q"].shape)
                db_q = colsum(dq, tag + "_dbq")
                du = mm_nt(dq[None], full(W["q"]), F32, tag + "_du", natural=True)
            outs = adaln_bwd(h_in, gains[l, s], shift, scale, du, dh, tag + "_dnorm",
                             nxt=branch(l, s - 1) if s > 0 else None)
            dh, d_gain[l][s], dsh_, dsc_ = outs[:4]
            d_mod[l][s] = jnp.concatenate([dsh_, dsc_, d_gate], axis=1)
            if s > 0:
                dy, d_gate = outs[4:]
    grad_x = dh[None]
    G = sp.got

    pad = lambda a, n: jnp.pad(a, ((0, 0), (0, n - a.shape[1])))
    pieces = [jnp.concatenate(d_mod[l], axis=1) for l in range(L)]
    pieces += [d_kv_shift, d_kv_scale]
    pieces += [d_gain[l][s] for l in range(L) for s in range(3)]
    pieces += [d_head_gain, d_lb0, d_kv_gain, db_kv, db_q, pad(d_sink.reshape(1, NQ), LANES), d_final_gain]
    (g3,) = exchange(gather_side([(jnp.concatenate(pieces, axis=1), None)]), "gather_small_grads")
    g3 = g3.reshape(N_DEV, -1)
    tot = rowsum(g3, "sum_small_grads")
    offs = [0]
    for p in pieces:
        offs.append(offs[-1] + p.shape[1])
    seg = lambda k: tot[:, offs[k]:offs[k + 1]]
    k0 = 0
    g_b_ada = jnp.concatenate([seg(l) for l in range(L)], axis=0)
    k0 += L
    g_b_ada_kv = jnp.concatenate([seg(k0), seg(k0 + 1)], axis=1)
    k0 += 2
    g_norm_gain = jnp.concatenate([my_cols(seg(k0 + j), dsh) for j in range(3 * L)], axis=0)
    k0 += 3 * L
    g_head_gain = my_cols(seg(k0), dsh)
    d_lb_sh = my_cols(seg(k0 + 1), dsh)
    g_kv_gain = seg(k0 + 2)
    g_b_kv = seg(k0 + 3)
    g_b_q = seg(k0 + 4)
    g_sinks = seg(k0 + 5)[:, :NQ]
    g_final_gain = seg(k0 + 6)
    dl0, dl1 = lower_bound_bwd(hgrn_lb_logits[0:1], hgrn_lb_logits[1:2], d_lb_sh, "lb_bwd")
    g_lb_logits = jnp.concatenate([dl0, dl1], axis=0)

    g_w_ada = jnp.stack([cond_grad(c_all, lax.dynamic_slice_in_dim(g3, offs[l] + me * ada_n, ada_n, axis=1),
                                   f"dw_ada{l}") for l in range(L)])
    g_w_ada_kv = cond_grad(c_all, lax.dynamic_slice_in_dim(g3, offs[L] + me * kv_n, kv_n, axis=1), "dw_ada_kv")

    def update(w, m, v, grads, name, ncol=1):
        n = len(grads) // ncol
        width = w.shape[-1]
        three = lambda a: a.reshape((n, -1, width))
        outs = None
        for k, g in enumerate(grads):
            j, cp = divmod(k, ncol)
            g = g.reshape((g.shape[0], -1, width // ncol))
            if len(grads) == 1:
                outs = sp.run(adamw, three(w), three(m), three(v), j, g, name=f"{name}{k}")
            else:
                outs = adamw(three(w), three(m), three(v), j, g, f"{name}{k}", col=(cp, ncol), into=outs)
        return [o.reshape(w.shape) for o in outs]

    def one(w, m, v, g, name):
        return update(w, m, v, [g.reshape((1, -1, w.shape[-1]))], name)

    res = {}
    res["norm_gain"] = one(norm_gain, m_norm_gain, v_norm_gain, g_norm_gain, "adam_norm_gain")
    res["w_ada"] = one(w_ada, m_w_ada, v_w_ada, g_w_ada, "adam_w_ada")
    res["b_ada"] = one(b_ada, m_b_ada, v_b_ada, g_b_ada, "adam_b_ada")
    sent = {k.key if isinstance(k, _Part) else k for k in G}
    rest = [k for k in sp.source if k not in sent]
    if rest:
        G.update(zip(rest, exchange(scatter_side([(sp.source[k], None) for k in rest]), "scatter_rest")))
    g_ffn_in = [G[_rows(("in", l, i, hf), p)] for l in range(L) for i in range(2) for hf in range(2) for p in range(2)]
    g_ffn_out = [G[_cols(("out", l, i), p)] for l in range(L) for i in range(2) for p in range(2)]
    g_hgrn_in = [G[_rows(("hin", hf), p)] for hf in range(2) for p in range(2)]
    res["w_ffn_in"] = update(w_ffn_in, m_w_ffn_in, v_w_ffn_in, g_ffn_in, "adam_w_ffn_in")
    res["w_ffn_out"] = update(w_ffn_out, m_w_ffn_out, v_w_ffn_out, g_ffn_out, "adam_w_ffn_out", ncol=2)
    res["w_hgrn_in"] = update(w_hgrn_in, m_w_hgrn_in, v_w_hgrn_in, g_hgrn_in, "adam_w_hgrn_in")
    res["hgrn_lb_logits"] = one(hgrn_lb_logits, m_hgrn_lb_logits, v_hgrn_lb_logits, g_lb_logits, "adam_lb")
    res["hgrn_head_gain"] = one(hgrn_head_gain, m_hgrn_head_gain, v_hgrn_head_gain, g_head_gain, "adam_head_gain")
    res["w_hgrn_out"] = update(w_hgrn_out, m_w_hgrn_out, v_w_hgrn_out, [G["hout"]], "adam_w_hgrn_out")
    res["kv_gain"] = one(kv_gain, m_kv_gain, v_kv_gain, g_kv_gain, "adam_kv_gain")
    res["w_ada_kv"] = one(w_ada_kv, m_w_ada_kv, v_w_ada_kv, g_w_ada_kv, "adam_w_ada_kv")
    res["b_ada_kv"] = one(b_ada_kv, m_b_ada_kv, v_b_ada_kv, g_b_ada_kv, "adam_b_ada_kv")
    res["w_kv"] = update(w_kv, m_w_kv, v_w_kv, [G["kv"]], "adam_w_kv")
    res["b_kv"] = one(b_kv, m_b_kv, v_b_kv, g_b_kv, "adam_b_kv")
    res["w_q"] = update(w_q, m_w_q, v_w_q, [G["q"]], "adam_w_q")
    res["b_q"] = one(b_q, m_b_q, v_b_q, g_b_q, "adam_b_q")
    res["attn_sinks"] = one(attn_sinks, m_attn_sinks, v_attn_sinks, g_sinks, "adam_sinks")
    res["w_attn_out"] = update(w_attn_out, m_w_attn_out, v_w_attn_out, [G["o"]], "adam_w_attn_out")
    res["final_gain"] = one(final_gain, m_final_gain, v_final_gain, g_final_gain, "adam_final_gain")

    names = ["norm_gain", "w_ada", "b_ada", "w_ffn_in", "w_ffn_out", "w_hgrn_in", "hgrn_lb_logits", "hgrn_head_gain",
             "w_hgrn_out", "kv_gain", "w_ada_kv", "b_ada_kv", "w_kv", "b_kv", "w_q", "b_q", "attn_sinks", "w_attn_out",
             "final_gain"]
    return (loss, grad_x, *[res[n][0] for n in names], *[res[n][1] for n in names], *[res[n][2] for n in names],
            *[res[n][3] for n in names])
```

```python
import functools

import jax
import jax.numpy as jnp
from jax import lax
from jax.experimental import pallas as pl
from jax.experimental.pallas import tpu as pltpu

F32 = jnp.float32
BF16 = jnp.bfloat16
MESH = pl.DeviceIdType.MESH

N_DEV = 8
V7X_VMEM_LIMIT_BYTES = 56 * 1024 * 1024
LANES = 128

NORM_EPS = 1e-6
NEG_INF = -1e30
HGRN_CHUNK = 32
HGRN_HEAD = 128
ATT_HEAD = 64
ATT_WINDOW = 128
ATT_GROUP = 8
ROT_DIM = 16
ROPE_THETA = 500000.0

ADAM_LR = 0.001
ADAM_B1 = 0.9
ADAM_B2 = 0.999
ADAM_EPS = 1e-08
ADAM_WD = 0.01
ADAM_STEP = 10


def _params(*sem):
    return pltpu.CompilerParams(dimension_semantics=sem, vmem_limit_bytes=V7X_VMEM_LIMIT_BYTES)


def _tile(n, pref, unit=LANES):
    t = (min(n, pref) // unit) * unit
    while t >= unit:
        if n % t == 0:
            return t
        t -= unit
    return n


_ANY = pl.BlockSpec(memory_space=pl.ANY)


class _Side:
    def __init__(self, inputs, out_shapes, sem_shapes, start, finish):
        self.inputs, self.out_shapes, self.sem_shapes = list(inputs), list(out_shapes), list(sem_shapes)
        self.start, self.finish = start, finish


def _call(body, *, name, grid, in_specs, out_specs, out_shape, args, semantics, scratch_shapes=(), side=None):
    in_specs, out_specs, out_shape = list(in_specs), list(out_specs), list(out_shape)
    scratch_shapes = list(scratch_shapes)
    if side is None:
        outs = pl.pallas_call(
            body, name=name, grid=grid, in_specs=in_specs, out_specs=out_specs, out_shape=out_shape,
            scratch_shapes=scratch_shapes, compiler_params=_params(*semantics))(*args)
        return list(outs)
    n_in, n_out, n_scr = len(in_specs), len(out_specs), len(scratch_shapes)
    s_in, s_out = len(side.inputs), len(side.out_shapes)

    def carried(*refs):
        ins, refs = refs[:n_in], refs[n_in:]
        side_ins, refs = refs[:s_in], refs[s_in:]
        outs, refs = refs[:n_out], refs[n_out:]
        side_outs, refs = refs[:s_out], refs[s_out:]
        scratch, sems = refs[:n_scr], refs[n_scr:]
        step, steps = pl.program_id(0), grid[0]
        for d in range(1, len(grid)):
            step, steps = step * grid[d] + pl.program_id(d), steps * grid[d]

        @pl.when(step == 0)
        def _():
            side.start(side_ins, side_outs, sems)

        body(*ins, *outs, *scratch)

        @pl.when(step == steps - 1)
        def _():
            side.finish(side_ins, side_outs, sems)

    outs = pl.pallas_call(
        carried, name=name, grid=grid, in_specs=in_specs + [_ANY] * s_in, out_specs=out_specs + [_ANY] * s_out,
        out_shape=out_shape + side.out_shapes, scratch_shapes=scratch_shapes + side.sem_shapes,
        compiler_params=_params(*(["arbitrary"] * len(grid))))(*args, *side.inputs)
    return list(outs[:n_out]), list(outs[n_out:])


def _accumulate(prod, o_ref, acc_ref, k, nk):
    if nk == 1:
        o_ref[...] = prod.astype(o_ref.dtype)
        return

    @pl.when(k == 0)
    def _():
        acc_ref[...] = prod

    @pl.when(k > 0)
    def _():
        acc_ref[...] += prod

    @pl.when(k == nk - 1)
    def _():
        o_ref[...] = acc_ref[...].astype(o_ref.dtype)


def mm_nn(a3, b4, out_dtype, name, natural=False, a_natural=False, b_natural=False, jn=None, a_sel=None, bias=None,
          resid=None, tm=512, tk=None, kgroup=1, side=None):
    if b_natural:
        JK, JN, kb = 1, jn, b4.shape[0]
        nb = b4.shape[1] // JN
    else:
        JK, JN, kb, nb = b4.shape
    M = a3.shape[0] if a_natural else a3.shape[1]
    tm = _tile(M, tm, 16)
    tn = _tile(nb, 1024)
    tk = kb if tk is None else _tile(kb, tk)
    ntn, nkt = nb // tn, kb // tk
    nk = JK * nkt // kgroup

    def body(*refs):
        a_ref, b_ref = refs[:2]
        acc_ref = refs[-1]
        if kgroup == 1:
            prod = jnp.dot(a_ref[...].astype(BF16), b_ref[...].astype(BF16), preferred_element_type=F32)
        else:
            prod = jnp.dot(a_ref[0].astype(BF16), b_ref[0].astype(BF16), preferred_element_type=F32)
            for g in range(1, kgroup):
                prod += jnp.dot(a_ref[g].astype(BF16), b_ref[g].astype(BF16), preferred_element_type=F32)
        if bias is not None:
            prod = prod + refs[2][...]
        if resid is None:
            _accumulate(prod, refs[-2], acc_ref, pl.program_id(2), nk)
            return
        h_ref, g_ref, y_ref, o_ref = refs[2:6]
        k = pl.program_id(2)

        @pl.when(k == 0)
        def _():
            acc_ref[...] = prod

        @pl.when(k > 0)
        def _():
            acc_ref[...] += prod

        @pl.when(k == nk - 1)
        def _():
            y = acc_ref[...]
            y_ref[...] = y
            o_ref[...] = h_ref[...] + (resid[2] * g_ref[...]) * y

    if kgroup > 1:
        assert JN == 1 and nkt == 1 and not (a_natural or b_natural) and a_sel is None
        a3 = a3.reshape(nk, kgroup, M, kb)
        b4 = b4.reshape(nk, kgroup, kb, nb)
        a_spec = pl.BlockSpec((None, kgroup, tm, kb), lambda j, i, k: (k, 0, i, 0))
    elif a_natural:
        a_spec = pl.BlockSpec((tm, tk), lambda j, i, k: (i, k))
    elif a_sel is not None:
        a_spec = pl.BlockSpec((None, tm, tk), lambda j, i, k: (a_sel, i, k))
    else:
        a_spec = pl.BlockSpec((None, tm, tk), lambda j, i, k: (k // nkt, i, k % nkt))
    if kgroup > 1:
        b_spec = pl.BlockSpec((None, kgroup, kb, tn), lambda j, i, k: (k, 0, 0, j))
    elif b_natural:
        b_spec = pl.BlockSpec((tk, tn), lambda j, i, k: (k, j))
    else:
        b_spec = pl.BlockSpec((None, None, tk, tn), lambda j, i, k: (k // nkt, j // ntn, k % nkt, j % ntn))
    in_specs = [a_spec, b_spec]
    args = [a3, b4]
    if bias is not None:
        assert natural and nk == 1
        in_specs.append(pl.BlockSpec((1, tn), lambda j, i, k: (0, j)))
        args.append(bias)
    if natural:
        out_shape = jax.ShapeDtypeStruct((M, JN * nb), out_dtype)
        o_spec = pl.BlockSpec((tm, tn), lambda j, i, k: (i, j))
    else:
        out_shape = jax.ShapeDtypeStruct((JN, M, nb), out_dtype)
        o_spec = pl.BlockSpec((None, tm, tn), lambda j, i, k: (j // ntn, i, j % ntn))
    out_specs, out_shapes = [o_spec], [out_shape]
    if resid is not None:
        assert natural and bias is None and out_dtype == F32
        in_specs += [o_spec, pl.BlockSpec((1, tn), lambda j, i, k: (0, j))]
        args += [resid[0], resid[1]]
        out_specs, out_shapes = [o_spec, o_spec], [out_shape, out_shape]
    res = _call(body, name=name, grid=(JN * ntn, M // tm, nk), in_specs=in_specs, out_specs=out_specs,
                out_shape=out_shapes, scratch_shapes=[pltpu.VMEM((tm, tn), F32)], args=args,
                semantics=("parallel", "parallel", "arbitrary"), side=side)
    outs = res if side is None else res[0]
    ans = outs[0] if resid is None else (outs[0], outs[1])
    return ans if side is None else (ans, res[1])


def mm_nt(a3, b4, out_dtype, name, natural=False, a_natural=False, kgroup=1, side=None):
    JK, JN, nb, kb = b4.shape
    M = a3.shape[0] if a_natural else a3.shape[1]
    tm = min(M, 512)
    tn = _tile(nb, 1024)
    ntn = nb // tn
    nk = JK // kgroup

    def body(a_ref, b_ref, o_ref, acc_ref):
        nt = (((1,), (1,)), ((), ()))
        if kgroup == 1:
            prod = lax.dot_general(a_ref[...].astype(BF16), b_ref[...].astype(BF16), nt, preferred_element_type=F32)
        else:
            prod = lax.dot_general(a_ref[0].astype(BF16), b_ref[0].astype(BF16), nt, preferred_element_type=F32)
            for g in range(1, kgroup):
                prod += lax.dot_general(a_ref[g].astype(BF16), b_ref[g].astype(BF16), nt, preferred_element_type=F32)
        _accumulate(prod, o_ref, acc_ref, pl.program_id(2), nk)

    if kgroup > 1:
        assert JN == 1 and not a_natural
        a3 = a3.reshape(nk, kgroup, M, kb)
        b4 = b4.reshape(nk, kgroup, nb, kb)
        a_spec = pl.BlockSpec((None, kgroup, tm, kb), lambda j, i, k: (k, 0, i, 0))
        b_spec = pl.BlockSpec((None, kgroup, tn, kb), lambda j, i, k: (k, 0, j, 0))
    elif a_natural:
        a_spec = pl.BlockSpec((tm, kb), lambda j, i, k: (i, k))
        b_spec = pl.BlockSpec((None, None, tn, kb), lambda j, i, k: (k, j // ntn, j % ntn, 0))
    else:
        a_spec = pl.BlockSpec((None, tm, kb), lambda j, i, k: (k, i, 0))
        b_spec = pl.BlockSpec((None, None, tn, kb), lambda j, i, k: (k, j // ntn, j % ntn, 0))
    if natural:
        out_shape = jax.ShapeDtypeStruct((M, JN * nb), out_dtype)
        o_spec = pl.BlockSpec((tm, tn), lambda j, i, k: (i, j))
    else:
        out_shape = jax.ShapeDtypeStruct((JN, M, nb), out_dtype)
        o_spec = pl.BlockSpec((None, tm, tn), lambda j, i, k: (j // ntn, i, j % ntn))
    res = _call(body, name=name, grid=(JN * ntn, M // tm, nk), in_specs=[a_spec, b_spec], out_specs=[o_spec],
                out_shape=[out_shape], scratch_shapes=[pltpu.VMEM((tm, tn), F32)], args=[a3, b4],
                semantics=("parallel", "parallel", "arbitrary"), side=side)
    return res[0] if side is None else (res[0][0], res[1])


def _row_tile(S):
    return min(S, 256)


def _adaln(h, gain, shift, scale):
    y = h * lax.rsqrt(jnp.mean(h * h, axis=-1, keepdims=True) + NORM_EPS) * gain
    return y * (1.0 + scale) + shift


def adaln_fwd(h, gain, shift, scale, name):
    S, D = h.shape
    tm = _row_tile(S)

    def body(h_ref, g_ref, sh_ref, sc_ref, u_ref, ut_ref):
        u = _adaln(h_ref[...], g_ref[...], sh_ref[...], sc_ref[...])
        u_ref[...] = u.astype(BF16)
        ut_ref[...] = u.T.astype(BF16)

    row = pl.BlockSpec((tm, D), lambda i: (i, 0))
    vec = pl.BlockSpec((1, D), lambda i: (0, 0))
    return pl.pallas_call(
        body, name=name, grid=(S // tm,), in_specs=[row, vec, vec, vec],
        out_specs=[row, pl.BlockSpec((D, tm), lambda i: (0, i))],
        out_shape=[jax.ShapeDtypeStruct((S, D), BF16), jax.ShapeDtypeStruct((D, S), BF16)],
        compiler_params=_params("parallel"),
    )(h, gain, shift, scale)


def _gate_specs(S, D, tm, nxt):
    if nxt is None:
        return [], [], [], []
    row = pl.BlockSpec((tm, D), lambda i: (i, 0))
    vec = pl.BlockSpec((1, D), lambda i: (0, 0))
    return ([row, vec], [nxt[0], nxt[1]], [row, vec],
            [jax.ShapeDtypeStruct((S, D), BF16), jax.ShapeDtypeStruct((1, D), F32)])


def _gate_grads(dh, nxt, y_ref, g_ref, dy_ref, dg_ref):
    coef = nxt[2]
    dy_ref[...] = ((coef * g_ref[...]) * dh).astype(BF16)

    @pl.when(pl.program_id(0) == 0)
    def _():
        dg_ref[...] = jnp.zeros_like(dg_ref)

    dg_ref[...] += coef * jnp.sum(dh * y_ref[...], axis=0, keepdims=True)


def adaln_bwd(h, gain, shift, scale, du, dres, name, nxt=None):
    S, D = h.shape
    tm = _row_tile(S)

    def body(h_ref, g_ref, sh_ref, sc_ref, du_ref, dres_ref, *rest):
        dh_ref, dg_ref, dsh_ref, dsc_ref = rest[-6:-2] if nxt is not None else rest
        _, vjp = jax.vjp(_adaln, h_ref[...], g_ref[...], sh_ref[...], sc_ref[...])
        dh, dg, dsh, dsc = vjp(du_ref[...].astype(F32))
        dh = dres_ref[...] + dh
        dh_ref[...] = dh

        @pl.when(pl.program_id(0) == 0)
        def _():
            dg_ref[...] = jnp.zeros_like(dg_ref)
            dsh_ref[...] = jnp.zeros_like(dsh_ref)
            dsc_ref[...] = jnp.zeros_like(dsc_ref)

        dg_ref[...] += dg
        dsh_ref[...] += dsh
        dsc_ref[...] += dsc
        if nxt is not None:
            _gate_grads(dh, nxt, rest[0], rest[1], rest[-2], rest[-1])

    row = pl.BlockSpec((tm, D), lambda i: (i, 0))
    vec = pl.BlockSpec((1, D), lambda i: (0, 0))
    vs = jax.ShapeDtypeStruct((1, D), F32)
    xin, xargs, xout, xshape = _gate_specs(S, D, tm, nxt)
    return pl.pallas_call(
        body, name=name, grid=(S // tm,), in_specs=[row, vec, vec, vec, row, row] + xin,
        out_specs=[row, vec, vec, vec] + xout, out_shape=[jax.ShapeDtypeStruct((S, D), F32), vs, vs, vs] + xshape,
        compiler_params=_params("arbitrary"),
    )(h, gain, shift, scale, du, dres, *xargs)


def _swiglu(a, b):
    return a * jax.nn.sigmoid(a) * b


def ffn_in_act(u, w_in, name, side=None):
    J2, D, nb = w_in.shape
    J = J2 // 2
    S = u.shape[0]
    tm = _tile(S, 512, 16)

    def body(u_ref, wa_ref, wb_ref, ab_ref, h_ref, ht_ref):
        uv = u_ref[...]
        a = jnp.dot(uv, wa_ref[...], preferred_element_type=F32)
        b = jnp.dot(uv, wb_ref[...], preferred_element_type=F32)
        ab_ref[0] = a
        ab_ref[1] = b
        hv = _swiglu(a, b)
        h_ref[...] = hv.astype(BF16)
        ht_ref[...] = hv.T.astype(BF16)

    res = _call(body, name=name, grid=(J, S // tm),
                in_specs=[pl.BlockSpec((tm, D), lambda j, i: (i, 0)),
                          pl.BlockSpec((None, D, nb), lambda j, i: (j, 0, 0)),
                          pl.BlockSpec((None, D, nb), lambda j, i: (j + J, 0, 0))],
                out_specs=[pl.BlockSpec((2, None, tm, nb), lambda j, i: (0, j, i, 0)),
                           pl.BlockSpec((None, tm, nb), lambda j, i: (j, i, 0)),
                           pl.BlockSpec((None, nb, tm), lambda j, i: (j, 0, i))],
                out_shape=[jax.ShapeDtypeStruct((2, J, S, nb), F32), jax.ShapeDtypeStruct((J, S, nb), BF16),
                           jax.ShapeDtypeStruct((J, nb, S), BF16)],
                args=[u, w_in, w_in], semantics=("parallel", "parallel"), side=side)
    return res if side is None else (res[0], res[1])


def ffn_dact(dy, w_out4, ab4, name, side=None):
    J, nb, D = w_out4.shape
    S = dy.shape[0]
    tm = _tile(S, 512, 16)

    def body(dy_ref, w_ref, ab_ref, o_ref):
        dh = lax.dot_general(dy_ref[...], w_ref[...], (((1,), (1,)), ((), ())), preferred_element_type=F32)
        _, vjp = jax.vjp(_swiglu, ab_ref[0], ab_ref[1])
        da, db = vjp(dh)
        o_ref[0] = da.astype(BF16)
        o_ref[1] = db.astype(BF16)

    both = pl.BlockSpec((2, None, tm, nb), lambda j, i: (0, j, i, 0))
    res = _call(body, name=name, grid=(J, S // tm),
                in_specs=[pl.BlockSpec((tm, D), lambda j, i: (i, 0)),
                          pl.BlockSpec((None, nb, D), lambda j, i: (j, 0, 0)), both],
                out_specs=[both], out_shape=[jax.ShapeDtypeStruct((2, J, S, nb), BF16)],
                args=[dy, w_out4, ab4], semantics=("parallel", "parallel"), side=side)
    if side is None:
        return res[0].reshape(2 * J, S, nb)
    return res[0][0].reshape(2 * J, S, nb), res[1]


def colsum(x, name):
    S, N = x.shape
    tm = _row_tile(S)

    def body(x_ref, o_ref):
        @pl.when(pl.program_id(0) == 0)
        def _():
            o_ref[...] = jnp.zeros_like(o_ref)

        o_ref[...] += jnp.sum(x_ref[...].astype(F32), axis=0, keepdims=True)

    return pl.pallas_call(
        body, name=name, grid=(S // tm,), in_specs=[pl.BlockSpec((tm, N), lambda i: (i, 0))],
        out_specs=pl.BlockSpec((1, N), lambda i: (0, 0)), out_shape=jax.ShapeDtypeStruct((1, N), F32),
        compiler_params=_params("arbitrary"),
    )(x)


def _final_loss(h, gain, target):
    y = h * lax.rsqrt(jnp.mean(h * h, axis=-1, keepdims=True) + NORM_EPS) * gain
    err = y - target
    return 0.5 * jnp.sum(jnp.mean(err * err, axis=-1))


def final_loss_grad(h, gain, target, name, nxt):
    S, D = h.shape
    tm = _row_tile(S)

    def body(h_ref, g_ref, t_ref, y_ref, gate_ref, loss_ref, dh_ref, dg_ref, dy_ref, dgate_ref):
        loss, (dh, dg) = jax.value_and_grad(_final_loss, argnums=(0, 1))(h_ref[...], g_ref[...], t_ref[...])
        dh_ref[...] = dh

        @pl.when(pl.program_id(0) == 0)
        def _():
            loss_ref[...] = jnp.zeros_like(loss_ref)
            dg_ref[...] = jnp.zeros_like(dg_ref)

        loss_ref[...] += jnp.full(loss_ref.shape, loss, F32)
        dg_ref[...] += dg
        _gate_grads(dh, nxt, y_ref, gate_ref, dy_ref, dgate_ref)

    row = pl.BlockSpec((tm, D), lambda i: (i, 0))
    vec = pl.BlockSpec((1, D), lambda i: (0, 0))
    xin, xargs, xout, xshape = _gate_specs(S, D, tm, nxt)
    return pl.pallas_call(
        body, name=name, grid=(S // tm,), in_specs=[row, vec, row] + xin,
        out_specs=[pl.BlockSpec((1, LANES), lambda i: (0, 0)), row, vec] + xout,
        out_shape=[jax.ShapeDtypeStruct((1, LANES), F32), jax.ShapeDtypeStruct((S, D), F32),
                   jax.ShapeDtypeStruct((1, D), F32)] + xshape,
        compiler_params=_params("arbitrary"),
    )(h, gain, target, *xargs)


def _chunk_consts(H):
    C = HGRN_CHUNK
    t = lax.broadcasted_iota(jnp.int32, (H, C, C), 1)
    s = lax.broadcasted_iota(jnp.int32, (H, C, C), 2)
    return (s <= t).astype(F32), s <= t


def _hgrn_chunk(q_raw, f_raw, i_raw, lb, st):
    H, C, _ = q_raw.shape
    lower, causal = _chunk_consts(H)
    forget = lb + (1.0 - lb) * jax.nn.sigmoid(f_raw)
    g = jnp.log(forget)
    kk = 1.0 - forget
    qs = q_raw * jax.nn.sigmoid(q_raw)
    bnn = (((2,), (1,)), ((0,), (0,)))
    bnt = (((2,), (2,)), ((0,), (0,)))
    btn = (((1,), (1,)), ((0,), (0,)))
    b = lax.dot_general(lower, g, bnn, precision=lax.Precision.HIGHEST, preferred_element_type=F32)
    bm = b[:, C // 2 - 1:C // 2, :]
    bl = b[:, C - 1:C, :]
    inter = lax.dot_general((qs * jnp.exp(b)).astype(BF16), st.astype(BF16), bnt, preferred_element_type=F32)
    qt = (qs * jnp.exp(b - bm)).astype(BF16)
    kt = (kk * jnp.exp(bm - b)).astype(BF16)
    scores = lax.dot_general(qt, kt, bnt, preferred_element_type=F32)
    scores = jnp.where(causal, scores, 0.0)
    vb = i_raw.astype(BF16)
    out = inter + lax.dot_general(scores.astype(BF16), vb, bnn, preferred_element_type=F32)
    kdec = (kk * jnp.exp(bl - b)).astype(BF16)
    new_st = st * jnp.exp(bl) + lax.dot_general(vb, kdec, btn, preferred_element_type=F32)
    return out, new_st


def _heads(ref, rows, H):
    return jnp.stack([ref[rows, pl.ds(h * HGRN_HEAD, HGRN_HEAD)] for h in range(H)])


def hgrn_scan_fwd(proj, lb, name, side=None):
    S, D4 = proj.shape
    D = D4 // 4
    H = D // HGRN_HEAD
    C = HGRN_CHUNK
    R = min(S, 128)
    ncr = R // C

    def body(q_ref, f_ref, i_ref, lb_ref, o_ref, st_ref, state):
        @pl.when(pl.program_id(0) == 0)
        def _():
            state[...] = jnp.zeros_like(state)

        lbh = _heads(lb_ref, slice(None), H)

        def chunk(cc, carry):
            rows = pl.ds(pl.multiple_of(cc * C, C), C)
            st = state[...]
            st_ref[cc] = st.astype(BF16)
            out, new_st = _hgrn_chunk(_heads(q_ref, rows, H), _heads(f_ref, rows, H), _heads(i_ref, rows, H), lbh, st)
            for h in range(H):
                o_ref[rows, pl.ds(h * HGRN_HEAD, HGRN_HEAD)] = out[h]
            state[...] = new_st
            return carry

        lax.fori_loop(0, ncr, chunk, 0)

    col = lambda j: pl.BlockSpec((R, D), lambda i: (i, j))
    res = _call(body, name=name, grid=(S // R,),
                in_specs=[col(0), col(1), col(2), pl.BlockSpec((1, D), lambda i: (0, 0))],
                out_specs=[pl.BlockSpec((R, D), lambda i: (i, 0)),
                           pl.BlockSpec((ncr, H, HGRN_HEAD, HGRN_HEAD), lambda i: (i, 0, 0, 0))],
                out_shape=[jax.ShapeDtypeStruct((S, D), F32),
                           jax.ShapeDtypeStruct((S // C, H, HGRN_HEAD, HGRN_HEAD), BF16)],
                scratch_shapes=[pltpu.VMEM((H, HGRN_HEAD, HGRN_HEAD), F32)], args=[proj, proj, proj, lb],
                semantics=("arbitrary",), side=side)
    return res if side is None else (res[0], res[1])


def hgrn_scan_bwd(proj, lb, states, do, dg, name, side=None):
    S, D4 = proj.shape
    D = D4 // 4
    H = D // HGRN_HEAD
    C = HGRN_CHUNK
    R = min(S, 128)
    ncr = R // C
    ng = S // R

    def body(q_ref, f_ref, i_ref, lb_ref, st_ref, do_ref, dg_ref, dp_ref, dlb_ref, dstate):
        @pl.when(pl.program_id(0) == 0)
        def _():
            dstate[...] = jnp.zeros_like(dstate)
            dlb_ref[...] = jnp.zeros_like(dlb_ref)

        dp_ref[:, pl.ds(3 * D, D)] = dg_ref[...].astype(BF16)
        lbh = _heads(lb_ref, slice(None), H)

        def chunk(t, carry):
            cc = ncr - 1 - t
            rows = pl.ds(pl.multiple_of(cc * C, C), C)
            _, vjp = jax.vjp(_hgrn_chunk, _heads(q_ref, rows, H), _heads(f_ref, rows, H), _heads(i_ref, rows, H),
                             lbh, st_ref[cc].astype(F32))
            dq, df, di, dlb, dst = vjp((_heads(do_ref, rows, H), dstate[...]))
            for h in range(H):
                dp_ref[rows, pl.ds(h * HGRN_HEAD, HGRN_HEAD)] = dq[h].astype(BF16)
                dp_ref[rows, pl.ds(D + h * HGRN_HEAD, HGRN_HEAD)] = df[h].astype(BF16)
                dp_ref[rows, pl.ds(2 * D + h * HGRN_HEAD, HGRN_HEAD)] = di[h].astype(BF16)
                dlb_ref[:, pl.ds(h * HGRN_HEAD, HGRN_HEAD)] += dlb[h]
            dstate[...] = dst
            return carry

        lax.fori_loop(0, ncr, chunk, 0)

    col = lambda j: pl.BlockSpec((R, D), lambda i: (ng - 1 - i, j))
    res = _call(body, name=name, grid=(ng,),
                in_specs=[col(0), col(1), col(2), pl.BlockSpec((1, D), lambda i: (0, 0)),
                          pl.BlockSpec((ncr, H, HGRN_HEAD, HGRN_HEAD), lambda i: (ng - 1 - i, 0, 0, 0)),
                          pl.BlockSpec((R, D), lambda i: (ng - 1 - i, 0)),
                          pl.BlockSpec((R, D), lambda i: (ng - 1 - i, 0))],
                out_specs=[pl.BlockSpec((R, D4), lambda i: (ng - 1 - i, 0)), pl.BlockSpec((1, D), lambda i: (0, 0))],
                out_shape=[jax.ShapeDtypeStruct((S, D4), BF16), jax.ShapeDtypeStruct((1, D), F32)],
                scratch_shapes=[pltpu.VMEM((H, HGRN_HEAD, HGRN_HEAD), F32)],
                args=[proj, proj, proj, lb, states, do, dg], semantics=("arbitrary",), side=side)
    return res if side is None else (res[0], res[1])


def _head_out(o, g, gain):
    y = o * lax.rsqrt(jnp.mean(o * o, axis=-1, keepdims=True) + NORM_EPS) * gain
    return y * jax.nn.sigmoid(g)


def hgrn_post_fwd(o, proj, gain, name):
    S, D = o.shape
    H = D // HGRN_HEAD
    tm = _row_tile(S)

    def body(o_ref, g_ref, gain_ref, z_ref, zt_ref):
        for h in range(H):
            ls = pl.ds(h * HGRN_HEAD, HGRN_HEAD)
            z = _head_out(o_ref[:, ls], g_ref[:, ls], gain_ref[:, ls])
            z_ref[:, ls] = z.astype(BF16)
            zt_ref[ls, :] = z.T.astype(BF16)

    row = pl.BlockSpec((tm, D), lambda i: (i, 0))
    return pl.pallas_call(
        body, name=name, grid=(S // tm,),
        in_specs=[row, pl.BlockSpec((tm, D), lambda i: (i, 3)), pl.BlockSpec((1, D), lambda i: (0, 0))],
        out_specs=[row, pl.BlockSpec((D, tm), lambda i: (0, i))],
        out_shape=[jax.ShapeDtypeStruct((S, D), BF16), jax.ShapeDtypeStruct((D, S), BF16)],
        compiler_params=_params("parallel"),
    )(o, proj, gain)


def hgrn_post_bwd(o, proj, gain, dz, name):
    S, D = o.shape
    H = D // HGRN_HEAD
    tm = _row_tile(S)

    def body(o_ref, g_ref, gain_ref, dz_ref, do_ref, dg_ref, dgain_ref):
        @pl.when(pl.program_id(0) == 0)
        def _():
            dgain_ref[...] = jnp.zeros_like(dgain_ref)

        for h in range(H):
            ls = pl.ds(h * HGRN_HEAD, HGRN_HEAD)
            _, vjp = jax.vjp(_head_out, o_ref[:, ls], g_ref[:, ls], gain_ref[:, ls])
            do, dg, dgain = vjp(dz_ref[:, ls].astype(F32))
            do_ref[:, ls] = do
            dg_ref[:, ls] = dg
            dgain_ref[:, ls] += dgain

    row = pl.BlockSpec((tm, D), lambda i: (i, 0))
    vec = pl.BlockSpec((1, D), lambda i: (0, 0))
    return pl.pallas_call(
        body, name=name, grid=(S // tm,),
        in_specs=[row, pl.BlockSpec((tm, D), lambda i: (i, 3)), vec, row], out_specs=[row, row, vec],
        out_shape=[jax.ShapeDtypeStruct((S, D), F32), jax.ShapeDtypeStruct((S, D), F32),
                   jax.ShapeDtypeStruct((1, D), F32)],
        compiler_params=_params("arbitrary"),
    )(o, proj, gain, dz)


def lower_bound_fwd(l0, l1, name):
    def body(a_ref, b_ref, o_ref):
        o_ref[...] = jax.nn.sigmoid(a_ref[...] - b_ref[...])

    return pl.pallas_call(body, name=name, out_shape=jax.ShapeDtypeStruct(l0.shape, F32))(l0, l1)


def lower_bound_bwd(l0, l1, dlb, name):
    def body(a_ref, b_ref, d_ref, o0_ref, o1_ref):
        s = jax.nn.sigmoid(a_ref[...] - b_ref[...])
        d0 = d_ref[...] * s * (1.0 - s)
        o0_ref[...] = d0
        o1_ref[...] = -d0

    sd = jax.ShapeDtypeStruct(l0.shape, F32)
    return pl.pallas_call(body, name=name, out_shape=[sd, sd])(l0, l1, dlb)


def _attn_tile(q, kp, kc, vp, vc, sink, first):
    W = ATT_WINDOW
    nt = (((1,), (1,)), ((), ()))
    qb = q.astype(BF16)
    scale = ATT_HEAD ** -0.5
    sp = lax.dot_general(qb, kp.astype(BF16), nt, preferred_element_type=F32) * scale
    sc = lax.dot_general(qb, kc.astype(BF16), nt, preferred_element_type=F32) * scale
    qi = lax.broadcasted_iota(jnp.int32, sp.shape, 0) & (W - 1)
    kj = lax.broadcasted_iota(jnp.int32, sp.shape, 1)
    sp = jnp.where((kj > qi) & jnp.logical_not(first), sp, NEG_INF)
    sc = jnp.where(kj <= qi, sc, NEG_INF)
    m = jnp.maximum(jnp.maximum(jnp.max(sp, axis=-1, keepdims=True), jnp.max(sc, axis=-1, keepdims=True)), sink)
    pp = jnp.exp(sp - m)
    pc = jnp.exp(sc - m)
    denom = jnp.sum(pp, axis=-1, keepdims=True) + jnp.sum(pc, axis=-1, keepdims=True) + jnp.exp(sink - m)
    out = jnp.dot((pp / denom).astype(BF16), vp.astype(BF16), preferred_element_type=F32)
    return out + jnp.dot((pc / denom).astype(BF16), vc.astype(BF16), preferred_element_type=F32)


def _attn_specs(G, W, Dh):
    q_spec = pl.BlockSpec((None, G, W, Dh), lambda j, n: (j, 0, n, 0))
    prev = pl.BlockSpec((None, W, Dh), lambda j, n: (j, jnp.maximum(n - 1, 0), 0))
    cur = pl.BlockSpec((None, W, Dh), lambda j, n: (j, n, 0))
    sink = pl.BlockSpec((None, G * W, 1), lambda j, n: (j, 0, 0))
    return q_spec, prev, cur, sink


def attn_fwd(q4, k3, v3, sink, name, side=None):
    NKV, G, S, Dh = q4.shape
    W = ATT_WINDOW

    def body(q_ref, kp_ref, kc_ref, vp_ref, vc_ref, s_ref, o_ref):
        first = pl.program_id(1) == 0
        out = _attn_tile(q_ref[...].reshape(G * W, Dh), kp_ref[...], kc_ref[...], vp_ref[...], vc_ref[...],
                         s_ref[...], first)
        o_ref[...] = out.reshape(G, W, Dh)

    q_spec, prev, cur, sk = _attn_specs(G, W, Dh)
    res = _call(body, name=name, grid=(NKV, S // W), in_specs=[q_spec, prev, cur, prev, cur, sk], out_specs=[q_spec],
                out_shape=[jax.ShapeDtypeStruct(q4.shape, F32)], args=[q4, k3, k3, v3, v3, sink],
                semantics=("parallel", "parallel"), side=side)
    return res[0] if side is None else (res[0][0], res[1])


def attn_bwd(q4, k3, v3, sink, do4, name, side=None):
    NKV, G, S, Dh = q4.shape
    W = ATT_WINDOW
    nb = S // W

    def body(q_ref, kp_ref, kc_ref, vp_ref, vc_ref, s_ref, do_ref, dq_ref, dkp_ref, dkc_ref, dvp_ref, dvc_ref,
             ds_ref):
        first = pl.program_id(1) == 0
        _, vjp = jax.vjp(functools.partial(_attn_tile, first=first), q_ref[...].reshape(G * W, Dh), kp_ref[...],
                         kc_ref[...], vp_ref[...], vc_ref[...], s_ref[...])
        dq, dkp, dkc, dvp, dvc, ds = vjp(do_ref[...].reshape(G * W, Dh))
        dq_ref[...] = dq.reshape(G, W, Dh)
        dkp_ref[...] = dkp
        dkc_ref[...] = dkc
        dvp_ref[...] = dvp
        dvc_ref[...] = dvc

        @pl.when(first)
        def _():
            ds_ref[...] = jnp.zeros_like(ds_ref)

        ds_ref[...] += jnp.sum(ds.reshape(G, W, 1), axis=1)

    q_spec, prev, cur, sk = _attn_specs(G, W, Dh)
    part = pl.BlockSpec((None, None, W, Dh), lambda j, n: (j, n, 0, 0))
    ps = jax.ShapeDtypeStruct((NKV, nb, W, Dh), F32)
    res = _call(body, name=name, grid=(NKV, nb), in_specs=[q_spec, prev, cur, prev, cur, sk, q_spec],
                out_specs=[q_spec, part, part, part, part, pl.BlockSpec((None, G, 1), lambda j, n: (j, 0, 0))],
                out_shape=[jax.ShapeDtypeStruct(q4.shape, F32), ps, ps, ps, ps,
                           jax.ShapeDtypeStruct((NKV, G, 1), F32)],
                args=[q4, k3, k3, v3, v3, sink, do4], semantics=("parallel", "arbitrary"), side=side)
    outs = res if side is None else res[0]
    ans = (outs[0], outs[1:5], outs[5])
    return ans if side is None else (ans, res[1])


def band_combine(kc, kp, vc, vp, name):
    NKV, nb, W, Dh = kc.shape

    def shifted_sum(c_ref, p_ref, o_ref):
        o_ref[pl.ds(0, nb - 1)] = c_ref[pl.ds(0, nb - 1)] + p_ref[pl.ds(1, nb - 1)]
        o_ref[nb - 1] = c_ref[nb - 1]

    def body(kc_ref, kp_ref, vc_ref, vp_ref, dk_ref, dv_ref):
        shifted_sum(kc_ref, kp_ref, dk_ref)
        shifted_sum(vc_ref, vp_ref, dv_ref)

    spec = pl.BlockSpec((None, nb, W, Dh), lambda j: (j, 0, 0, 0))
    sd = jax.ShapeDtypeStruct((NKV, nb, W, Dh), F32)
    dk, dv = pl.pallas_call(
        body, name=name, grid=(NKV,), in_specs=[spec] * 4, out_specs=[spec] * 2, out_shape=[sd, sd],
        compiler_params=_params("parallel"),
    )(kc, kp, vc, vp)
    return dk.reshape(NKV, nb * W, Dh), dv.reshape(NKV, nb * W, Dh)


def rope_tables(S):
    half = ROT_DIM // 2
    inv_freq = jnp.power(jnp.float32(ROPE_THETA), -jnp.arange(0, ROT_DIM, 2, dtype=F32) / ROT_DIM)
    ang = jnp.arange(S, dtype=F32)[:, None] * inv_freq[None, :]
    sin, cos = jnp.sin(ang), jnp.cos(ang)
    zeros = jnp.zeros((S, ATT_HEAD - ROT_DIM), F32)
    z8 = jnp.zeros((S, half), F32)
    cfull = jnp.concatenate([cos, cos, jnp.ones((S, ATT_HEAD - ROT_DIM), F32)], axis=1)
    s_next = jnp.concatenate([-sin, z8, zeros], axis=1)
    s_prev = jnp.concatenate([z8, sin, zeros], axis=1)
    two = lambda t: jnp.concatenate([t, t], axis=1)
    return two(cfull), two(s_next), two(s_prev)


def rope(x, tables, sign, name):
    S, Wd = x.shape
    tm = _row_tile(S)
    rep = Wd // LANES
    half = ROT_DIM // 2

    def body(x_ref, c_ref, sn_ref, sp_ref, o_ref):
        xv = x_ref[...]
        c = jnp.tile(c_ref[...], (1, rep))
        sn = jnp.tile(sn_ref[...], (1, rep))
        sp = jnp.tile(sp_ref[...], (1, rep))
        if sign > 0:
            nxt = pltpu.roll(xv, Wd - half, 1)
            prv = pltpu.roll(xv, half, 1)
            o_ref[...] = xv * c + nxt * sn + prv * sp
        else:
            o_ref[...] = xv * c + pltpu.roll(xv * sn, half, 1) + pltpu.roll(xv * sp, Wd - half, 1)

    row = pl.BlockSpec((tm, Wd), lambda i: (i, 0))
    tab = pl.BlockSpec((tm, LANES), lambda i: (i, 0))
    return pl.pallas_call(
        body, name=name, grid=(S // tm,), in_specs=[row, tab, tab, tab], out_specs=row,
        out_shape=jax.ShapeDtypeStruct((S, Wd), F32), compiler_params=_params("parallel"),
    )(x, *tables)


def cond_proj(c_all, w, bias, name):
    B, D = c_all.shape
    N = w.shape[1]
    tn = _tile(N, 512)

    def body(c_ref, w_ref, b_ref, o_ref):
        cv = c_ref[...]
        cs = (cv * jax.nn.sigmoid(cv)).astype(BF16)
        o_ref[...] = jnp.dot(cs, w_ref[...].astype(BF16), preferred_element_type=F32) + b_ref[...]

    return pl.pallas_call(
        body, name=name, grid=(N // tn,),
        in_specs=[pl.BlockSpec((B, D), lambda j: (0, 0)), pl.BlockSpec((D, tn), lambda j: (0, j)),
                  pl.BlockSpec((1, tn), lambda j: (0, j))],
        out_specs=pl.BlockSpec((B, tn), lambda j: (0, j)), out_shape=jax.ShapeDtypeStruct((B, N), F32),
        compiler_params=_params("parallel"),
    )(c_all, w, bias)


def cond_grad(c_all, dmod, name):
    B, D = c_all.shape
    N = dmod.shape[1]
    tn = _tile(N, 512)

    def body(c_ref, d_ref, o_ref):
        cv = c_ref[...]
        cs = (cv * jax.nn.sigmoid(cv)).astype(BF16)
        o_ref[...] = lax.dot_general(cs, d_ref[...].astype(BF16), (((0,), (0,)), ((), ())),
                                     preferred_element_type=F32)

    return pl.pallas_call(
        body, name=name, grid=(N // tn,),
        in_specs=[pl.BlockSpec((B, D), lambda j: (0, 0)), pl.BlockSpec((B, tn), lambda j: (0, j))],
        out_specs=pl.BlockSpec((D, tn), lambda j: (0, j)), out_shape=jax.ShapeDtypeStruct((D, N), F32),
        compiler_params=_params("parallel"),
    )(c_all, dmod)


def rowsum(g, name):
    B, N = g.shape
    tn = N if B * N * 4 <= (4 << 20) else _tile(N, 8192)

    def body(g_ref, o_ref):
        acc = g_ref[0:1, :]
        for r in range(1, B):
            acc = acc + g_ref[r:r + 1, :]
        o_ref[...] = acc

    return pl.pallas_call(
        body, name=name, grid=(N // tn,), in_specs=[pl.BlockSpec((B, tn), lambda j: (0, j))],
        out_specs=pl.BlockSpec((1, tn), lambda j: (0, j)), out_shape=jax.ShapeDtypeStruct((1, N), F32),
        compiler_params=_params("parallel"),
    )(g)


def _adam_rows(R, C):
    limit = 3 << 19
    if R * C * 4 <= limit or R % 8:
        return R
    best = 8
    for t in range(8, R + 1, 8):
        if R % t == 0 and t * C * 4 <= limit:
            best = t
    return best


def adamw(w3, m3, v3, j, g3, name, col=(0, 1), into=None, side=None):
    n, R, C = w3.shape
    P = g3.shape[0]
    cp, ncol = col
    Cp = C // ncol
    tr = _adam_rows(R, Cp)

    def body(*refs):
        w_ref, m_ref, v_ref, g_ref = refs[:4]
        go_ref, d_ref, mo_ref, vo_ref = refs[-4:]
        g = g_ref[0].astype(F32)
        for p in range(1, P):
            g = g + g_ref[p].astype(F32)
        mn = ADAM_B1 * m_ref[...] + (1.0 - ADAM_B1) * g
        vn = ADAM_B2 * v_ref[...] + (1.0 - ADAM_B2) * jnp.square(g)
        m_hat = mn / (1.0 - ADAM_B1 ** ADAM_STEP)
        v_hat = vn / (1.0 - ADAM_B2 ** ADAM_STEP)
        go_ref[...] = g
        d_ref[...] = -ADAM_LR * (m_hat / (jnp.sqrt(v_hat) + ADAM_EPS) + ADAM_WD * w_ref[...])
        mo_ref[...] = mn
        vo_ref[...] = vn

    spec = pl.BlockSpec((None, tr, Cp), lambda i: (j, i, cp))
    g_spec = pl.BlockSpec((P, tr, Cp), lambda i: (0, i, 0))
    sd = jax.ShapeDtypeStruct((n, R, C), F32)
    in_specs, args, aliases = [spec, spec, spec, g_spec], [w3, m3, v3, g3], {}
    if side is not None:
        assert into is None
        return _call(body, name=name, grid=(R // tr,), in_specs=in_specs, out_specs=[spec] * 4, out_shape=[sd] * 4,
                     args=args, semantics=("parallel",), side=side)
    if into is not None:
        in_specs += [_ANY] * 4
        args += list(into)
        aliases = {4 + k: k for k in range(4)}
    return pl.pallas_call(
        body, name=name, grid=(R // tr,), in_specs=in_specs, out_specs=[spec] * 4, out_shape=[sd] * 4,
        input_output_aliases=aliases, compiler_params=_params("parallel"),
    )(*args)


def _place():
    return lax.axis_index("x"), lax.axis_index("y"), lax.axis_index("c")


def _slot(p):
    return 4 * p[0] + 2 * p[1] + p[2]


def gather_side(items):
    xs = [a for a, _ in items]
    n = len(xs)

    def copy(ins, outs, sems, t, k, block, to, from_input=False):
        dst = outs[t].at[_slot(block)]
        return pltpu.make_async_remote_copy(
            src_ref=ins[t] if from_input else dst, dst_ref=dst, send_sem=sems[0].at[t, k],
            recv_sem=sems[1].at[t, k], device_id=to, device_id_type=MESH)

    def peers():
        x, y, c = _place()
        return (x, y, c), (x, y, 1 - c), [(1 - x, y), (x, 1 - y), (1 - x, 1 - y)]

    def start(ins, outs, sems):
        me, sibling, chips = peers()
        c = me[2]
        for t in range(n):
            pltpu.make_async_copy(ins[t], outs[t].at[_slot(me)], sems[2].at[t]).start()
            copy(ins, outs, sems, t, 0, me, sibling, True).start()
            for j, chip in enumerate(chips):
                copy(ins, outs, sems, t, 1 + j, me, (*chip, c), True).start()

    def finish(ins, outs, sems):
        me, sibling, chips = peers()
        c = me[2]
        for t in range(n):
            for j, chip in enumerate(chips):
                copy(ins, outs, sems, t, 1 + j, (*chip, c), me).wait_recv()
                copy(ins, outs, sems, t, 4 + j, (*chip, c), sibling).start()
        for t in range(n):
            copy(ins, outs, sems, t, 0, sibling, me).wait_recv()
            for j, chip in enumerate(chips):
                copy(ins, outs, sems, t, 4 + j, (*chip, 1 - c), me).wait_recv()
        for t in range(n):
            copy(ins, outs, sems, t, 0, me, sibling, True).wait_send()
            for j, chip in enumerate(chips):
                copy(ins, outs, sems, t, 1 + j, me, (*chip, c), True).wait_send()
                copy(ins, outs, sems, t, 4 + j, (*chip, c), sibling).wait_send()
            pltpu.make_async_copy(ins[t], outs[t].at[_slot(me)], sems[2].at[t]).wait()

    return _Side(xs, [jax.ShapeDtypeStruct((N_DEV,) + a.shape, a.dtype) for a in xs],
                 [pltpu.SemaphoreType.DMA((n, 7)), pltpu.SemaphoreType.DMA((n, 7)), pltpu.SemaphoreType.DMA((n,))],
                 start, finish)


def scatter_side(items):
    gs = [g for g, _ in items]
    n = len(gs)

    def part_shape(g, part):
        _, R, C = g.shape
        if part is None:
            return R, C
        axis, _, cnt = part
        return (R // cnt, C) if axis == 0 else (R, C // cnt)

    def block(ref, slot, g, part):
        if part is None:
            return ref.at[slot]
        axis, idx, cnt = part
        R, C = part_shape(g, part)
        return ref.at[slot, pl.ds(idx * R, R)] if axis == 0 else ref.at[slot, :, pl.ds(idx * C, C)]

    def copies(ins, outs, sems):
        x, y, c = _place()
        me = (x, y, c)
        out = []
        for t, (g, part) in enumerate(items):
            out.append(pltpu.make_async_copy(block(ins[t], _slot(me), g, part), outs[t].at[_slot(me)], sems[2].at[t]))
            for r in range(1, N_DEV):
                peer = (1 - x if r & 4 else x, 1 - y if r & 2 else y, 1 - c if r & 1 else c)
                out.append(pltpu.make_async_remote_copy(
                    src_ref=block(ins[t], _slot(peer), g, part), dst_ref=outs[t].at[_slot(me)],
                    send_sem=sems[0].at[t, r - 1], recv_sem=sems[1].at[t, r - 1], device_id=peer, device_id_type=MESH))
        return out

    def start(ins, outs, sems):
        for cp in copies(ins, outs, sems):
            cp.start()

    def finish(ins, outs, sems):
        for cp in copies(ins, outs, sems):
            cp.wait()

    return _Side(gs, [jax.ShapeDtypeStruct((N_DEV,) + part_shape(g, part), g.dtype) for g, part in items],
                 [pltpu.SemaphoreType.DMA((n, 7)), pltpu.SemaphoreType.DMA((n, 7)), pltpu.SemaphoreType.DMA((n,))],
                 start, finish)


def exchange(side, name):
    def body(*refs):
        n_in, n_out = len(side.inputs), len(side.out_shapes)
        ins, outs, sems = refs[:n_in], refs[n_in:n_in + n_out], refs[n_in + n_out:]
        side.start(ins, outs, sems)
        side.finish(ins, outs, sems)

    return pl.pallas_call(
        body, name=name, in_specs=[_ANY] * len(side.inputs), out_specs=[_ANY] * len(side.out_shapes),
        out_shape=side.out_shapes, scratch_shapes=side.sem_shapes,
    )(*side.inputs)


class _Part:
    def __init__(self, key, axis, index, count):
        self.key, self.axis, self.index, self.count = key, axis, index, count

    def __hash__(self):
        return hash((self.key, self.axis, self.index, self.count))

    def __eq__(self, other):
        return isinstance(other, _Part) and (self.key, self.axis, self.index, self.count) == (
            other.key, other.axis, other.index, other.count)


def _rows(key, p):
    return _Part(key, 0, p, 2)


def _cols(key, p):
    return _Part(key, 1, p, 2)


class _Plan:
    def __init__(self, plan, make_side):
        self.plan, self.make_side, self.source, self.got = plan, make_side, {}, {}

    def item(self, k):
        return (self.source[k.key], (k.axis, k.index, k.count)) if isinstance(k, _Part) else (self.source[k], None)

    def run(self, fn, *args, name, **kw):
        keys = self.plan.get(name)
        if not keys:
            return fn(*args, name=name, **kw)
        result, outs = fn(*args, name=name, side=self.make_side([self.item(k) for k in keys]), **kw)
        self.got.update(zip(keys, outs))
        return result


GATHER_PLAN = {
    "l0s0_in": [("out", 0, 0), "hout", "kv"],
    "l0s0_out": ["hin"],
    "l0s1_proj": [("out", 0, 1)],
    "l0s1_scan": [("in", 0, 1)],
    "l0s2_in": [("in", 1, 0)],
    "l0s2_out": [("out", 1, 0)],
    "l1s0_in": [("out", 1, 1), "q", "o"],
    "l1s1_attn": [("in", 1, 1)],
}
GATHER_FIRST = [("in", 0, 0)]
SCATTER_PLAN = {
    "l1s2_dact": [_cols(("out", 1, 1), 0)],
    "l1s2_dwin0": [_cols(("out", 1, 1), 1)],
    "l1s2_dwin1": [_rows(("in", 1, 1, 0), 0)],
    "l1s2_du": [_rows(("in", 1, 1, 0), 1)],
    "l1s1_dattn": [_rows(("in", 1, 1, 1), 0), _rows(("in", 1, 1, 1), 1), "o"],
    "l1s0_dwout": ["q"],
    "l1s0_dact": [_cols(("out", 1, 0), 0)],
    "l1s0_dwin0": [_cols(("out", 1, 0), 1)],
    "l1s0_dwin1": [_rows(("in", 1, 0, 0), 0)],
    "l1s0_du": [_rows(("in", 1, 0, 0), 1), _rows(("in", 1, 0, 1), 0)],
    "l0s2_dwout": [_rows(("in", 1, 0, 1), 1)],
    "l0s2_dact": [_cols(("out", 0, 1), 0)],
    "l0s2_dwin0": [_cols(("out", 0, 1), 1)],
    "l0s2_dwin1": [_rows(("in", 0, 1, 0), 0)],
    "l0s2_du": [_rows(("in", 0, 1, 0), 1)],
    "l0s1_dwout": ["kv"],
    "l0s1_dscan": [_rows(("in", 0, 1, 1), 0), _rows(("in", 0, 1, 1), 1), "hout"],
    "l0s1_dwin1": [_rows(("hin", 0), 0)],
    "l0s1_du": [_rows(("hin", 0), 1), _rows(("hin", 1), 0)],
    "l0s0_dwout": [_rows(("hin", 1), 1)],
    "l0s0_dact": [_cols(("out", 0, 0), 0)],
    "l0s0_dwin0": [_cols(("out", 0, 0), 1)],
    "l0s0_dwin1": [_rows(("in", 0, 0, 0), 0)],
    "l0s0_du": [_rows(("in", 0, 0, 0), 1), _rows(("in", 0, 0, 1), 0)],
    "adam_w_ada0": [_rows(("in", 0, 0, 1), 1)],
}
GRAD_TM, GRAD_TK = 512, 4096


def _ffn_fwd(gp, W, h, gate, u, l, i, tag):
    ab4, h3, h3t = gp.run(ffn_in_act, u, W[("in", l, i)], name=tag + "_in")
    w_out = W[("out", l, i)]
    J = h3.shape[0]
    y, h_new = gp.run(mm_nn, h3, w_out.reshape(J, 1, -1, w_out.shape[-1]), F32, name=tag + "_out", natural=True,
                      resid=(h, gate, 0.5), kgroup=J)
    return y, h_new, (ab4, h3t)


def _ffn_bwd(sp, W, dy, ut, ab4, h3t, l, i, tag):
    w_in, w_out = W[("in", l, i)], W[("out", l, i)]
    J, nb, S = h3t.shape
    D = w_out.shape[-1]
    dw_out = sp.run(mm_nn, h3t.reshape(1, J * nb, S), dy[None, None], BF16, name=tag + "_dwout", natural=True,
                    tm=w_out.shape[1], tk=GRAD_TK)
    sp.source[("out", l, i)] = dw_out.reshape(w_out.shape)
    dab3 = sp.run(ffn_dact, dy, w_out.reshape(J, nb, D), ab4, name=tag + "_dact")
    for hf in range(2):
        sp.source[("in", l, i, hf)] = sp.run(mm_nn, ut.reshape(2, D // 2, S), dab3[None], BF16, name=f"{tag}_dwin{hf}",
                                             a_sel=hf, tm=GRAD_TM, tk=GRAD_TK)
    return sp.run(mm_nt, dab3, w_in[:, None], F32, name=tag + "_du", natural=True, kgroup=4)


def kernel(x, c, norm_gain, w_ada, b_ada, w_ffn_in, w_ffn_out, w_hgrn_in, hgrn_lb_logits, hgrn_head_gain, w_hgrn_out, kv_gain, w_ada_kv, b_ada_kv, w_kv, b_kv, w_q, b_q, attn_sinks, w_attn_out, final_gain, loss_target, m_norm_gain, m_w_ada, m_b_ada, m_w_ffn_in, m_w_ffn_out, m_w_hgrn_in, m_hgrn_lb_logits, m_hgrn_head_gain, m_w_hgrn_out, m_kv_gain, m_w_ada_kv, m_b_ada_kv, m_w_kv, m_b_kv, m_w_q, m_b_q, m_attn_sinks, m_w_attn_out, m_final_gain, v_norm_gain, v_w_ada, v_b_ada, v_w_ffn_in, v_w_ffn_out, v_w_hgrn_in, v_hgrn_lb_logits, v_hgrn_head_gain, v_w_hgrn_out, v_kv_gain, v_w_ada_kv, v_b_ada_kv, v_w_kv, v_b_kv, v_w_q, v_b_q, v_attn_sinks, v_w_attn_out, v_final_gain):
    xi, yi, ci = _place()
    me = 4 * xi + 2 * yi + ci
    _, S, D = x.shape
    L = norm_gain.shape[0]
    dsh = D // N_DEV
    ada_n = w_ada.shape[2]
    kv_n = w_ada_kv.shape[1]
    NQ = D // ATT_HEAD
    NKV = NQ // ATT_GROUP
    kvd = NKV * ATT_HEAD
    h0 = x[0]
    target = loss_target[0]

    def my_cols(a, n):
        return lax.dynamic_slice_in_dim(a, me * n, n, axis=a.ndim - 1)

    gp = _Plan(GATHER_PLAN, gather_side)
    bf = lambda a: a.astype(BF16)
    for l in range(L):
        for i in range(2):
            gp.source[("in", l, i)] = bf(w_ffn_in[l, i])
            gp.source[("out", l, i)] = bf(w_ffn_out[l, i])
    gp.source.update(hin=bf(w_hgrn_in[0]), hout=bf(w_hgrn_out[0]), kv=bf(w_kv), q=bf(w_q[0]), o=bf(w_attn_out[0]))
    W = gp.got
    W.update(zip(GATHER_FIRST, exchange(gather_side([(gp.source[k], None) for k in GATHER_FIRST]), "gather_first")))
    full = lambda w: w.reshape(1, 1, -1, w.shape[-1])

    lb_sh = lower_bound_fwd(hgrn_lb_logits[0:1], hgrn_lb_logits[1:2], "lb_fwd")
    small = jnp.concatenate([c, norm_gain.reshape(1, L * 3 * dsh), hgrn_head_gain, lb_sh], axis=1)
    (g1,) = exchange(gather_side([(small, None)]), "gather_cond")
    g1 = g1.reshape(N_DEV, -1)
    c_all = g1[:, :D]
    gains = g1[:, D:D + L * 3 * dsh].reshape(N_DEV, L * 3, dsh).transpose(1, 0, 2).reshape(L, 3, 1, D)
    head_gain = g1[:, D + L * 3 * dsh:D + (L * 3 + 1) * dsh].reshape(1, D)
    lb0 = g1[:, D + (L * 3 + 1) * dsh:].reshape(1, D)

    parts = [cond_proj(c_all, w_ada[l], my_cols(b_ada[l:l + 1], ada_n), f"mod{l}") for l in range(L)]
    parts.append(cond_proj(c_all, w_ada_kv, my_cols(b_ada_kv[None], kv_n), "mod_kv"))
    (g2,) = exchange(gather_side([(jnp.concatenate(parts, axis=1), None)]), "gather_mod")
    mine2 = lax.dynamic_index_in_dim(g2, me, axis=1, keepdims=False)
    mod = [mine2[:, l * ada_n:(l + 1) * ada_n].reshape(3, 3, 1, D) for l in range(L)]
    mod_kv = mine2[:, L * ada_n:].reshape(2, 1, D)

    tables = rope_tables(S)
    sink_col = jnp.broadcast_to(attn_sinks.reshape(NKV, ATT_GROUP, 1, 1), (NKV, ATT_GROUP, ATT_WINDOW, 1))
    sink_col = sink_col.reshape(NKV, ATT_GROUP * ATT_WINDOW, 1)

    def to_heads(t, n):
        return t.reshape(S, n, ATT_HEAD).transpose(1, 0, 2)

    def from_heads(t):
        return t.transpose(1, 0, 2).reshape(S, -1)

    h = h0
    saved = {}
    for l in range(L):
        for s in (0, 1, 2):
            tag = f"l{l}s{s}"
            shift, scale, gate = mod[l][s, 0], mod[l][s, 1], mod[l][s, 2]
            u, ut = adaln_fwd(h, gains[l, s], shift, scale, tag + "_norm")
            if s != 1:
                y, h_new, res = _ffn_fwd(gp, W, h, gate, u, l, s // 2, tag)
            elif l == 0:
                proj = gp.run(mm_nn, u[None], W["hin"][None], F32, name=tag + "_proj", natural=True)
                o, states = gp.run(hgrn_scan_fwd, proj, lb0, name=tag + "_scan")
                z, zt = hgrn_post_fwd(o, proj, head_gain, tag + "_post")
                y, h_new = mm_nn(z[None], full(W["hout"]), F32, tag + "_out", natural=True, resid=(h, gate, 1.0))
                res = (proj, o, states, zt)
            else:
                q = mm_nn(u[None], full(W["q"]), F32, tag + "_q", natural=True, bias=b_q)
                q4 = to_heads(rope(q, tables, 1, tag + "_rope"), NQ).reshape(NKV, ATT_GROUP, S, ATT_HEAD)
                att4 = gp.run(attn_fwd, q4, k3, v3, sink_col, name=tag + "_attn")
                att = from_heads(att4.reshape(NQ, S, ATT_HEAD))
                att_t = att4.reshape(NQ, S, ATT_HEAD).transpose(0, 2, 1).reshape(D, S)
                y, h_new = mm_nn(att[None], full(W["o"]), F32, tag + "_out", natural=True, resid=(h, gate, 1.0))
                res = (q4, att_t)
            saved[(l, s)] = (h, ut, y, res)
            h = h_new
        if l == 0:
            h_kv = h
            u_kv, u_kv_t = adaln_fwd(h, kv_gain[None], mod_kv[0], mod_kv[1], "kv_norm")
            kvp = mm_nn(u_kv[None], full(W["kv"]), F32, "kv_proj", natural=True, bias=b_kv[None])
            k3 = to_heads(rope(kvp[:, :kvd], tables, 1, "kv_rope"), NKV)
            v3 = to_heads(kvp[:, kvd:], NKV)

    def branch(l, s):
        return saved[(l, s)][2], mod[l][s, 2], 0.5 if s != 1 else 1.0

    loss_row, dh, d_final_gain, dy, d_gate = final_loss_grad(h, final_gain[None], target, "final", branch(L - 1, 2))
    loss = lax.psum(loss_row[0, 0], ("x", "y", "c"))

    sp = _Plan(SCATTER_PLAN, scatter_side)

    def grad_w(a_t, b, name):
        return sp.run(mm_nn, a_t[None], b[None, None], BF16, name=name, natural=True, tm=GRAD_TM, tk=GRAD_TK)

    d_mod = [[None] * 3 for _ in range(L)]
    d_gain = [[None] * 3 for _ in range(L)]
    for l in reversed(range(L)):
        if l == 0:
            dkv = jnp.concatenate([rope(from_heads(dk3), tables, -1, "kv_drope"), from_heads(dv3)], axis=1)
            sp.source["kv"] = grad_w(u_kv_t, dkv, "kv_dw").reshape(W["kv"].shape)
            db_kv = colsum(dkv, "kv_db")
            du_kv = mm_nt(dkv[None], full(W["kv"]), F32, "kv_du", natural=True)
            dh, d_kv_gain, d_kv_shift, d_kv_scale, dy, d_gate = adaln_bwd(
                h_kv, kv_gain[None], mod_kv[0], mod_kv[1], du_kv, dh, "kv_dnorm", nxt=branch(0, 2))
        for s in (2, 1, 0):
            tag = f"l{l}s{s}"
            shift, scale, gate = mod[l][s, 0], mod[l][s, 1], mod[l][s, 2]
            h_in, ut, y, res = saved[(l, s)]
            if s != 1:
                du = _ffn_bwd(sp, W, dy, ut, res[0], res[1], l, s // 2, tag)
            elif l == 0:
                proj, o, states, zt = res
                sp.source["hout"] = grad_w(zt, dy, tag + "_dwout").reshape(W["hout"].shape)
                dz = mm_nt(dy[None], full(W["hout"]), F32, tag + "_dz", natural=True)
                do, dg, d_head_gain = hgrn_post_bwd(o, proj, head_gain, dz, tag + "_dpost")
                dproj, d_lb0 = sp.run(hgrn_scan_bwd, proj, lb0, states, do, dg, name=tag + "_dscan")
                for hf in range(2):
                    sp.source[("hin", hf)] = sp.run(mm_nn, ut.reshape(2, D // 2, S), dproj, BF16, name=f"{tag}_dwin{hf}",
                                                    b_natural=True, jn=N_DEV, a_sel=hf, tm=GRAD_TM, tk=GRAD_TK)
                du = sp.run(mm_nt, dproj, W["hin"][:, None], F32, name=tag + "_du", natural=True, a_natural=True)
            else:
                q4, att_t = res
                sp.source["o"] = grad_w(att_t, dy, tag + "_dwout").reshape(W["o"].shape)
                datt = mm_nt(dy[None], full(W["o"]), F32, tag + "_datt", natural=True)
                datt4 = to_heads(datt, NQ).reshape(NKV, ATT_GROUP, S, ATT_HEAD)
                dq4, (dkp, dkc, dvp, dvc), d_sink = sp.run(attn_bwd, q4, k3, v3, sink_col, datt4, name=tag + "_dattn")
                dk3, dv3 = band_combine(dkc, dkp, dvc, dvp, tag + "_dkv")
                dq = rope(from_heads(dq4.reshape(NQ, S, ATT_HEAD)), tables, -1, tag + "_drope")
                sp.source["q"] = grad_w(ut, dq, tag + "_dwq").reshape(W["q"].shape)
                db_q = colsum(dq, tag + "_dbq")
                du = mm_nt(dq[None], full(W["q"]), F32, tag + "_du", natural=True)
            outs = adaln_bwd(h_in, gains[l, s], shift, scale, du, dh, tag + "_dnorm",
                             nxt=branch(l, s - 1) if s > 0 else None)
            dh, d_gain[l][s], dsh_, dsc_ = outs[:4]
            d_mod[l][s] = jnp.concatenate([dsh_, dsc_, d_gate], axis=1)
            if s > 0:
                dy, d_gate = outs[4:]
    grad_x = dh[None]
    G = sp.got

    pad = lambda a, n: jnp.pad(a, ((0, 0), (0, n - a.shape[1])))
    pieces = [jnp.concatenate(d_mod[l], axis=1) for l in range(L)]
    pieces += [d_kv_shift, d_kv_scale]
    pieces += [d_gain[l][s] for l in range(L) for s in range(3)]
    pieces += [d_head_gain, d_lb0, d_kv_gain, db_kv, db_q, pad(d_sink.reshape(1, NQ), LANES), d_final_gain]
    (g3,) = exchange(gather_side([(jnp.concatenate(pieces, axis=1), None)]), "gather_small_grads")
    g3 = g3.reshape(N_DEV, -1)
    tot = rowsum(g3, "sum_small_grads")
    offs = [0]
    for p in pieces:
        offs.append(offs[-1] + p.shape[1])
    seg = lambda k: tot[:, offs[k]:offs[k + 1]]
    k0 = 0
    g_b_ada = jnp.concatenate([seg(l) for l in range(L)], axis=0)
    k0 += L
    g_b_ada_kv = jnp.concatenate([seg(k0), seg(k0 + 1)], axis=1)
    k0 += 2
    g_norm_gain = jnp.concatenate([my_cols(seg(k0 + j), dsh) for j in range(3 * L)], axis=0)
    k0 += 3 * L
    g_head_gain = my_cols(seg(k0), dsh)
    d_lb_sh = my_cols(seg(k0 + 1), dsh)
    g_kv_gain = seg(k0 + 2)
    g_b_kv = seg(k0 + 3)
    g_b_q = seg(k0 + 4)
    g_sinks = seg(k0 + 5)[:, :NQ]
    g_final_gain = seg(k0 + 6)
    dl0, dl1 = lower_bound_bwd(hgrn_lb_logits[0:1], hgrn_lb_logits[1:2], d_lb_sh, "lb_bwd")
    g_lb_logits = jnp.concatenate([dl0, dl1], axis=0)

    g_w_ada = jnp.stack([cond_grad(c_all, lax.dynamic_slice_in_dim(g3, offs[l] + me * ada_n, ada_n, axis=1),
                                   f"dw_ada{l}") for l in range(L)])
    g_w_ada_kv = cond_grad(c_all, lax.dynamic_slice_in_dim(g3, offs[L] + me * kv_n, kv_n, axis=1), "dw_ada_kv")

    def update(w, m, v, grads, name, ncol=1):
        n = len(grads) // ncol
        width = w.shape[-1]
        three = lambda a: a.reshape((n, -1, width))
        outs = None
        for k, g in enumerate(grads):
            j, cp = divmod(k, ncol)
            g = g.reshape((g.shape[0], -1, width // ncol))
            if len(grads) == 1:
                outs = sp.run(adamw, three(w), three(m), three(v), j, g, name=f"{name}{k}")
            else:
                outs = adamw(three(w), three(m), three(v), j, g, f"{name}{k}", col=(cp, ncol), into=outs)
        return [o.reshape(w.shape) for o in outs]

    def one(w, m, v, g, name):
        return update(w, m, v, [g.reshape((1, -1, w.shape[-1]))], name)

    res = {}
    res["norm_gain"] = one(norm_gain, m_norm_gain, v_norm_gain, g_norm_gain, "adam_norm_gain")
    res["w_ada"] = one(w_ada, m_w_ada, v_w_ada, g_w_ada, "adam_w_ada")
    res["b_ada"] = one(b_ada, m_b_ada, v_b_ada, g_b_ada, "adam_b_ada")
    sent = {k.key if isinstance(k, _Part) else k for k in G}
    rest = [k for k in sp.source if k not in sent]
    if rest:
        G.update(zip(rest, exchange(scatter_side([(sp.source[k], None) for k in rest]), "scatter_rest")))
    g_ffn_in = [G[_rows(("in", l, i, hf), p)] for l in range(L) for i in range(2) for hf in range(2) for p in range(2)]
    g_ffn_out = [G[_cols(("out", l, i), p)] for l in range(L) for i in range(2) for p in range(2)]
    g_hgrn_in = [G[_rows(("hin", hf), p)] for hf in range(2) for p in range(2)]
    res["w_ffn_in"] = update(w_ffn_in, m_w_ffn_in, v_w_ffn_in, g_ffn_in, "adam_w_ffn_in")
    res["w_ffn_out"] = update(w_ffn_out, m_w_ffn_out, v_w_ffn_out, g_ffn_out, "adam_w_ffn_out", ncol=2)
    res["w_hgrn_in"] = update(w_hgrn_in, m_w_hgrn_in, v_w_hgrn_in, g_hgrn_in, "adam_w_hgrn_in")
    res["hgrn_lb_logits"] = one(hgrn_lb_logits, m_hgrn_lb_logits, v_hgrn_lb_logits, g_lb_logits, "adam_lb")
    res["hgrn_head_gain"] = one(hgrn_head_gain, m_hgrn_head_gain, v_hgrn_head_gain, g_head_gain, "adam_head_gain")
    res["w_hgrn_out"] = update(w_hgrn_out, m_w_hgrn_out, v_w_hgrn_out, [G["hout"]], "adam_w_hgrn_out")
    res["kv_gain"] = one(kv_gain, m_kv_gain, v_kv_gain, g_kv_gain, "adam_kv_gain")
    res["w_ada_kv"] = one(w_ada_kv, m_w_ada_kv, v_w_ada_kv, g_w_ada_kv, "adam_w_ada_kv")
    res["b_ada_kv"] = one(b_ada_kv, m_b_ada_kv, v_b_ada_kv, g_b_ada_kv, "adam_b_ada_kv")
    res["w_kv"] = update(w_kv, m_w_kv, v_w_kv, [G["kv"]], "adam_w_kv")
    res["b_kv"] = one(b_kv, m_b_kv, v_b_kv, g_b_kv, "adam_b_kv")
    res["w_q"] = update(w_q, m_w_q, v_w_q, [G["q"]], "adam_w_q")
    res["b_q"] = one(b_q, m_b_q, v_b_q, g_b_q, "adam_b_q")
    res["attn_sinks"] = one(attn_sinks, m_attn_sinks, v_attn_sinks, g_sinks, "adam_sinks")
    res["w_attn_out"] = update(w_attn_out, m_w_attn_out, v_w_attn_out, [G["o"]], "adam_w_attn_out")
    res["final_gain"] = one(final_gain, m_final_gain, v_final_gain, g_final_gain, "adam_final_gain")

    names = ["norm_gain", "w_ada", "b_ada", "w_ffn_in", "w_ffn_out", "w_hgrn_in", "hgrn_lb_logits", "hgrn_head_gain",
             "w_hgrn_out", "kv_gain", "w_ada_kv", "b_ada_kv", "w_kv", "b_kv", "w_q", "b_q", "attn_sinks", "w_attn_out",
             "final_gain"]
    return (loss, grad_x, *[res[n][0] for n in names], *[res[n][1] for n in names], *[res[n][2] for n in names],
            *[res[n][3] for n in names])
```

```python
import functools

import jax
import jax.numpy as jnp
from jax import lax
from jax.experimental import pallas as pl
from jax.experimental.pallas import tpu as pltpu

F32 = jnp.float32
BF16 = jnp.bfloat16
MESH = pl.DeviceIdType.MESH

N_DEV = 8
V7X_VMEM_LIMIT_BYTES = 56 * 1024 * 1024
LANES = 128

NORM_EPS = 1e-6
NEG_INF = -1e30
HGRN_CHUNK = 32
HGRN_HEAD = 128
ATT_HEAD = 64
ATT_WINDOW = 128
ATT_GROUP = 8
ROT_DIM = 16
ROPE_THETA = 500000.0

ADAM_LR = 0.001
ADAM_B1 = 0.9
ADAM_B2 = 0.999
ADAM_EPS = 1e-08
ADAM_WD = 0.01
ADAM_STEP = 10


def _params(*sem):
    return pltpu.CompilerParams(dimension_semantics=sem, vmem_limit_bytes=V7X_VMEM_LIMIT_BYTES)


def _tile(n, pref, unit=LANES):
    t = (min(n, pref) // unit) * unit
    while t >= unit:
        if n % t == 0:
            return t
        t -= unit
    return n


_ANY = pl.BlockSpec(memory_space=pl.ANY)


class _Side:
    def __init__(self, inputs, out_shapes, sem_shapes, start, finish):
        self.inputs, self.out_shapes, self.sem_shapes = list(inputs), list(out_shapes), list(sem_shapes)
        self.start, self.finish = start, finish


def _call(body, *, name, grid, in_specs, out_specs, out_shape, args, semantics, scratch_shapes=(), side=None):
    in_specs, out_specs, out_shape = list(in_specs), list(out_specs), list(out_shape)
    scratch_shapes = list(scratch_shapes)
    if side is None:
        outs = pl.pallas_call(
            body, name=name, grid=grid, in_specs=in_specs, out_specs=out_specs, out_shape=out_shape,
            scratch_shapes=scratch_shapes, compiler_params=_params(*semantics))(*args)
        return list(outs)
    n_in, n_out, n_scr = len(in_specs), len(out_specs), len(scratch_shapes)
    s_in, s_out = len(side.inputs), len(side.out_shapes)

    def carried(*refs):
        ins, refs = refs[:n_in], refs[n_in:]
        side_ins, refs = refs[:s_in], refs[s_in:]
        outs, refs = refs[:n_out], refs[n_out:]
        side_outs, refs = refs[:s_out], refs[s_out:]
        scratch, sems = refs[:n_scr], refs[n_scr:]
        step, steps = pl.program_id(0), grid[0]
        for d in range(1, len(grid)):
            step, steps = step * grid[d] + pl.program_id(d), steps * grid[d]

        @pl.when(step == 0)
        def _():
            side.start(side_ins, side_outs, sems)

        body(*ins, *outs, *scratch)

        @pl.when(step == steps - 1)
        def _():
            side.finish(side_ins, side_outs, sems)

    outs = pl.pallas_call(
        carried, name=name, grid=grid, in_specs=in_specs + [_ANY] * s_in, out_specs=out_specs + [_ANY] * s_out,
        out_shape=out_shape + side.out_shapes, scratch_shapes=scratch_shapes + side.sem_shapes,
        compiler_params=_params(*(["arbitrary"] * len(grid))))(*args, *side.inputs)
    return list(outs[:n_out]), list(outs[n_out:])


def _accumulate(prod, o_ref, acc_ref, k, nk):
    if nk == 1:
        o_ref[...] = prod.astype(o_ref.dtype)
        return

    @pl.when(k == 0)
    def _():
        acc_ref[...] = prod

    @pl.when(k > 0)
    def _():
        acc_ref[...] += prod

    @pl.when(k == nk - 1)
    def _():
        o_ref[...] = acc_ref[...].astype(o_ref.dtype)


def mm_nn(a3, b4, out_dtype, name, natural=False, a_natural=False, b_natural=False, jn=None, a_sel=None, bias=None,
          resid=None, tm=512, tk=None, kgroup=1, side=None):
    if b_natural:
        JK, JN, kb = 1, jn, b4.shape[0]
        nb = b4.shape[1] // JN
    else:
        JK, JN, kb, nb = b4.shape
    M = a3.shape[0] if a_natural else a3.shape[1]
    tm = _tile(M, tm, 16)
    tn = _tile(nb, 1024)
    tk = kb if tk is None else _tile(kb, tk)
    ntn, nkt = nb // tn, kb // tk
    nk = JK * nkt // kgroup

    def body(*refs):
        a_ref, b_ref = refs[:2]
        acc_ref = refs[-1]
        if kgroup == 1:
            prod = jnp.dot(a_ref[...].astype(BF16), b_ref[...].astype(BF16), preferred_element_type=F32)
        else:
            prod = jnp.dot(a_ref[0].astype(BF16), b_ref[0].astype(BF16), preferred_element_type=F32)
            for g in range(1, kgroup):
                prod += jnp.dot(a_ref[g].astype(BF16), b_ref[g].astype(BF16), preferred_element_type=F32)
        if bias is not None:
            prod = prod + refs[2][...]
        if resid is None:
            _accumulate(prod, refs[-2], acc_ref, pl.program_id(2), nk)
            return
        h_ref, g_ref, y_ref, o_ref = refs[2:6]
        k = pl.program_id(2)

        @pl.when(k == 0)
        def _():
            acc_ref[...] = prod

        @pl.when(k > 0)
        def _():
            acc_ref[...] += prod

        @pl.when(k == nk - 1)
        def _():
            y = acc_ref[...]
            y_ref[...] = y
            o_ref[...] = h_ref[...] + (resid[2] * g_ref[...]) * y

    if kgroup > 1:
        assert JN == 1 and nkt == 1 and not (a_natural or b_natural) and a_sel is None
        a3 = a3.reshape(nk, kgroup, M, kb)
        b4 = b4.reshape(nk, kgroup, kb, nb)
        a_spec = pl.BlockSpec((None, kgroup, tm, kb), lambda j, i, k: (k, 0, i, 0))
    elif a_natural:
        a_spec = pl.BlockSpec((tm, tk), lambda j, i, k: (i, k))
    elif a_sel is not None:
        a_spec = pl.BlockSpec((None, tm, tk), lambda j, i, k: (a_sel, i, k))
    else:
        a_spec = pl.BlockSpec((None, tm, tk), lambda j, i, k: (k // nkt, i, k % nkt))
    if kgroup > 1:
        b_spec = pl.BlockSpec((None, kgroup, kb, tn), lambda j, i, k: (k, 0, 0, j))
    elif b_natural:
        b_spec = pl.BlockSpec((tk, tn), lambda j, i, k: (k, j))
    else:
        b_spec = pl.BlockSpec((None, None, tk, tn), lambda j, i, k: (k // nkt, j // ntn, k % nkt, j % ntn))
    in_specs = [a_spec, b_spec]
    args = [a3, b4]
    if bias is not None:
        assert natural and nk == 1
        in_specs.append(pl.BlockSpec((1, tn), lambda j, i, k: (0, j)))
        args.append(bias)
    if natural:
        out_shape = jax.ShapeDtypeStruct((M, JN * nb), out_dtype)
        o_spec = pl.BlockSpec((tm, tn), lambda j, i, k: (i, j))
    else:
        out_shape = jax.ShapeDtypeStruct((JN, M, nb), out_dtype)
        o_spec = pl.BlockSpec((None, tm, tn), lambda j, i, k: (j // ntn, i, j % ntn))
    out_specs, out_shapes = [o_spec], [out_shape]
    if resid is not None:
        assert natural and bias is None and out_dtype == F32
        in_specs += [o_spec, pl.BlockSpec((1, tn), lambda j, i, k: (0, j))]
        args += [resid[0], resid[1]]
        out_specs, out_shapes = [o_spec, o_spec], [out_shape, out_shape]
    res = _call(body, name=name, grid=(JN * ntn, M // tm, nk), in_specs=in_specs, out_specs=out_specs,
                out_shape=out_shapes, scratch_shapes=[pltpu.VMEM((tm, tn), F32)], args=args,
                semantics=("parallel", "parallel", "arbitrary"), side=side)
    outs = res if side is None else res[0]
    ans = outs[0] if resid is None else (outs[0], outs[1])
    return ans if side is None else (ans, res[1])


def mm_nt(a3, b4, out_dtype, name, natural=False, a_natural=False, kgroup=1, side=None):
    JK, JN, nb, kb = b4.shape
    M = a3.shape[0] if a_natural else a3.shape[1]
    tm = min(M, 512)
    tn = _tile(nb, 1024)
    ntn = nb // tn
    nk = JK // kgroup

    def body(a_ref, b_ref, o_ref, acc_ref):
        nt = (((1,), (1,)), ((), ()))
        if kgroup == 1:
            prod = lax.dot_general(a_ref[...].astype(BF16), b_ref[...].astype(BF16), nt, preferred_element_type=F32)
        else:
            prod = lax.dot_general(a_ref[0].astype(BF16), b_ref[0].astype(BF16), nt, preferred_element_type=F32)
            for g in range(1, kgroup):
                prod += lax.dot_general(a_ref[g].astype(BF16), b_ref[g].astype(BF16), nt, preferred_element_type=F32)
        _accumulate(prod, o_ref, acc_ref, pl.program_id(2), nk)

    if kgroup > 1:
        assert JN == 1 and not a_natural
        a3 = a3.reshape(nk, kgroup, M, kb)
        b4 = b4.reshape(nk, kgroup, nb, kb)
        a_spec = pl.BlockSpec((None, kgroup, tm, kb), lambda j, i, k: (k, 0, i, 0))
        b_spec = pl.BlockSpec((None, kgroup, tn, kb), lambda j, i, k: (k, 0, j, 0))
    elif a_natural:
        a_spec = pl.BlockSpec((tm, kb), lambda j, i, k: (i, k))
        b_spec = pl.BlockSpec((None, None, tn, kb), lambda j, i, k: (k, j // ntn, j % ntn, 0))
    else:
        a_spec = pl.BlockSpec((None, tm, kb), lambda j, i, k: (k, i, 0))
        b_spec = pl.BlockSpec((None, None, tn, kb), lambda j, i, k: (k, j // ntn, j % ntn, 0))
    if natural:
        out_shape = jax.ShapeDtypeStruct((M, JN * nb), out_dtype)
        o_spec = pl.BlockSpec((tm, tn), lambda j, i, k: (i, j))
    else:
        out_shape = jax.ShapeDtypeStruct((JN, M, nb), out_dtype)
        o_spec = pl.BlockSpec((None, tm, tn), lambda j, i, k: (j // ntn, i, j % ntn))
    res = _call(body, name=name, grid=(JN * ntn, M // tm, nk), in_specs=[a_spec, b_spec], out_specs=[o_spec],
                out_shape=[out_shape], scratch_shapes=[pltpu.VMEM((tm, tn), F32)], args=[a3, b4],
                semantics=("parallel", "parallel", "arbitrary"), side=side)
    return res[0] if side is None else (res[0][0], res[1])


def _row_tile(S):
    return min(S, 256)


def _adaln(h, gain, shift, scale):
    y = h * lax.rsqrt(jnp.mean(h * h, axis=-1, keepdims=True) + NORM_EPS) * gain
    return y * (1.0 + scale) + shift


def adaln_fwd(h, gain, shift, scale, name):
    S, D = h.shape
    tm = _row_tile(S)

    def body(h_ref, g_ref, sh_ref, sc_ref, u_ref, ut_ref):
        u = _adaln(h_ref[...], g_ref[...], sh_ref[...], sc_ref[...])
        u_ref[...] = u.astype(BF16)
        ut_ref[...] = u.T.astype(BF16)

    row = pl.BlockSpec((tm, D), lambda i: (i, 0))
    vec = pl.BlockSpec((1, D), lambda i: (0, 0))
    return pl.pallas_call(
        body, name=name, grid=(S // tm,), in_specs=[row, vec, vec, vec],
        out_specs=[row, pl.BlockSpec((D, tm), lambda i: (0, i))],
        out_shape=[jax.ShapeDtypeStruct((S, D), BF16), jax.ShapeDtypeStruct((D, S), BF16)],
        compiler_params=_params("parallel"),
    )(h, gain, shift, scale)


def _gate_specs(S, D, tm, nxt):
    if nxt is None:
        return [], [], [], []
    row = pl.BlockSpec((tm, D), lambda i: (i, 0))
    vec = pl.BlockSpec((1, D), lambda i: (0, 0))
    return ([row, vec], [nxt[0], nxt[1]], [row, vec],
            [jax.ShapeDtypeStruct((S, D), BF16), jax.ShapeDtypeStruct((1, D), F32)])


def _gate_grads(dh, nxt, y_ref, g_ref, dy_ref, dg_ref):
    coef = nxt[2]
    dy_ref[...] = ((coef * g_ref[...]) * dh).astype(BF16)

    @pl.when(pl.program_id(0) == 0)
    def _():
        dg_ref[...] = jnp.zeros_like(dg_ref)

    dg_ref[...] += coef * jnp.sum(dh * y_ref[...], axis=0, keepdims=True)


def adaln_bwd(h, gain, shift, scale, du, dres, name, nxt=None):
    S, D = h.shape
    tm = _row_tile(S)

    def body(h_ref, g_ref, sh_ref, sc_ref, du_ref, dres_ref, *rest):
        dh_ref, dg_ref, dsh_ref, dsc_ref = rest[-6:-2] if nxt is not None else rest
        _, vjp = jax.vjp(_adaln, h_ref[...], g_ref[...], sh_ref[...], sc_ref[...])
        dh, dg, dsh, dsc = vjp(du_ref[...].astype(F32))
        dh = dres_ref[...] + dh
        dh_ref[...] = dh

        @pl.when(pl.program_id(0) == 0)
        def _():
            dg_ref[...] = jnp.zeros_like(dg_ref)
            dsh_ref[...] = jnp.zeros_like(dsh_ref)
            dsc_ref[...] = jnp.zeros_like(dsc_ref)

        dg_ref[...] += dg
        dsh_ref[...] += dsh
        dsc_ref[...] += dsc
        if nxt is not None:
            _gate_grads(dh, nxt, rest[0], rest[1], rest[-2], rest[-1])

    row = pl.BlockSpec((tm, D), lambda i: (i, 0))
    vec = pl.BlockSpec((1, D), lambda i: (0, 0))
    vs = jax.ShapeDtypeStruct((1, D), F32)
    xin, xargs, xout, xshape = _gate_specs(S, D, tm, nxt)
    return pl.pallas_call(
        body, name=name, grid=(S // tm,), in_specs=[row, vec, vec, vec, row, row] + xin,
        out_specs=[row, vec, vec, vec] + xout, out_shape=[jax.ShapeDtypeStruct((S, D), F32), vs, vs, vs] + xshape,
        compiler_params=_params("arbitrary"),
    )(h, gain, shift, scale, du, dres, *xargs)


def _swiglu(a, b):
    return a * jax.nn.sigmoid(a) * b


def ffn_in_act(u, w_in, name, side=None):
    J2, D, nb = w_in.shape
    J = J2 // 2
    S = u.shape[0]
    tm = _tile(S, 512, 16)

    def body(u_ref, wa_ref, wb_ref, ab_ref, h_ref, ht_ref):
        uv = u_ref[...]
        a = jnp.dot(uv, wa_ref[...], preferred_element_type=F32)
        b = jnp.dot(uv, wb_ref[...], preferred_element_type=F32)
        ab_ref[0] = a
        ab_ref[1] = b
        hv = _swiglu(a, b)
        h_ref[...] = hv.astype(BF16)
        ht_ref[...] = hv.T.astype(BF16)

    res = _call(body, name=name, grid=(J, S // tm),
                in_specs=[pl.BlockSpec((tm, D), lambda j, i: (i, 0)),
                          pl.BlockSpec((None, D, nb), lambda j, i: (j, 0, 0)),
                          pl.BlockSpec((None, D, nb), lambda j, i: (j + J, 0, 0))],
                out_specs=[pl.BlockSpec((2, None, tm, nb), lambda j, i: (0, j, i, 0)),
                           pl.BlockSpec((None, tm, nb), lambda j, i: (j, i, 0)),
                           pl.BlockSpec((None, nb, tm), lambda j, i: (j, 0, i))],
                out_shape=[jax.ShapeDtypeStruct((2, J, S, nb), F32), jax.ShapeDtypeStruct((J, S, nb), BF16),
                           jax.ShapeDtypeStruct((J, nb, S), BF16)],
                args=[u, w_in, w_in], semantics=("parallel", "parallel"), side=side)
    return res if side is None else (res[0], res[1])


def ffn_dact(dy, w_out4, ab4, name, side=None):
    J, nb, D = w_out4.shape
    S = dy.shape[0]
    tm = _tile(S, 512, 16)

    def body(dy_ref, w_ref, ab_ref, o_ref):
        dh = lax.dot_general(dy_ref[...], w_ref[...], (((1,), (1,)), ((), ())), preferred_element_type=F32)
        _, vjp = jax.vjp(_swiglu, ab_ref[0], ab_ref[1])
        da, db = vjp(dh)
        o_ref[0] = da.astype(BF16)
        o_ref[1] = db.astype(BF16)

    both = pl.BlockSpec((2, None, tm, nb), lambda j, i: (0, j, i, 0))
    res = _call(body, name=name, grid=(J, S // tm),
                in_specs=[pl.BlockSpec((tm, D), lambda j, i: (i, 0)),
                          pl.BlockSpec((None, nb, D), lambda j, i: (j, 0, 0)), both],
                out_specs=[both], out_shape=[jax.ShapeDtypeStruct((2, J, S, nb), BF16)],
                args=[dy, w_out4, ab4], semantics=("parallel", "parallel"), side=side)
    if side is None:
        return res[0].reshape(2 * J, S, nb)
    return res[0][0].reshape(2 * J, S, nb), res[1]


def colsum(x, name):
    S, N = x.shape
    tm = _row_tile(S)

    def body(x_ref, o_ref):
        @pl.when(pl.program_id(0) == 0)
        def _():
            o_ref[...] = jnp.zeros_like(o_ref)

        o_ref[...] += jnp.sum(x_ref[...].astype(F32), axis=0, keepdims=True)

    return pl.pallas_call(
        body, name=name, grid=(S // tm,), in_specs=[pl.BlockSpec((tm, N), lambda i: (i, 0))],
        out_specs=pl.BlockSpec((1, N), lambda i: (0, 0)), out_shape=jax.ShapeDtypeStruct((1, N), F32),
        compiler_params=_params("arbitrary"),
    )(x)


def _final_loss(h, gain, target):
    y = h * lax.rsqrt(jnp.mean(h * h, axis=-1, keepdims=True) + NORM_EPS) * gain
    err = y - target
    return 0.5 * jnp.sum(jnp.mean(err * err, axis=-1))


def final_loss_grad(h, gain, target, name, nxt):
    S, D = h.shape
    tm = _row_tile(S)

    def body(h_ref, g_ref, t_ref, y_ref, gate_ref, loss_ref, dh_ref, dg_ref, dy_ref, dgate_ref):
        loss, (dh, dg) = jax.value_and_grad(_final_loss, argnums=(0, 1))(h_ref[...], g_ref[...], t_ref[...])
        dh_ref[...] = dh

        @pl.when(pl.program_id(0) == 0)
        def _():
            loss_ref[...] = jnp.zeros_like(loss_ref)
            dg_ref[...] = jnp.zeros_like(dg_ref)

        loss_ref[...] += jnp.full(loss_ref.shape, loss, F32)
        dg_ref[...] += dg
        _gate_grads(dh, nxt, y_ref, gate_ref, dy_ref, dgate_ref)

    row = pl.BlockSpec((tm, D), lambda i: (i, 0))
    vec = pl.BlockSpec((1, D), lambda i: (0, 0))
    xin, xargs, xout, xshape = _gate_specs(S, D, tm, nxt)
    return pl.pallas_call(
        body, name=name, grid=(S // tm,), in_specs=[row, vec, row] + xin,
        out_specs=[pl.BlockSpec((1, LANES), lambda i: (0, 0)), row, vec] + xout,
        out_shape=[jax.ShapeDtypeStruct((1, LANES), F32), jax.ShapeDtypeStruct((S, D), F32),
                   jax.ShapeDtypeStruct((1, D), F32)] + xshape,
        compiler_params=_params("arbitrary"),
    )(h, gain, target, *xargs)


def _chunk_consts(H):
    C = HGRN_CHUNK
    t = lax.broadcasted_iota(jnp.int32, (H, C, C), 1)
    s = lax.broadcasted_iota(jnp.int32, (H, C, C), 2)
    return (s <= t).astype(F32), s <= t


def _hgrn_chunk(q_raw, f_raw, i_raw, lb, st):
    H, C, _ = q_raw.shape
    lower, causal = _chunk_consts(H)
    forget = lb + (1.0 - lb) * jax.nn.sigmoid(f_raw)
    g = jnp.log(forget)
    kk = 1.0 - forget
    qs = q_raw * jax.nn.sigmoid(q_raw)
    bnn = (((2,), (1,)), ((0,), (0,)))
    bnt = (((2,), (2,)), ((0,), (0,)))
    btn = (((1,), (1,)), ((0,), (0,)))
    b = lax.dot_general(lower, g, bnn, precision=lax.Precision.HIGHEST, preferred_element_type=F32)
    bm = b[:, C // 2 - 1:C // 2, :]
    bl = b[:, C - 1:C, :]
    inter = lax.dot_general((qs * jnp.exp(b)).astype(BF16), st.astype(BF16), bnt, preferred_element_type=F32)
    qt = (qs * jnp.exp(b - bm)).astype(BF16)
    kt = (kk * jnp.exp(bm - b)).astype(BF16)
    scores = lax.dot_general(qt, kt, bnt, preferred_element_type=F32)
    scores = jnp.where(causal, scores, 0.0)
    vb = i_raw.astype(BF16)
    out = inter + lax.dot_general(scores.astype(BF16), vb, bnn, preferred_element_type=F32)
    kdec = (kk * jnp.exp(bl - b)).astype(BF16)
    new_st = st * jnp.exp(bl) + lax.dot_general(vb, kdec, btn, preferred_element_type=F32)
    return out, new_st


def _heads(ref, rows, H):
    return jnp.stack([ref[rows, pl.ds(h * HGRN_HEAD, HGRN_HEAD)] for h in range(H)])


def hgrn_scan_fwd(proj, lb, name, side=None):
    S, D4 = proj.shape
    D = D4 // 4
    H = D // HGRN_HEAD
    C = HGRN_CHUNK
    R = min(S, 128)
    ncr = R // C

    def body(q_ref, f_ref, i_ref, lb_ref, o_ref, st_ref, state):
        @pl.when(pl.program_id(0) == 0)
        def _():
            state[...] = jnp.zeros_like(state)

        lbh = _heads(lb_ref, slice(None), H)

        def chunk(cc, carry):
            rows = pl.ds(pl.multiple_of(cc * C, C), C)
            st = state[...]
            st_ref[cc] = st.astype(BF16)
            out, new_st = _hgrn_chunk(_heads(q_ref, rows, H), _heads(f_ref, rows, H), _heads(i_ref, rows, H), lbh, st)
            for h in range(H):
                o_ref[rows, pl.ds(h * HGRN_HEAD, HGRN_HEAD)] = out[h]
            state[...] = new_st
            return carry

        lax.fori_loop(0, ncr, chunk, 0)

    col = lambda j: pl.BlockSpec((R, D), lambda i: (i, j))
    res = _call(body, name=name, grid=(S // R,),
                in_specs=[col(0), col(1), col(2), pl.BlockSpec((1, D), lambda i: (0, 0))],
                out_specs=[pl.BlockSpec((R, D), lambda i: (i, 0)),
                           pl.BlockSpec((ncr, H, HGRN_HEAD, HGRN_HEAD), lambda i: (i, 0, 0, 0))],
                out_shape=[jax.ShapeDtypeStruct((S, D), F32),
                           jax.ShapeDtypeStruct((S // C, H, HGRN_HEAD, HGRN_HEAD), BF16)],
                scratch_shapes=[pltpu.VMEM((H, HGRN_HEAD, HGRN_HEAD), F32)], args=[proj, proj, proj, lb],
                semantics=("arbitrary",), side=side)
    return res if side is None else (res[0], res[1])


def hgrn_scan_bwd(proj, lb, states, do, dg, name, side=None):
    S, D4 = proj.shape
    D = D4 // 4
    H = D // HGRN_HEAD
    C = HGRN_CHUNK
    R = min(S, 128)
    ncr = R // C
    ng = S // R

    def body(q_ref, f_ref, i_ref, lb_ref, st_ref, do_ref, dg_ref, dp_ref, dlb_ref, dstate):
        @pl.when(pl.program_id(0) == 0)
        def _():
            dstate[...] = jnp.zeros_like(dstate)
            dlb_ref[...] = jnp.zeros_like(dlb_ref)

        dp_ref[:, pl.ds(3 * D, D)] = dg_ref[...].astype(BF16)
        lbh = _heads(lb_ref, slice(None), H)

        def chunk(t, carry):
            cc = ncr - 1 - t
            rows = pl.ds(pl.multiple_of(cc * C, C), C)
            _, vjp = jax.vjp(_hgrn_chunk, _heads(q_ref, rows, H), _heads(f_ref, rows, H), _heads(i_ref, rows, H),
                             lbh, st_ref[cc].astype(F32))
            dq, df, di, dlb, dst = vjp((_heads(do_ref, rows, H), dstate[...]))
            for h in range(H):
                dp_ref[rows, pl.ds(h * HGRN_HEAD, HGRN_HEAD)] = dq[h].astype(BF16)
                dp_ref[rows, pl.ds(D + h * HGRN_HEAD, HGRN_HEAD)] = df[h].astype(BF16)
                dp_ref[rows, pl.ds(2 * D + h * HGRN_HEAD, HGRN_HEAD)] = di[h].astype(BF16)
                dlb_ref[:, pl.ds(h * HGRN_HEAD, HGRN_HEAD)] += dlb[h]
            dstate[...] = dst
            return carry

        lax.fori_loop(0, ncr, chunk, 0)

    col = lambda j: pl.BlockSpec((R, D), lambda i: (ng - 1 - i, j))
    res = _call(body, name=name, grid=(ng,),
                in_specs=[col(0), col(1), col(2), pl.BlockSpec((1, D), lambda i: (0, 0)),
                          pl.BlockSpec((ncr, H, HGRN_HEAD, HGRN_HEAD), lambda i: (ng - 1 - i, 0, 0, 0)),
                          pl.BlockSpec((R, D), lambda i: (ng - 1 - i, 0)),
                          pl.BlockSpec((R, D), lambda i: (ng - 1 - i, 0))],
                out_specs=[pl.BlockSpec((R, D4), lambda i: (ng - 1 - i, 0)), pl.BlockSpec((1, D), lambda i: (0, 0))],
                out_shape=[jax.ShapeDtypeStruct((S, D4), BF16), jax.ShapeDtypeStruct((1, D), F32)],
                scratch_shapes=[pltpu.VMEM((H, HGRN_HEAD, HGRN_HEAD), F32)],
                args=[proj, proj, proj, lb, states, do, dg], semantics=("arbitrary",), side=side)
    return res if side is None else (res[0], res[1])


def _head_out(o, g, gain):
    y = o * lax.rsqrt(jnp.mean(o * o, axis=-1, keepdims=True) + NORM_EPS) * gain
    return y * jax.nn.sigmoid(g)


def hgrn_post_fwd(o, proj, gain, name):
    S, D = o.shape
    H = D // HGRN_HEAD
    tm = _row_tile(S)

    def body(o_ref, g_ref, gain_ref, z_ref, zt_ref):
        for h in range(H):
            ls = pl.ds(h * HGRN_HEAD, HGRN_HEAD)
            z = _head_out(o_ref[:, ls], g_ref[:, ls], gain_ref[:, ls])
            z_ref[:, ls] = z.astype(BF16)
            zt_ref[ls, :] = z.T.astype(BF16)

    row = pl.BlockSpec((tm, D), lambda i: (i, 0))
    return pl.pallas_call(
        body, name=name, grid=(S // tm,),
        in_specs=[row, pl.BlockSpec((tm, D), lambda i: (i, 3)), pl.BlockSpec((1, D), lambda i: (0, 0))],
        out_specs=[row, pl.BlockSpec((D, tm), lambda i: (0, i))],
        out_shape=[jax.ShapeDtypeStruct((S, D), BF16), jax.ShapeDtypeStruct((D, S), BF16)],
        compiler_params=_params("parallel"),
    )(o, proj, gain)


def hgrn_post_bwd(o, proj, gain, dz, name):
    S, D = o.shape
    H = D // HGRN_HEAD
    tm = _row_tile(S)

    def body(o_ref, g_ref, gain_ref, dz_ref, do_ref, dg_ref, dgain_ref):
        @pl.when(pl.program_id(0) == 0)
        def _():
            dgain_ref[...] = jnp.zeros_like(dgain_ref)

        for h in range(H):
            ls = pl.ds(h * HGRN_HEAD, HGRN_HEAD)
            _, vjp = jax.vjp(_head_out, o_ref[:, ls], g_ref[:, ls], gain_ref[:, ls])
            do, dg, dgain = vjp(dz_ref[:, ls].astype(F32))
            do_ref[:, ls] = do
            dg_ref[:, ls] = dg
            dgain_ref[:, ls] += dgain

    row = pl.BlockSpec((tm, D), lambda i: (i, 0))
    vec = pl.BlockSpec((1, D), lambda i: (0, 0))
    return pl.pallas_call(
        body, name=name, grid=(S // tm,),
        in_specs=[row, pl.BlockSpec((tm, D), lambda i: (i, 3)), vec, row], out_specs=[row, row, vec],
        out_shape=[jax.ShapeDtypeStruct((S, D), F32), jax.ShapeDtypeStruct((S, D), F32),
                   jax.ShapeDtypeStruct((1, D), F32)],
        compiler_params=_params("arbitrary"),
    )(o, proj, gain, dz)


def lower_bound_fwd(l0, l1, name):
    def body(a_ref, b_ref, o_ref):
        o_ref[...] = jax.nn.sigmoid(a_ref[...] - b_ref[...])

    return pl.pallas_call(body, name=name, out_shape=jax.ShapeDtypeStruct(l0.shape, F32))(l0, l1)


def lower_bound_bwd(l0, l1, dlb, name):
    def body(a_ref, b_ref, d_ref, o0_ref, o1_ref):
        s = jax.nn.sigmoid(a_ref[...] - b_ref[...])
        d0 = d_ref[...] * s * (1.0 - s)
        o0_ref[...] = d0
        o1_ref[...] = -d0

    sd = jax.ShapeDtypeStruct(l0.shape, F32)
    return pl.pallas_call(body, name=name, out_shape=[sd, sd])(l0, l1, dlb)


def _attn_tile(q, kp, kc, vp, vc, sink, first):
    W = ATT_WINDOW
    nt = (((1,), (1,)), ((), ()))
    tn = (((0,), (0,)), ((), ()))
    qb = q.astype(BF16)
    scale = ATT_HEAD ** -0.5
    sp = lax.dot_general(kp.astype(BF16), qb, nt, preferred_element_type=F32) * scale
    sc = lax.dot_general(kc.astype(BF16), qb, nt, preferred_element_type=F32) * scale
    qi = lax.broadcasted_iota(jnp.int32, sp.shape, 1) & (W - 1)
    kj = lax.broadcasted_iota(jnp.int32, sp.shape, 0)
    sp = jnp.where((kj > qi) & jnp.logical_not(first), sp, NEG_INF)
    sc = jnp.where(kj <= qi, sc, NEG_INF)
    m = jnp.maximum(jnp.maximum(jnp.max(sp, axis=0, keepdims=True), jnp.max(sc, axis=0, keepdims=True)), sink)
    pp = jnp.exp(sp - m)
    pc = jnp.exp(sc - m)
    denom = jnp.sum(pp, axis=0, keepdims=True) + jnp.sum(pc, axis=0, keepdims=True) + jnp.exp(sink - m)
    out = lax.dot_general(vp.astype(BF16), (pp / denom).astype(BF16), tn, preferred_element_type=F32)
    return out + lax.dot_general(vc.astype(BF16), (pc / denom).astype(BF16), tn, preferred_element_type=F32)


def _attn_specs(G, W, Dh):
    q_spec = pl.BlockSpec((None, G, W, Dh), lambda j, n: (j, 0, n, 0))
    prev = pl.BlockSpec((None, W, Dh), lambda j, n: (j, jnp.maximum(n - 1, 0), 0))
    cur = pl.BlockSpec((None, W, Dh), lambda j, n: (j, n, 0))
    sink = pl.BlockSpec((None, 1, G * W), lambda j, n: (j, 0, 0))
    return q_spec, prev, cur, sink


def attn_fwd(q4, k3, v3, sink, name, side=None):
    NKV, G, S, Dh = q4.shape
    W = ATT_WINDOW

    def body(q_ref, kp_ref, kc_ref, vp_ref, vc_ref, s_ref, o_ref):
        first = pl.program_id(1) == 0
        out_t = _attn_tile(q_ref[...].reshape(G * W, Dh), kp_ref[...], kc_ref[...], vp_ref[...], vc_ref[...],
                           s_ref[...], first)
        o_ref[...] = out_t.T.reshape(G, W, Dh)

    q_spec, prev, cur, sk = _attn_specs(G, W, Dh)
    res = _call(body, name=name, grid=(NKV, S // W), in_specs=[q_spec, prev, cur, prev, cur, sk], out_specs=[q_spec],
                out_shape=[jax.ShapeDtypeStruct(q4.shape, F32)], args=[q4, k3, k3, v3, v3, sink],
                semantics=("parallel", "parallel"), side=side)
    return res[0] if side is None else (res[0][0], res[1])


def attn_bwd(q4, k3, v3, sink, do4, name, side=None):
    NKV, G, S, Dh = q4.shape
    W = ATT_WINDOW
    nb = S // W

    def body(q_ref, kp_ref, kc_ref, vp_ref, vc_ref, s_ref, do_ref, dq_ref, dkp_ref, dkc_ref, dvp_ref, dvc_ref,
             ds_ref):
        first = pl.program_id(1) == 0
        _, vjp = jax.vjp(functools.partial(_attn_tile, first=first), q_ref[...].reshape(G * W, Dh), kp_ref[...],
                         kc_ref[...], vp_ref[...], vc_ref[...], s_ref[...])
        dq, dkp, dkc, dvp, dvc, ds = vjp(do_ref[...].reshape(G * W, Dh).T)
        dq_ref[...] = dq.reshape(G, W, Dh)
        dkp_ref[...] = dkp
        dkc_ref[...] = dkc
        dvp_ref[...] = dvp
        dvc_ref[...] = dvc

        @pl.when(first)
        def _():
            ds_ref[...] = jnp.zeros_like(ds_ref)

        head_of = lax.broadcasted_iota(jnp.int32, (G * W, G), 0) // W
        sel = (head_of == lax.broadcasted_iota(jnp.int32, (G * W, G), 1)).astype(F32)
        ds_ref[...] += jnp.dot(ds, sel, precision=lax.Precision.HIGHEST, preferred_element_type=F32)

    q_spec, prev, cur, sk = _attn_specs(G, W, Dh)
    part = pl.BlockSpec((None, None, W, Dh), lambda j, n: (j, n, 0, 0))
    ps = jax.ShapeDtypeStruct((NKV, nb, W, Dh), F32)
    res = _call(body, name=name, grid=(NKV, nb), in_specs=[q_spec, prev, cur, prev, cur, sk, q_spec],
                out_specs=[q_spec, part, part, part, part, pl.BlockSpec((None, 1, G), lambda j, n: (j, 0, 0))],
                out_shape=[jax.ShapeDtypeStruct(q4.shape, F32), ps, ps, ps, ps,
                           jax.ShapeDtypeStruct((NKV, 1, G), F32)],
                args=[q4, k3, k3, v3, v3, sink, do4], semantics=("parallel", "arbitrary"), side=side)
    outs = res if side is None else res[0]
    ans = (outs[0], outs[1:5], outs[5])
    return ans if side is None else (ans, res[1])


def band_combine(kc, kp, vc, vp, name):
    NKV, nb, W, Dh = kc.shape

    def shifted_sum(c_ref, p_ref, o_ref):
        o_ref[pl.ds(0, nb - 1)] = c_ref[pl.ds(0, nb - 1)] + p_ref[pl.ds(1, nb - 1)]
        o_ref[nb - 1] = c_ref[nb - 1]

    def body(kc_ref, kp_ref, vc_ref, vp_ref, dk_ref, dv_ref):
        shifted_sum(kc_ref, kp_ref, dk_ref)
        shifted_sum(vc_ref, vp_ref, dv_ref)

    spec = pl.BlockSpec((None, nb, W, Dh), lambda j: (j, 0, 0, 0))
    sd = jax.ShapeDtypeStruct((NKV, nb, W, Dh), F32)
    dk, dv = pl.pallas_call(
        body, name=name, grid=(NKV,), in_specs=[spec] * 4, out_specs=[spec] * 2, out_shape=[sd, sd],
        compiler_params=_params("parallel"),
    )(kc, kp, vc, vp)
    return dk.reshape(NKV, nb * W, Dh), dv.reshape(NKV, nb * W, Dh)


def rope_tables(S):
    half = ROT_DIM // 2
    inv_freq = jnp.power(jnp.float32(ROPE_THETA), -jnp.arange(0, ROT_DIM, 2, dtype=F32) / ROT_DIM)
    ang = jnp.arange(S, dtype=F32)[:, None] * inv_freq[None, :]
    sin, cos = jnp.sin(ang), jnp.cos(ang)
    zeros = jnp.zeros((S, ATT_HEAD - ROT_DIM), F32)
    z8 = jnp.zeros((S, half), F32)
    cfull = jnp.concatenate([cos, cos, jnp.ones((S, ATT_HEAD - ROT_DIM), F32)], axis=1)
    s_next = jnp.concatenate([-sin, z8, zeros], axis=1)
    s_prev = jnp.concatenate([z8, sin, zeros], axis=1)
    two = lambda t: jnp.concatenate([t, t], axis=1)
    return two(cfull), two(s_next), two(s_prev)


def rope(x, tables, sign, name):
    S, Wd = x.shape
    tm = _row_tile(S)
    rep = Wd // LANES
    half = ROT_DIM // 2

    def body(x_ref, c_ref, sn_ref, sp_ref, o_ref):
        xv = x_ref[...]
        c = jnp.tile(c_ref[...], (1, rep))
        sn = jnp.tile(sn_ref[...], (1, rep))
        sp = jnp.tile(sp_ref[...], (1, rep))
        if sign > 0:
            nxt = pltpu.roll(xv, Wd - half, 1)
            prv = pltpu.roll(xv, half, 1)
            o_ref[...] = xv * c + nxt * sn + prv * sp
        else:
            o_ref[...] = xv * c + pltpu.roll(xv * sn, half, 1) + pltpu.roll(xv * sp, Wd - half, 1)

    row = pl.BlockSpec((tm, Wd), lambda i: (i, 0))
    tab = pl.BlockSpec((tm, LANES), lambda i: (i, 0))
    return pl.pallas_call(
        body, name=name, grid=(S // tm,), in_specs=[row, tab, tab, tab], out_specs=row,
        out_shape=jax.ShapeDtypeStruct((S, Wd), F32), compiler_params=_params("parallel"),
    )(x, *tables)


def cond_proj(c_all, w, bias, name):
    B, D = c_all.shape
    N = w.shape[1]
    tn = _tile(N, 512)

    def body(c_ref, w_ref, b_ref, o_ref):
        cv = c_ref[...]
        cs = (cv * jax.nn.sigmoid(cv)).astype(BF16)
        o_ref[...] = jnp.dot(cs, w_ref[...].astype(BF16), preferred_element_type=F32) + b_ref[...]

    return pl.pallas_call(
        body, name=name, grid=(N // tn,),
        in_specs=[pl.BlockSpec((B, D), lambda j: (0, 0)), pl.BlockSpec((D, tn), lambda j: (0, j)),
                  pl.BlockSpec((1, tn), lambda j: (0, j))],
        out_specs=pl.BlockSpec((B, tn), lambda j: (0, j)), out_shape=jax.ShapeDtypeStruct((B, N), F32),
        compiler_params=_params("parallel"),
    )(c_all, w, bias)


def cond_grad(c_all, dmod, name):
    B, D = c_all.shape
    N = dmod.shape[1]
    tn = _tile(N, 512)

    def body(c_ref, d_ref, o_ref):
        cv = c_ref[...]
        cs = (cv * jax.nn.sigmoid(cv)).astype(BF16)
        o_ref[...] = lax.dot_general(cs, d_ref[...].astype(BF16), (((0,), (0,)), ((), ())),
                                     preferred_element_type=F32)

    return pl.pallas_call(
        body, name=name, grid=(N // tn,),
        in_specs=[pl.BlockSpec((B, D), lambda j: (0, 0)), pl.BlockSpec((B, tn), lambda j: (0, j))],
        out_specs=pl.BlockSpec((D, tn), lambda j: (0, j)), out_shape=jax.ShapeDtypeStruct((D, N), F32),
        compiler_params=_params("parallel"),
    )(c_all, dmod)


def rowsum(g, name):
    B, N = g.shape
    tn = N if B * N * 4 <= (4 << 20) else _tile(N, 8192)

    def body(g_ref, o_ref):
        acc = g_ref[0:1, :]
        for r in range(1, B):
            acc = acc + g_ref[r:r + 1, :]
        o_ref[...] = acc

    return pl.pallas_call(
        body, name=name, grid=(N // tn,), in_specs=[pl.BlockSpec((B, tn), lambda j: (0, j))],
        out_specs=pl.BlockSpec((1, tn), lambda j: (0, j)), out_shape=jax.ShapeDtypeStruct((1, N), F32),
        compiler_params=_params("parallel"),
    )(g)


def _adam_rows(R, C):
    limit = 3 << 19
    if R * C * 4 <= limit or R % 8:
        return R
    best = 8
    for t in range(8, R + 1, 8):
        if R % t == 0 and t * C * 4 <= limit:
            best = t
    return best


def adamw(w3, m3, v3, j, g3, name, col=(0, 1), into=None, side=None):
    n, R, C = w3.shape
    P = g3.shape[0]
    cp, ncol = col
    Cp = C // ncol
    tr = _adam_rows(R, Cp)

    def body(*refs):
        w_ref, m_ref, v_ref, g_ref = refs[:4]
        go_ref, d_ref, mo_ref, vo_ref = refs[-4:]
        g = g_ref[0].astype(F32)
        for p in range(1, P):
            g = g + g_ref[p].astype(F32)
        mn = ADAM_B1 * m_ref[...] + (1.0 - ADAM_B1) * g
        vn = ADAM_B2 * v_ref[...] + (1.0 - ADAM_B2) * jnp.square(g)
        m_hat = mn / (1.0 - ADAM_B1 ** ADAM_STEP)
        v_hat = vn / (1.0 - ADAM_B2 ** ADAM_STEP)
        go_ref[...] = g
        d_ref[...] = -ADAM_LR * (m_hat / (jnp.sqrt(v_hat) + ADAM_EPS) + ADAM_WD * w_ref[...])
        mo_ref[...] = mn
        vo_ref[...] = vn

    spec = pl.BlockSpec((None, tr, Cp), lambda i: (j, i, cp))
    g_spec = pl.BlockSpec((P, tr, Cp), lambda i: (0, i, 0))
    sd = jax.ShapeDtypeStruct((n, R, C), F32)
    in_specs, args, aliases = [spec, spec, spec, g_spec], [w3, m3, v3, g3], {}
    if side is not None:
        assert into is None
        return _call(body, name=name, grid=(R // tr,), in_specs=in_specs, out_specs=[spec] * 4, out_shape=[sd] * 4,
                     args=args, semantics=("parallel",), side=side)
    if into is not None:
        in_specs += [_ANY] * 4
        args += list(into)
        aliases = {4 + k: k for k in range(4)}
    return pl.pallas_call(
        body, name=name, grid=(R // tr,), in_specs=in_specs, out_specs=[spec] * 4, out_shape=[sd] * 4,
        input_output_aliases=aliases, compiler_params=_params("parallel"),
    )(*args)


def _place():
    return lax.axis_index("x"), lax.axis_index("y"), lax.axis_index("c")


def _slot(p):
    return 4 * p[0] + 2 * p[1] + p[2]


def gather_side(items):
    xs = [a for a, _ in items]
    n = len(xs)

    def copy(ins, outs, sems, t, k, block, to, from_input=False):
        dst = outs[t].at[_slot(block)]
        return pltpu.make_async_remote_copy(
            src_ref=ins[t] if from_input else dst, dst_ref=dst, send_sem=sems[0].at[t, k],
            recv_sem=sems[1].at[t, k], device_id=to, device_id_type=MESH)

    def peers():
        x, y, c = _place()
        return (x, y, c), (x, y, 1 - c), [(1 - x, y), (x, 1 - y), (1 - x, 1 - y)]

    def start(ins, outs, sems):
        me, sibling, chips = peers()
        c = me[2]
        for t in range(n):
            pltpu.make_async_copy(ins[t], outs[t].at[_slot(me)], sems[2].at[t]).start()
            copy(ins, outs, sems, t, 0, me, sibling, True).start()
            for j, chip in enumerate(chips):
                copy(ins, outs, sems, t, 1 + j, me, (*chip, c), True).start()

    def finish(ins, outs, sems):
        me, sibling, chips = peers()
        c = me[2]
        for t in range(n):
            for j, chip in enumerate(chips):
                copy(ins, outs, sems, t, 1 + j, (*chip, c), me).wait_recv()
                copy(ins, outs, sems, t, 4 + j, (*chip, c), sibling).start()
        for t in range(n):
            copy(ins, outs, sems, t, 0, sibling, me).wait_recv()
            for j, chip in enumerate(chips):
                copy(ins, outs, sems, t, 4 + j, (*chip, 1 - c), me).wait_recv()
        for t in range(n):
            copy(ins, outs, sems, t, 0, me, sibling, True).wait_send()
            for j, chip in enumerate(chips):
                copy(ins, outs, sems, t, 1 + j, me, (*chip, c), True).wait_send()
                copy(ins, outs, sems, t, 4 + j, (*chip, c), sibling).wait_send()
            pltpu.make_async_copy(ins[t], outs[t].at[_slot(me)], sems[2].at[t]).wait()

    return _Side(xs, [jax.ShapeDtypeStruct((N_DEV,) + a.shape, a.dtype) for a in xs],
                 [pltpu.SemaphoreType.DMA((n, 7)), pltpu.SemaphoreType.DMA((n, 7)), pltpu.SemaphoreType.DMA((n,))],
                 start, finish)


def scatter_side(items):
    gs = [g for g, _ in items]
    n = len(gs)

    def part_shape(g, part):
        _, R, C = g.shape
        if part is None:
            return R, C
        axis, _, cnt = part
        return (R // cnt, C) if axis == 0 else (R, C // cnt)

    def block(ref, slot, g, part):
        if part is None:
            return ref.at[slot]
        axis, idx, cnt = part
        R, C = part_shape(g, part)
        return ref.at[slot, pl.ds(idx * R, R)] if axis == 0 else ref.at[slot, :, pl.ds(idx * C, C)]

    def copies(ins, outs, sems):
        x, y, c = _place()
        me = (x, y, c)
        out = []
        for t, (g, part) in enumerate(items):
            out.append(pltpu.make_async_copy(block(ins[t], _slot(me), g, part), outs[t].at[_slot(me)], sems[2].at[t]))
            for r in range(1, N_DEV):
                peer = (1 - x if r & 4 else x, 1 - y if r & 2 else y, 1 - c if r & 1 else c)
                out.append(pltpu.make_async_remote_copy(
                    src_ref=block(ins[t], _slot(peer), g, part), dst_ref=outs[t].at[_slot(me)],
                    send_sem=sems[0].at[t, r - 1], recv_sem=sems[1].at[t, r - 1], device_id=peer, device_id_type=MESH))
        return out

    def start(ins, outs, sems):
        for cp in copies(ins, outs, sems):
            cp.start()

    def finish(ins, outs, sems):
        for cp in copies(ins, outs, sems):
            cp.wait()

    return _Side(gs, [jax.ShapeDtypeStruct((N_DEV,) + part_shape(g, part), g.dtype) for g, part in items],
                 [pltpu.SemaphoreType.DMA((n, 7)), pltpu.SemaphoreType.DMA((n, 7)), pltpu.SemaphoreType.DMA((n,))],
                 start, finish)


def exchange(side, name):
    def body(*refs):
        n_in, n_out = len(side.inputs), len(side.out_shapes)
        ins, outs, sems = refs[:n_in], refs[n_in:n_in + n_out], refs[n_in + n_out:]
        side.start(ins, outs, sems)
        side.finish(ins, outs, sems)

    return pl.pallas_call(
        body, name=name, in_specs=[_ANY] * len(side.inputs), out_specs=[_ANY] * len(side.out_shapes),
        out_shape=side.out_shapes, scratch_shapes=side.sem_shapes,
    )(*side.inputs)


class _Part:
    def __init__(self, key, axis, index, count):
        self.key, self.axis, self.index, self.count = key, axis, index, count

    def __hash__(self):
        return hash((self.key, self.axis, self.index, self.count))

    def __eq__(self, other):
        return isinstance(other, _Part) and (self.key, self.axis, self.index, self.count) == (
            other.key, other.axis, other.index, other.count)


def _rows(key, p):
    return _Part(key, 0, p, 2)


def _cols(key, p):
    return _Part(key, 1, p, 2)


class _Plan:
    def __init__(self, plan, make_side):
        self.plan, self.make_side, self.source, self.got = plan, make_side, {}, {}

    def item(self, k):
        return (self.source[k.key], (k.axis, k.index, k.count)) if isinstance(k, _Part) else (self.source[k], None)

    def run(self, fn, *args, name, **kw):
        keys = self.plan.get(name)
        if not keys:
            return fn(*args, name=name, **kw)
        result, outs = fn(*args, name=name, side=self.make_side([self.item(k) for k in keys]), **kw)
        self.got.update(zip(keys, outs))
        return result


GATHER_PLAN = {
    "l0s0_in": [("out", 0, 0), "hout", "kv"],
    "l0s0_out": ["hin"],
    "l0s1_proj": [("out", 0, 1)],
    "l0s1_scan": [("in", 0, 1)],
    "l0s2_in": [("in", 1, 0)],
    "l0s2_out": [("out", 1, 0)],
    "l1s0_in": [("out", 1, 1), "q", "o"],
    "l1s1_attn": [("in", 1, 1)],
}
GATHER_FIRST = [("in", 0, 0)]
SCATTER_PLAN = {
    "l1s2_dact": [_cols(("out", 1, 1), 0)],
    "l1s2_dwin0": [_cols(("out", 1, 1), 1)],
    "l1s2_dwin1": [_rows(("in", 1, 1, 0), 0)],
    "l1s2_du": [_rows(("in", 1, 1, 0), 1)],
    "l1s1_dattn": [_rows(("in", 1, 1, 1), 0), _rows(("in", 1, 1, 1), 1), "o"],
    "l1s0_dwout": ["q"],
    "l1s0_dact": [_cols(("out", 1, 0), 0)],
    "l1s0_dwin0": [_cols(("out", 1, 0), 1)],
    "l1s0_dwin1": [_rows(("in", 1, 0, 0), 0)],
    "l1s0_du": [_rows(("in", 1, 0, 0), 1), _rows(("in", 1, 0, 1), 0)],
    "l0s2_dwout": [_rows(("in", 1, 0, 1), 1)],
    "l0s2_dact": [_cols(("out", 0, 1), 0)],
    "l0s2_dwin0": [_cols(("out", 0, 1), 1)],
    "l0s2_dwin1": [_rows(("in", 0, 1, 0), 0)],
    "l0s2_du": [_rows(("in", 0, 1, 0), 1)],
    "l0s1_dwout": ["kv"],
    "l0s1_dscan": [_rows(("in", 0, 1, 1), 0), _rows(("in", 0, 1, 1), 1), "hout"],
    "l0s1_dwin1": [_rows(("hin", 0), 0)],
    "l0s1_du": [_rows(("hin", 0), 1), _rows(("hin", 1), 0)],
    "l0s0_dwout": [_rows(("hin", 1), 1)],
    "l0s0_dact": [_cols(("out", 0, 0), 0)],
    "l0s0_dwin0": [_cols(("out", 0, 0), 1)],
    "l0s0_dwin1": [_rows(("in", 0, 0, 0), 0)],
    "l0s0_du": [_rows(("in", 0, 0, 0), 1), _rows(("in", 0, 0, 1), 0)],
    "adam_w_ada0": [_rows(("in", 0, 0, 1), 1)],
}
GRAD_TM, GRAD_TK = 512, 4096


def _ffn_fwd(gp, W, h, gate, u, l, i, tag):
    ab4, h3, h3t = gp.run(ffn_in_act, u, W[("in", l, i)], name=tag + "_in")
    w_out = W[("out", l, i)]
    J = h3.shape[0]
    y, h_new = gp.run(mm_nn, h3, w_out.reshape(J, 1, -1, w_out.shape[-1]), F32, name=tag + "_out", natural=True,
                      resid=(h, gate, 0.5), kgroup=J)
    return y, h_new, (ab4, h3t)


def _ffn_bwd(sp, W, dy, ut, ab4, h3t, l, i, tag):
    w_in, w_out = W[("in", l, i)], W[("out", l, i)]
    J, nb, S = h3t.shape
    D = w_out.shape[-1]
    dw_out = sp.run(mm_nn, h3t.reshape(1, J * nb, S), dy[None, None], BF16, name=tag + "_dwout", natural=True,
                    tm=w_out.shape[1], tk=GRAD_TK)
    sp.source[("out", l, i)] = dw_out.reshape(w_out.shape)
    dab3 = sp.run(ffn_dact, dy, w_out.reshape(J, nb, D), ab4, name=tag + "_dact")
    for hf in range(2):
        sp.source[("in", l, i, hf)] = sp.run(mm_nn, ut.reshape(2, D // 2, S), dab3[None], BF16, name=f"{tag}_dwin{hf}",
                                             a_sel=hf, tm=GRAD_TM, tk=GRAD_TK)
    return sp.run(mm_nt, dab3, w_in[:, None], F32, name=tag + "_du", natural=True, kgroup=4)


def kernel(x, c, norm_gain, w_ada, b_ada, w_ffn_in, w_ffn_out, w_hgrn_in, hgrn_lb_logits, hgrn_head_gain, w_hgrn_out, kv_gain, w_ada_kv, b_ada_kv, w_kv, b_kv, w_q, b_q, attn_sinks, w_attn_out, final_gain, loss_target, m_norm_gain, m_w_ada, m_b_ada, m_w_ffn_in, m_w_ffn_out, m_w_hgrn_in, m_hgrn_lb_logits, m_hgrn_head_gain, m_w_hgrn_out, m_kv_gain, m_w_ada_kv, m_b_ada_kv, m_w_kv, m_b_kv, m_w_q, m_b_q, m_attn_sinks, m_w_attn_out, m_final_gain, v_norm_gain, v_w_ada, v_b_ada, v_w_ffn_in, v_w_ffn_out, v_w_hgrn_in, v_hgrn_lb_logits, v_hgrn_head_gain, v_w_hgrn_out, v_kv_gain, v_w_ada_kv, v_b_ada_kv, v_w_kv, v_b_kv, v_w_q, v_b_q, v_attn_sinks, v_w_attn_out, v_final_gain):
    xi, yi, ci = _place()
    me = 4 * xi + 2 * yi + ci
    _, S, D = x.shape
    L = norm_gain.shape[0]
    dsh = D // N_DEV
    ada_n = w_ada.shape[2]
    kv_n = w_ada_kv.shape[1]
    NQ = D // ATT_HEAD
    NKV = NQ // ATT_GROUP
    kvd = NKV * ATT_HEAD
    h0 = x[0]
    target = loss_target[0]

    def my_cols(a, n):
        return lax.dynamic_slice_in_dim(a, me * n, n, axis=a.ndim - 1)

    gp = _Plan(GATHER_PLAN, gather_side)
    bf = lambda a: a.astype(BF16)
    for l in range(L):
        for i in range(2):
            gp.source[("in", l, i)] = bf(w_ffn_in[l, i])
            gp.source[("out", l, i)] = bf(w_ffn_out[l, i])
    gp.source.update(hin=bf(w_hgrn_in[0]), hout=bf(w_hgrn_out[0]), kv=bf(w_kv), q=bf(w_q[0]), o=bf(w_attn_out[0]))
    W = gp.got
    W.update(zip(GATHER_FIRST, exchange(gather_side([(gp.source[k], None) for k in GATHER_FIRST]), "gather_first")))
    full = lambda w: w.reshape(1, 1, -1, w.shape[-1])

    lb_sh = lower_bound_fwd(hgrn_lb_logits[0:1], hgrn_lb_logits[1:2], "lb_fwd")
    small = jnp.concatenate([c, norm_gain.reshape(1, L * 3 * dsh), hgrn_head_gain, lb_sh], axis=1)
    (g1,) = exchange(gather_side([(small, None)]), "gather_cond")
    g1 = g1.reshape(N_DEV, -1)
    c_all = g1[:, :D]
    gains = g1[:, D:D + L * 3 * dsh].reshape(N_DEV, L * 3, dsh).transpose(1, 0, 2).reshape(L, 3, 1, D)
    head_gain = g1[:, D + L * 3 * dsh:D + (L * 3 + 1) * dsh].reshape(1, D)
    lb0 = g1[:, D + (L * 3 + 1) * dsh:].reshape(1, D)

    parts = [cond_proj(c_all, w_ada[l], my_cols(b_ada[l:l + 1], ada_n), f"mod{l}") for l in range(L)]
    parts.append(cond_proj(c_all, w_ada_kv, my_cols(b_ada_kv[None], kv_n), "mod_kv"))
    (g2,) = exchange(gather_side([(jnp.concatenate(parts, axis=1), None)]), "gather_mod")
    mine2 = lax.dynamic_index_in_dim(g2, me, axis=1, keepdims=False)
    mod = [mine2[:, l * ada_n:(l + 1) * ada_n].reshape(3, 3, 1, D) for l in range(L)]
    mod_kv = mine2[:, L * ada_n:].reshape(2, 1, D)

    tables = rope_tables(S)
    sink_col = jnp.broadcast_to(attn_sinks.reshape(NKV, ATT_GROUP, 1, 1), (NKV, ATT_GROUP, ATT_WINDOW, 1))
    sink_col = sink_col.reshape(NKV, 1, ATT_GROUP * ATT_WINDOW)

    def to_heads(t, n):
        return t.reshape(S, n, ATT_HEAD).transpose(1, 0, 2)

    def from_heads(t):
        return t.transpose(1, 0, 2).reshape(S, -1)

    h = h0
    saved = {}
    for l in range(L):
        for s in (0, 1, 2):
            tag = f"l{l}s{s}"
            shift, scale, gate = mod[l][s, 0], mod[l][s, 1], mod[l][s, 2]
            u, ut = adaln_fwd(h, gains[l, s], shift, scale, tag + "_norm")
            if s != 1:
                y, h_new, res = _ffn_fwd(gp, W, h, gate, u, l, s // 2, tag)
            elif l == 0:
                proj = gp.run(mm_nn, u[None], W["hin"][None], F32, name=tag + "_proj", natural=True)
                o, states = gp.run(hgrn_scan_fwd, proj, lb0, name=tag + "_scan")
                z, zt = hgrn_post_fwd(o, proj, head_gain, tag + "_post")
                y, h_new = mm_nn(z[None], full(W["hout"]), F32, tag + "_out", natural=True, resid=(h, gate, 1.0))
                res = (proj, o, states, zt)
            else:
                q = mm_nn(u[None], full(W["q"]), F32, tag + "_q", natural=True, bias=b_q)
                q4 = to_heads(rope(q, tables, 1, tag + "_rope"), NQ).reshape(NKV, ATT_GROUP, S, ATT_HEAD)
                att4 = gp.run(attn_fwd, q4, k3, v3, sink_col, name=tag + "_attn")
                att = from_heads(att4.reshape(NQ, S, ATT_HEAD))
                att_t = att4.reshape(NQ, S, ATT_HEAD).transpose(0, 2, 1).reshape(D, S)
                y, h_new = mm_nn(att[None], full(W["o"]), F32, tag + "_out", natural=True, resid=(h, gate, 1.0))
                res = (q4, att_t)
            saved[(l, s)] = (h, ut, y, res)
            h = h_new
        if l == 0:
            h_kv = h
            u_kv, u_kv_t = adaln_fwd(h, kv_gain[None], mod_kv[0], mod_kv[1], "kv_norm")
            kvp = mm_nn(u_kv[None], full(W["kv"]), F32, "kv_proj", natural=True, bias=b_kv[None])
            k3 = to_heads(rope(kvp[:, :kvd], tables, 1, "kv_rope"), NKV)
            v3 = to_heads(kvp[:, kvd:], NKV)

    def branch(l, s):
        return saved[(l, s)][2], mod[l][s, 2], 0.5 if s != 1 else 1.0

    loss_row, dh, d_final_gain, dy, d_gate = final_loss_grad(h, final_gain[None], target, "final", branch(L - 1, 2))
    loss = lax.psum(loss_row[0, 0], ("x", "y", "c"))

    sp = _Plan(SCATTER_PLAN, scatter_side)

    def grad_w(a_t, b, name):
        return sp.run(mm_nn, a_t[None], b[None, None], BF16, name=name, natural=True, tm=GRAD_TM, tk=GRAD_TK)

    d_mod = [[None] * 3 for _ in range(L)]
    d_gain = [[None] * 3 for _ in range(L)]
    for l in reversed(range(L)):
        if l == 0:
            dkv = jnp.concatenate([rope(from_heads(dk3), tables, -1, "kv_drope"), from_heads(dv3)], axis=1)
            sp.source["kv"] = grad_w(u_kv_t, dkv, "kv_dw").reshape(W["kv"].shape)
            db_kv = colsum(dkv, "kv_db")
            du_kv = mm_nt(dkv[None], full(W["kv"]), F32, "kv_du", natural=True)
            dh, d_kv_gain, d_kv_shift, d_kv_scale, dy, d_gate = adaln_bwd(
                h_kv, kv_gain[None], mod_kv[0], mod_kv[1], du_kv, dh, "kv_dnorm", nxt=branch(0, 2))
        for s in (2, 1, 0):
            tag = f"l{l}s{s}"
            shift, scale, gate = mod[l][s, 0], mod[l][s, 1], mod[l][s, 2]
            h_in, ut, y, res = saved[(l, s)]
            if s != 1:
                du = _ffn_bwd(sp, W, dy, ut, res[0], res[1], l, s // 2, tag)
            elif l == 0:
                proj, o, states, zt = res
                sp.source["hout"] = grad_w(zt, dy, tag + "_dwout").reshape(W["hout"].shape)
                dz = mm_nt(dy[None], full(W["hout"]), F32, tag + "_dz", natural=True)
                do, dg, d_head_gain = hgrn_post_bwd(o, proj, head_gain, dz, tag + "_dpost")
                dproj, d_lb0 = sp.run(hgrn_scan_bwd, proj, lb0, states, do, dg, name=tag + "_dscan")
                for hf in range(2):
                    sp.source[("hin", hf)] = sp.run(mm_nn, ut.reshape(2, D // 2, S), dproj, BF16, name=f"{tag}_dwin{hf}",
                                                    b_natural=True, jn=N_DEV, a_sel=hf, tm=GRAD_TM, tk=GRAD_TK)
                du = sp.run(mm_nt, dproj, W["hin"][:, None], F32, name=tag + "_du", natural=True, a_natural=True)
            else:
                q4, att_t = res
                sp.source["o"] = grad_w(att_t, dy, tag + "_dwout").reshape(W["o"].shape)
                datt = mm_nt(dy[None], full(W["o"]), F32, tag + "_datt", natural=True)
                datt4 = to_heads(datt, NQ).reshape(NKV, ATT_GROUP, S, ATT_HEAD)
                dq4, (dkp, dkc, dvp, dvc), d_sink = sp.run(attn_bwd, q4, k3, v3, sink_col, datt4, name=tag + "_dattn")
                dk3, dv3 = band_combine(dkc, dkp, dvc, dvp, tag + "_dkv")
                dq = rope(from_heads(dq4.reshape(NQ, S, ATT_HEAD)), tables, -1, tag + "_drope")
                sp.source["q"] = grad_w(ut, dq, tag + "_dwq").reshape(W["q"].shape)
                db_q = colsum(dq, tag + "_dbq")
                du = mm_nt(dq[None], full(W["q"]), F32, tag + "_du", natural=True)
            outs = adaln_bwd(h_in, gains[l, s], shift, scale, du, dh, tag + "_dnorm",
                             nxt=branch(l, s - 1) if s > 0 else None)
            dh, d_gain[l][s], dsh_, dsc_ = outs[:4]
            d_mod[l][s] = jnp.concatenate([dsh_, dsc_, d_gate], axis=1)
            if s > 0:
                dy, d_gate = outs[4:]
    grad_x = dh[None]
    G = sp.got

    pad = lambda a, n: jnp.pad(a, ((0, 0), (0, n - a.shape[1])))
    pieces = [jnp.concatenate(d_mod[l], axis=1) for l in range(L)]
    pieces += [d_kv_shift, d_kv_scale]
    pieces += [d_gain[l][s] for l in range(L) for s in range(3)]
    pieces += [d_head_gain, d_lb0, d_kv_gain, db_kv, db_q, pad(d_sink.reshape(1, NQ), LANES), d_final_gain]
    (g3,) = exchange(gather_side([(jnp.concatenate(pieces, axis=1), None)]), "gather_small_grads")
    g3 = g3.reshape(N_DEV, -1)
    tot = rowsum(g3, "sum_small_grads")
    offs = [0]
    for p in pieces:
        offs.append(offs[-1] + p.shape[1])
    seg = lambda k: tot[:, offs[k]:offs[k + 1]]
    k0 = 0
    g_b_ada = jnp.concatenate([seg(l) for l in range(L)], axis=0)
    k0 += L
    g_b_ada_kv = jnp.concatenate([seg(k0), seg(k0 + 1)], axis=1)
    k0 += 2
    g_norm_gain = jnp.concatenate([my_cols(seg(k0 + j), dsh) for j in range(3 * L)], axis=0)
    k0 += 3 * L
    g_head_gain = my_cols(seg(k0), dsh)
    d_lb_sh = my_cols(seg(k0 + 1), dsh)
    g_kv_gain = seg(k0 + 2)
    g_b_kv = seg(k0 + 3)
    g_b_q = seg(k0 + 4)
    g_sinks = seg(k0 + 5)[:, :NQ]
    g_final_gain = seg(k0 + 6)
    dl0, dl1 = lower_bound_bwd(hgrn_lb_logits[0:1], hgrn_lb_logits[1:2], d_lb_sh, "lb_bwd")
    g_lb_logits = jnp.concatenate([dl0, dl1], axis=0)

    g_w_ada = jnp.stack([cond_grad(c_all, lax.dynamic_slice_in_dim(g3, offs[l] + me * ada_n, ada_n, axis=1),
                                   f"dw_ada{l}") for l in range(L)])
    g_w_ada_kv = cond_grad(c_all, lax.dynamic_slice_in_dim(g3, offs[L] + me * kv_n, kv_n, axis=1), "dw_ada_kv")

    def update(w, m, v, grads, name, ncol=1):
        n = len(grads) // ncol
        width = w.shape[-1]
        three = lambda a: a.reshape((n, -1, width))
        outs = None
        for k, g in enumerate(grads):
            j, cp = divmod(k, ncol)
            g = g.reshape((g.shape[0], -1, width // ncol))
            if len(grads) == 1:
                outs = sp.run(adamw, three(w), three(m), three(v), j, g, name=f"{name}{k}")
            else:
                outs = adamw(three(w), three(m), three(v), j, g, f"{name}{k}", col=(cp, ncol), into=outs)
        return [o.reshape(w.shape) for o in outs]

    def one(w, m, v, g, name):
        return update(w, m, v, [g.reshape((1, -1, w.shape[-1]))], name)

    res = {}
    res["norm_gain"] = one(norm_gain, m_norm_gain, v_norm_gain, g_norm_gain, "adam_norm_gain")
    res["w_ada"] = one(w_ada, m_w_ada, v_w_ada, g_w_ada, "adam_w_ada")
    res["b_ada"] = one(b_ada, m_b_ada, v_b_ada, g_b_ada, "adam_b_ada")
    sent = {k.key if isinstance(k, _Part) else k for k in G}
    rest = [k for k in sp.source if k not in sent]
    if rest:
        G.update(zip(rest, exchange(scatter_side([(sp.source[k], None) for k in rest]), "scatter_rest")))
    g_ffn_in = [G[_rows(("in", l, i, hf), p)] for l in range(L) for i in range(2) for hf in range(2) for p in range(2)]
    g_ffn_out = [G[_cols(("out", l, i), p)] for l in range(L) for i in range(2) for p in range(2)]
    g_hgrn_in = [G[_rows(("hin", hf), p)] for hf in range(2) for p in range(2)]
    res["w_ffn_in"] = update(w_ffn_in, m_w_ffn_in, v_w_ffn_in, g_ffn_in, "adam_w_ffn_in")
    res["w_ffn_out"] = update(w_ffn_out, m_w_ffn_out, v_w_ffn_out, g_ffn_out, "adam_w_ffn_out", ncol=2)
    res["w_hgrn_in"] = update(w_hgrn_in, m_w_hgrn_in, v_w_hgrn_in, g_hgrn_in, "adam_w_hgrn_in")
    res["hgrn_lb_logits"] = one(hgrn_lb_logits, m_hgrn_lb_logits, v_hgrn_lb_logits, g_lb_logits, "adam_lb")
    res["hgrn_head_gain"] = one(hgrn_head_gain, m_hgrn_head_gain, v_hgrn_head_gain, g_head_gain, "adam_head_gain")
    res["w_hgrn_out"] = update(w_hgrn_out, m_w_hgrn_out, v_w_hgrn_out, [G["hout"]], "adam_w_hgrn_out")
    res["kv_gain"] = one(kv_gain, m_kv_gain, v_kv_gain, g_kv_gain, "adam_kv_gain")
    res["w_ada_kv"] = one(w_ada_kv, m_w_ada_kv, v_w_ada_kv, g_w_ada_kv, "adam_w_ada_kv")
    res["b_ada_kv"] = one(b_ada_kv, m_b_ada_kv, v_b_ada_kv, g_b_ada_kv, "adam_b_ada_kv")
    res["w_kv"] = update(w_kv, m_w_kv, v_w_kv, [G["kv"]], "adam_w_kv")
    res["b_kv"] = one(b_kv, m_b_kv, v_b_kv, g_b_kv, "adam_b_kv")
    res["w_q"] = update(w_q, m_w_q, v_w_q, [G["q"]], "adam_w_q")
    res["b_q"] = one(b_q, m_b_q, v_b_q, g_b_q, "adam_b_q")
    res["attn_sinks"] = one(attn_sinks, m_attn_sinks, v_attn_sinks, g_sinks, "adam_sinks")
    res["w_attn_out"] = update(w_attn_out, m_w_attn_out, v_w_attn_out, [G["o"]], "adam_w_attn_out")
    res["final_gain"] = one(final_gain, m_final_gain, v_final_gain, g_final_gain, "adam_final_gain")

    names = ["norm_gain", "w_ada", "b_ada", "w_ffn_in", "w_ffn_out", "w_hgrn_in", "hgrn_lb_logits", "hgrn_head_gain",
             "w_hgrn_out", "kv_gain", "w_ada_kv", "b_ada_kv", "w_kv", "b_kv", "w_q", "b_q", "attn_sinks", "w_attn_out",
             "final_gain"]
    return (loss, grad_x, *[res[n][0] for n in names], *[res[n][1] for n in names], *[res[n][2] for n in names],
            *[res[n][3] for n in names])
```

```python
import functools

import jax
import jax.numpy as jnp
from jax import lax
from jax.experimental import pallas as pl
from jax.experimental.pallas import tpu as pltpu

F32 = jnp.float32
BF16 = jnp.bfloat16
MESH = pl.DeviceIdType.MESH

N_DEV = 8
V7X_VMEM_LIMIT_BYTES = 56 * 1024 * 1024
LANES = 128

NORM_EPS = 1e-6
NEG_INF = -1e30
HGRN_CHUNK = 32
HGRN_HEAD = 128
ATT_HEAD = 64
ATT_WINDOW = 128
ATT_GROUP = 8
ROT_DIM = 16
ROPE_THETA = 500000.0

ADAM_LR = 0.001
ADAM_B1 = 0.9
ADAM_B2 = 0.999
ADAM_EPS = 1e-08
ADAM_WD = 0.01
ADAM_STEP = 10


def _params(*sem):
    return pltpu.CompilerParams(dimension_semantics=sem, vmem_limit_bytes=V7X_VMEM_LIMIT_BYTES)


def _tile(n, pref, unit=LANES):
    t = (min(n, pref) // unit) * unit
    while t >= unit:
        if n % t == 0:
            return t
        t -= unit
    return n


_ANY = pl.BlockSpec(memory_space=pl.ANY)


class _Side:
    def __init__(self, inputs, out_shapes, sem_shapes, start, finish):
        self.inputs, self.out_shapes, self.sem_shapes = list(inputs), list(out_shapes), list(sem_shapes)
        self.start, self.finish = start, finish


def _call(body, *, name, grid, in_specs, out_specs, out_shape, args, semantics, scratch_shapes=(), side=None):
    in_specs, out_specs, out_shape = list(in_specs), list(out_specs), list(out_shape)
    scratch_shapes = list(scratch_shapes)
    if side is None:
        outs = pl.pallas_call(
            body, name=name, grid=grid, in_specs=in_specs, out_specs=out_specs, out_shape=out_shape,
            scratch_shapes=scratch_shapes, compiler_params=_params(*semantics))(*args)
        return list(outs)
    n_in, n_out, n_scr = len(in_specs), len(out_specs), len(scratch_shapes)
    s_in, s_out = len(side.inputs), len(side.out_shapes)

    def carried(*refs):
        ins, refs = refs[:n_in], refs[n_in:]
        side_ins, refs = refs[:s_in], refs[s_in:]
        outs, refs = refs[:n_out], refs[n_out:]
        side_outs, refs = refs[:s_out], refs[s_out:]
        scratch, sems = refs[:n_scr], refs[n_scr:]
        step, steps = pl.program_id(0), grid[0]
        for d in range(1, len(grid)):
            step, steps = step * grid[d] + pl.program_id(d), steps * grid[d]

        @pl.when(step == 0)
        def _():
            side.start(side_ins, side_outs, sems)

        body(*ins, *outs, *scratch)

        @pl.when(step == steps - 1)
        def _():
            side.finish(side_ins, side_outs, sems)

    outs = pl.pallas_call(
        carried, name=name, grid=grid, in_specs=in_specs + [_ANY] * s_in, out_specs=out_specs + [_ANY] * s_out,
        out_shape=out_shape + side.out_shapes, scratch_shapes=scratch_shapes + side.sem_shapes,
        compiler_params=_params(*(["arbitrary"] * len(grid))))(*args, *side.inputs)
    return list(outs[:n_out]), list(outs[n_out:])


def _accumulate(prod, o_ref, acc_ref, k, nk):
    if nk == 1:
        o_ref[...] = prod.astype(o_ref.dtype)
        return

    @pl.when(k == 0)
    def _():
        acc_ref[...] = prod

    @pl.when(k > 0)
    def _():
        acc_ref[...] += prod

    @pl.when(k == nk - 1)
    def _():
        o_ref[...] = acc_ref[...].astype(o_ref.dtype)


def mm_nn(a3, b4, out_dtype, name, natural=False, a_natural=False, b_natural=False, jn=None, a_sel=None, bias=None,
          resid=None, tm=512, tk=None, kgroup=1, side=None):
    if b_natural:
        JK, JN, kb = 1, jn, b4.shape[0]
        nb = b4.shape[1] // JN
    else:
        JK, JN, kb, nb = b4.shape
    M = a3.shape[0] if a_natural else a3.shape[1]
    tm = _tile(M, tm, 16)
    tn = _tile(nb, 1024)
    tk = kb if tk is None else _tile(kb, tk)
    ntn, nkt = nb // tn, kb // tk
    nk = JK * nkt // kgroup

    def body(*refs):
        a_ref, b_ref = refs[:2]
        acc_ref = refs[-1]
        if kgroup == 1:
            prod = jnp.dot(a_ref[...].astype(BF16), b_ref[...].astype(BF16), preferred_element_type=F32)
        else:
            prod = jnp.dot(a_ref[0].astype(BF16), b_ref[0].astype(BF16), preferred_element_type=F32)
            for g in range(1, kgroup):
                prod += jnp.dot(a_ref[g].astype(BF16), b_ref[g].astype(BF16), preferred_element_type=F32)
        if bias is not None:
            prod = prod + refs[2][...]
        if resid is None:
            _accumulate(prod, refs[-2], acc_ref, pl.program_id(2), nk)
            return
        h_ref, g_ref, y_ref, o_ref = refs[2:6]
        k = pl.program_id(2)

        @pl.when(k == 0)
        def _():
            acc_ref[...] = prod

        @pl.when(k > 0)
        def _():
            acc_ref[...] += prod

        @pl.when(k == nk - 1)
        def _():
            y = acc_ref[...]
            y_ref[...] = y
            o_ref[...] = h_ref[...] + (resid[2] * g_ref[...]) * y

    if kgroup > 1:
        assert JN == 1 and nkt == 1 and not (a_natural or b_natural) and a_sel is None
        a3 = a3.reshape(nk, kgroup, M, kb)
        b4 = b4.reshape(nk, kgroup, kb, nb)
        a_spec = pl.BlockSpec((None, kgroup, tm, kb), lambda j, i, k: (k, 0, i, 0))
    elif a_natural:
        a_spec = pl.BlockSpec((tm, tk), lambda j, i, k: (i, k))
    elif a_sel is not None:
        a_spec = pl.BlockSpec((None, tm, tk), lambda j, i, k: (a_sel, i, k))
    else:
        a_spec = pl.BlockSpec((None, tm, tk), lambda j, i, k: (k // nkt, i, k % nkt))
    if kgroup > 1:
        b_spec = pl.BlockSpec((None, kgroup, kb, tn), lambda j, i, k: (k, 0, 0, j))
    elif b_natural:
        b_spec = pl.BlockSpec((tk, tn), lambda j, i, k: (k, j))
    else:
        b_spec = pl.BlockSpec((None, None, tk, tn), lambda j, i, k: (k // nkt, j // ntn, k % nkt, j % ntn))
    in_specs = [a_spec, b_spec]
    args = [a3, b4]
    if bias is not None:
        assert natural and nk == 1
        in_specs.append(pl.BlockSpec((1, tn), lambda j, i, k: (0, j)))
        args.append(bias)
    if natural:
        out_shape = jax.ShapeDtypeStruct((M, JN * nb), out_dtype)
        o_spec = pl.BlockSpec((tm, tn), lambda j, i, k: (i, j))
    else:
        out_shape = jax.ShapeDtypeStruct((JN, M, nb), out_dtype)
        o_spec = pl.BlockSpec((None, tm, tn), lambda j, i, k: (j // ntn, i, j % ntn))
    out_specs, out_shapes = [o_spec], [out_shape]
    if resid is not None:
        assert natural and bias is None and out_dtype == F32
        in_specs += [o_spec, pl.BlockSpec((1, tn), lambda j, i, k: (0, j))]
        args += [resid[0], resid[1]]
        out_specs, out_shapes = [o_spec, o_spec], [out_shape, out_shape]
    res = _call(body, name=name, grid=(JN * ntn, M // tm, nk), in_specs=in_specs, out_specs=out_specs,
                out_shape=out_shapes, scratch_shapes=[pltpu.VMEM((tm, tn), F32)], args=args,
                semantics=("parallel", "parallel", "arbitrary"), side=side)
    outs = res if side is None else res[0]
    ans = outs[0] if resid is None else (outs[0], outs[1])
    return ans if side is None else (ans, res[1])


def mm_nt(a3, b4, out_dtype, name, natural=False, a_natural=False, kgroup=1, side=None):
    JK, JN, nb, kb = b4.shape
    M = a3.shape[0] if a_natural else a3.shape[1]
    tm = min(M, 512)
    tn = _tile(nb, 1024)
    ntn = nb // tn
    nk = JK // kgroup

    def body(a_ref, b_ref, o_ref, acc_ref):
        nt = (((1,), (1,)), ((), ()))
        if kgroup == 1:
            prod = lax.dot_general(a_ref[...].astype(BF16), b_ref[...].astype(BF16), nt, preferred_element_type=F32)
        else:
            prod = lax.dot_general(a_ref[0].astype(BF16), b_ref[0].astype(BF16), nt, preferred_element_type=F32)
            for g in range(1, kgroup):
                prod += lax.dot_general(a_ref[g].astype(BF16), b_ref[g].astype(BF16), nt, preferred_element_type=F32)
        _accumulate(prod, o_ref, acc_ref, pl.program_id(2), nk)

    if kgroup > 1:
        assert JN == 1 and not a_natural
        a3 = a3.reshape(nk, kgroup, M, kb)
        b4 = b4.reshape(nk, kgroup, nb, kb)
        a_spec = pl.BlockSpec((None, kgroup, tm, kb), lambda j, i, k: (k, 0, i, 0))
        b_spec = pl.BlockSpec((None, kgroup, tn, kb), lambda j, i, k: (k, 0, j, 0))
    elif a_natural:
        a_spec = pl.BlockSpec((tm, kb), lambda j, i, k: (i, k))
        b_spec = pl.BlockSpec((None, None, tn, kb), lambda j, i, k: (k, j // ntn, j % ntn, 0))
    else:
        a_spec = pl.BlockSpec((None, tm, kb), lambda j, i, k: (k, i, 0))
        b_spec = pl.BlockSpec((None, None, tn, kb), lambda j, i, k: (k, j // ntn, j % ntn, 0))
    if natural:
        out_shape = jax.ShapeDtypeStruct((M, JN * nb), out_dtype)
        o_spec = pl.BlockSpec((tm, tn), lambda j, i, k: (i, j))
    else:
        out_shape = jax.ShapeDtypeStruct((JN, M, nb), out_dtype)
        o_spec = pl.BlockSpec((None, tm, tn), lambda j, i, k: (j // ntn, i, j % ntn))
    res = _call(body, name=name, grid=(JN * ntn, M // tm, nk), in_specs=[a_spec, b_spec], out_specs=[o_spec],
                out_shape=[out_shape], scratch_shapes=[pltpu.VMEM((tm, tn), F32)], args=[a3, b4],
                semantics=("parallel", "parallel", "arbitrary"), side=side)
    return res[0] if side is None else (res[0][0], res[1])


def _row_tile(S):
    return min(S, 256)


def _adaln(h, gain, shift, scale):
    y = h * lax.rsqrt(jnp.mean(h * h, axis=-1, keepdims=True) + NORM_EPS) * gain
    return y * (1.0 + scale) + shift


def adaln_fwd(h, gain, shift, scale, name):
    S, D = h.shape
    tm = _row_tile(S)

    def body(h_ref, g_ref, sh_ref, sc_ref, u_ref, ut_ref):
        u = _adaln(h_ref[...], g_ref[...], sh_ref[...], sc_ref[...])
        u_ref[...] = u.astype(BF16)
        ut_ref[...] = u.T.astype(BF16)

    row = pl.BlockSpec((tm, D), lambda i: (i, 0))
    vec = pl.BlockSpec((1, D), lambda i: (0, 0))
    return pl.pallas_call(
        body, name=name, grid=(S // tm,), in_specs=[row, vec, vec, vec],
        out_specs=[row, pl.BlockSpec((D, tm), lambda i: (0, i))],
        out_shape=[jax.ShapeDtypeStruct((S, D), BF16), jax.ShapeDtypeStruct((D, S), BF16)],
        compiler_params=_params("parallel"),
    )(h, gain, shift, scale)


def _gate_specs(S, D, tm, nxt):
    if nxt is None:
        return [], [], [], []
    row = pl.BlockSpec((tm, D), lambda i: (i, 0))
    vec = pl.BlockSpec((1, D), lambda i: (0, 0))
    return ([row, vec], [nxt[0], nxt[1]], [row, vec],
            [jax.ShapeDtypeStruct((S, D), BF16), jax.ShapeDtypeStruct((1, D), F32)])


def _gate_grads(dh, nxt, y_ref, g_ref, dy_ref, dg_ref):
    coef = nxt[2]
    dy_ref[...] = ((coef * g_ref[...]) * dh).astype(BF16)

    @pl.when(pl.program_id(0) == 0)
    def _():
        dg_ref[...] = jnp.zeros_like(dg_ref)

    dg_ref[...] += coef * jnp.sum(dh * y_ref[...], axis=0, keepdims=True)


def adaln_bwd(h, gain, shift, scale, du, dres, name, nxt=None):
    S, D = h.shape
    tm = _row_tile(S)

    def body(h_ref, g_ref, sh_ref, sc_ref, du_ref, dres_ref, *rest):
        dh_ref, dg_ref, dsh_ref, dsc_ref = rest[-6:-2] if nxt is not None else rest
        _, vjp = jax.vjp(_adaln, h_ref[...], g_ref[...], sh_ref[...], sc_ref[...])
        dh, dg, dsh, dsc = vjp(du_ref[...].astype(F32))
        dh = dres_ref[...] + dh
        dh_ref[...] = dh

        @pl.when(pl.program_id(0) == 0)
        def _():
            dg_ref[...] = jnp.zeros_like(dg_ref)
            dsh_ref[...] = jnp.zeros_like(dsh_ref)
            dsc_ref[...] = jnp.zeros_like(dsc_ref)

        dg_ref[...] += dg
        dsh_ref[...] += dsh
        dsc_ref[...] += dsc
        if nxt is not None:
            _gate_grads(dh, nxt, rest[0], rest[1], rest[-2], rest[-1])

    row = pl.BlockSpec((tm, D), lambda i: (i, 0))
    vec = pl.BlockSpec((1, D), lambda i: (0, 0))
    vs = jax.ShapeDtypeStruct((1, D), F32)
    xin, xargs, xout, xshape = _gate_specs(S, D, tm, nxt)
    return pl.pallas_call(
        body, name=name, grid=(S // tm,), in_specs=[row, vec, vec, vec, row, row] + xin,
        out_specs=[row, vec, vec, vec] + xout, out_shape=[jax.ShapeDtypeStruct((S, D), F32), vs, vs, vs] + xshape,
        compiler_params=_params("arbitrary"),
    )(h, gain, shift, scale, du, dres, *xargs)


def _swiglu(a, b):
    return a * jax.nn.sigmoid(a) * b


def ffn_in_act(u, w_in, name, side=None):
    J2, D, nb = w_in.shape
    J = J2 // 2
    S = u.shape[0]
    tm = _tile(S, 512, 16)

    def body(u_ref, wa_ref, wb_ref, ab_ref, h_ref, ht_ref):
        uv = u_ref[...]
        a = jnp.dot(uv, wa_ref[...], preferred_element_type=F32)
        b = jnp.dot(uv, wb_ref[...], preferred_element_type=F32)
        ab_ref[0] = a
        ab_ref[1] = b
        hv = _swiglu(a, b)
        h_ref[...] = hv.astype(BF16)
        ht_ref[...] = hv.T.astype(BF16)

    res = _call(body, name=name, grid=(J, S // tm),
                in_specs=[pl.BlockSpec((tm, D), lambda j, i: (i, 0)),
                          pl.BlockSpec((None, D, nb), lambda j, i: (j, 0, 0)),
                          pl.BlockSpec((None, D, nb), lambda j, i: (j + J, 0, 0))],
                out_specs=[pl.BlockSpec((2, None, tm, nb), lambda j, i: (0, j, i, 0)),
                           pl.BlockSpec((None, tm, nb), lambda j, i: (j, i, 0)),
                           pl.BlockSpec((None, nb, tm), lambda j, i: (j, 0, i))],
                out_shape=[jax.ShapeDtypeStruct((2, J, S, nb), F32), jax.ShapeDtypeStruct((J, S, nb), BF16),
                           jax.ShapeDtypeStruct((J, nb, S), BF16)],
                args=[u, w_in, w_in], semantics=("parallel", "parallel"), side=side)
    return res if side is None else (res[0], res[1])


def ffn_dact(dy, w_out4, ab4, name, side=None):
    J, nb, D = w_out4.shape
    S = dy.shape[0]
    tm = _tile(S, 512, 16)

    def body(dy_ref, w_ref, ab_ref, o_ref):
        dh = lax.dot_general(dy_ref[...], w_ref[...], (((1,), (1,)), ((), ())), preferred_element_type=F32)
        _, vjp = jax.vjp(_swiglu, ab_ref[0], ab_ref[1])
        da, db = vjp(dh)
        o_ref[0] = da.astype(BF16)
        o_ref[1] = db.astype(BF16)

    both = pl.BlockSpec((2, None, tm, nb), lambda j, i: (0, j, i, 0))
    res = _call(body, name=name, grid=(J, S // tm),
                in_specs=[pl.BlockSpec((tm, D), lambda j, i: (i, 0)),
                          pl.BlockSpec((None, nb, D), lambda j, i: (j, 0, 0)), both],
                out_specs=[both], out_shape=[jax.ShapeDtypeStruct((2, J, S, nb), BF16)],
                args=[dy, w_out4, ab4], semantics=("parallel", "parallel"), side=side)
    if side is None:
        return res[0].reshape(2 * J, S, nb)
    return res[0][0].reshape(2 * J, S, nb), res[1]


def colsum(x, name):
    S, N = x.shape
    tm = _row_tile(S)

    def body(x_ref, o_ref):
        @pl.when(pl.program_id(0) == 0)
        def _():
            o_ref[...] = jnp.zeros_like(o_ref)

        o_ref[...] += jnp.sum(x_ref[...].astype(F32), axis=0, keepdims=True)

    return pl.pallas_call(
        body, name=name, grid=(S // tm,), in_specs=[pl.BlockSpec((tm, N), lambda i: (i, 0))],
        out_specs=pl.BlockSpec((1, N), lambda i: (0, 0)), out_shape=jax.ShapeDtypeStruct((1, N), F32),
        compiler_params=_params("arbitrary"),
    )(x)


def _final_loss(h, gain, target):
    y = h * lax.rsqrt(jnp.mean(h * h, axis=-1, keepdims=True) + NORM_EPS) * gain
    err = y - target
    return 0.5 * jnp.sum(jnp.mean(err * err, axis=-1))


def final_loss_grad(h, gain, target, name, nxt):
    S, D = h.shape
    tm = _row_tile(S)

    def body(h_ref, g_ref, t_ref, y_ref, gate_ref, loss_ref, dh_ref, dg_ref, dy_ref, dgate_ref):
        loss, (dh, dg) = jax.value_and_grad(_final_loss, argnums=(0, 1))(h_ref[...], g_ref[...], t_ref[...])
        dh_ref[...] = dh

        @pl.when(pl.program_id(0) == 0)
        def _():
            loss_ref[...] = jnp.zeros_like(loss_ref)
            dg_ref[...] = jnp.zeros_like(dg_ref)

        loss_ref[...] += jnp.full(loss_ref.shape, loss, F32)
        dg_ref[...] += dg
        _gate_grads(dh, nxt, y_ref, gate_ref, dy_ref, dgate_ref)

    row = pl.BlockSpec((tm, D), lambda i: (i, 0))
    vec = pl.BlockSpec((1, D), lambda i: (0, 0))
    xin, xargs, xout, xshape = _gate_specs(S, D, tm, nxt)
    return pl.pallas_call(
        body, name=name, grid=(S // tm,), in_specs=[row, vec, row] + xin,
        out_specs=[pl.BlockSpec((1, LANES), lambda i: (0, 0)), row, vec] + xout,
        out_shape=[jax.ShapeDtypeStruct((1, LANES), F32), jax.ShapeDtypeStruct((S, D), F32),
                   jax.ShapeDtypeStruct((1, D), F32)] + xshape,
        compiler_params=_params("arbitrary"),
    )(h, gain, target, *xargs)


def _chunk_consts(H):
    C = HGRN_CHUNK
    t = lax.broadcasted_iota(jnp.int32, (H, C, C), 1)
    s = lax.broadcasted_iota(jnp.int32, (H, C, C), 2)
    return (s <= t).astype(F32), s <= t


def _hgrn_chunk(q_raw, f_raw, i_raw, lb, st):
    H, C, _ = q_raw.shape
    lower, causal = _chunk_consts(H)
    forget = lb + (1.0 - lb) * jax.nn.sigmoid(f_raw)
    g = jnp.log(forget)
    kk = 1.0 - forget
    qs = q_raw * jax.nn.sigmoid(q_raw)
    bnn = (((2,), (1,)), ((0,), (0,)))
    bnt = (((2,), (2,)), ((0,), (0,)))
    btn = (((1,), (1,)), ((0,), (0,)))
    b = lax.dot_general(lower, g, bnn, precision=lax.Precision.HIGHEST, preferred_element_type=F32)
    bm = b[:, C // 2 - 1:C // 2, :]
    bl = b[:, C - 1:C, :]
    inter = lax.dot_general((qs * jnp.exp(b)).astype(BF16), st.astype(BF16), bnt, preferred_element_type=F32)
    qt = (qs * jnp.exp(b - bm)).astype(BF16)
    kt = (kk * jnp.exp(bm - b)).astype(BF16)
    scores = lax.dot_general(qt, kt, bnt, preferred_element_type=F32)
    scores = jnp.where(causal, scores, 0.0)
    vb = i_raw.astype(BF16)
    out = inter + lax.dot_general(scores.astype(BF16), vb, bnn, preferred_element_type=F32)
    kdec = (kk * jnp.exp(bl - b)).astype(BF16)
    new_st = st * jnp.exp(bl) + lax.dot_general(vb, kdec, btn, preferred_element_type=F32)
    return out, new_st


def _heads(ref, rows, H):
    return jnp.stack([ref[rows, pl.ds(h * HGRN_HEAD, HGRN_HEAD)] for h in range(H)])


def hgrn_scan_fwd(proj, lb, name, side=None):
    S, D4 = proj.shape
    D = D4 // 4
    H = D // HGRN_HEAD
    C = HGRN_CHUNK
    R = min(S, 128)
    ncr = R // C

    def body(q_ref, f_ref, i_ref, lb_ref, o_ref, st_ref, state):
        @pl.when(pl.program_id(0) == 0)
        def _():
            state[...] = jnp.zeros_like(state)

        lbh = _heads(lb_ref, slice(None), H)

        def chunk(cc, carry):
            rows = pl.ds(pl.multiple_of(cc * C, C), C)
            st = state[...]
            st_ref[cc] = st.astype(BF16)
            out, new_st = _hgrn_chunk(_heads(q_ref, rows, H), _heads(f_ref, rows, H), _heads(i_ref, rows, H), lbh, st)
            for h in range(H):
                o_ref[rows, pl.ds(h * HGRN_HEAD, HGRN_HEAD)] = out[h]
            state[...] = new_st
            return carry

        lax.fori_loop(0, ncr, chunk, 0)

    col = lambda j: pl.BlockSpec((R, D), lambda i: (i, j))
    res = _call(body, name=name, grid=(S // R,),
                in_specs=[col(0), col(1), col(2), pl.BlockSpec((1, D), lambda i: (0, 0))],
                out_specs=[pl.BlockSpec((R, D), lambda i: (i, 0)),
                           pl.BlockSpec((ncr, H, HGRN_HEAD, HGRN_HEAD), lambda i: (i, 0, 0, 0))],
                out_shape=[jax.ShapeDtypeStruct((S, D), F32),
                           jax.ShapeDtypeStruct((S // C, H, HGRN_HEAD, HGRN_HEAD), BF16)],
                scratch_shapes=[pltpu.VMEM((H, HGRN_HEAD, HGRN_HEAD), F32)], args=[proj, proj, proj, lb],
                semantics=("arbitrary",), side=side)
    return res if side is None else (res[0], res[1])


def hgrn_scan_bwd(proj, lb, states, do, dg, name, side=None):
    S, D4 = proj.shape
    D = D4 // 4
    H = D // HGRN_HEAD
    C = HGRN_CHUNK
    R = min(S, 128)
    ncr = R // C
    ng = S // R

    def body(q_ref, f_ref, i_ref, lb_ref, st_ref, do_ref, dg_ref, dp_ref, dlb_ref, dstate):
        @pl.when(pl.program_id(0) == 0)
        def _():
            dstate[...] = jnp.zeros_like(dstate)
            dlb_ref[...] = jnp.zeros_like(dlb_ref)

        dp_ref[:, pl.ds(3 * D, D)] = dg_ref[...].astype(BF16)
        lbh = _heads(lb_ref, slice(None), H)

        def chunk(t, carry):
            cc = ncr - 1 - t
            rows = pl.ds(pl.multiple_of(cc * C, C), C)
            _, vjp = jax.vjp(_hgrn_chunk, _heads(q_ref, rows, H), _heads(f_ref, rows, H), _heads(i_ref, rows, H),
                             lbh, st_ref[cc].astype(F32))
            dq, df, di, dlb, dst = vjp((_heads(do_ref, rows, H), dstate[...]))
            for h in range(H):
                dp_ref[rows, pl.ds(h * HGRN_HEAD, HGRN_HEAD)] = dq[h].astype(BF16)
                dp_ref[rows, pl.ds(D + h * HGRN_HEAD, HGRN_HEAD)] = df[h].astype(BF16)
                dp_ref[rows, pl.ds(2 * D + h * HGRN_HEAD, HGRN_HEAD)] = di[h].astype(BF16)
                dlb_ref[:, pl.ds(h * HGRN_HEAD, HGRN_HEAD)] += dlb[h]
            dstate[...] = dst
            return carry

        lax.fori_loop(0, ncr, chunk, 0)

    col = lambda j: pl.BlockSpec((R, D), lambda i: (ng - 1 - i, j))
    res = _call(body, name=name, grid=(ng,),
                in_specs=[col(0), col(1), col(2), pl.BlockSpec((1, D), lambda i: (0, 0)),
                          pl.BlockSpec((ncr, H, HGRN_HEAD, HGRN_HEAD), lambda i: (ng - 1 - i, 0, 0, 0)),
                          pl.BlockSpec((R, D), lambda i: (ng - 1 - i, 0)),
                          pl.BlockSpec((R, D), lambda i: (ng - 1 - i, 0))],
                out_specs=[pl.BlockSpec((R, D4), lambda i: (ng - 1 - i, 0)), pl.BlockSpec((1, D), lambda i: (0, 0))],
                out_shape=[jax.ShapeDtypeStruct((S, D4), BF16), jax.ShapeDtypeStruct((1, D), F32)],
                scratch_shapes=[pltpu.VMEM((H, HGRN_HEAD, HGRN_HEAD), F32)],
                args=[proj, proj, proj, lb, states, do, dg], semantics=("arbitrary",), side=side)
    return res if side is None else (res[0], res[1])


def _head_out(o, g, gain):
    y = o * lax.rsqrt(jnp.mean(o * o, axis=-1, keepdims=True) + NORM_EPS) * gain
    return y * jax.nn.sigmoid(g)


def hgrn_post_fwd(o, proj, gain, name):
    S, D = o.shape
    H = D // HGRN_HEAD
    tm = _row_tile(S)

    def body(o_ref, g_ref, gain_ref, z_ref, zt_ref):
        for h in range(H):
            ls = pl.ds(h * HGRN_HEAD, HGRN_HEAD)
            z = _head_out(o_ref[:, ls], g_ref[:, ls], gain_ref[:, ls])
            z_ref[:, ls] = z.astype(BF16)
            zt_ref[ls, :] = z.T.astype(BF16)

    row = pl.BlockSpec((tm, D), lambda i: (i, 0))
    return pl.pallas_call(
        body, name=name, grid=(S // tm,),
        in_specs=[row, pl.BlockSpec((tm, D), lambda i: (i, 3)), pl.BlockSpec((1, D), lambda i: (0, 0))],
        out_specs=[row, pl.BlockSpec((D, tm), lambda i: (0, i))],
        out_shape=[jax.ShapeDtypeStruct((S, D), BF16), jax.ShapeDtypeStruct((D, S), BF16)],
        compiler_params=_params("parallel"),
    )(o, proj, gain)


def hgrn_post_bwd(o, proj, gain, dz, name):
    S, D = o.shape
    H = D // HGRN_HEAD
    tm = _row_tile(S)

    def body(o_ref, g_ref, gain_ref, dz_ref, do_ref, dg_ref, dgain_ref):
        @pl.when(pl.program_id(0) == 0)
        def _():
            dgain_ref[...] = jnp.zeros_like(dgain_ref)

        for h in range(H):
            ls = pl.ds(h * HGRN_HEAD, HGRN_HEAD)
            _, vjp = jax.vjp(_head_out, o_ref[:, ls], g_ref[:, ls], gain_ref[:, ls])
            do, dg, dgain = vjp(dz_ref[:, ls].astype(F32))
            do_ref[:, ls] = do
            dg_ref[:, ls] = dg
            dgain_ref[:, ls] += dgain

    row = pl.BlockSpec((tm, D), lambda i: (i, 0))
    vec = pl.BlockSpec((1, D), lambda i: (0, 0))
    return pl.pallas_call(
        body, name=name, grid=(S // tm,),
        in_specs=[row, pl.BlockSpec((tm, D), lambda i: (i, 3)), vec, row], out_specs=[row, row, vec],
        out_shape=[jax.ShapeDtypeStruct((S, D), F32), jax.ShapeDtypeStruct((S, D), F32),
                   jax.ShapeDtypeStruct((1, D), F32)],
        compiler_params=_params("arbitrary"),
    )(o, proj, gain, dz)


def lower_bound_fwd(l0, l1, name):
    def body(a_ref, b_ref, o_ref):
        o_ref[...] = jax.nn.sigmoid(a_ref[...] - b_ref[...])

    return pl.pallas_call(body, name=name, out_shape=jax.ShapeDtypeStruct(l0.shape, F32))(l0, l1)


def lower_bound_bwd(l0, l1, dlb, name):
    def body(a_ref, b_ref, d_ref, o0_ref, o1_ref):
        s = jax.nn.sigmoid(a_ref[...] - b_ref[...])
        d0 = d_ref[...] * s * (1.0 - s)
        o0_ref[...] = d0
        o1_ref[...] = -d0

    sd = jax.ShapeDtypeStruct(l0.shape, F32)
    return pl.pallas_call(body, name=name, out_shape=[sd, sd])(l0, l1, dlb)


def _attn_tile(q, kp, kc, vp, vc, sink, first):
    W = ATT_WINDOW
    nt = (((1,), (1,)), ((), ()))
    tn = (((0,), (0,)), ((), ()))
    qb = q.astype(BF16)
    scale = ATT_HEAD ** -0.5
    sp = lax.dot_general(kp.astype(BF16), qb, nt, preferred_element_type=F32) * scale
    sc = lax.dot_general(kc.astype(BF16), qb, nt, preferred_element_type=F32) * scale
    qi = lax.broadcasted_iota(jnp.int32, sp.shape, 1) & (W - 1)
    kj = lax.broadcasted_iota(jnp.int32, sp.shape, 0)
    sp = jnp.where((kj > qi) & jnp.logical_not(first), sp, NEG_INF)
    sc = jnp.where(kj <= qi, sc, NEG_INF)
    m = jnp.maximum(jnp.maximum(jnp.max(sp, axis=0, keepdims=True), jnp.max(sc, axis=0, keepdims=True)), sink)
    pp = jnp.exp(sp - m)
    pc = jnp.exp(sc - m)
    denom = jnp.sum(pp, axis=0, keepdims=True) + jnp.sum(pc, axis=0, keepdims=True) + jnp.exp(sink - m)
    out = lax.dot_general(vp.astype(BF16), (pp / denom).astype(BF16), tn, preferred_element_type=F32)
    return out + lax.dot_general(vc.astype(BF16), (pc / denom).astype(BF16), tn, preferred_element_type=F32)


def _attn_specs(G, W, Dh):
    q_spec = pl.BlockSpec((None, G, W, Dh), lambda j, n: (j, 0, n, 0))
    prev = pl.BlockSpec((None, W, Dh), lambda j, n: (j, jnp.maximum(n - 1, 0), 0))
    cur = pl.BlockSpec((None, W, Dh), lambda j, n: (j, n, 0))
    sink = pl.BlockSpec((None, 1, G * W), lambda j, n: (j, 0, 0))
    return q_spec, prev, cur, sink


def attn_fwd(q4, k3, v3, sink, name, side=None):
    NKV, G, S, Dh = q4.shape
    W = ATT_WINDOW

    def body(q_ref, kp_ref, kc_ref, vp_ref, vc_ref, s_ref, o_ref):
        first = pl.program_id(1) == 0
        out_t = _attn_tile(q_ref[...].reshape(G * W, Dh), kp_ref[...], kc_ref[...], vp_ref[...], vc_ref[...],
                           s_ref[...], first)
        o_ref[...] = out_t.T.reshape(G, W, Dh)

    q_spec, prev, cur, sk = _attn_specs(G, W, Dh)
    res = _call(body, name=name, grid=(NKV, S // W), in_specs=[q_spec, prev, cur, prev, cur, sk], out_specs=[q_spec],
                out_shape=[jax.ShapeDtypeStruct(q4.shape, F32)], args=[q4, k3, k3, v3, v3, sink],
                semantics=("parallel", "parallel"), side=side)
    return res[0] if side is None else (res[0][0], res[1])


def attn_bwd(q4, k3, v3, sink, do4, name, side=None):
    NKV, G, S, Dh = q4.shape
    W = ATT_WINDOW
    nb = S // W

    def body(q_ref, kp_ref, kc_ref, vp_ref, vc_ref, s_ref, do_ref, dq_ref, dkp_ref, dkc_ref, dvp_ref, dvc_ref,
             ds_ref):
        first = pl.program_id(1) == 0
        _, vjp = jax.vjp(functools.partial(_attn_tile, first=first), q_ref[...].reshape(G * W, Dh), kp_ref[...],
                         kc_ref[...], vp_ref[...], vc_ref[...], s_ref[...])
        dq, dkp, dkc, dvp, dvc, ds = vjp(do_ref[...].reshape(G * W, Dh).T)
        dq_ref[...] = dq.reshape(G, W, Dh)
        dkp_ref[...] = dkp
        dkc_ref[...] = dkc
        dvp_ref[...] = dvp
        dvc_ref[...] = dvc

        @pl.when(first)
        def _():
            ds_ref[...] = jnp.zeros_like(ds_ref)

        head_of = lax.broadcasted_iota(jnp.int32, (G * W, G), 0) // W
        sel = (head_of == lax.broadcasted_iota(jnp.int32, (G * W, G), 1)).astype(F32)
        ds_ref[...] += jnp.dot(ds, sel, precision=lax.Precision.HIGHEST, preferred_element_type=F32)

    q_spec, prev, cur, sk = _attn_specs(G, W, Dh)
    part = pl.BlockSpec((None, None, W, Dh), lambda j, n: (j, n, 0, 0))
    ps = jax.ShapeDtypeStruct((NKV, nb, W, Dh), F32)
    res = _call(body, name=name, grid=(NKV, nb), in_specs=[q_spec, prev, cur, prev, cur, sk, q_spec],
                out_specs=[q_spec, part, part, part, part, pl.BlockSpec((None, 1, G), lambda j, n: (j, 0, 0))],
                out_shape=[jax.ShapeDtypeStruct(q4.shape, F32), ps, ps, ps, ps,
                           jax.ShapeDtypeStruct((NKV, 1, G), F32)],
                args=[q4, k3, k3, v3, v3, sink, do4], semantics=("parallel", "arbitrary"), side=side)
    outs = res if side is None else res[0]
    ans = (outs[0], outs[1:5], outs[5])
    return ans if side is None else (ans, res[1])


def band_combine(kc, kp, vc, vp, name):
    NKV, nb, W, Dh = kc.shape

    def shifted_sum(c_ref, p_ref, o_ref):
        o_ref[pl.ds(0, nb - 1)] = c_ref[pl.ds(0, nb - 1)] + p_ref[pl.ds(1, nb - 1)]
        o_ref[nb - 1] = c_ref[nb - 1]

    def body(kc_ref, kp_ref, vc_ref, vp_ref, dk_ref, dv_ref):
        shifted_sum(kc_ref, kp_ref, dk_ref)
        shifted_sum(vc_ref, vp_ref, dv_ref)

    spec = pl.BlockSpec((None, nb, W, Dh), lambda j: (j, 0, 0, 0))
    sd = jax.ShapeDtypeStruct((NKV, nb, W, Dh), F32)
    dk, dv = pl.pallas_call(
        body, name=name, grid=(NKV,), in_specs=[spec] * 4, out_specs=[spec] * 2, out_shape=[sd, sd],
        compiler_params=_params("parallel"),
    )(kc, kp, vc, vp)
    return dk.reshape(NKV, nb * W, Dh), dv.reshape(NKV, nb * W, Dh)


def rope_tables(S):
    half = ROT_DIM // 2
    inv_freq = jnp.power(jnp.float32(ROPE_THETA), -jnp.arange(0, ROT_DIM, 2, dtype=F32) / ROT_DIM)
    ang = jnp.arange(S, dtype=F32)[:, None] * inv_freq[None, :]
    sin, cos = jnp.sin(ang), jnp.cos(ang)
    zeros = jnp.zeros((S, ATT_HEAD - ROT_DIM), F32)
    z8 = jnp.zeros((S, half), F32)
    cfull = jnp.concatenate([cos, cos, jnp.ones((S, ATT_HEAD - ROT_DIM), F32)], axis=1)
    s_next = jnp.concatenate([-sin, z8, zeros], axis=1)
    s_prev = jnp.concatenate([z8, sin, zeros], axis=1)
    two = lambda t: jnp.concatenate([t, t], axis=1)
    return two(cfull), two(s_next), two(s_prev)


def rope(x, tables, sign, name):
    S, Wd = x.shape
    tm = _row_tile(S)
    rep = Wd // LANES
    half = ROT_DIM // 2

    def body(x_ref, c_ref, sn_ref, sp_ref, o_ref):
        xv = x_ref[...]
        c = jnp.tile(c_ref[...], (1, rep))
        sn = jnp.tile(sn_ref[...], (1, rep))
        sp = jnp.tile(sp_ref[...], (1, rep))
        if sign > 0:
            nxt = pltpu.roll(xv, Wd - half, 1)
            prv = pltpu.roll(xv, half, 1)
            o_ref[...] = xv * c + nxt * sn + prv * sp
        else:
            o_ref[...] = xv * c + pltpu.roll(xv * sn, half, 1) + pltpu.roll(xv * sp, Wd - half, 1)

    row = pl.BlockSpec((tm, Wd), lambda i: (i, 0))
    tab = pl.BlockSpec((tm, LANES), lambda i: (i, 0))
    return pl.pallas_call(
        body, name=name, grid=(S // tm,), in_specs=[row, tab, tab, tab], out_specs=row,
        out_shape=jax.ShapeDtypeStruct((S, Wd), F32), compiler_params=_params("parallel"),
    )(x, *tables)


def cond_proj(c_all, w, bias, name):
    B, D = c_all.shape
    N = w.shape[1]
    tn = _tile(N, 512)

    def body(c_ref, w_ref, b_ref, o_ref):
        cv = c_ref[...]
        cs = (cv * jax.nn.sigmoid(cv)).astype(BF16)
        o_ref[...] = jnp.dot(cs, w_ref[...].astype(BF16), preferred_element_type=F32) + b_ref[...]

    return pl.pallas_call(
        body, name=name, grid=(N // tn,),
        in_specs=[pl.BlockSpec((B, D), lambda j: (0, 0)), pl.BlockSpec((D, tn), lambda j: (0, j)),
                  pl.BlockSpec((1, tn), lambda j: (0, j))],
        out_specs=pl.BlockSpec((B, tn), lambda j: (0, j)), out_shape=jax.ShapeDtypeStruct((B, N), F32),
        compiler_params=_params("parallel"),
    )(c_all, w, bias)


def cond_grad(c_all, dmod, name):
    B, D = c_all.shape
    N = dmod.shape[1]
    tn = _tile(N, 512)

    def body(c_ref, d_ref, o_ref):
        cv = c_ref[...]
        cs = (cv * jax.nn.sigmoid(cv)).astype(BF16)
        o_ref[...] = lax.dot_general(cs, d_ref[...].astype(BF16), (((0,), (0,)), ((), ())),
                                     preferred_element_type=F32)

    return pl.pallas_call(
        body, name=name, grid=(N // tn,),
        in_specs=[pl.BlockSpec((B, D), lambda j: (0, 0)), pl.BlockSpec((B, tn), lambda j: (0, j))],
        out_specs=pl.BlockSpec((D, tn), lambda j: (0, j)), out_shape=jax.ShapeDtypeStruct((D, N), F32),
        compiler_params=_params("parallel"),
    )(c_all, dmod)


def rowsum(g, name):
    B, N = g.shape
    tn = N if B * N * 4 <= (4 << 20) else _tile(N, 8192)

    def body(g_ref, o_ref):
        acc = g_ref[0:1, :]
        for r in range(1, B):
            acc = acc + g_ref[r:r + 1, :]
        o_ref[...] = acc

    return pl.pallas_call(
        body, name=name, grid=(N // tn,), in_specs=[pl.BlockSpec((B, tn), lambda j: (0, j))],
        out_specs=pl.BlockSpec((1, tn), lambda j: (0, j)), out_shape=jax.ShapeDtypeStruct((1, N), F32),
        compiler_params=_params("parallel"),
    )(g)


def _adam_rows(R, C):
    limit = 3 << 19
    if R * C * 4 <= limit or R % 8:
        return R
    best = 8
    for t in range(8, R + 1, 8):
        if R % t == 0 and t * C * 4 <= limit:
            best = t
    return best


def adamw(w3, m3, v3, j, g3, name, col=(0, 1), into=None, side=None):
    n, R, C = w3.shape
    P = g3.shape[0]
    cp, ncol = col
    Cp = C // ncol
    tr = _adam_rows(R, Cp)

    def body(*refs):
        w_ref, m_ref, v_ref, g_ref = refs[:4]
        go_ref, d_ref, mo_ref, vo_ref = refs[-4:]
        g = g_ref[0].astype(F32)
        for p in range(1, P):
            g = g + g_ref[p].astype(F32)
        mn = ADAM_B1 * m_ref[...] + (1.0 - ADAM_B1) * g
        vn = ADAM_B2 * v_ref[...] + (1.0 - ADAM_B2) * jnp.square(g)
        m_hat = mn / (1.0 - ADAM_B1 ** ADAM_STEP)
        v_hat = vn / (1.0 - ADAM_B2 ** ADAM_STEP)
        go_ref[...] = g
        d_ref[...] = -ADAM_LR * (m_hat / (jnp.sqrt(v_hat) + ADAM_EPS) + ADAM_WD * w_ref[...])
        mo_ref[...] = mn
        vo_ref[...] = vn

    spec = pl.BlockSpec((None, tr, Cp), lambda i: (j, i, cp))
    g_spec = pl.BlockSpec((P, tr, Cp), lambda i: (0, i, 0))
    sd = jax.ShapeDtypeStruct((n, R, C), F32)
    in_specs, args, aliases = [spec, spec, spec, g_spec], [w3, m3, v3, g3], {}
    if side is not None:
        assert into is None
        return _call(body, name=name, grid=(R // tr,), in_specs=in_specs, out_specs=[spec] * 4, out_shape=[sd] * 4,
                     args=args, semantics=("parallel",), side=side)
    if into is not None:
        in_specs += [_ANY] * 4
        args += list(into)
        aliases = {4 + k: k for k in range(4)}
    return pl.pallas_call(
        body, name=name, grid=(R // tr,), in_specs=in_specs, out_specs=[spec] * 4, out_shape=[sd] * 4,
        input_output_aliases=aliases, compiler_params=_params("parallel"),
    )(*args)


def _place():
    return lax.axis_index("x"), lax.axis_index("y"), lax.axis_index("c")


def _slot(p):
    return 4 * p[0] + 2 * p[1] + p[2]


def gather_side(items):
    xs = [a for a, _ in items]
    n = len(xs)

    def copy(ins, outs, sems, t, k, block, to, from_input=False):
        dst = outs[t].at[_slot(block)]
        return pltpu.make_async_remote_copy(
            src_ref=ins[t] if from_input else dst, dst_ref=dst, send_sem=sems[0].at[t, k],
            recv_sem=sems[1].at[t, k], device_id=to, device_id_type=MESH)

    def peers():
        x, y, c = _place()
        return (x, y, c), (x, y, 1 - c), [(1 - x, y), (x, 1 - y), (1 - x, 1 - y)]

    def start(ins, outs, sems):
        me, sibling, chips = peers()
        c = me[2]
        for t in range(n):
            pltpu.make_async_copy(ins[t], outs[t].at[_slot(me)], sems[2].at[t]).start()
            copy(ins, outs, sems, t, 0, me, sibling, True).start()
            for j, chip in enumerate(chips):
                copy(ins, outs, sems, t, 1 + j, me, (*chip, c), True).start()

    def finish(ins, outs, sems):
        me, sibling, chips = peers()
        c = me[2]
        for t in range(n):
            for j, chip in enumerate(chips):
                copy(ins, outs, sems, t, 1 + j, (*chip, c), me).wait_recv()
                copy(ins, outs, sems, t, 4 + j, (*chip, c), sibling).start()
        for t in range(n):
            copy(ins, outs, sems, t, 0, sibling, me).wait_recv()
            for j, chip in enumerate(chips):
                copy(ins, outs, sems, t, 4 + j, (*chip, 1 - c), me).wait_recv()
        for t in range(n):
            copy(ins, outs, sems, t, 0, me, sibling, True).wait_send()
            for j, chip in enumerate(chips):
                copy(ins, outs, sems, t, 1 + j, me, (*chip, c), True).wait_send()
                copy(ins, outs, sems, t, 4 + j, (*chip, c), sibling).wait_send()
            pltpu.make_async_copy(ins[t], outs[t].at[_slot(me)], sems[2].at[t]).wait()

    return _Side(xs, [jax.ShapeDtypeStruct((N_DEV,) + a.shape, a.dtype) for a in xs],
                 [pltpu.SemaphoreType.DMA((n, 7)), pltpu.SemaphoreType.DMA((n, 7)), pltpu.SemaphoreType.DMA((n,))],
                 start, finish)


def scatter_side(items):
    gs = [g for g, _ in items]
    n = len(gs)

    def part_shape(g, part):
        _, R, C = g.shape
        if part is None:
            return R, C
        axis, _, cnt = part
        return (R // cnt, C) if axis == 0 else (R, C // cnt)

    def block(ref, slot, g, part):
        if part is None:
            return ref.at[slot]
        axis, idx, cnt = part
        R, C = part_shape(g, part)
        return ref.at[slot, pl.ds(idx * R, R)] if axis == 0 else ref.at[slot, :, pl.ds(idx * C, C)]

    def copies(ins, outs, sems):
        x, y, c = _place()
        me = (x, y, c)
        out = []
        for t, (g, part) in enumerate(items):
            out.append(pltpu.make_async_copy(block(ins[t], _slot(me), g, part), outs[t].at[_slot(me)], sems[2].at[t]))
            for r in range(1, N_DEV):
                peer = (1 - x if r & 4 else x, 1 - y if r & 2 else y, 1 - c if r & 1 else c)
                out.append(pltpu.make_async_remote_copy(
                    src_ref=block(ins[t], _slot(peer), g, part), dst_ref=outs[t].at[_slot(me)],
                    send_sem=sems[0].at[t, r - 1], recv_sem=sems[1].at[t, r - 1], device_id=peer, device_id_type=MESH))
        return out

    def start(ins, outs, sems):
        for cp in copies(ins, outs, sems):
            cp.start()

    def finish(ins, outs, sems):
        for cp in copies(ins, outs, sems):
            cp.wait()

    return _Side(gs, [jax.ShapeDtypeStruct((N_DEV,) + part_shape(g, part), g.dtype) for g, part in items],
                 [pltpu.SemaphoreType.DMA((n, 7)), pltpu.SemaphoreType.DMA((n, 7)), pltpu.SemaphoreType.DMA((n,))],
                 start, finish)


def exchange(side, name):
    def body(*refs):
        n_in, n_out = len(side.inputs), len(side.out_shapes)
        ins, outs, sems = refs[:n_in], refs[n_in:n_in + n_out], refs[n_in + n_out:]
        side.start(ins, outs, sems)
        side.finish(ins, outs, sems)

    return pl.pallas_call(
        body, name=name, in_specs=[_ANY] * len(side.inputs), out_specs=[_ANY] * len(side.out_shapes),
        out_shape=side.out_shapes, scratch_shapes=side.sem_shapes,
    )(*side.inputs)


class _Part:
    def __init__(self, key, axis, index, count):
        self.key, self.axis, self.index, self.count = key, axis, index, count

    def __hash__(self):
        return hash((self.key, self.axis, self.index, self.count))

    def __eq__(self, other):
        return isinstance(other, _Part) and (self.key, self.axis, self.index, self.count) == (
            other.key, other.axis, other.index, other.count)


def _rows(key, p):
    return _Part(key, 0, p, 2)


def _cols(key, p):
    return _Part(key, 1, p, 2)


class _Plan:
    def __init__(self, plan, make_side):
        self.plan, self.make_side, self.source, self.got = plan, make_side, {}, {}

    def item(self, k):
        return (self.source[k.key], (k.axis, k.index, k.count)) if isinstance(k, _Part) else (self.source[k], None)

    def run(self, fn, *args, name, **kw):
        keys = self.plan.get(name)
        if not keys:
            return fn(*args, name=name, **kw)
        result, outs = fn(*args, name=name, side=self.make_side([self.item(k) for k in keys]), **kw)
        self.got.update(zip(keys, outs))
        return result


GATHER_PLAN = {
    "l0s0_in": [("out", 0, 0), "hout", "kv"],
    "l0s0_out": ["hin"],
    "l0s1_proj": [("out", 0, 1)],
    "l0s1_scan": [("in", 0, 1)],
    "l0s2_in": [("in", 1, 0)],
    "l0s2_out": [("out", 1, 0)],
    "l1s0_in": [("in", 1, 1)],
    "l1s0_out": ["q", "o"],
    "l1s1_attn": [("out", 1, 1)],
}
GATHER_FIRST = [("in", 0, 0)]
SCATTER_PLAN = {
    "l1s2_dact": [_cols(("out", 1, 1), 0)],
    "l1s2_dwin0": [_cols(("out", 1, 1), 1)],
    "l1s2_dwin1": [_rows(("in", 1, 1, 0), 0)],
    "l1s2_du": [_rows(("in", 1, 1, 0), 1), _rows(("in", 1, 1, 1), 0)],
    "l1s1_dattn": [_rows(("in", 1, 1, 1), 1), "o"],
    "l1s0_dwout": ["q"],
    "l1s0_dact": [_cols(("out", 1, 0), 0)],
    "l1s0_dwin0": [_cols(("out", 1, 0), 1)],
    "l1s0_dwin1": [_rows(("in", 1, 0, 0), 0)],
    "l1s0_du": [_rows(("in", 1, 0, 0), 1), _rows(("in", 1, 0, 1), 0)],
    "l0s2_dwout": [_rows(("in", 1, 0, 1), 1)],
    "l0s2_dact": [_cols(("out", 0, 1), 0)],
    "l0s2_dwin0": [_cols(("out", 0, 1), 1)],
    "l0s2_dwin1": [_rows(("in", 0, 1, 0), 0)],
    "l0s2_du": [_rows(("in", 0, 1, 0), 1)],
    "l0s1_dwout": ["kv"],
    "l0s1_dscan": [_rows(("in", 0, 1, 1), 0), _rows(("in", 0, 1, 1), 1), "hout"],
    "l0s1_dwin1": [_rows(("hin", 0), 0)],
    "l0s1_du": [_rows(("hin", 0), 1), _rows(("hin", 1), 0)],
    "l0s0_dwout": [_rows(("hin", 1), 1)],
    "l0s0_dact": [_cols(("out", 0, 0), 0)],
    "l0s0_dwin0": [_cols(("out", 0, 0), 1)],
    "l0s0_dwin1": [_rows(("in", 0, 0, 0), 0)],
    "l0s0_du": [_rows(("in", 0, 0, 0), 1), _rows(("in", 0, 0, 1), 0)],
    "adam_w_ada0": [_rows(("in", 0, 0, 1), 1)],
}
GRAD_TM, GRAD_TK = 512, 4096


def _ffn_fwd(gp, W, h, gate, u, l, i, tag):
    ab4, h3, h3t = gp.run(ffn_in_act, u, W[("in", l, i)], name=tag + "_in")
    w_out = W[("out", l, i)]
    J = h3.shape[0]
    y, h_new = gp.run(mm_nn, h3, w_out.reshape(J, 1, -1, w_out.shape[-1]), F32, name=tag + "_out", natural=True,
                      resid=(h, gate, 0.5), kgroup=J)
    return y, h_new, (ab4, h3t)


def _ffn_bwd(sp, W, dy, ut, ab4, h3t, l, i, tag):
    w_in, w_out = W[("in", l, i)], W[("out", l, i)]
    J, nb, S = h3t.shape
    D = w_out.shape[-1]
    dw_out = sp.run(mm_nn, h3t.reshape(1, J * nb, S), dy[None, None], BF16, name=tag + "_dwout", natural=True,
                    tm=w_out.shape[1], tk=GRAD_TK)
    sp.source[("out", l, i)] = dw_out.reshape(w_out.shape)
    dab3 = sp.run(ffn_dact, dy, w_out.reshape(J, nb, D), ab4, name=tag + "_dact")
    for hf in range(2):
        sp.source[("in", l, i, hf)] = sp.run(mm_nn, ut.reshape(2, D // 2, S), dab3[None], BF16, name=f"{tag}_dwin{hf}",
                                             a_sel=hf, tm=GRAD_TM, tk=GRAD_TK)
    return sp.run(mm_nt, dab3, w_in[:, None], F32, name=tag + "_du", natural=True, kgroup=4)


def kernel(x, c, norm_gain, w_ada, b_ada, w_ffn_in, w_ffn_out, w_hgrn_in, hgrn_lb_logits, hgrn_head_gain, w_hgrn_out, kv_gain, w_ada_kv, b_ada_kv, w_kv, b_kv, w_q, b_q, attn_sinks, w_attn_out, final_gain, loss_target, m_norm_gain, m_w_ada, m_b_ada, m_w_ffn_in, m_w_ffn_out, m_w_hgrn_in, m_hgrn_lb_logits, m_hgrn_head_gain, m_w_hgrn_out, m_kv_gain, m_w_ada_kv, m_b_ada_kv, m_w_kv, m_b_kv, m_w_q, m_b_q, m_attn_sinks, m_w_attn_out, m_final_gain, v_norm_gain, v_w_ada, v_b_ada, v_w_ffn_in, v_w_ffn_out, v_w_hgrn_in, v_hgrn_lb_logits, v_hgrn_head_gain, v_w_hgrn_out, v_kv_gain, v_w_ada_kv, v_b_ada_kv, v_w_kv, v_b_kv, v_w_q, v_b_q, v_attn_sinks, v_w_attn_out, v_final_gain):
    xi, yi, ci = _place()
    me = 4 * xi + 2 * yi + ci
    _, S, D = x.shape
    L = norm_gain.shape[0]
    dsh = D // N_DEV
    ada_n = w_ada.shape[2]
    kv_n = w_ada_kv.shape[1]
    NQ = D // ATT_HEAD
    NKV = NQ // ATT_GROUP
    kvd = NKV * ATT_HEAD
    h0 = x[0]
    target = loss_target[0]

    def my_cols(a, n):
        return lax.dynamic_slice_in_dim(a, me * n, n, axis=a.ndim - 1)

    gp = _Plan(GATHER_PLAN, gather_side)
    bf = lambda a: a.astype(BF16)
    for l in range(L):
        for i in range(2):
            gp.source[("in", l, i)] = bf(w_ffn_in[l, i])
            gp.source[("out", l, i)] = bf(w_ffn_out[l, i])
    gp.source.update(hin=bf(w_hgrn_in[0]), hout=bf(w_hgrn_out[0]), kv=bf(w_kv), q=bf(w_q[0]), o=bf(w_attn_out[0]))
    W = gp.got
    W.update(zip(GATHER_FIRST, exchange(gather_side([(gp.source[k], None) for k in GATHER_FIRST]), "gather_first")))
    full = lambda w: w.reshape(1, 1, -1, w.shape[-1])

    lb_sh = lower_bound_fwd(hgrn_lb_logits[0:1], hgrn_lb_logits[1:2], "lb_fwd")
    small = jnp.concatenate([c, norm_gain.reshape(1, L * 3 * dsh), hgrn_head_gain, lb_sh], axis=1)
    (g1,) = exchange(gather_side([(small, None)]), "gather_cond")
    g1 = g1.reshape(N_DEV, -1)
    c_all = g1[:, :D]
    gains = g1[:, D:D + L * 3 * dsh].reshape(N_DEV, L * 3, dsh).transpose(1, 0, 2).reshape(L, 3, 1, D)
    head_gain = g1[:, D + L * 3 * dsh:D + (L * 3 + 1) * dsh].reshape(1, D)
    lb0 = g1[:, D + (L * 3 + 1) * dsh:].reshape(1, D)

    parts = [cond_proj(c_all, w_ada[l], my_cols(b_ada[l:l + 1], ada_n), f"mod{l}") for l in range(L)]
    parts.append(cond_proj(c_all, w_ada_kv, my_cols(b_ada_kv[None], kv_n), "mod_kv"))
    (g2,) = exchange(gather_side([(jnp.concatenate(parts, axis=1), None)]), "gather_mod")
    mine2 = lax.dynamic_index_in_dim(g2, me, axis=1, keepdims=False)
    mod = [mine2[:, l * ada_n:(l + 1) * ada_n].reshape(3, 3, 1, D) for l in range(L)]
    mod_kv = mine2[:, L * ada_n:].reshape(2, 1, D)

    tables = rope_tables(S)
    sink_col = jnp.broadcast_to(attn_sinks.reshape(NKV, ATT_GROUP, 1, 1), (NKV, ATT_GROUP, ATT_WINDOW, 1))
    sink_col = sink_col.reshape(NKV, 1, ATT_GROUP * ATT_WINDOW)

    def to_heads(t, n):
        return t.reshape(S, n, ATT_HEAD).transpose(1, 0, 2)

    def from_heads(t):
        return t.transpose(1, 0, 2).reshape(S, -1)

    h = h0
    saved = {}
    for l in range(L):
        for s in (0, 1, 2):
            tag = f"l{l}s{s}"
            shift, scale, gate = mod[l][s, 0], mod[l][s, 1], mod[l][s, 2]
            u, ut = adaln_fwd(h, gains[l, s], shift, scale, tag + "_norm")
            if s != 1:
                y, h_new, res = _ffn_fwd(gp, W, h, gate, u, l, s // 2, tag)
            elif l == 0:
                proj = gp.run(mm_nn, u[None], W["hin"][None], F32, name=tag + "_proj", natural=True)
                o, states = gp.run(hgrn_scan_fwd, proj, lb0, name=tag + "_scan")
                z, zt = hgrn_post_fwd(o, proj, head_gain, tag + "_post")
                y, h_new = mm_nn(z[None], full(W["hout"]), F32, tag + "_out", natural=True, resid=(h, gate, 1.0))
                res = (proj, o, states, zt)
            else:
                q = mm_nn(u[None], full(W["q"]), F32, tag + "_q", natural=True, bias=b_q)
                q4 = to_heads(rope(q, tables, 1, tag + "_rope"), NQ).reshape(NKV, ATT_GROUP, S, ATT_HEAD)
                att4 = gp.run(attn_fwd, q4, k3, v3, sink_col, name=tag + "_attn")
                att = from_heads(att4.reshape(NQ, S, ATT_HEAD))
                att_t = att4.reshape(NQ, S, ATT_HEAD).transpose(0, 2, 1).reshape(D, S)
                y, h_new = mm_nn(att[None], full(W["o"]), F32, tag + "_out", natural=True, resid=(h, gate, 1.0))
                res = (q4, att_t)
            saved[(l, s)] = (h, ut, y, res)
            h = h_new
        if l == 0:
            h_kv = h
            u_kv, u_kv_t = adaln_fwd(h, kv_gain[None], mod_kv[0], mod_kv[1], "kv_norm")
            kvp = mm_nn(u_kv[None], full(W["kv"]), F32, "kv_proj", natural=True, bias=b_kv[None])
            k3 = to_heads(rope(kvp[:, :kvd], tables, 1, "kv_rope"), NKV)
            v3 = to_heads(kvp[:, kvd:], NKV)

    def branch(l, s):
        return saved[(l, s)][2], mod[l][s, 2], 0.5 if s != 1 else 1.0

    loss_row, dh, d_final_gain, dy, d_gate = final_loss_grad(h, final_gain[None], target, "final", branch(L - 1, 2))
    loss = lax.psum(loss_row[0, 0], ("x", "y", "c"))

    sp = _Plan(SCATTER_PLAN, scatter_side)

    def grad_w(a_t, b, name):
        return sp.run(mm_nn, a_t[None], b[None, None], BF16, name=name, natural=True, tm=GRAD_TM, tk=GRAD_TK)

    d_mod = [[None] * 3 for _ in range(L)]
    d_gain = [[None] * 3 for _ in range(L)]
    for l in reversed(range(L)):
        if l == 0:
            dkv = jnp.concatenate([rope(from_heads(dk3), tables, -1, "kv_drope"), from_heads(dv3)], axis=1)
            sp.source["kv"] = grad_w(u_kv_t, dkv, "kv_dw").reshape(W["kv"].shape)
            db_kv = colsum(dkv, "kv_db")
            du_kv = mm_nt(dkv[None], full(W["kv"]), F32, "kv_du", natural=True)
            dh, d_kv_gain, d_kv_shift, d_kv_scale, dy, d_gate = adaln_bwd(
                h_kv, kv_gain[None], mod_kv[0], mod_kv[1], du_kv, dh, "kv_dnorm", nxt=branch(0, 2))
        for s in (2, 1, 0):
            tag = f"l{l}s{s}"
            shift, scale, gate = mod[l][s, 0], mod[l][s, 1], mod[l][s, 2]
            h_in, ut, y, res = saved[(l, s)]
            if s != 1:
                du = _ffn_bwd(sp, W, dy, ut, res[0], res[1], l, s // 2, tag)
            elif l == 0:
                proj, o, states, zt = res
                sp.source["hout"] = grad_w(zt, dy, tag + "_dwout").reshape(W["hout"].shape)
                dz = mm_nt(dy[None], full(W["hout"]), F32, tag + "_dz", natural=True)
                do, dg, d_head_gain = hgrn_post_bwd(o, proj, head_gain, dz, tag + "_dpost")
                dproj, d_lb0 = sp.run(hgrn_scan_bwd, proj, lb0, states, do, dg, name=tag + "_dscan")
                for hf in range(2):
                    sp.source[("hin", hf)] = sp.run(mm_nn, ut.reshape(2, D // 2, S), dproj, BF16, name=f"{tag}_dwin{hf}",
                                                    b_natural=True, jn=N_DEV, a_sel=hf, tm=GRAD_TM, tk=GRAD_TK)
                du = sp.run(mm_nt, dproj, W["hin"][:, None], F32, name=tag + "_du", natural=True, a_natural=True)
            else:
                q4, att_t = res
                sp.source["o"] = grad_w(att_t, dy, tag + "_dwout").reshape(W["o"].shape)
                datt = mm_nt(dy[None], full(W["o"]), F32, tag + "_datt", natural=True)
                datt4 = to_heads(datt, NQ).reshape(NKV, ATT_GROUP, S, ATT_HEAD)
                dq4, (dkp, dkc, dvp, dvc), d_sink = sp.run(attn_bwd, q4, k3, v3, sink_col, datt4, name=tag + "_dattn")
                dk3, dv3 = band_combine(dkc, dkp, dvc, dvp, tag + "_dkv")
                dq = rope(from_heads(dq4.reshape(NQ, S, ATT_HEAD)), tables, -1, tag + "_drope")
                sp.source["q"] = grad_w(ut, dq, tag + "_dwq").reshape(W["q"].shape)
                db_q = colsum(dq, tag + "_dbq")
                du = mm_nt(dq[None], full(W["q"]), F32, tag + "_du", natural=True)
            outs = adaln_bwd(h_in, gains[l, s], shift, scale, du, dh, tag + "_dnorm",
                             nxt=branch(l, s - 1) if s > 0 else None)
            dh, d_gain[l][s], dsh_, dsc_ = outs[:4]
            d_mod[l][s] = jnp.concatenate([dsh_, dsc_, d_gate], axis=1)
            if s > 0:
                dy, d_gate = outs[4:]
    grad_x = dh[None]
    G = sp.got

    pad = lambda a, n: jnp.pad(a, ((0, 0), (0, n - a.shape[1])))
    pieces = [jnp.concatenate(d_mod[l], axis=1) for l in range(L)]
    pieces += [d_kv_shift, d_kv_scale]
    pieces += [d_gain[l][s] for l in range(L) for s in range(3)]
    pieces += [d_head_gain, d_lb0, d_kv_gain, db_kv, db_q, pad(d_sink.reshape(1, NQ), LANES), d_final_gain]
    (g3,) = exchange(gather_side([(jnp.concatenate(pieces, axis=1), None)]), "gather_small_grads")
    g3 = g3.reshape(N_DEV, -1)
    tot = rowsum(g3, "sum_small_grads")
    offs = [0]
    for p in pieces:
        offs.append(offs[-1] + p.shape[1])
    seg = lambda k: tot[:, offs[k]:offs[k + 1]]
    k0 = 0
    g_b_ada = jnp.concatenate([seg(l) for l in range(L)], axis=0)
    k0 += L
    g_b_ada_kv = jnp.concatenate([seg(k0), seg(k0 + 1)], axis=1)
    k0 += 2
    g_norm_gain = jnp.concatenate([my_cols(seg(k0 + j), dsh) for j in range(3 * L)], axis=0)
    k0 += 3 * L
    g_head_gain = my_cols(seg(k0), dsh)
    d_lb_sh = my_cols(seg(k0 + 1), dsh)
    g_kv_gain = seg(k0 + 2)
    g_b_kv = seg(k0 + 3)
    g_b_q = seg(k0 + 4)
    g_sinks = seg(k0 + 5)[:, :NQ]
    g_final_gain = seg(k0 + 6)
    dl0, dl1 = lower_bound_bwd(hgrn_lb_logits[0:1], hgrn_lb_logits[1:2], d_lb_sh, "lb_bwd")
    g_lb_logits = jnp.concatenate([dl0, dl1], axis=0)

    g_w_ada = jnp.stack([cond_grad(c_all, lax.dynamic_slice_in_dim(g3, offs[l] + me * ada_n, ada_n, axis=1),
                                   f"dw_ada{l}") for l in range(L)])
    g_w_ada_kv = cond_grad(c_all, lax.dynamic_slice_in_dim(g3, offs[L] + me * kv_n, kv_n, axis=1), "dw_ada_kv")

    def update(w, m, v, grads, name, ncol=1):
        n = len(grads) // ncol
        width = w.shape[-1]
        three = lambda a: a.reshape((n, -1, width))
        outs = None
        for k, g in enumerate(grads):
            j, cp = divmod(k, ncol)
            g = g.reshape((g.shape[0], -1, width // ncol))
            if len(grads) == 1:
                outs = sp.run(adamw, three(w), three(m), three(v), j, g, name=f"{name}{k}")
            else:
                outs = adamw(three(w), three(m), three(v), j, g, f"{name}{k}", col=(cp, ncol), into=outs)
        return [o.reshape(w.shape) for o in outs]

    def one(w, m, v, g, name):
        return update(w, m, v, [g.reshape((1, -1, w.shape[-1]))], name)

    res = {}
    res["norm_gain"] = one(norm_gain, m_norm_gain, v_norm_gain, g_norm_gain, "adam_norm_gain")
    res["w_ada"] = one(w_ada, m_w_ada, v_w_ada, g_w_ada, "adam_w_ada")
    res["b_ada"] = one(b_ada, m_b_ada, v_b_ada, g_b_ada, "adam_b_ada")
    sent = {k.key if isinstance(k, _Part) else k for k in G}
    rest = [k for k in sp.source if k not in sent]
    if rest:
        G.update(zip(rest, exchange(scatter_side([(sp.source[k], None) for k in rest]), "scatter_rest")))
    g_ffn_in = [G[_rows(("in", l, i, hf), p)] for l in range(L) for i in range(2) for hf in range(2) for p in range(2)]
    g_ffn_out = [G[_cols(("out", l, i), p)] for l in range(L) for i in range(2) for p in range(2)]
    g_hgrn_in = [G[_rows(("hin", hf), p)] for hf in range(2) for p in range(2)]
    res["w_ffn_in"] = update(w_ffn_in, m_w_ffn_in, v_w_ffn_in, g_ffn_in, "adam_w_ffn_in")
    res["w_ffn_out"] = update(w_ffn_out, m_w_ffn_out, v_w_ffn_out, g_ffn_out, "adam_w_ffn_out", ncol=2)
    res["w_hgrn_in"] = update(w_hgrn_in, m_w_hgrn_in, v_w_hgrn_in, g_hgrn_in, "adam_w_hgrn_in")
    res["hgrn_lb_logits"] = one(hgrn_lb_logits, m_hgrn_lb_logits, v_hgrn_lb_logits, g_lb_logits, "adam_lb")
    res["hgrn_head_gain"] = one(hgrn_head_gain, m_hgrn_head_gain, v_hgrn_head_gain, g_head_gain, "adam_head_gain")
    res["w_hgrn_out"] = update(w_hgrn_out, m_w_hgrn_out, v_w_hgrn_out, [G["hout"]], "adam_w_hgrn_out")
    res["kv_gain"] = one(kv_gain, m_kv_gain, v_kv_gain, g_kv_gain, "adam_kv_gain")
    res["w_ada_kv"] = one(w_ada_kv, m_w_ada_kv, v_w_ada_kv, g_w_ada_kv, "adam_w_ada_kv")
    res["b_ada_kv"] = one(b_ada_kv, m_b_ada_kv, v_b_ada_kv, g_b_ada_kv, "adam_b_ada_kv")
    res["w_kv"] = update(w_kv, m_w_kv, v_w_kv, [G["kv"]], "adam_w_kv")
    res["b_kv"] = one(b_kv, m_b_kv, v_b_kv, g_b_kv, "adam_b_kv")
    res["w_q"] = update(w_q, m_w_q, v_w_q, [G["q"]], "adam_w_q")
    res["b_q"] = one(b_q, m_b_q, v_b_q, g_b_q, "adam_b_q")
    res["attn_sinks"] = one(attn_sinks, m_attn_sinks, v_attn_sinks, g_sinks, "adam_sinks")
    res["w_attn_out"] = update(w_attn_out, m_w_attn_out, v_w_attn_out, [G["o"]], "adam_w_attn_out")
    res["final_gain"] = one(final_gain, m_final_gain, v_final_gain, g_final_gain, "adam_final_gain")

    names = ["norm_gain", "w_ada", "b_ada", "w_ffn_in", "w_ffn_out", "w_hgrn_in", "hgrn_lb_logits", "hgrn_head_gain",
             "w_hgrn_out", "kv_gain", "w_ada_kv", "b_ada_kv", "w_kv", "b_kv", "w_q", "b_q", "attn_sinks", "w_attn_out",
             "final_gain"]
    return (loss, grad_x, *[res[n][0] for n in names], *[res[n][1] for n in names], *[res[n][2] for n in names],
            *[res[n][3] for n in names])
```
